```python
import math
import jax
import jax.numpy as jnp
from jax import lax
import numpy as np

D_MODEL = 2048
BATCH = 4
SEQ = 2048
DEPTH = 2

GRID_W = 64
CTX_LEN = 256
ROPE_BASE = 10000.0
EPS = 1e-6
Q_BLOCK = 128
F32 = jnp.float32

N_BRANCH = 3
BRANCH_W = 1024

M_HEADS = 4
M_DH = BRANCH_W // M_HEADS
M_CHUNK = 64
M_CONV = 5

DA_HEADS = 4
DA_DV = BRANCH_W // DA_HEADS
DA_DH = DA_DV // 2
DA_QK_W = DA_HEADS * 2 * DA_DH

MLA_HEADS = 8
MLA_Q_RANK = 512
MLA_KV_RANK = 256
MLA_NOPE = 128
MLA_ROPE = 64
MLA_DV = BRANCH_W // MLA_HEADS
MLA_QK = MLA_NOPE + MLA_ROPE

N_EXPERTS = 16
N_GROUPS = 4
GROUP_SIZE = N_EXPERTS // N_GROUPS
GROUP_SCORE_K = 2
TOP_K = 2
D_EXPERT = 512

OFF_MQK = 0
OFF_MV = OFF_MQK + 2 * BRANCH_W
OFF_MO = OFF_MV + BRANCH_W
OFF_MG = OFF_MO + BRANCH_W
OFF_DQ = OFF_MG + 4 * M_HEADS
OFF_DK = OFF_DQ + DA_QK_W
OFF_DV = OFF_DK + DA_QK_W
OFF_CQ = OFF_DV + BRANCH_W
OFF_CKV = OFF_CQ + MLA_Q_RANK
OFF_KR = OFF_CKV + MLA_KV_RANK
OFF_G = OFF_KR + MLA_ROPE
D_IN = OFF_G + N_BRANCH * D_MODEL

kernel_name = 'hybrid_mlstm_diffattn_mla_moe_dit'


def rms_norm(x, g=None):
    xf = x.astype(F32)
    y = (xf * lax.rsqrt(jnp.mean(xf * xf, axis=-1, keepdims=True) + EPS)).astype(x.dtype)
    return y if g is None else y * g


def modulate(x, shift, scale):
    return rms_norm(x) * (1 + scale) + shift


def split_heads(a, n_heads):
    b, t, _ = a.shape
    return a.reshape(b, t, n_heads, -1).transpose(0, 2, 1, 3)


def merge_heads(a):
    b, h, t, d = a.shape
    return a.transpose(0, 2, 1, 3).reshape(b, t, h * d)


def axial_angles(n, rot_dim):
    rows = n // GRID_W
    r = jnp.repeat(jnp.arange(rows, dtype=F32), GRID_W)
    col = jnp.tile(jnp.arange(GRID_W, dtype=F32), rows)
    n_freq = rot_dim // 4
    inv = ROPE_BASE ** (-jnp.arange(n_freq, dtype=F32) / n_freq)
    return jnp.concatenate([r[:, None] * inv, col[:, None] * inv], axis=-1)


def rope(x, ang):
    half = x.shape[-1] // 2
    cos = jnp.cos(ang).astype(x.dtype)
    sin = jnp.sin(ang).astype(x.dtype)
    x1, x2 = x[..., :half], x[..., half:]
    return jnp.concatenate([x1 * cos - x2 * sin, x1 * sin + x2 * cos], axis=-1)


def softmax32(s):
    return jax.nn.softmax(s.astype(F32), axis=-1)


def sweep_query_blocks(block_fn, qs):
    b, h, t, _ = qs[0].shape
    nb = t // Q_BLOCK
    blocks = tuple(jnp.moveaxis(q.reshape(b, h, nb, Q_BLOCK, q.shape[-1]), 2, 0) for q in qs)
    out = lax.map(block_fn, blocks)
    return jnp.moveaxis(out, 0, 2).reshape(b, h, t, out.shape[-1])


def dwconv_centered(x, w, bias):
    k = w.shape[0]
    y = lax.conv_general_dilated(x, w[:, None, :].astype(x.dtype), window_strides=(1,),
                                 padding=[(k // 2, k // 2)], dimension_numbers=('NWC', 'WIO', 'NWC'),
                                 feature_group_count=x.shape[-1])
    return y + bias.astype(x.dtype)


def mlstm_chunkwise(q, k, v, ig, lf, state):
    b, h, t, dh = q.shape
    nc = t // M_CHUNK

    def to_chunks(a):
        return jnp.moveaxis(a.reshape(b, h, nc, M_CHUNK, *a.shape[3:]), 2, 0)

    xs = tuple(to_chunks(a) for a in (q.astype(F32) * dh ** -0.5, k.astype(F32), v.astype(F32), ig, lf))
    causal = jnp.tril(jnp.ones((M_CHUNK, M_CHUNK), dtype=bool))

    def step(carry, chunk):
        c_mat, n_vec, m = carry
        qc, kc, vc, ic, fc = chunk
        bcum = jnp.cumsum(fc, axis=-1)
        logw = jnp.where(causal, bcum[..., :, None] - bcum[..., None, :] + ic[..., None, :], -jnp.inf)
        inter = bcum + m[..., None]
        m_t = jnp.maximum(inter, logw.max(-1))
        w = jnp.exp(logw - m_t[..., None]) * jnp.einsum('bhtd,bhsd->bhts', qc, kc)
        decay = jnp.exp(inter - m_t)
        num = jnp.einsum('bhts,bhsd->bhtd', w, vc) + decay[..., None] * jnp.einsum('bhvk,bhtk->bhtv', c_mat, qc)
        den = w.sum(-1) + decay * jnp.einsum('bhk,bhtk->bht', n_vec, qc)
        h_out = num / jnp.maximum(jnp.abs(den), jnp.exp(-m_t))[..., None]
        b_last = bcum[..., -1]
        src = b_last[..., None] - bcum + ic
        m_new = jnp.maximum(b_last + m, src.max(-1))
        g = jnp.exp(src - m_new[..., None])
        keep = jnp.exp(b_last + m - m_new)
        c_mat = keep[..., None, None] * c_mat + jnp.einsum('bhsv,bhsk->bhvk', g[..., None] * vc, kc)
        n_vec = keep[..., None] * n_vec + jnp.einsum('bhs,bhsk->bhk', g, kc)
        return (c_mat, n_vec, m_new), h_out

    state, hs = lax.scan(step, state, xs)
    return jnp.moveaxis(hs, 0, 2).reshape(b, h, t, dh), state


def mlstm_inputs(p, conv_w, conv_b):
    b, t, _ = p.shape
    qk = jax.nn.silu(dwconv_centered(p[..., OFF_MQK:OFF_MQK + 2 * BRANCH_W], conv_w, conv_b))
    q = split_heads(qk[..., :BRANCH_W], M_HEADS)
    k = split_heads(qk[..., BRANCH_W:], M_HEADS)
    v = split_heads(p[..., OFF_MV:OFF_MV + BRANCH_W], M_HEADS)
    o = jax.nn.sigmoid(p[..., OFF_MO:OFF_MO + BRANCH_W])
    gates = p[..., OFF_MG:OFF_MG + 4 * M_HEADS].astype(F32).reshape(b, t, 4, M_HEADS).transpose(2, 0, 3, 1)
    return q, k, v, o, gates


def mlstm_bidirectional(q, k, v, gates, state_f, state_b):
    h_f, state_f = mlstm_chunkwise(q, k, v, gates[0], jax.nn.log_sigmoid(gates[1]), state_f)
    flip = lambda a: jnp.flip(a, axis=2)
    h_b, state_b = mlstm_chunkwise(flip(q), flip(k), flip(v), flip(gates[2]),
                                   jax.nn.log_sigmoid(flip(gates[3])), state_b)
    return h_f + flip(h_b), state_f, state_b


def mlstm_output(h, o, norm_g):
    hn = rms_norm(h, norm_g.reshape(M_HEADS, 1, M_DH).astype(F32))
    return merge_heads(hn).astype(o.dtype) * o


def diff_qkv(p, q_g, k_g, ang):
    b, t, _ = p.shape
    halves = lambda a: a.reshape(b, t, DA_HEADS, 2, DA_DH).transpose(3, 0, 2, 1, 4)
    q = rms_norm(halves(p[..., OFF_DQ:OFF_DQ + DA_QK_W]), q_g)
    k = rms_norm(halves(p[..., OFF_DK:OFF_DK + DA_QK_W]), k_g)
    if ang is not None:
        q = rope(q, ang)
        k = rope(k, ang)
    v = split_heads(p[..., OFF_DV:OFF_DV + BRANCH_W], DA_HEADS)
    return q, k, v


def diff_attention(q, k, v, lam):
    scale = DA_DH ** -0.5

    def block(qb):
        q1b, q2b = qb
        p1 = softmax32(jnp.einsum('bhqd,bhkd->bhqk', q1b, k[0]) * scale)
        p2 = softmax32(jnp.einsum('bhqd,bhkd->bhqk', q2b, k[1]) * scale)
        return jnp.einsum('bhqk,bhkd->bhqd', (p1 - lam * p2).astype(v.dtype), v)

    return sweep_query_blocks(block, (q[0], q[1]))


def diff_output(o, subln_g, lam_init):
    return merge_heads(rms_norm(o, subln_g) * (1.0 - lam_init))


def mla_qkv(p, cq_g, ckv_g, w_uq, w_ukv, q_g, k_g, ang):
    cq = rms_norm(p[..., OFF_CQ:OFF_CQ + MLA_Q_RANK], cq_g)
    ckv = rms_norm(p[..., OFF_CKV:OFF_CKV + MLA_KV_RANK], ckv_g)
    q = split_heads(cq @ w_uq, MLA_HEADS)
    kv = split_heads(ckv @ w_ukv, MLA_HEADS)
    k_rope = p[..., OFF_KR:OFF_KR + MLA_ROPE][:, None]
    q_nope = rms_norm(q[..., :MLA_NOPE], q_g[:MLA_NOPE])
    q_rope = rms_norm(q[..., MLA_NOPE:], q_g[MLA_NOPE:])
    k_nope = rms_norm(kv[..., :MLA_NOPE], k_g[:MLA_NOPE])
    k_rope = rms_norm(k_rope, k_g[MLA_NOPE:])
    v = kv[..., MLA_NOPE:]
    if ang is not None:
        q_rope = rope(q_rope, ang)
        k_rope = rope(k_rope, ang)
    q = jnp.concatenate([q_nope, q_rope], axis=-1)
    k = jnp.concatenate([k_nope, jnp.broadcast_to(k_rope, k_nope.shape[:-1] + (MLA_ROPE,))], axis=-1)
    return q, k, v


def mla_attention(q, k, v):
    scale = MLA_QK ** -0.5

    def block(qb):
        p = softmax32(jnp.einsum('bhqd,bhkd->bhqk', qb[0], k) * scale)
        return jnp.einsum('bhqk,bhkd->bhqd', p.astype(v.dtype), v)

    return sweep_query_blocks(block, (q,))


def merge_branches(p, ys, w_branch, w_out):
    b, t, _ = p.shape
    gates = jax.nn.sigmoid(p[..., OFF_G:OFF_G + N_BRANCH * D_MODEL].reshape(b, t, N_BRANCH, D_MODEL))
    yb = jnp.einsum('btnc,ncd->btnd', jnp.stack(ys, axis=2), w_branch)
    return (gates * yb).sum(axis=2) @ w_out


def moe(h, router_w, router_bias, w1, w3, w2):
    shp = h.shape
    hf = h.reshape(-1, D_MODEL)
    scores = jax.nn.sigmoid((hf @ router_w).astype(F32))
    sel = scores + router_bias.astype(F32)
    group_score = lax.top_k(sel.reshape(-1, N_GROUPS, GROUP_SIZE), GROUP_SCORE_K)[0].sum(-1)
    best = jnp.argmax(group_score, axis=-1)
    in_group = (jnp.arange(N_EXPERTS) // GROUP_SIZE)[None, :] == best[:, None]
    _, idx = lax.top_k(jnp.where(in_group, sel, -jnp.inf), TOP_K)
    w_sel = jnp.take_along_axis(scores, idx, axis=-1)
    w_sel = w_sel / w_sel.sum(-1, keepdims=True)
    combine = (jax.nn.one_hot(idx, N_EXPERTS, dtype=F32) * w_sel[..., None]).sum(1)
    a = jnp.einsum('nd,edf->enf', hf, w1)
    g = jnp.einsum('nd,edf->enf', hf, w3)
    hidden = jax.nn.silu(a) * g * combine.T[:, :, None].astype(h.dtype)
    return jnp.einsum('enf,efd->nd', hidden, w2).reshape(shp)


def token_mixer(h_lat, h_ctx, lam, lam_init, w_in, b_in, m_conv_w, m_conv_b, m_norm_g,
                da_q_norm_g, da_k_norm_g, da_subln_g, mla_cq_norm_g, mla_ckv_norm_g, mla_w_uq,
                mla_w_ukv, mla_q_norm_g, mla_k_norm_g, w_branch, w_out, with_ctx_out):
    n_lat = h_lat.shape[1]
    bsz = h_ctx.shape[0]
    p_lat = h_lat @ w_in + b_in
    p_ctx = h_ctx @ w_in + b_in

    zero = (jnp.zeros((bsz, M_HEADS, M_DH, M_DH), F32), jnp.zeros((bsz, M_HEADS, M_DH), F32),
            jnp.zeros((bsz, M_HEADS), F32))
    mq, mk, mv, mo_ctx, mg = mlstm_inputs(p_ctx, m_conv_w, m_conv_b)
    hm_ctx, st_f, st_b = mlstm_bidirectional(mq, mk, mv, mg, zero, zero)
    mq, mk, mv, mo_lat, mg = mlstm_inputs(p_lat, m_conv_w, m_conv_b)
    hm_lat, _, _ = mlstm_bidirectional(mq, mk, mv, mg, st_f, st_b)

    dq_c, dk_c, dv_c = diff_qkv(p_ctx, da_q_norm_g, da_k_norm_g, None)
    dq_l, dk_l, dv_l = diff_qkv(p_lat, da_q_norm_g, da_k_norm_g, axial_angles(n_lat, DA_DH))
    od_lat = diff_attention(dq_l, jnp.concatenate([dk_l, dk_c], axis=3), jnp.concatenate([dv_l, dv_c], axis=2), lam)

    aq_c, ak_c, av_c = mla_qkv(p_ctx, mla_cq_norm_g, mla_ckv_norm_g, mla_w_uq, mla_w_ukv,
                               mla_q_norm_g, mla_k_norm_g, None)
    aq_l, ak_l, av_l = mla_qkv(p_lat, mla_cq_norm_g, mla_ckv_norm_g, mla_w_uq, mla_w_ukv,
                               mla_q_norm_g, mla_k_norm_g, axial_angles(n_lat, MLA_ROPE))
    oa_lat = mla_attention(aq_l, jnp.concatenate([ak_l, ak_c], axis=2), jnp.concatenate([av_l, av_c], axis=2))

    y_lat = merge_branches(p_lat, (mlstm_output(hm_lat, mo_lat, m_norm_g),
                                   diff_output(od_lat, da_subln_g, lam_init),
                                   merge_heads(oa_lat)), w_branch, w_out)
    if not with_ctx_out:
        return y_lat, None
    od_ctx = diff_attention(dq_c, dk_c, dv_c, lam)
    oa_ctx = mla_attention(aq_c, ak_c, av_c)
    y_ctx = merge_branches(p_ctx, (mlstm_output(hm_ctx, mo_ctx, m_norm_g),
                                   diff_output(od_ctx, da_subln_g, lam_init),
                                   merge_heads(oa_ctx)), w_branch, w_out)
    return y_lat, y_ctx


def setup_inputs(seed: int = 0) -> dict:
    key = jax.random.key(seed)
    ks = jax.random.split(key, 32)
    nrm = lambda k, shape, s: jax.random.normal(k, shape, F32) * s
    gain = lambda k, shape: 1.0 + 0.02 * jax.random.normal(k, shape, F32)
    f_bias = jnp.linspace(3.0, 6.0, M_HEADS)
    b_in = nrm(ks[7], (DEPTH, D_IN), 0.02)
    b_in = b_in.at[:, OFF_MG + M_HEADS:OFF_MG + 2 * M_HEADS].add(f_bias)
    b_in = b_in.at[:, OFF_MG + 3 * M_HEADS:OFF_MG + 4 * M_HEADS].add(f_bias)
    return {
        'x': nrm(ks[0], (BATCH, SEQ, D_MODEL), 1.0),
        'c': nrm(ks[1], (BATCH, D_MODEL), 1.0),
        'ctx': nrm(ks[2], (BATCH, CTX_LEN, D_MODEL), 1.0),
        'c_ctx': nrm(ks[3], (D_MODEL,), 1.0),
        'w_ada': nrm(ks[4], (DEPTH, D_MODEL, 6 * D_MODEL), 0.5 * D_MODEL ** -0.5),
        'b_ada': nrm(ks[5], (DEPTH, 6 * D_MODEL), 0.02),
        'w_in': nrm(ks[6], (DEPTH, D_MODEL, D_IN), D_MODEL ** -0.5),
        'b_in': b_in,
        'm_conv_w': nrm(ks[8], (DEPTH, M_CONV, 2 * BRANCH_W), M_CONV ** -0.5),
        'm_conv_b': nrm(ks[9], (DEPTH, 2 * BRANCH_W), 0.02),
        'm_norm_g': gain(ks[10], (DEPTH, BRANCH_W)),
        'da_q_norm_g': gain(ks[11], (DEPTH, DA_DH)),
        'da_k_norm_g': gain(ks[12], (DEPTH, DA_DH)),
        'da_lambda': nrm(ks[13], (DEPTH, 4, DA_DH), 0.1),
        'da_subln_g': gain(ks[14], (DEPTH, DA_DV)),
        'mla_cq_norm_g': gain(ks[15], (DEPTH, MLA_Q_RANK)),
        'mla_ckv_norm_g': gain(ks[16], (DEPTH, MLA_KV_RANK)),
        'mla_w_uq': nrm(ks[17], (DEPTH, MLA_Q_RANK, MLA_HEADS * MLA_QK), MLA_Q_RANK ** -0.5),
        'mla_w_ukv': nrm(ks[18], (DEPTH, MLA_KV_RANK, MLA_HEADS * (MLA_NOPE + MLA_DV)), MLA_KV_RANK ** -0.5),
        'mla_q_norm_g': gain(ks[19], (DEPTH, MLA_QK)),
        'mla_k_norm_g': gain(ks[20], (DEPTH, MLA_QK)),
        'w_branch': nrm(ks[21], (DEPTH, N_BRANCH, BRANCH_W, D_MODEL), BRANCH_W ** -0.5),
        'w_out': nrm(ks[22], (DEPTH, D_MODEL, D_MODEL), D_MODEL ** -0.5),
        'moe_w1': nrm(ks[23], (DEPTH, N_EXPERTS, D_MODEL, D_EXPERT), D_MODEL ** -0.5),
        'moe_w3': nrm(ks[24], (DEPTH, N_EXPERTS, D_MODEL, D_EXPERT), D_MODEL ** -0.5),
        'moe_w2': nrm(ks[25], (DEPTH, N_EXPERTS, D_EXPERT, D_MODEL), D_EXPERT ** -0.5),
        'router_w': nrm(ks[26], (D_MODEL, N_EXPERTS), D_MODEL ** -0.5),
        'router_bias': nrm(ks[27], (N_EXPERTS,), 0.01),
    }


def reference(x, c, ctx, c_ctx, w_ada, b_ada, w_in, b_in, m_conv_w, m_conv_b, m_norm_g,
              da_q_norm_g, da_k_norm_g, da_lambda, da_subln_g, mla_cq_norm_g, mla_ckv_norm_g,
              mla_w_uq, mla_w_ukv, mla_q_norm_g, mla_k_norm_g, w_branch, w_out,
              moe_w1, moe_w3, moe_w2, router_w, router_bias):
    x_lat, x_ctx = x, ctx
    for l in range(DEPTH):
        last = l == DEPTH - 1
        mod_lat = (jax.nn.silu(c) @ w_ada[l] + b_ada[l])[:, None, :]
        mod_ctx = jax.nn.silu(c_ctx) @ w_ada[l] + b_ada[l]
        sh1, sc1, g1, sh2, sc2, g2 = jnp.split(mod_lat, 6, axis=-1)
        csh1, csc1, cg1, csh2, csc2, cg2 = jnp.split(mod_ctx, 6, axis=-1)
        lam_init = 0.8 - 0.6 * math.exp(-0.3 * l)
        lam_p = da_lambda[l].astype(F32)
        lam = jnp.exp(jnp.sum(lam_p[0] * lam_p[1])) - jnp.exp(jnp.sum(lam_p[2] * lam_p[3])) + lam_init
        y_lat, y_ctx = token_mixer(
            modulate(x_lat, sh1, sc1), modulate(x_ctx, csh1, csc1), lam, lam_init,
            w_in[l], b_in[l], m_conv_w[l], m_conv_b[l], m_norm_g[l],
            da_q_norm_g[l], da_k_norm_g[l], da_subln_g[l], mla_cq_norm_g[l], mla_ckv_norm_g[l],
            mla_w_uq[l], mla_w_ukv[l], mla_q_norm_g[l], mla_k_norm_g[l], w_branch[l], w_out[l],
            not last)
        x_lat = x_lat + g1 * y_lat
        x_lat = x_lat + g2 * moe(modulate(x_lat, sh2, sc2), router_w, router_bias,
                                 moe_w1[l], moe_w3[l], moe_w2[l])
        if not last:
            x_ctx = x_ctx + cg1 * y_ctx
            x_ctx = x_ctx + cg2 * moe(modulate(x_ctx, csh2, csc2), router_w, router_bias,
                                      moe_w1[l], moe_w3[l], moe_w2[l])
    return x_lat
```

```python
import functools
import math

import jax
import jax.numpy as jnp
from jax import lax
from jax.experimental import pallas as pl
from jax.experimental.pallas import tpu as pltpu

F32 = jnp.float32
BF16 = jnp.bfloat16

GRID_W = 64
ROPE_BASE = 10000.0
EPS = 1e-6
N_BRANCH = 3
BRANCH_W = 1024
M_HEADS = 4
M_DH = BRANCH_W // M_HEADS
M_CHUNK = 64
M_CONV = 5
DA_HEADS = 4
DA_DV = BRANCH_W // DA_HEADS
DA_DH = DA_DV // 2
DA_QK_W = DA_HEADS * 2 * DA_DH
MLA_HEADS = 8
MLA_Q_RANK = 512
MLA_KV_RANK = 256
MLA_NOPE = 128
MLA_ROPE = 64
MLA_DV = BRANCH_W // MLA_HEADS
MLA_QK = MLA_NOPE + MLA_ROPE
N_EXPERTS = 16
N_GROUPS = 4
GROUP_SIZE = N_EXPERTS // N_GROUPS
D_EXPERT = 512

OFF_MQK = 0
OFF_MV = OFF_MQK + 2 * BRANCH_W
OFF_MO = OFF_MV + BRANCH_W
OFF_MG = OFF_MO + BRANCH_W
OFF_DQ = OFF_MG + 4 * M_HEADS
OFF_DK = OFF_DQ + DA_QK_W
OFF_DV = OFF_DK + DA_QK_W
OFF_CQ = OFF_DV + BRANCH_W
OFF_CKV = OFF_CQ + MLA_Q_RANK
OFF_KR = OFF_CKV + MLA_KV_RANK
OFF_G = OFF_KR + MLA_ROPE

LANES = 128
MOD_ROWS = 8
VMEM_LIMIT = 56 * 1024 * 1024
M_HPB = 2
MLA_HW = 256
PC_GATE_COL = MLA_Q_RANK + MLA_KV_RANK + LANES


def _cp(*sem):
    return pltpu.CompilerParams(dimension_semantics=sem, vmem_limit_bytes=VMEM_LIMIT)


def _rms(x):
    return x * lax.rsqrt(jnp.mean(x * x, axis=-1, keepdims=True) + EPS)


def _sigmoid(x):
    return 1.0 / (1.0 + jnp.exp(-x))


def _dot(a, b):
    return jnp.dot(a, b, preferred_element_type=F32)


def _dot_nt(a, b):
    return lax.dot_general(a, b, (((1,), (1,)), ((), ())), preferred_element_type=F32)


def _dot_tn(a, b):
    return lax.dot_general(a, b, (((0,), (0,)), ((), ())), preferred_element_type=F32)


def _mod_kernel(c_ref, w_ref, b_ref, o_ref):
    c = c_ref[...]
    s = (c * _sigmoid(c)).astype(BF16)
    o_ref[...] = _dot(s, w_ref[...].astype(BF16)) + b_ref[...]


def _mod_vectors(cvec, w_ada, b_ada):
    depth, d, n = w_ada.shape
    tn = 1024
    return pl.pallas_call(
        _mod_kernel,
        grid=(depth, n // tn),
        in_specs=[
            pl.BlockSpec((MOD_ROWS, d), lambda l, j: (0, 0)),
            pl.BlockSpec((None, d, tn), lambda l, j: (l, 0, j)),
            pl.BlockSpec((None, 1, tn), lambda l, j: (l, 0, j)),
        ],
        out_specs=pl.BlockSpec((None, MOD_ROWS, tn), lambda l, j: (l, 0, j)),
        out_shape=jax.ShapeDtypeStruct((depth, MOD_ROWS, n), F32),
        compiler_params=_cp("parallel", "parallel"),
        name="adaln_mod",
    )(cvec, w_ada, b_ada.reshape(depth, 1, n))


class _Geom:
    def __init__(self, b, t, lc, d, tm):
        assert t % tm == 0 and (b * lc) % tm == 0 and t % lc == 0 and t % GRID_W == 0
        assert lc % M_CHUNK == 0 and t % M_CHUNK == 0
        self.b, self.t, self.lc, self.d, self.tm = b, t, lc, d, tm
        self.rl, self.rc = b * t, b * lc
        self.r = self.rl + self.rc
        self.n_lat_tiles = self.rl // tm
        self.n_tiles = self.r // tm
        self.tiles_per_seq = t // tm

    def mod_row(self, i):
        return jnp.minimum(i // self.tiles_per_seq, self.b)

    def pos_block(self, i):
        return jnp.where(i < self.n_lat_tiles, i % self.tiles_per_seq, self.tiles_per_seq)


def _mod_spec(g, which, width=None, col_of=None):
    width = g.d if width is None else width
    per = g.d // width
    if col_of is None:
        return pl.BlockSpec((None, 1, width), lambda i, *_: (g.mod_row(i), 0, which * per))
    return pl.BlockSpec((None, 1, width), lambda i, j: (g.mod_row(i), 0, which * per + col_of(j)))


def _prenorm_kernel(x_ref, sh_ref, sc_ref, o_ref):
    o_ref[...] = (_rms(x_ref[...]) * (1.0 + sc_ref[...]) + sh_ref[...]).astype(o_ref.dtype)


def _prenorm(g, x_all, mods3, which_shift, which_scale, n_tiles):
    return pl.pallas_call(
        _prenorm_kernel,
        grid=(n_tiles,),
        in_specs=[
            pl.BlockSpec((g.tm, g.d), lambda i: (i, 0)),
            _mod_spec(g, which_shift),
            _mod_spec(g, which_scale),
        ],
        out_specs=pl.BlockSpec((g.tm, g.d), lambda i: (i, 0)),
        out_shape=jax.ShapeDtypeStruct((n_tiles * g.tm, g.d), BF16),
        compiler_params=_cp("parallel"),
        name="prenorm",
    )(x_all, mods3, mods3)


def _proj_raw_kernel(h_ref, w_ref, b_ref, o_ref):
    o_ref[...] = (_dot(h_ref[...], w_ref[...]) + b_ref[...]).astype(o_ref.dtype)


def _proj_sigmoid_kernel(h_ref, w_ref, b_ref, o_ref):
    o_ref[...] = _sigmoid(_dot(h_ref[...], w_ref[...]) + b_ref[...]).astype(o_ref.dtype)


def _proj_normrope_kernel(h_ref, w_ref, b_ref, gain_ref, cos_ref, sin_ref, o_ref):
    acc = _dot(h_ref[...], w_ref[...]) + b_ref[...]
    cos = cos_ref[...]
    sin = sin_ref[...]
    for j in range(acc.shape[1] // LANES):
        sl = slice(j * LANES, (j + 1) * LANES)
        x = _rms(acc[:, sl]) * gain_ref[:, sl]
        o_ref[:, sl] = (x * cos + pltpu.roll(x, LANES // 2, 1) * sin).astype(o_ref.dtype)


def _proj(g, kern, h, w, bias, out_dtype, tn, extra=(), extra_specs=(), name="proj"):
    k, n = w.shape
    assert n % tn == 0
    return pl.pallas_call(
        kern,
        grid=(g.n_tiles, n // tn),
        in_specs=[
            pl.BlockSpec((g.tm, k), lambda i, j: (i, 0)),
            pl.BlockSpec((k, tn), lambda i, j: (0, j)),
            pl.BlockSpec((1, tn), lambda i, j: (0, j)),
            *extra_specs,
        ],
        out_specs=pl.BlockSpec((g.tm, tn), lambda i, j: (i, j)),
        out_shape=jax.ShapeDtypeStruct((g.r, n), out_dtype),
        compiler_params=_cp("parallel", "arbitrary"),
        name=name,
    )(h, w, bias, *extra)


def _mlstm_kernel(ql_ref, kl_ref, vl_ref, ol_ref, gl_ref, qc_ref, kc_ref, vc_ref, oc_ref, gc_ref,
                  cwq_ref, cwk_ref, cbq_ref, cbk_ref, ng_ref, yl_ref, yc_ref,
                  qs, ks, vs, xs, bcs, hf, hb, ct, nv, *, lc, t):
    n = lc + t
    ncc = lc // M_CHUNK
    nch = n // M_CHUNK
    L = M_CHUNK

    def conv_silu(x_ref, w_ref, b_ref, scale):
        x = x_ref[...].astype(F32)
        rows = x.shape[0]
        row = lax.broadcasted_iota(jnp.int32, (rows, 1), 0)
        acc = x * w_ref[M_CONV // 2:M_CONV // 2 + 1, :] + b_ref[...]
        for j in range(M_CONV):
            s = M_CONV // 2 - j
            if s == 0:
                continue
            xs_ = pltpu.roll(x, s % rows, 0)
            ok = jnp.logical_and(row - s >= 0, row - s < rows)
            acc = acc + jnp.where(ok, xs_, 0.0) * w_ref[j:j + 1, :]
        return (acc * _sigmoid(acc) * scale).astype(BF16)

    qs[0:lc, :] = conv_silu(qc_ref, cwq_ref, cbq_ref, M_DH ** -0.5)
    qs[lc:n, :] = conv_silu(ql_ref, cwq_ref, cbq_ref, M_DH ** -0.5)
    ks[0:lc, :] = conv_silu(kc_ref, cwk_ref, cbk_ref, 1.0)
    ks[lc:n, :] = conv_silu(kl_ref, cwk_ref, cbk_ref, 1.0)
    vs[0:lc, :] = vc_ref[...]
    vs[lc:n, :] = vl_ref[...]

    lane = lax.broadcasted_iota(jnp.int32, (1, LANES), 1)
    is_f = jnp.logical_or(jnp.logical_and(lane >= M_HPB, lane < 2 * M_HPB),
                          jnp.logical_and(lane >= 3 * M_HPB, lane < 4 * M_HPB))

    def gate_prep(g_ref, lo, rows):
        gx = g_ref[...]
        lsg = jnp.minimum(gx, 0.0) - jnp.log(1.0 + jnp.exp(-jnp.abs(gx)))
        x = jnp.where(is_f, lsg, gx)
        xs[lo:lo + rows, :] = x
        pos = lax.broadcasted_iota(jnp.int32, (rows, 1), 0) % L
        pre = x
        suf = x
        k = 1
        while k < L:
            pre = pre + jnp.where(pos >= k, pltpu.roll(pre, k, 0), 0.0)
            suf = suf + jnp.where(pos < L - k, pltpu.roll(suf, rows - k, 0), 0.0)
            k *= 2
        bcs[lo:lo + rows, :] = jnp.where(lane >= 2 * M_HPB, suf, pre)

    gate_prep(gc_ref, 0, lc)
    gate_prep(gl_ref, lc, t)

    ct[...] = jnp.zeros_like(ct)
    nv[...] = jnp.zeros_like(nv)

    r_i = lax.broadcasted_iota(jnp.int32, (L, L), 0)
    c_i = lax.broadcasted_iota(jnp.int32, (L, L), 1)
    eye = r_i == c_i
    masks = (c_i <= r_i, c_i >= r_i)

    def chain_step(c, j, d, m):
        idx = d * M_HPB + j
        r0 = pl.multiple_of(c * L, L)
        hs = slice(j * M_DH, (j + 1) * M_DH)
        q = qs[pl.ds(r0, L), hs]
        k = ks[pl.ds(r0, L), hs]
        v = vs[pl.ds(r0, L), hs]
        xg = xs[pl.ds(r0, L), :]
        bg = bcs[pl.ds(r0, L), :]
        li = 2 * d * M_HPB + j
        lf = (2 * d + 1) * M_HPB + j
        ic = xg[:, li:li + 1]
        a = bg[:, lf:lf + 1]
        b_last = a[L - 1:L, :] if d == 0 else a[0:1, :]
        d_col = ic - a
        d_row = jnp.sum(jnp.where(eye, d_col, 0.0), axis=0, keepdims=True)
        logw = jnp.where(masks[d], a + d_row, -jnp.inf)
        inter = a + m
        m_t = jnp.maximum(inter, jnp.max(logw, axis=1, keepdims=True))
        w = jnp.exp(logw - m_t) * _dot_nt(q, k)
        decay = jnp.exp(inter - m_t)
        ctb = ct[idx].astype(BF16)
        num = _dot(w.astype(BF16), v) + decay * _dot(q, ctb)
        nrow = nv[idx]
        den = jnp.sum(w, axis=1, keepdims=True) + decay * jnp.sum(q.astype(F32) * nrow, axis=1, keepdims=True)
        h = num / jnp.maximum(jnp.abs(den), jnp.exp(-m_t))
        if d == 0:
            hf[pl.ds(r0, L), hs] = h
        else:
            hb[pl.ds(r0, L), hs] = h
        src = b_last - a + ic
        m_new = jnp.maximum(b_last + m, jnp.max(src, axis=0, keepdims=True))
        gsc = jnp.exp(src - m_new)
        keep = jnp.exp(b_last + m - m_new)
        gv = (gsc * v.astype(F32)).astype(BF16)
        ct[idx] = keep * ct[idx] + _dot_tn(k, gv)
        nv[idx] = keep * nrow + jnp.sum(gsc * k.astype(F32), axis=0, keepdims=True)
        return m_new

    def body(i, ms):
        cf = i
        cb = jnp.where(i < ncc, ncc - 1 - i, nch - 1 + ncc - i)
        out = []
        for d in range(2):
            for j in range(M_HPB):
                out.append(chain_step(cf if d == 0 else cb, j, d, ms[d * M_HPB + j]))
        return tuple(out)

    lax.fori_loop(0, nch, body, tuple(jnp.zeros((1, 1), F32) for _ in range(2 * M_HPB)))

    for j in range(M_HPB):
        hs = slice(j * M_DH, (j + 1) * M_DH)
        hn = _rms(hf[:, hs] + hb[:, hs]) * ng_ref[:, hs]
        yc_ref[:, hs] = (hn[0:lc] * _sigmoid(oc_ref[:, hs].astype(F32))).astype(BF16)
        yl_ref[:, hs] = (hn[lc:n] * _sigmoid(ol_ref[:, hs].astype(F32))).astype(BF16)


def _mlstm(g, pm, pc, conv_w, conv_b, norm_g):
    w = M_HPB * M_DH
    nhb = M_HEADS // M_HPB
    cb = BRANCH_W // w
    ctx0 = g.rl // g.lc
    gcol = PC_GATE_COL // LANES
    n = g.lc + g.t

    def lat(seg):
        return pl.BlockSpec((g.t, w), lambda b, hb: (b, seg * cb + hb))

    def ctx(seg):
        return pl.BlockSpec((g.lc, w), lambda b, hb: (ctx0 + b, seg * cb + hb))

    return pl.pallas_call(
        functools.partial(_mlstm_kernel, lc=g.lc, t=g.t),
        grid=(g.b, nhb),
        in_specs=[
            lat(0), lat(1), lat(2), lat(3),
            pl.BlockSpec((g.t, LANES), lambda b, hb: (b, gcol + hb)),
            ctx(0), ctx(1), ctx(2), ctx(3),
            pl.BlockSpec((g.lc, LANES), lambda b, hb: (ctx0 + b, gcol + hb)),
            pl.BlockSpec((M_CONV, w), lambda b, hb: (0, hb)),
            pl.BlockSpec((M_CONV, w), lambda b, hb: (0, cb + hb)),
            pl.BlockSpec((1, w), lambda b, hb: (0, hb)),
            pl.BlockSpec((1, w), lambda b, hb: (0, cb + hb)),
            pl.BlockSpec((1, w), lambda b, hb: (0, hb)),
        ],
        out_specs=[
            pl.BlockSpec((g.t, w), lambda b, hb: (b, hb)),
            pl.BlockSpec((g.lc, w), lambda b, hb: (b, hb)),
        ],
        out_shape=[
            jax.ShapeDtypeStruct((g.rl, BRANCH_W), BF16),
            jax.ShapeDtypeStruct((g.rc, BRANCH_W), BF16),
        ],
        scratch_shapes=[
            pltpu.VMEM((n, w), BF16), pltpu.VMEM((n, w), BF16), pltpu.VMEM((n, w), BF16),
            pltpu.VMEM((n, LANES), F32), pltpu.VMEM((n, LANES), F32),
            pltpu.VMEM((n, w), F32), pltpu.VMEM((n, w), F32),
            pltpu.VMEM((2 * M_HPB, M_DH, M_DH), F32), pltpu.VMEM((2 * M_HPB, 1, M_DH), F32),
        ],
        compiler_params=_cp("parallel", "parallel"),
        name="mlstm",
    )(pm, pm, pm, pm, pc, pm, pm, pm, pm, pc, conv_w, conv_w, conv_b, conv_b, norm_g)


def _mla_prep_kernel(cq_ref, ckv_ref, kr_ref, wuq_ref, wukv_ref, cqg_ref, ckvg_ref, qng_ref, qrg_ref,
                     kng_ref, krg_ref, cos_ref, sin_ref, aq_ref, ak_ref, av_ref):
    half = LANES // 2
    lane = lax.broadcasted_iota(jnp.int32, (1, LANES), 1)
    lo = lane < half
    first = (lane % half) < (half // 2)
    cos = cos_ref[...]
    sin = sin_ref[...]

    def rms_half(x):
        x2 = x * x
        s_lo = jnp.sum(jnp.where(lo, x2, 0.0), axis=-1, keepdims=True)
        s_hi = jnp.sum(jnp.where(lo, 0.0, x2), axis=-1, keepdims=True)
        ms = jnp.where(lo, s_lo, s_hi) * (1.0 / half)
        return x * lax.rsqrt(ms + EPS)

    def rope_half(x):
        partner = jnp.where(first, pltpu.roll(x, LANES - half // 2, 1), pltpu.roll(x, half // 2, 1))
        return x * cos + partner * sin

    cq = (_rms(cq_ref[...]) * cqg_ref[...]).astype(BF16)
    q = _dot(cq, wuq_ref[...])
    ckv = (_rms(ckv_ref[...]) * ckvg_ref[...]).astype(BF16)
    kv = _dot(ckv, wukv_ref[...])

    krn = rope_half(rms_half(kr_ref[...]) * krg_ref[...]).astype(BF16)

    rope0 = MLA_HEADS * MLA_NOPE
    for hp in range(MLA_HEADS // 2):
        slab = q[:, rope0 + hp * LANES: rope0 + (hp + 1) * LANES]
        r = rope_half(rms_half(slab) * qrg_ref[...])
        parts = (jnp.where(lo, r, 0.0), jnp.where(lo, pltpu.roll(r, half, 1), 0.0))
        for e in range(2):
            h = 2 * hp + e
            nope = _rms(q[:, h * MLA_NOPE:(h + 1) * MLA_NOPE]) * qng_ref[...]
            aq_ref[:, h * MLA_HW: h * MLA_HW + MLA_NOPE] = nope.astype(BF16)
            aq_ref[:, h * MLA_HW + MLA_NOPE:(h + 1) * MLA_HW] = parts[e].astype(BF16)
    kvw = MLA_NOPE + MLA_DV
    for h in range(MLA_HEADS):
        kn = _rms(kv[:, h * kvw: h * kvw + MLA_NOPE]) * kng_ref[...]
        ak_ref[:, h * MLA_HW: h * MLA_HW + MLA_NOPE] = kn.astype(BF16)
        ak_ref[:, h * MLA_HW + MLA_NOPE:(h + 1) * MLA_HW] = krn
        av_ref[:, h * MLA_DV:(h + 1) * MLA_DV] = kv[:, h * kvw + MLA_NOPE:(h + 1) * kvw].astype(BF16)


def _mla_prep(g, pc, wuq, wukv, gains, cos_t, sin_t):
    tm = g.tm
    full = lambda shape: pl.BlockSpec(shape, lambda i: (0, 0))
    kr_col = (MLA_Q_RANK + MLA_KV_RANK) // LANES
    return pl.pallas_call(
        _mla_prep_kernel,
        grid=(g.n_tiles,),
        in_specs=[
            pl.BlockSpec((tm, MLA_Q_RANK), lambda i: (i, 0)),
            pl.BlockSpec((tm, MLA_KV_RANK), lambda i: (i, MLA_Q_RANK // MLA_KV_RANK)),
            pl.BlockSpec((tm, LANES), lambda i: (i, kr_col)),
            full(wuq.shape), full(wukv.shape),
            full((1, MLA_Q_RANK)), full((1, MLA_KV_RANK)),
            full((1, LANES)), full((1, LANES)), full((1, LANES)), full((1, LANES)),
            pl.BlockSpec((tm, LANES), lambda i: (g.pos_block(i), 0)),
            pl.BlockSpec((tm, LANES), lambda i: (g.pos_block(i), 0)),
        ],
        out_specs=[
            pl.BlockSpec((tm, MLA_HEADS * MLA_HW), lambda i: (i, 0)),
            pl.BlockSpec((tm, MLA_HEADS * MLA_HW), lambda i: (i, 0)),
            pl.BlockSpec((tm, BRANCH_W), lambda i: (i, 0)),
        ],
        out_shape=[
            jax.ShapeDtypeStruct((g.r, MLA_HEADS * MLA_HW), BF16),
            jax.ShapeDtypeStruct((g.r, MLA_HEADS * MLA_HW), BF16),
            jax.ShapeDtypeStruct((g.r, BRANCH_W), BF16),
        ],
        compiler_params=_cp("parallel"),
        name="mla_prep",
    )(pc, pc, pc, wuq, wukv, *gains, cos_t, sin_t)


def _attn_kernel(*refs, n_soft, dh, has_lat, diff, lam_init):
    refs = list(refs)
    o_ref = refs.pop()
    if diff:
        lam_ref, sg_ref = refs[0], refs[1]
        refs = refs[2:]
    q_ref, kc_ref, vc_ref = refs[:3]
    segs = [(kc_ref, vc_ref)]
    if has_lat:
        segs.append((refs[3], refs[4]))
    q = q_ref[...]
    outs = []
    for s in range(n_soft):
        sl = slice(s * dh, (s + 1) * dh)
        sc = [_dot_nt(q[:, sl], k_ref[:, sl]) for k_ref, _ in segs]
        m = functools.reduce(jnp.maximum, [jnp.max(x, axis=-1, keepdims=True) for x in sc])
        ps = [jnp.exp(x - m) for x in sc]
        l = functools.reduce(lambda a, b: a + b, [jnp.sum(p, axis=-1, keepdims=True) for p in ps])
        o = functools.reduce(lambda a, b: a + b,
                             [_dot(p.astype(BF16), v_ref[...]) for p, (_, v_ref) in zip(ps, segs)])
        outs.append(o / l)
    if diff:
        lp = lam_ref[...]
        lam = (jnp.exp(jnp.sum(lp[0:1] * lp[1:2], axis=-1, keepdims=True))
               - jnp.exp(jnp.sum(lp[2:3] * lp[3:4], axis=-1, keepdims=True)) + lam_init)
        o = outs[0] - lam * outs[1]
        o = _rms(o) * sg_ref[...] * (1.0 - lam_init)
    else:
        o = outs[0]
    o_ref[...] = o.astype(o_ref.dtype)


def _attention(g, q_arr, k_arr, v_arr, *, q_col0, k_col0, heads, qk_w, dv, n_soft, ctx_queries,
               lam=None, subln_g=None, lam_init=0.0):
    diff = lam is not None
    ctx0 = g.rl // g.lc
    if ctx_queries:
        tq = g.lc
        nq = 1
        q_row = lambda b, qi: ctx0 + b
        out_rows = g.rc
        o_row = lambda b, qi: b
    else:
        tq = min(512, g.t)
        nq = g.t // tq
        q_row = lambda b, qi: b * nq + qi
        out_rows = g.rl
        o_row = q_row
    in_specs = []
    args = []
    if diff:
        in_specs += [pl.BlockSpec(lam.shape, lambda b, h, qi: (0, 0)),
                     pl.BlockSpec((1, dv), lambda b, h, qi: (0, 0))]
        args += [lam, subln_g]
    in_specs += [
        pl.BlockSpec((tq, qk_w), lambda b, h, qi: (q_row(b, qi), q_col0 + h)),
        pl.BlockSpec((g.lc, qk_w), lambda b, h, qi: (ctx0 + b, k_col0 + h)),
        pl.BlockSpec((g.lc, dv), lambda b, h, qi: (ctx0 + b, h)),
    ]
    args += [q_arr, k_arr, v_arr]
    if not ctx_queries:
        in_specs += [
            pl.BlockSpec((g.t, qk_w), lambda b, h, qi: (b, k_col0 + h)),
            pl.BlockSpec((g.t, dv), lambda b, h, qi: (b, h)),
        ]
        args += [k_arr, v_arr]
    return pl.pallas_call(
        functools.partial(_attn_kernel, n_soft=n_soft, dh=qk_w // n_soft, has_lat=not ctx_queries,
                          diff=diff, lam_init=lam_init),
        grid=(g.b, heads, nq),
        in_specs=in_specs,
        out_specs=pl.BlockSpec((tq, dv), lambda b, h, qi: (o_row(b, qi), h)),
        out_shape=jax.ShapeDtypeStruct((out_rows, heads * dv), BF16),
        compiler_params=_cp("parallel", "parallel", "arbitrary"),
        name="attn_diff" if diff else "attn_mla",
    )(*args)


def _branch_kernel(ym_ref, yd_ref, ya_ref, wb_ref, g0_ref, g1_ref, g2_ref, o_ref):
    acc = g0_ref[...].astype(F32) * _dot(ym_ref[...], wb_ref[0])
    acc = acc + g1_ref[...].astype(F32) * _dot(yd_ref[...], wb_ref[1])
    acc = acc + g2_ref[...].astype(F32) * _dot(ya_ref[...], wb_ref[2])
    o_ref[...] = acc.astype(o_ref.dtype)


def _branch_merge(g, ym, yd, ya, wb, gt, n_tiles):
    tn = 512
    nj = g.d // tn
    y_spec = pl.BlockSpec((g.tm, BRANCH_W), lambda i, j: (i, 0))
    return pl.pallas_call(
        _branch_kernel,
        grid=(n_tiles, nj),
        in_specs=[
            y_spec, y_spec, y_spec,
            pl.BlockSpec((N_BRANCH, BRANCH_W, tn), lambda i, j: (0, 0, j)),
            pl.BlockSpec((g.tm, tn), lambda i, j: (i, j)),
            pl.BlockSpec((g.tm, tn), lambda i, j: (i, nj + j)),
            pl.BlockSpec((g.tm, tn), lambda i, j: (i, 2 * nj + j)),
        ],
        out_specs=pl.BlockSpec((g.tm, tn), lambda i, j: (i, j)),
        out_shape=jax.ShapeDtypeStruct((n_tiles * g.tm, g.d), BF16),
        compiler_params=_cp("parallel", "arbitrary"),
        name="branch_merge",
    )(ym, yd, ya, wb, gt, gt, gt)


def _outproj_kernel(z_ref, w_ref, x_ref, g_ref, o_ref):
    o_ref[...] = x_ref[...] + g_ref[...] * _dot(z_ref[...], w_ref[...])


def _outproj(g, z, w_out, x_all, mods3, which_gate, n_tiles):
    tn = 512
    return pl.pallas_call(
        _outproj_kernel,
        grid=(n_tiles, g.d // tn),
        in_specs=[
            pl.BlockSpec((g.tm, g.d), lambda i, j: (i, 0)),
            pl.BlockSpec((g.d, tn), lambda i, j: (0, j)),
            pl.BlockSpec((g.tm, tn), lambda i, j: (i, j)),
            _mod_spec(g, which_gate, tn, lambda j: j),
        ],
        out_specs=pl.BlockSpec((g.tm, tn), lambda i, j: (i, j)),
        out_shape=jax.ShapeDtypeStruct((n_tiles * g.tm, g.d), F32),
        compiler_params=_cp("parallel", "arbitrary"),
        name="outproj",
    )(z, w_out, x_all, mods3)


def _moe_prep_kernel(x_ref, sh_ref, sc_ref, rwt_ref, rb_ref, h_ref, comb_ref):
    h = _rms(x_ref[...]) * (1.0 + sc_ref[...]) + sh_ref[...]
    h_ref[...] = h.astype(h_ref.dtype)
    logits = lax.dot_general(rwt_ref[...], h, (((1,), (1,)), ((), ())),
                             precision=lax.Precision.HIGHEST, preferred_element_type=F32)
    scores = _sigmoid(logits)
    selm = scores + rb_ref[...]
    sc_rows = [scores[e:e + 1, :] for e in range(N_EXPERTS)]
    sel = [selm[e:e + 1, :] for e in range(N_EXPERTS)]
    gscore = []
    for gi in range(N_GROUPS):
        v = sel[gi * GROUP_SIZE:(gi + 1) * GROUP_SIZE]
        pair = [v[a] + v[b] for a in range(GROUP_SIZE) for b in range(a + 1, GROUP_SIZE)]
        gscore.append(functools.reduce(jnp.maximum, pair))
    gmax = functools.reduce(jnp.maximum, gscore)
    taken = jnp.zeros_like(gmax, dtype=jnp.bool_)
    in_group = []
    for gi in range(N_GROUPS):
        hit = jnp.logical_and(gscore[gi] == gmax, jnp.logical_not(taken))
        taken = jnp.logical_or(taken, hit)
        in_group.append(hit)
    neg = -jnp.inf
    masked = [jnp.where(in_group[e // GROUP_SIZE], sel[e], neg) for e in range(N_EXPERTS)]

    def pick(vals):
        mx = functools.reduce(jnp.maximum, vals)
        seen = jnp.zeros_like(mx, dtype=jnp.bool_)
        hot = []
        for v in vals:
            hit = jnp.logical_and(v == mx, jnp.logical_not(seen))
            seen = jnp.logical_or(seen, hit)
            hot.append(hit)
        return hot

    hot1 = pick(masked)
    hot2 = pick([jnp.where(hot1[e], neg, masked[e]) for e in range(N_EXPERTS)])
    zero = jnp.zeros_like(gmax)
    s1 = functools.reduce(lambda a, b: a + b, [jnp.where(hot1[e], sc_rows[e], zero) for e in range(N_EXPERTS)])
    s2 = functools.reduce(lambda a, b: a + b, [jnp.where(hot2[e], sc_rows[e], zero) for e in range(N_EXPERTS)])
    tot = s1 + s2
    w1 = s1 / tot
    w2 = s2 / tot
    for e in range(N_EXPERTS):
        comb_ref[e:e + 1, :] = jnp.where(hot1[e], w1, zero) + jnp.where(hot2[e], w2, zero)


def _moe_prep(g, x_new, mods3, which_shift, which_scale, rwt, rb, n_tiles):
    return pl.pallas_call(
        _moe_prep_kernel,
        grid=(n_tiles,),
        in_specs=[
            pl.BlockSpec((g.tm, g.d), lambda i: (i, 0)),
            _mod_spec(g, which_shift),
            _mod_spec(g, which_scale),
            pl.BlockSpec((N_EXPERTS, g.d), lambda i: (0, 0)),
            pl.BlockSpec((N_EXPERTS, 1), lambda i: (0, 0)),
        ],
        out_specs=[
            pl.BlockSpec((g.tm, g.d), lambda i: (i, 0)),
            pl.BlockSpec((N_EXPERTS, g.tm), lambda i: (0, i)),
        ],
        out_shape=[
            jax.ShapeDtypeStruct((n_tiles * g.tm, g.d), BF16),
            jax.ShapeDtypeStruct((N_EXPERTS, n_tiles * g.tm), F32),
        ],
        compiler_params=_cp("parallel"),
        name="moe_prep",
    )(x_new, mods3, mods3, rwt, rb)


def _moe_dense_kernel(h_ref, w1_ref, w3_ref, w2_ref, comb_ref, x_ref, g_ref, o_ref, acc_ref):
    e = pl.program_id(1)

    @pl.when(e == 0)
    def _():
        acc_ref[...] = jnp.zeros_like(acc_ref)

    h = h_ref[...]
    a = _dot(h, w1_ref[...])
    gg = _dot(h, w3_ref[...])
    lane = lax.broadcasted_iota(jnp.int32, (1, N_EXPERTS), 1)
    cw = jnp.sum(jnp.where(lane == e, comb_ref[...], 0.0), axis=1, keepdims=True)
    hid = (a * _sigmoid(a) * gg * cw).astype(BF16)
    acc_ref[...] += _dot(hid, w2_ref[...])

    @pl.when(e == pl.num_programs(1) - 1)
    def _():
        o_ref[...] = x_ref[...] + g_ref[...] * acc_ref[...]


def _moe_dense(g, h2, w1, w3, w2, comb, x_new, mods3, which_gate, n_rows):
    tm = min(512, g.tm)
    sub = g.tm // tm
    return pl.pallas_call(
        _moe_dense_kernel,
        grid=(n_rows // tm, N_EXPERTS),
        in_specs=[
            pl.BlockSpec((tm, g.d), lambda i, e: (i, 0)),
            pl.BlockSpec((None, g.d, D_EXPERT), lambda i, e: (e, 0, 0)),
            pl.BlockSpec((None, g.d, D_EXPERT), lambda i, e: (e, 0, 0)),
            pl.BlockSpec((None, D_EXPERT, g.d), lambda i, e: (e, 0, 0)),
            pl.BlockSpec((tm, N_EXPERTS), lambda i, e: (i, 0)),
            pl.BlockSpec((tm, g.d), lambda i, e: (i, 0)),
            pl.BlockSpec((None, 1, g.d), lambda i, e: (g.mod_row(i // sub), 0, which_gate)),
        ],
        out_specs=pl.BlockSpec((tm, g.d), lambda i, e: (i, 0)),
        out_shape=jax.ShapeDtypeStruct((n_rows, g.d), F32),
        scratch_shapes=[pltpu.VMEM((tm, g.d), F32)],
        compiler_params=_cp("parallel", "arbitrary"),
        name="moe_dense",
    )(h2, w1, w3, w2, comb, x_new, mods3)


def _rope_tables(t, rot_dim, tile_rows, reps):
    rows = t // GRID_W
    r = jnp.repeat(jnp.arange(rows, dtype=F32), GRID_W)
    col = jnp.tile(jnp.arange(GRID_W, dtype=F32), rows)
    n_freq = rot_dim // 4
    inv = ROPE_BASE ** (-jnp.arange(n_freq, dtype=F32) / n_freq)
    ang = jnp.concatenate([r[:, None] * inv, col[:, None] * inv], axis=-1)
    cos = jnp.tile(jnp.concatenate([jnp.cos(ang), jnp.cos(ang)], axis=-1), (1, reps))
    sin = jnp.tile(jnp.concatenate([-jnp.sin(ang), jnp.sin(ang)], axis=-1), (1, reps))
    cos = jnp.concatenate([cos, jnp.ones((tile_rows, LANES), F32)], axis=0)
    sin = jnp.concatenate([sin, jnp.zeros((tile_rows, LANES), F32)], axis=0)
    return cos, sin


def _gate_cols():
    blocks = []
    for hb in range(M_HEADS // M_HPB):
        blocks.append([OFF_MG + gi * M_HEADS + hb * M_HPB + j for gi in range(4) for j in range(M_HPB)])
    return blocks


def _small_proj(w, b):
    d = w.shape[0]
    parts_w = [w[:, OFF_CQ:OFF_G], jnp.zeros((d, LANES - MLA_ROPE), w.dtype)]
    parts_b = [b[OFF_CQ:OFF_G], jnp.zeros((LANES - MLA_ROPE,), b.dtype)]
    for cols in _gate_cols():
        idx = jnp.asarray(cols)
        parts_w += [w[:, idx], jnp.zeros((d, LANES - len(cols)), w.dtype)]
        parts_b += [b[idx], jnp.zeros((LANES - len(cols),), b.dtype)]
    return jnp.concatenate(parts_w, axis=1), jnp.concatenate(parts_b)


def _uq_perm():
    nope = [h * MLA_QK + i for h in range(MLA_HEADS) for i in range(MLA_NOPE)]
    rope = [h * MLA_QK + MLA_NOPE + i for h in range(MLA_HEADS) for i in range(MLA_ROPE)]
    return jnp.asarray(nope + rope)


def kernel(x, c, ctx, c_ctx, w_ada, b_ada, w_in, b_in, m_conv_w, m_conv_b, m_norm_g, da_q_norm_g, da_k_norm_g,
           da_lambda, da_subln_g, mla_cq_norm_g, mla_ckv_norm_g, mla_w_uq, mla_w_ukv, mla_q_norm_g, mla_k_norm_g,
           w_branch, w_out, moe_w1, moe_w3, moe_w2, router_w, router_bias):
    bsz, t, d = x.shape
    lc = ctx.shape[1]
    depth = w_ada.shape[0]
    g = _Geom(bsz, t, lc, d, min(1024, bsz * lc, t))
    assert bsz + 1 <= MOD_ROWS

    cvec = jnp.concatenate([c, c_ctx[None, :], jnp.zeros((MOD_ROWS - bsz - 1, d), F32)], axis=0)
    mods = _mod_vectors(cvec, w_ada, b_ada)

    cos_d, sin_d = _rope_tables(t, DA_DH, g.tm, 1)
    cos_a, sin_a = _rope_tables(t, MLA_ROPE, g.tm, 2)
    rwt = router_w.T
    rb = router_bias.reshape(N_EXPERTS, 1)
    uq_perm = _uq_perm()
    x_all = jnp.concatenate([x.reshape(g.rl, d), ctx.reshape(g.rc, d)], axis=0)
    pos_spec = pl.BlockSpec((g.tm, LANES), lambda i, j: (g.pos_block(i), 0))

    for l in range(depth):
        last = l == depth - 1
        n_tiles = g.n_lat_tiles if last else g.n_tiles
        n_rows = n_tiles * g.tm
        lam_init = 0.8 - 0.6 * math.exp(-0.3 * l)
        mods3 = mods[l].reshape(MOD_ROWS, 1, 6 * d)
        wl, bl = w_in[l], b_in[l]

        h1 = _prenorm(g, x_all, mods3, 0, 1, g.n_tiles)

        row = lambda v: v.reshape(1, -1)
        pm = _proj(g, _proj_raw_kernel, h1, wl[:, OFF_MQK:OFF_MG].astype(BF16), row(bl[OFF_MQK:OFF_MG]),
                   BF16, 512, name="proj_mlstm")
        dv = _proj(g, _proj_raw_kernel, h1, wl[:, OFF_DV:OFF_CQ].astype(BF16), row(bl[OFF_DV:OFF_CQ]),
                   BF16, 512, name="proj_dv")
        da_gain = jnp.concatenate([jnp.tile(da_q_norm_g[l] * DA_DH ** -0.5, 2 * DA_HEADS),
                                   jnp.tile(da_k_norm_g[l], 2 * DA_HEADS)])
        dqk = _proj(g, _proj_normrope_kernel, h1, wl[:, OFF_DQ:OFF_DV].astype(BF16), row(bl[OFF_DQ:OFF_DV]),
                    BF16, 512, extra=(row(da_gain), cos_d, sin_d),
                    extra_specs=(pl.BlockSpec((1, 512), lambda i, j: (0, j)), pos_spec, pos_spec),
                    name="proj_dqk")
        w_pc, b_pc = _small_proj(wl, bl)
        pc = _proj(g, _proj_raw_kernel, h1, w_pc.astype(BF16), row(b_pc), F32, LANES, name="proj_small")
        gt = _proj(g, _proj_sigmoid_kernel, h1, wl[:, OFF_G:].astype(BF16), row(bl[OFF_G:]), BF16, 512,
                   name="proj_gates")

        ym_l, ym_c = _mlstm(g, pm, pc, m_conv_w[l], row(m_conv_b[l]), row(m_norm_g[l]))

        qg, kg = mla_q_norm_g[l], mla_k_norm_g[l]
        a_scale = MLA_QK ** -0.5
        gains = (row(mla_cq_norm_g[l]), row(mla_ckv_norm_g[l]),
                 row(qg[:MLA_NOPE] * a_scale), row(jnp.tile(qg[MLA_NOPE:], 2) * a_scale),
                 row(kg[:MLA_NOPE]), row(jnp.concatenate([kg[MLA_NOPE:], jnp.zeros((LANES - MLA_ROPE,), F32)])))
        aq, ak, av = _mla_prep(g, pc, mla_w_uq[l][:, uq_perm].astype(BF16), mla_w_ukv[l].astype(BF16),
                               gains, cos_a, sin_a)

        lam_p = da_lambda[l]
        sub_g = row(da_subln_g[l])
        da_kw = dict(q_col0=0, k_col0=DA_HEADS, heads=DA_HEADS, qk_w=2 * DA_DH, dv=DA_DV, n_soft=2,
                     lam=lam_p, subln_g=sub_g, lam_init=lam_init)
        mla_kw = dict(q_col0=0, k_col0=0, heads=MLA_HEADS, qk_w=MLA_HW, dv=MLA_DV, n_soft=1)
        yd_l = _attention(g, dqk, dqk, dv, ctx_queries=False, **da_kw)
        ya_l = _attention(g, aq, ak, av, ctx_queries=False, **mla_kw)
        if last:
            ym, yd, ya = ym_l, yd_l, ya_l
        else:
            yd_c = _attention(g, dqk, dqk, dv, ctx_queries=True, **da_kw)
            ya_c = _attention(g, aq, ak, av, ctx_queries=True, **mla_kw)
            ym = jnp.concatenate([ym_l, ym_c], axis=0)
            yd = jnp.concatenate([yd_l, yd_c], axis=0)
            ya = jnp.concatenate([ya_l, ya_c], axis=0)

        z = _branch_merge(g, ym, yd, ya, w_branch[l].astype(BF16), gt, n_tiles)
        x_mid = _outproj(g, z, w_out[l].astype(BF16), x_all, mods3, 2, n_tiles)

        h2, comb_t = _moe_prep(g, x_mid, mods3, 3, 4, rwt, rb, n_tiles)
        x_all = _moe_dense(g, h2, moe_w1[l].astype(BF16), moe_w3[l].astype(BF16), moe_w2[l].astype(BF16),
                           comb_t.T, x_mid, mods3, 5, n_rows)

    return x_all[:g.rl].reshape(bsz, t, d)
```

```python
import functools
import math

import jax
import jax.numpy as jnp
from jax import lax
from jax.experimental import pallas as pl
from jax.experimental.pallas import tpu as pltpu

F32 = jnp.float32
BF16 = jnp.bfloat16

GRID_W = 64
ROPE_BASE = 10000.0
EPS = 1e-6
N_BRANCH = 3
BRANCH_W = 1024
M_HEADS = 4
M_DH = BRANCH_W // M_HEADS
M_CHUNK = 64
M_CONV = 5
DA_HEADS = 4
DA_DV = BRANCH_W // DA_HEADS
DA_DH = DA_DV // 2
DA_QK_W = DA_HEADS * 2 * DA_DH
MLA_HEADS = 8
MLA_Q_RANK = 512
MLA_KV_RANK = 256
MLA_NOPE = 128
MLA_ROPE = 64
MLA_DV = BRANCH_W // MLA_HEADS
MLA_QK = MLA_NOPE + MLA_ROPE
N_EXPERTS = 16
N_GROUPS = 4
GROUP_SIZE = N_EXPERTS // N_GROUPS
D_EXPERT = 512

OFF_MQK = 0
OFF_MV = OFF_MQK + 2 * BRANCH_W
OFF_MO = OFF_MV + BRANCH_W
OFF_MG = OFF_MO + BRANCH_W
OFF_DQ = OFF_MG + 4 * M_HEADS
OFF_DK = OFF_DQ + DA_QK_W
OFF_DV = OFF_DK + DA_QK_W
OFF_CQ = OFF_DV + BRANCH_W
OFF_CKV = OFF_CQ + MLA_Q_RANK
OFF_KR = OFF_CKV + MLA_KV_RANK
OFF_G = OFF_KR + MLA_ROPE

LANES = 128
MOD_ROWS = 8
VMEM_LIMIT = 56 * 1024 * 1024
M_HPB = 2
MLA_HW = 256
MOE_TM = 256


def _cp(*sem):
    return pltpu.CompilerParams(dimension_semantics=sem, vmem_limit_bytes=VMEM_LIMIT)


def _rms(x):
    return x * lax.rsqrt(jnp.mean(x * x, axis=-1, keepdims=True) + EPS)


def _sigmoid(x):
    return 1.0 / (1.0 + jnp.exp(-x))


def _dot(a, b):
    return jnp.dot(a, b, preferred_element_type=F32)


def _dot_nt(a, b):
    return lax.dot_general(a, b, (((1,), (1,)), ((), ())), preferred_element_type=F32)


def _dot_tn(a, b):
    return lax.dot_general(a, b, (((0,), (0,)), ((), ())), preferred_element_type=F32)


def _mod_kernel(c_ref, w_ref, b_ref, o_ref):
    c = c_ref[...]
    s = (c * _sigmoid(c)).astype(BF16)
    o_ref[...] = _dot(s, w_ref[...].astype(BF16)) + b_ref[...]


def _mod_vectors(cvec, w_ada, b_ada):
    depth, d, n = w_ada.shape
    tn = 1024
    return pl.pallas_call(
        _mod_kernel,
        grid=(depth, n // tn),
        in_specs=[
            pl.BlockSpec((MOD_ROWS, d), lambda l, j: (0, 0)),
            pl.BlockSpec((None, d, tn), lambda l, j: (l, 0, j)),
            pl.BlockSpec((None, 1, tn), lambda l, j: (l, 0, j)),
        ],
        out_specs=pl.BlockSpec((None, MOD_ROWS, tn), lambda l, j: (l, 0, j)),
        out_shape=jax.ShapeDtypeStruct((depth, MOD_ROWS, n), F32),
        compiler_params=_cp("parallel", "parallel"),
        name="adaln_mod",
    )(cvec, w_ada, b_ada.reshape(depth, 1, n))


class _Geom:
    def __init__(self, b, t, lc, d, tm):
        assert t % tm == 0 and (b * lc) % tm == 0 and t % lc == 0 and t % GRID_W == 0
        assert lc % M_CHUNK == 0 and t % M_CHUNK == 0
        self.b, self.t, self.lc, self.d, self.tm = b, t, lc, d, tm
        self.rl, self.rc = b * t, b * lc
        self.r = self.rl + self.rc
        self.n_lat_tiles = self.rl // tm
        self.n_tiles = self.r // tm
        self.tiles_per_seq = t // tm

    def mod_row(self, i):
        return jnp.minimum(i // self.tiles_per_seq, self.b)

    def pos_block(self, i):
        return jnp.where(i < self.n_lat_tiles, i % self.tiles_per_seq, self.tiles_per_seq)


def _mod_spec(g, which, width=None, col_of=None):
    width = g.d if width is None else width
    per = g.d // width
    if col_of is None:
        return pl.BlockSpec((None, 1, width), lambda i, *_: (g.mod_row(i), 0, which * per))
    return pl.BlockSpec((None, 1, width), lambda i, j: (g.mod_row(i), 0, which * per + col_of(j)))


def _prenorm_kernel(x_ref, sh_ref, sc_ref, o_ref):
    o_ref[...] = (_rms(x_ref[...]) * (1.0 + sc_ref[...]) + sh_ref[...]).astype(o_ref.dtype)


def _prenorm(g, x_all, mods3, which_shift, which_scale, n_tiles):
    return pl.pallas_call(
        _prenorm_kernel,
        grid=(n_tiles,),
        in_specs=[
            pl.BlockSpec((g.tm, g.d), lambda i: (i, 0)),
            _mod_spec(g, which_shift),
            _mod_spec(g, which_scale),
        ],
        out_specs=pl.BlockSpec((g.tm, g.d), lambda i: (i, 0)),
        out_shape=jax.ShapeDtypeStruct((n_tiles * g.tm, g.d), BF16),
        compiler_params=_cp("parallel"),
        name="prenorm",
    )(x_all, mods3, mods3)


def _epi_raw(acc, o_ref):
    o_ref[...] = acc.astype(o_ref.dtype)


def _epi_sigmoid(acc, o_ref):
    o_ref[...] = _sigmoid(acc).astype(o_ref.dtype)


def _epi_normrope(acc, gain_ref, cos_ref, sin_ref, o_ref):
    cos = cos_ref[...]
    sin = sin_ref[...]
    for j in range(acc.shape[1] // LANES):
        sl = slice(j * LANES, (j + 1) * LANES)
        x = _rms(acc[:, sl]) * gain_ref[:, sl]
        o_ref[:, sl] = (x * cos + pltpu.roll(x, LANES // 2, 1) * sin).astype(o_ref.dtype)


def _epi_gate_perm(acc, perm_ref, o_ref):
    for hb in range(perm_ref.shape[0]):
        o_ref[:, hb * LANES:(hb + 1) * LANES] = jnp.dot(
            acc, perm_ref[hb], precision=lax.Precision.HIGHEST, preferred_element_type=F32)


def _proj_kernel(*refs, shift, epilogue):
    if shift:
        h_ref, wa_ref, wb_ref, ba_ref, bb_ref, *rest = refs
    else:
        h_ref, wa_ref, ba_ref, *rest = refs
    *extra, o_ref, w_s, b_s = rest
    kdim, tn = w_s.shape

    @pl.when(pl.program_id(1) == 0)
    def _():
        if shift:
            rows = 512
            for r0 in range(0, kdim, rows):
                both = jnp.concatenate([wa_ref[r0:r0 + rows, :], wb_ref[r0:r0 + rows, :]], axis=1)
                w_s[r0:r0 + rows, :] = pltpu.roll(both, 2 * tn - shift, 1)[:, :tn].astype(BF16)
            bb = jnp.concatenate([ba_ref[...], bb_ref[...]], axis=1)
            bb = jnp.broadcast_to(bb, (b_s.shape[0], 2 * tn))
            b_s[...] = pltpu.roll(bb, 2 * tn - shift, 1)[:, :tn]
        else:
            w_s[...] = wa_ref[...].astype(BF16)
            b_s[...] = jnp.broadcast_to(ba_ref[...], b_s.shape)

    acc = _dot(h_ref[...], w_s[...]) + b_s[0:1, :]
    epilogue(acc, *extra, o_ref)


def _proj(g, epilogue, h, w_in, b_in3, l, col0, width, out_dtype, *, tn=512, out_tn=None,
          extra=(), extra_specs=(), name="proj"):
    kdim, ncols = w_in.shape[1], w_in.shape[2]
    blk0 = col0 // tn
    shift = col0 - blk0 * tn
    nj = pl.cdiv(width, tn)
    last_blk = pl.cdiv(ncols, tn) - 1
    out_tn = tn if out_tn is None else out_tn
    w_specs = [pl.BlockSpec((None, kdim, tn), lambda j, i: (l, 0, blk0 + j))]
    b_specs = [pl.BlockSpec((None, 1, tn), lambda j, i: (l, 0, blk0 + j))]
    if shift:
        nxt = lambda j: jnp.minimum(blk0 + j + 1, last_blk)
        w_specs.append(pl.BlockSpec((None, kdim, tn), lambda j, i: (l, 0, nxt(j))))
        b_specs.append(pl.BlockSpec((None, 1, tn), lambda j, i: (l, 0, nxt(j))))
    n_w = len(w_specs)
    return pl.pallas_call(
        functools.partial(_proj_kernel, shift=shift, epilogue=epilogue),
        grid=(nj, g.n_tiles),
        in_specs=[pl.BlockSpec((g.tm, kdim), lambda j, i: (i, 0)), *w_specs, *b_specs, *extra_specs],
        out_specs=pl.BlockSpec((g.tm, out_tn), lambda j, i: (i, j)),
        out_shape=jax.ShapeDtypeStruct((g.r, nj * out_tn), out_dtype),
        scratch_shapes=[pltpu.VMEM((kdim, tn), BF16), pltpu.VMEM((MOD_ROWS, tn), F32)],
        compiler_params=_cp("arbitrary", "arbitrary"),
        name=name,
    )(h, *([w_in] * n_w), *([b_in3] * n_w), *extra)


def _mlstm_kernel(ql_ref, kl_ref, vl_ref, ol_ref, gl_ref, qc_ref, kc_ref, vc_ref, oc_ref, gc_ref,
                  cwq_ref, cwk_ref, cbq_ref, cbk_ref, ng_ref, yl_ref, yc_ref,
                  qs, ks, vs, xs, bcs, hf, hb, ct, nv, *, lc, t):
    n = lc + t
    ncc = lc // M_CHUNK
    nch = n // M_CHUNK
    L = M_CHUNK

    def conv_silu(x_ref, w_ref, b_ref, scale):
        x = x_ref[...].astype(F32)
        rows = x.shape[0]
        row = lax.broadcasted_iota(jnp.int32, (rows, 1), 0)
        acc = x * w_ref[M_CONV // 2:M_CONV // 2 + 1, :] + b_ref[...]
        for j in range(M_CONV):
            s = M_CONV // 2 - j
            if s == 0:
                continue
            xs_ = pltpu.roll(x, s % rows, 0)
            ok = jnp.logical_and(row - s >= 0, row - s < rows)
            acc = acc + jnp.where(ok, xs_, 0.0) * w_ref[j:j + 1, :]
        return (acc * _sigmoid(acc) * scale).astype(BF16)

    qs[0:lc, :] = conv_silu(qc_ref, cwq_ref, cbq_ref, M_DH ** -0.5)
    qs[lc:n, :] = conv_silu(ql_ref, cwq_ref, cbq_ref, M_DH ** -0.5)
    ks[0:lc, :] = conv_silu(kc_ref, cwk_ref, cbk_ref, 1.0)
    ks[lc:n, :] = conv_silu(kl_ref, cwk_ref, cbk_ref, 1.0)
    vs[0:lc, :] = vc_ref[...]
    vs[lc:n, :] = vl_ref[...]

    lane = lax.broadcasted_iota(jnp.int32, (1, LANES), 1)
    is_f = jnp.logical_or(jnp.logical_and(lane >= M_HPB, lane < 2 * M_HPB),
                          jnp.logical_and(lane >= 3 * M_HPB, lane < 4 * M_HPB))

    def gate_prep(g_ref, lo, rows):
        gx = g_ref[...]
        lsg = jnp.minimum(gx, 0.0) - jnp.log(1.0 + jnp.exp(-jnp.abs(gx)))
        x = jnp.where(is_f, lsg, gx)
        xs[lo:lo + rows, :] = x
        pos = lax.broadcasted_iota(jnp.int32, (rows, 1), 0) % L
        pre = x
        suf = x
        k = 1
        while k < L:
            pre = pre + jnp.where(pos >= k, pltpu.roll(pre, k, 0), 0.0)
            suf = suf + jnp.where(pos < L - k, pltpu.roll(suf, rows - k, 0), 0.0)
            k *= 2
        bcs[lo:lo + rows, :] = jnp.where(lane >= 2 * M_HPB, suf, pre)

    gate_prep(gc_ref, 0, lc)
    gate_prep(gl_ref, lc, t)

    ct[...] = jnp.zeros_like(ct)
    nv[...] = jnp.zeros_like(nv)

    r_i = lax.broadcasted_iota(jnp.int32, (L, L), 0)
    c_i = lax.broadcasted_iota(jnp.int32, (L, L), 1)
    eye = r_i == c_i
    masks = (c_i <= r_i, c_i >= r_i)

    def chain_step(c, j, d, m):
        idx = d * M_HPB + j
        r0 = pl.multiple_of(c * L, L)
        hs = slice(j * M_DH, (j + 1) * M_DH)
        q = qs[pl.ds(r0, L), hs]
        k = ks[pl.ds(r0, L), hs]
        v = vs[pl.ds(r0, L), hs]
        xg = xs[pl.ds(r0, L), :]
        bg = bcs[pl.ds(r0, L), :]
        li = 2 * d * M_HPB + j
        lf = (2 * d + 1) * M_HPB + j
        ic = xg[:, li:li + 1]
        a = bg[:, lf:lf + 1]
        b_last = a[L - 1:L, :] if d == 0 else a[0:1, :]
        d_col = ic - a
        d_row = jnp.sum(jnp.where(eye, d_col, 0.0), axis=0, keepdims=True)
        logw = jnp.where(masks[d], a + d_row, -jnp.inf)
        inter = a + m
        m_t = jnp.maximum(inter, jnp.max(logw, axis=1, keepdims=True))
        w = jnp.exp(logw - m_t) * _dot_nt(q, k)
        decay = jnp.exp(inter - m_t)
        ctb = ct[idx].astype(BF16)
        num = _dot(w.astype(BF16), v) + decay * _dot(q, ctb)
        nrow = nv[idx]
        den = jnp.sum(w, axis=1, keepdims=True) + decay * jnp.sum(q.astype(F32) * nrow, axis=1, keepdims=True)
        h = num / jnp.maximum(jnp.abs(den), jnp.exp(-m_t))
        if d == 0:
            hf[pl.ds(r0, L), hs] = h
        else:
            hb[pl.ds(r0, L), hs] = h
        src = b_last - a + ic
        m_new = jnp.maximum(b_last + m, jnp.max(src, axis=0, keepdims=True))
        gsc = jnp.exp(src - m_new)
        keep = jnp.exp(b_last + m - m_new)
        gv = (gsc * v.astype(F32)).astype(BF16)
        ct[idx] = keep * ct[idx] + _dot_tn(k, gv)
        nv[idx] = keep * nrow + jnp.sum(gsc * k.astype(F32), axis=0, keepdims=True)
        return m_new

    def body(i, ms):
        cf = i
        cb = jnp.where(i < ncc, ncc - 1 - i, nch - 1 + ncc - i)
        out = []
        for d in range(2):
            for j in range(M_HPB):
                out.append(chain_step(cf if d == 0 else cb, j, d, ms[d * M_HPB + j]))
        return tuple(out)

    lax.fori_loop(0, nch, body, tuple(jnp.zeros((1, 1), F32) for _ in range(2 * M_HPB)))

    for j in range(M_HPB):
        hs = slice(j * M_DH, (j + 1) * M_DH)
        hn = _rms(hf[:, hs] + hb[:, hs]) * ng_ref[:, hs]
        yc_ref[:, hs] = (hn[0:lc] * _sigmoid(oc_ref[:, hs].astype(F32))).astype(BF16)
        yl_ref[:, hs] = (hn[lc:n] * _sigmoid(ol_ref[:, hs].astype(F32))).astype(BF16)


def _mlstm(g, pm, pg, conv_w, conv_b, norm_g):
    w = M_HPB * M_DH
    nhb = M_HEADS // M_HPB
    cb = BRANCH_W // w
    ctx0 = g.rl // g.lc
    gcol = 0
    n = g.lc + g.t

    def lat(seg):
        return pl.BlockSpec((g.t, w), lambda b, hb: (b, seg * cb + hb))

    def ctx(seg):
        return pl.BlockSpec((g.lc, w), lambda b, hb: (ctx0 + b, seg * cb + hb))

    return pl.pallas_call(
        functools.partial(_mlstm_kernel, lc=g.lc, t=g.t),
        grid=(g.b, nhb),
        in_specs=[
            lat(0), lat(1), lat(2), lat(3),
            pl.BlockSpec((g.t, LANES), lambda b, hb: (b, gcol + hb)),
            ctx(0), ctx(1), ctx(2), ctx(3),
            pl.BlockSpec((g.lc, LANES), lambda b, hb: (ctx0 + b, gcol + hb)),
            pl.BlockSpec((M_CONV, w), lambda b, hb: (0, hb)),
            pl.BlockSpec((M_CONV, w), lambda b, hb: (0, cb + hb)),
            pl.BlockSpec((1, w), lambda b, hb: (0, hb)),
            pl.BlockSpec((1, w), lambda b, hb: (0, cb + hb)),
            pl.BlockSpec((1, w), lambda b, hb: (0, hb)),
        ],
        out_specs=[
            pl.BlockSpec((g.t, w), lambda b, hb: (b, hb)),
            pl.BlockSpec((g.lc, w), lambda b, hb: (b, hb)),
        ],
        out_shape=[
            jax.ShapeDtypeStruct((g.rl, BRANCH_W), BF16),
            jax.ShapeDtypeStruct((g.rc, BRANCH_W), BF16),
        ],
        scratch_shapes=[
            pltpu.VMEM((n, w), BF16), pltpu.VMEM((n, w), BF16), pltpu.VMEM((n, w), BF16),
            pltpu.VMEM((n, LANES), F32), pltpu.VMEM((n, LANES), F32),
            pltpu.VMEM((n, w), F32), pltpu.VMEM((n, w), F32),
            pltpu.VMEM((2 * M_HPB, M_DH, M_DH), F32), pltpu.VMEM((2 * M_HPB, 1, M_DH), F32),
        ],
        compiler_params=_cp("parallel", "parallel"),
        name="mlstm",
    )(pm, pm, pm, pm, pg, pm, pm, pm, pm, pg, conv_w, conv_w, conv_b, conv_b, norm_g)


def _mla_prep_kernel(cq_ref, ckv_ref, kr_ref, wuq_ref, wukv_ref, cqg_ref, ckvg_ref, qng_ref, qrg_ref,
                     kng_ref, krg_ref, cos_ref, sin_ref, aq_ref, ak_ref, av_ref, wuq_s, wukv_s):
    @pl.when(pl.program_id(0) == 0)
    def _():
        wuq_s[...] = wuq_ref[...].astype(BF16)
        wukv_s[...] = wukv_ref[...].astype(BF16)

    half = LANES // 2
    lane = lax.broadcasted_iota(jnp.int32, (1, LANES), 1)
    lo = lane < half
    first = (lane % half) < (half // 2)
    cos = cos_ref[...]
    sin = sin_ref[...]

    def rms_half(x):
        x2 = x * x
        s_lo = jnp.sum(jnp.where(lo, x2, 0.0), axis=-1, keepdims=True)
        s_hi = jnp.sum(jnp.where(lo, 0.0, x2), axis=-1, keepdims=True)
        ms = jnp.where(lo, s_lo, s_hi) * (1.0 / half)
        return x * lax.rsqrt(ms + EPS)

    def rope_half(x):
        partner = jnp.where(first, pltpu.roll(x, LANES - half // 2, 1), pltpu.roll(x, half // 2, 1))
        return x * cos + partner * sin

    cq = (_rms(cq_ref[...]) * cqg_ref[...]).astype(BF16)
    q = _dot(cq, wuq_s[...])
    ckv = (_rms(ckv_ref[...]) * ckvg_ref[...]).astype(BF16)
    kv = _dot(ckv, wukv_s[...])

    krn = rope_half(rms_half(kr_ref[...]) * krg_ref[...]).astype(BF16)

    rope0 = MLA_HEADS * MLA_NOPE
    for hp in range(MLA_HEADS // 2):
        slab = q[:, rope0 + hp * LANES: rope0 + (hp + 1) * LANES]
        r = rope_half(rms_half(slab) * qrg_ref[...])
        parts = (jnp.where(lo, r, 0.0), jnp.where(lo, pltpu.roll(r, half, 1), 0.0))
        for e in range(2):
            h = 2 * hp + e
            nope = _rms(q[:, h * MLA_NOPE:(h + 1) * MLA_NOPE]) * qng_ref[...]
            aq_ref[:, h * MLA_HW: h * MLA_HW + MLA_NOPE] = nope.astype(BF16)
            aq_ref[:, h * MLA_HW + MLA_NOPE:(h + 1) * MLA_HW] = parts[e].astype(BF16)
    kvw = MLA_NOPE + MLA_DV
    for h in range(MLA_HEADS):
        kn = _rms(kv[:, h * kvw: h * kvw + MLA_NOPE]) * kng_ref[...]
        ak_ref[:, h * MLA_HW: h * MLA_HW + MLA_NOPE] = kn.astype(BF16)
        ak_ref[:, h * MLA_HW + MLA_NOPE:(h + 1) * MLA_HW] = krn
        av_ref[:, h * MLA_DV:(h + 1) * MLA_DV] = kv[:, h * kvw + MLA_NOPE:(h + 1) * kvw].astype(BF16)


def _mla_prep(g, pc, wuq, wukv, gains, cos_t, sin_t):
    tm = g.tm
    full = lambda shape: pl.BlockSpec(shape, lambda i: (0, 0))
    kr_col = (MLA_Q_RANK + MLA_KV_RANK) // LANES
    return pl.pallas_call(
        _mla_prep_kernel,
        grid=(g.n_tiles,),
        in_specs=[
            pl.BlockSpec((tm, MLA_Q_RANK), lambda i: (i, 0)),
            pl.BlockSpec((tm, MLA_KV_RANK), lambda i: (i, MLA_Q_RANK // MLA_KV_RANK)),
            pl.BlockSpec((tm, LANES), lambda i: (i, kr_col)),
            full(wuq.shape), full(wukv.shape),
            full((1, MLA_Q_RANK)), full((1, MLA_KV_RANK)),
            full((1, LANES)), full((1, LANES)), full((1, LANES)), full((1, LANES)),
            pl.BlockSpec((tm, LANES), lambda i: (g.pos_block(i), 0)),
            pl.BlockSpec((tm, LANES), lambda i: (g.pos_block(i), 0)),
        ],
        out_specs=[
            pl.BlockSpec((tm, MLA_HEADS * MLA_HW), lambda i: (i, 0)),
            pl.BlockSpec((tm, MLA_HEADS * MLA_HW), lambda i: (i, 0)),
            pl.BlockSpec((tm, BRANCH_W), lambda i: (i, 0)),
        ],
        out_shape=[
            jax.ShapeDtypeStruct((g.r, MLA_HEADS * MLA_HW), BF16),
            jax.ShapeDtypeStruct((g.r, MLA_HEADS * MLA_HW), BF16),
            jax.ShapeDtypeStruct((g.r, BRANCH_W), BF16),
        ],
        scratch_shapes=[pltpu.VMEM(wuq.shape, BF16), pltpu.VMEM(wukv.shape, BF16)],
        compiler_params=_cp("arbitrary"),
        name="mla_prep",
    )(pc, pc, pc, wuq, wukv, *gains, cos_t, sin_t)


def _attn_kernel(*refs, n_soft, dh, has_lat, diff, lam_init):
    refs = list(refs)
    o_ref = refs.pop()
    if diff:
        lam_ref, sg_ref = refs[0], refs[1]
        refs = refs[2:]
    q_ref, kc_ref, vc_ref = refs[:3]
    segs = [(kc_ref, vc_ref)]
    if has_lat:
        segs.append((refs[3], refs[4]))
    q = q_ref[...]
    outs = []
    for s in range(n_soft):
        sl = slice(s * dh, (s + 1) * dh)
        sc = [_dot_nt(q[:, sl], k_ref[:, sl]) for k_ref, _ in segs]
        m = functools.reduce(jnp.maximum, [jnp.max(x, axis=-1, keepdims=True) for x in sc])
        ps = [jnp.exp(x - m) for x in sc]
        l = functools.reduce(lambda a, b: a + b, [jnp.sum(p, axis=-1, keepdims=True) for p in ps])
        o = functools.reduce(lambda a, b: a + b,
                             [_dot(p.astype(BF16), v_ref[...]) for p, (_, v_ref) in zip(ps, segs)])
        outs.append(o / l)
    if diff:
        lp = lam_ref[...]
        lam = (jnp.exp(jnp.sum(lp[0:1] * lp[1:2], axis=-1, keepdims=True))
               - jnp.exp(jnp.sum(lp[2:3] * lp[3:4], axis=-1, keepdims=True)) + lam_init)
        o = outs[0] - lam * outs[1]
        o = _rms(o) * sg_ref[...] * (1.0 - lam_init)
    else:
        o = outs[0]
    o_ref[...] = o.astype(o_ref.dtype)


def _attention(g, q_arr, k_arr, v_arr, *, q_col0, k_col0, heads, qk_w, dv, n_soft, ctx_queries,
               lam=None, subln_g=None, lam_init=0.0):
    diff = lam is not None
    ctx0 = g.rl // g.lc
    if ctx_queries:
        tq = g.lc
        nq = 1
        q_row = lambda b, qi: ctx0 + b
        out_rows = g.rc
        o_row = lambda b, qi: b
    else:
        tq = min(512, g.t)
        nq = g.t // tq
        q_row = lambda b, qi: b * nq + qi
        out_rows = g.rl
        o_row = q_row
    in_specs = []
    args = []
    if diff:
        in_specs += [pl.BlockSpec(lam.shape, lambda b, h, qi: (0, 0)),
                     pl.BlockSpec((1, dv), lambda b, h, qi: (0, 0))]
        args += [lam, subln_g]
    in_specs += [
        pl.BlockSpec((tq, qk_w), lambda b, h, qi: (q_row(b, qi), q_col0 + h)),
        pl.BlockSpec((g.lc, qk_w), lambda b, h, qi: (ctx0 + b, k_col0 + h)),
        pl.BlockSpec((g.lc, dv), lambda b, h, qi: (ctx0 + b, h)),
    ]
    args += [q_arr, k_arr, v_arr]
    if not ctx_queries:
        in_specs += [
            pl.BlockSpec((g.t, qk_w), lambda b, h, qi: (b, k_col0 + h)),
            pl.BlockSpec((g.t, dv), lambda b, h, qi: (b, h)),
        ]
        args += [k_arr, v_arr]
    return pl.pallas_call(
        functools.partial(_attn_kernel, n_soft=n_soft, dh=qk_w // n_soft, has_lat=not ctx_queries,
                          diff=diff, lam_init=lam_init),
        grid=(g.b, heads, nq),
        in_specs=in_specs,
        out_specs=pl.BlockSpec((tq, dv), lambda b, h, qi: (o_row(b, qi), h)),
        out_shape=jax.ShapeDtypeStruct((out_rows, heads * dv), BF16),
        compiler_params=_cp("parallel", "parallel", "arbitrary"),
        name="attn_diff" if diff else "attn_mla",
    )(*args)


def _branch_kernel(ym_ref, yd_ref, ya_ref, wb_ref, g0_ref, g1_ref, g2_ref, o_ref, wb_s):
    @pl.when(pl.program_id(1) == 0)
    def _():
        wb_s[...] = wb_ref[...].astype(BF16)

    acc = g0_ref[...].astype(F32) * _dot(ym_ref[...], wb_s[0])
    acc = acc + g1_ref[...].astype(F32) * _dot(yd_ref[...], wb_s[1])
    acc = acc + g2_ref[...].astype(F32) * _dot(ya_ref[...], wb_s[2])
    o_ref[...] = acc.astype(o_ref.dtype)


def _branch_merge(g, ym, yd, ya, w_branch, l, gt, n_tiles):
    tn = 512
    nj = g.d // tn
    y_spec = pl.BlockSpec((g.tm, BRANCH_W), lambda j, i: (i, 0))
    return pl.pallas_call(
        _branch_kernel,
        grid=(nj, n_tiles),
        in_specs=[
            y_spec, y_spec, y_spec,
            pl.BlockSpec((None, N_BRANCH, BRANCH_W, tn), lambda j, i: (l, 0, 0, j)),
            pl.BlockSpec((g.tm, tn), lambda j, i: (i, j)),
            pl.BlockSpec((g.tm, tn), lambda j, i: (i, nj + j)),
            pl.BlockSpec((g.tm, tn), lambda j, i: (i, 2 * nj + j)),
        ],
        out_specs=pl.BlockSpec((g.tm, tn), lambda j, i: (i, j)),
        out_shape=jax.ShapeDtypeStruct((n_tiles * g.tm, g.d), BF16),
        scratch_shapes=[pltpu.VMEM((N_BRANCH, BRANCH_W, tn), BF16)],
        compiler_params=_cp("arbitrary", "arbitrary"),
        name="branch_merge",
    )(ym, yd, ya, w_branch, gt, gt, gt)


def _outproj_kernel(z_ref, w_ref, x_ref, g_ref, o_ref, w_s):
    @pl.when(pl.program_id(1) == 0)
    def _():
        w_s[...] = w_ref[...].astype(BF16)

    o_ref[...] = x_ref[...] + g_ref[...] * _dot(z_ref[...], w_s[...])


def _outproj(g, z, w_out, l, x_all, mods3, which_gate, n_tiles):
    tn = 512
    per = g.d // tn
    return pl.pallas_call(
        _outproj_kernel,
        grid=(per, n_tiles),
        in_specs=[
            pl.BlockSpec((g.tm, g.d), lambda j, i: (i, 0)),
            pl.BlockSpec((None, g.d, tn), lambda j, i: (l, 0, j)),
            pl.BlockSpec((g.tm, tn), lambda j, i: (i, j)),
            pl.BlockSpec((None, 1, tn), lambda j, i: (g.mod_row(i), 0, which_gate * per + j)),
        ],
        out_specs=pl.BlockSpec((g.tm, tn), lambda j, i: (i, j)),
        out_shape=jax.ShapeDtypeStruct((n_tiles * g.tm, g.d), F32),
        scratch_shapes=[pltpu.VMEM((g.d, tn), BF16)],
        compiler_params=_cp("arbitrary", "arbitrary"),
        name="outproj",
    )(z, w_out, x_all, mods3)


def _moe_prep_kernel(x_ref, sh_ref, sc_ref, rwt_ref, rb_ref, tri_ref, xr_ref, rank_ref, grp_ref, cnt_ref,
                     carry, *, mod_row):
    i = pl.program_id(0)

    @pl.when(i == 0)
    def _():
        carry[...] = jnp.zeros_like(carry)

    x = x_ref[...]
    h = _rms(x) * (1.0 + sc_ref[...]) + sh_ref[...]
    logits = lax.dot_general(rwt_ref[...], h, (((1,), (1,)), ((), ())),
                             precision=lax.Precision.HIGHEST, preferred_element_type=F32)
    scores = _sigmoid(logits)
    selm = scores + rb_ref[...]
    sc_rows = [scores[e:e + 1, :] for e in range(N_EXPERTS)]
    sel = [selm[e:e + 1, :] for e in range(N_EXPERTS)]
    gscore = []
    for gi in range(N_GROUPS):
        v = sel[gi * GROUP_SIZE:(gi + 1) * GROUP_SIZE]
        pair = [v[a] + v[b] for a in range(GROUP_SIZE) for b in range(a + 1, GROUP_SIZE)]
        gscore.append(functools.reduce(jnp.maximum, pair))
    gmax = functools.reduce(jnp.maximum, gscore)
    taken = jnp.zeros_like(gmax, dtype=jnp.bool_)
    in_group = []
    for gi in range(N_GROUPS):
        hit = jnp.logical_and(gscore[gi] == gmax, jnp.logical_not(taken))
        taken = jnp.logical_or(taken, hit)
        in_group.append(hit)
    neg = -jnp.inf
    masked = [jnp.where(in_group[e // GROUP_SIZE], sel[e], neg) for e in range(N_EXPERTS)]

    def pick(vals):
        mx = functools.reduce(jnp.maximum, vals)
        seen = jnp.zeros_like(mx, dtype=jnp.bool_)
        hot = []
        for v in vals:
            hit = jnp.logical_and(v == mx, jnp.logical_not(seen))
            seen = jnp.logical_or(seen, hit)
            hot.append(hit)
        return hot

    hot1 = pick(masked)
    hot2 = pick([jnp.where(hot1[e], neg, masked[e]) for e in range(N_EXPERTS)])
    zero = jnp.zeros_like(gmax)
    s1 = functools.reduce(lambda a, b: a + b, [jnp.where(hot1[e], sc_rows[e], zero) for e in range(N_EXPERTS)])
    s2 = functools.reduce(lambda a, b: a + b, [jnp.where(hot2[e], sc_rows[e], zero) for e in range(N_EXPERTS)])
    tot = s1 + s2
    w1 = s1 / tot
    w2 = s2 / tot
    comb = [jnp.where(hot1[e], w1, zero) + jnp.where(hot2[e], w2, zero) for e in range(N_EXPERTS)]

    tm = x.shape[0]
    gmat = jnp.concatenate([jnp.where(in_group[gi], 1.0, zero) for gi in range(N_GROUPS)]
                           + [jnp.zeros((MOD_ROWS - N_GROUPS, tm), F32)], axis=0)
    incl = _dot(gmat.astype(BF16), tri_ref[...])
    excl = incl - gmat + carry[:, 0:1]
    rank = functools.reduce(lambda a, b: a + b,
                            [jnp.where(in_group[gi], excl[gi:gi + 1, :], zero) for gi in range(N_GROUPS)])
    grp = functools.reduce(lambda a, b: a + b,
                           [jnp.where(in_group[gi], float(gi), zero) for gi in range(N_GROUPS)])
    rank_ref[...] = rank.astype(jnp.int32)
    grp_ref[...] = grp.astype(jnp.int32)
    carry[...] = carry[...] + jnp.sum(gmat, axis=1, keepdims=True)
    cnt_ref[...] = carry[...]

    side = jnp.concatenate(comb + [jnp.zeros((LANES - N_EXPERTS, tm), F32)], axis=0).T
    lane = lax.broadcasted_iota(jnp.int32, (1, LANES), 1)
    side = jnp.where(lane == N_EXPERTS + mod_row(i), 1.0, side)
    d = x.shape[1]
    xr_ref[:, 0:d] = x
    xr_ref[:, d:d + LANES] = side


def _moe_prep(g, x_mid, mods3, which_shift, which_scale, rwt, rb, tri, n_tiles):
    n_rows = n_tiles * g.tm
    return pl.pallas_call(
        functools.partial(_moe_prep_kernel, mod_row=g.mod_row),
        grid=(n_tiles,),
        in_specs=[
            pl.BlockSpec((g.tm, g.d), lambda i: (i, 0)),
            _mod_spec(g, which_shift),
            _mod_spec(g, which_scale),
            pl.BlockSpec((N_EXPERTS, g.d), lambda i: (0, 0)),
            pl.BlockSpec((N_EXPERTS, 1), lambda i: (0, 0)),
            pl.BlockSpec((g.tm, g.tm), lambda i: (0, 0)),
        ],
        out_specs=[
            pl.BlockSpec((g.tm, g.d + LANES), lambda i: (i, 0)),
            pl.BlockSpec((1, g.tm), lambda i: (0, i)),
            pl.BlockSpec((1, g.tm), lambda i: (0, i)),
            pl.BlockSpec((MOD_ROWS, LANES), lambda i: (0, 0)),
        ],
        out_shape=[
            jax.ShapeDtypeStruct((n_rows, g.d + LANES), F32),
            jax.ShapeDtypeStruct((1, n_rows), jnp.int32),
            jax.ShapeDtypeStruct((1, n_rows), jnp.int32),
            jax.ShapeDtypeStruct((MOD_ROWS, LANES), F32),
        ],
        scratch_shapes=[pltpu.VMEM((MOD_ROWS, LANES), F32)],
        compiler_params=_cp("arbitrary"),
        name="moe_prep",
    )(x_mid, mods3, mods3, rwt, rb, tri)


def _moe_group_kernel(tg_ref, nv_ref, nu_ref, src_ref, xr_hbm, mods_ref, w1_hbm, w3_hbm, w2_hbm, out_hbm,
                      gbuf, obuf, w1s, w3s, w2s, stg_a, stg_b, gsem, ssem, wsem, *, n_mod_rows, e_base):
    k = pl.program_id(0)
    n_used = nu_ref[0]
    slot = k % 2
    d = obuf.shape[2]
    tme = obuf.shape[1]

    def gather_start(kk, sl):
        base = kk * tme

        def issue(r, c):
            tok = src_ref[base + r]
            pltpu.make_async_copy(xr_hbm.at[pl.ds(tok, 1), :], gbuf.at[sl, pl.ds(r, 1), :], gsem.at[sl]).start()
            return c

        lax.fori_loop(0, tme, issue, 0, unroll=8)

    def gather_wait(sl):
        def w(r, c):
            pltpu.make_async_copy(xr_hbm.at[pl.ds(0, 1), :], gbuf.at[sl, pl.ds(0, 1), :], gsem.at[sl]).wait()
            return c

        lax.fori_loop(0, tme, w, 0, unroll=8)

    def scatter_start(kk, sl):
        base = kk * tme

        def issue(r, c):
            tok = src_ref[base + r]
            pltpu.make_async_copy(obuf.at[sl, pl.ds(r, 1), :], out_hbm.at[pl.ds(tok, 1), :], ssem.at[sl]).start()
            return c

        lax.fori_loop(0, nv_ref[kk], issue, 0)

    def scatter_wait(kk, sl):
        def w(r, c):
            pltpu.make_async_copy(obuf.at[sl, pl.ds(0, 1), :], out_hbm.at[pl.ds(0, 1), :], ssem.at[sl]).wait()
            return c

        lax.fori_loop(0, nv_ref[kk], w, 0)

    def load_weights(grp):
        half_a = stg_a.shape[0]
        half_b = stg_b.shape[0]
        for e in range(GROUP_SIZE):
            ex = e_base + grp * GROUP_SIZE + e
            for src, dst in ((w1_hbm, w1s), (w3_hbm, w3s)):
                for c0 in range(0, src.shape[1], half_a):
                    cp = pltpu.make_async_copy(src.at[ex, pl.ds(c0, half_a), :], stg_a, wsem)
                    cp.start()
                    cp.wait()
                    dst[e, c0:c0 + half_a, :] = stg_a[...].astype(BF16)
            for c0 in range(0, w2_hbm.shape[1], half_b):
                cp = pltpu.make_async_copy(w2_hbm.at[ex, pl.ds(c0, half_b), :], stg_b, wsem)
                cp.start()
                cp.wait()
                w2s[e, c0:c0 + half_b, :] = stg_b[...].astype(BF16)

    @pl.when(k < n_used)
    def _():
        grp = tg_ref[k]

        @pl.when(k == 0)
        def _():
            gather_start(0, 0)

        @pl.when(k + 1 < n_used)
        def _():
            gather_start(k + 1, 1 - slot)

        @pl.when(jnp.logical_or(k == 0, grp != tg_ref[jnp.maximum(k - 1, 0)]))
        def _():
            load_weights(grp)

        gather_wait(slot)
        buf = gbuf[slot]
        x = buf[:, 0:d]
        side = buf[:, d:d + LANES]
        lane = lax.broadcasted_iota(jnp.int32, (1, LANES), 1)

        def cond_vec(which):
            out = jnp.zeros((tme, d), F32)
            for r in range(n_mod_rows):
                hot = side[:, N_EXPERTS + r:N_EXPERTS + r + 1] > 0.5
                out = jnp.where(hot, mods_ref[r:r + 1, which * d:(which + 1) * d], out)
            return out

        h = (_rms(x) * (1.0 + cond_vec(1)) + cond_vec(0)).astype(BF16)
        y = jnp.zeros((tme, d), F32)
        for e in range(GROUP_SIZE):
            a = _dot(h, w1s[e])
            b = _dot(h, w3s[e])
            cw = jnp.sum(jnp.where(lane == grp * GROUP_SIZE + e, side, 0.0), axis=1, keepdims=True)
            hid = (a * _sigmoid(a) * b * cw).astype(BF16)
            y = y + _dot(hid, w2s[e])
        res = x + cond_vec(2) * y

        @pl.when(k >= 2)
        def _():
            scatter_wait(k - 2, slot)

        obuf[slot] = res
        scatter_start(k, slot)

        @pl.when(k == n_used - 1)
        def _():
            @pl.when(k >= 1)
            def _():
                scatter_wait(k - 1, 1 - slot)

            scatter_wait(k, slot)


def _moe_group(g, xr, mods_tail, w1, w3, w2, meta, n_rows, n_mod_rows, e_base):
    tile_group, n_valid, n_used, src = meta
    kt = tile_group.shape[0]
    tme = MOE_TM
    any_spec = pl.BlockSpec(memory_space=pl.ANY)
    grid_spec = pltpu.PrefetchScalarGridSpec(
        num_scalar_prefetch=4,
        grid=(kt,),
        in_specs=[any_spec, pl.BlockSpec(mods_tail.shape, lambda k, *_: (0, 0)), any_spec, any_spec, any_spec],
        out_specs=any_spec,
        scratch_shapes=[
            pltpu.VMEM((2, tme, g.d + LANES), F32),
            pltpu.VMEM((2, tme, g.d), F32),
            pltpu.VMEM((GROUP_SIZE, g.d, D_EXPERT), BF16),
            pltpu.VMEM((GROUP_SIZE, g.d, D_EXPERT), BF16),
            pltpu.VMEM((GROUP_SIZE, D_EXPERT, g.d), BF16),
            pltpu.VMEM((g.d // 2, D_EXPERT), F32),
            pltpu.VMEM((D_EXPERT // 2, g.d), F32),
            pltpu.SemaphoreType.DMA((2,)),
            pltpu.SemaphoreType.DMA((2,)),
            pltpu.SemaphoreType.DMA,
        ],
    )
    return pl.pallas_call(
        functools.partial(_moe_group_kernel, n_mod_rows=n_mod_rows, e_base=e_base),
        grid_spec=grid_spec,
        out_shape=jax.ShapeDtypeStruct((n_rows, g.d), F32),
        compiler_params=_cp("arbitrary"),
        name="moe_group",
    )(tile_group, n_valid, n_used, src, xr, mods_tail, w1, w3, w2)


def _moe_meta(rank, grp, cnt, n_rows):
    tme = MOE_TM
    kt = n_rows // tme + N_GROUPS
    counts = cnt[:N_GROUPS, 0].astype(jnp.int32)
    ntile = (counts + tme - 1) // tme
    tile_end = jnp.cumsum(ntile)
    tile_off = tile_end - ntile
    pos = (tile_off * tme)[grp[0]] + rank[0]
    p = kt * tme
    src = jnp.zeros((p,), jnp.int32).at[pos].set(jnp.arange(n_rows, dtype=jnp.int32))
    ks = jnp.arange(kt, dtype=jnp.int32)
    tile_group = jnp.minimum(jnp.searchsorted(tile_end, ks, side="right").astype(jnp.int32), N_GROUPS - 1)
    n_used = tile_end[-1:]
    n_valid = jnp.clip(counts[tile_group] - (ks - tile_off[tile_group]) * tme, 0, tme)
    n_valid = jnp.where(ks < n_used[0], n_valid, 0)
    rows = jnp.arange(p, dtype=jnp.int32)
    in_tile = rows % tme
    valid = in_tile < n_valid[rows // tme]
    src = jnp.where(valid, src, src[rows - in_tile])
    return tile_group, n_valid, n_used, src


def _rope_tables(t, rot_dim, tile_rows, reps):
    rows = t // GRID_W
    r = jnp.repeat(jnp.arange(rows, dtype=F32), GRID_W)
    col = jnp.tile(jnp.arange(GRID_W, dtype=F32), rows)
    n_freq = rot_dim // 4
    inv = ROPE_BASE ** (-jnp.arange(n_freq, dtype=F32) / n_freq)
    ang = jnp.concatenate([r[:, None] * inv, col[:, None] * inv], axis=-1)
    cos = jnp.tile(jnp.concatenate([jnp.cos(ang), jnp.cos(ang)], axis=-1), (1, reps))
    sin = jnp.tile(jnp.concatenate([-jnp.sin(ang), jnp.sin(ang)], axis=-1), (1, reps))
    cos = jnp.concatenate([cos, jnp.ones((tile_rows, LANES), F32)], axis=0)
    sin = jnp.concatenate([sin, jnp.zeros((tile_rows, LANES), F32)], axis=0)
    return cos, sin


def _gate_perm():
    import numpy as np
    nhb = M_HEADS // M_HPB
    perm = np.zeros((nhb, LANES, LANES), np.float32)
    for hb in range(nhb):
        for gi in range(4):
            for j in range(M_HPB):
                perm[hb, gi * M_HEADS + hb * M_HPB + j, gi * M_HPB + j] = 1.0
    return jnp.asarray(perm)


def _uq_perm():
    nope = [h * MLA_QK + i for h in range(MLA_HEADS) for i in range(MLA_NOPE)]
    rope = [h * MLA_QK + MLA_NOPE + i for h in range(MLA_HEADS) for i in range(MLA_ROPE)]
    return jnp.asarray(nope + rope)


def kernel(x, c, ctx, c_ctx, w_ada, b_ada, w_in, b_in, m_conv_w, m_conv_b, m_norm_g, da_q_norm_g, da_k_norm_g,
           da_lambda, da_subln_g, mla_cq_norm_g, mla_ckv_norm_g, mla_w_uq, mla_w_ukv, mla_q_norm_g, mla_k_norm_g,
           w_branch, w_out, moe_w1, moe_w3, moe_w2, router_w, router_bias):
    bsz, t, d = x.shape
    lc = ctx.shape[1]
    depth = w_ada.shape[0]
    g = _Geom(bsz, t, lc, d, min(1024, bsz * lc, t))
    assert bsz + 1 <= MOD_ROWS

    cvec = jnp.concatenate([c, c_ctx[None, :], jnp.zeros((MOD_ROWS - bsz - 1, d), F32)], axis=0)
    mods = _mod_vectors(cvec, w_ada, b_ada)

    cos_d, sin_d = _rope_tables(t, DA_DH, g.tm, 1)
    cos_a, sin_a = _rope_tables(t, MLA_ROPE, g.tm, 2)
    rwt = router_w.T
    rb = router_bias.reshape(N_EXPERTS, 1)
    uq_perm = _uq_perm()
    x_all = jnp.concatenate([x.reshape(g.rl, d), ctx.reshape(g.rc, d)], axis=0)
    pos_spec = pl.BlockSpec((g.tm, LANES), lambda j, i: (g.pos_block(i), 0))
    b_in3 = b_in.reshape(depth, 1, -1)
    nhb = M_HEADS // M_HPB
    gate_perm = _gate_perm()
    tri = (jnp.arange(g.tm)[:, None] <= jnp.arange(g.tm)[None, :]).astype(BF16)
    w1r = moe_w1.reshape(depth * N_EXPERTS, d, D_EXPERT)
    w3r = moe_w3.reshape(depth * N_EXPERTS, d, D_EXPERT)
    w2r = moe_w2.reshape(depth * N_EXPERTS, D_EXPERT, d)

    for l in range(depth):
        last = l == depth - 1
        n_tiles = g.n_lat_tiles if last else g.n_tiles
        n_rows = n_tiles * g.tm
        lam_init = 0.8 - 0.6 * math.exp(-0.3 * l)
        mods3 = mods[l].reshape(MOD_ROWS, 1, 6 * d)

        h1 = _prenorm(g, x_all, mods3, 0, 1, g.n_tiles)

        row = lambda v: v.reshape(1, -1)
        proj = functools.partial(_proj, g, h=h1, w_in=w_in, b_in3=b_in3, l=l)
        pm = proj(_epi_raw, col0=OFF_MQK, width=OFF_MG - OFF_MQK, out_dtype=BF16, name="proj_mlstm")
        pg = proj(_epi_gate_perm, col0=OFF_MG, width=LANES, out_dtype=F32, tn=LANES, out_tn=nhb * LANES,
                  extra=(gate_perm,), extra_specs=(pl.BlockSpec(gate_perm.shape, lambda j, i: (0, 0, 0)),),
                  name="proj_mgates")
        da_gain = jnp.concatenate([jnp.tile(da_q_norm_g[l] * DA_DH ** -0.5, 2 * DA_HEADS),
                                   jnp.tile(da_k_norm_g[l], 2 * DA_HEADS)])
        dqk = proj(_epi_normrope, col0=OFF_DQ, width=2 * DA_QK_W, out_dtype=BF16,
                   extra=(row(da_gain), cos_d, sin_d),
                   extra_specs=(pl.BlockSpec((1, 512), lambda j, i: (0, j)), pos_spec, pos_spec),
                   name="proj_dqk")
        dv = proj(_epi_raw, col0=OFF_DV, width=BRANCH_W, out_dtype=BF16, name="proj_dv")
        pc = proj(_epi_raw, col0=OFF_CQ, width=OFF_G - OFF_CQ, out_dtype=F32, name="proj_small")
        gt = proj(_epi_sigmoid, col0=OFF_G, width=N_BRANCH * d, out_dtype=BF16, name="proj_gates")

        ym_l, ym_c = _mlstm(g, pm, pg, m_conv_w[l], row(m_conv_b[l]), row(m_norm_g[l]))

        qg, kg = mla_q_norm_g[l], mla_k_norm_g[l]
        a_scale = MLA_QK ** -0.5
        gains = (row(mla_cq_norm_g[l]), row(mla_ckv_norm_g[l]),
                 row(qg[:MLA_NOPE] * a_scale), row(jnp.tile(qg[MLA_NOPE:], 2) * a_scale),
                 row(kg[:MLA_NOPE]), row(jnp.concatenate([kg[MLA_NOPE:], jnp.zeros((LANES - MLA_ROPE,), F32)])))
        aq, ak, av = _mla_prep(g, pc, mla_w_uq[l][:, uq_perm], mla_w_ukv[l], gains, cos_a, sin_a)

        lam_p = da_lambda[l]
        sub_g = row(da_subln_g[l])
        da_kw = dict(q_col0=0, k_col0=DA_HEADS, heads=DA_HEADS, qk_w=2 * DA_DH, dv=DA_DV, n_soft=2,
                     lam=lam_p, subln_g=sub_g, lam_init=lam_init)
        mla_kw = dict(q_col0=0, k_col0=0, heads=MLA_HEADS, qk_w=MLA_HW, dv=MLA_DV, n_soft=1)
        yd_l = _attention(g, dqk, dqk, dv, ctx_queries=False, **da_kw)
        ya_l = _attention(g, aq, ak, av, ctx_queries=False, **mla_kw)
        if last:
            ym, yd, ya = ym_l, yd_l, ya_l
        else:
            yd_c = _attention(g, dqk, dqk, dv, ctx_queries=True, **da_kw)
            ya_c = _attention(g, aq, ak, av, ctx_queries=True, **mla_kw)
            ym = jnp.concatenate([ym_l, ym_c], axis=0)
            yd = jnp.concatenate([yd_l, yd_c], axis=0)
            ya = jnp.concatenate([ya_l, ya_c], axis=0)

        z = _branch_merge(g, ym, yd, ya, w_branch, l, gt, n_tiles)
        x_mid = _outproj(g, z, w_out, l, x_all, mods3, 2, n_tiles)

        xr, rank, grp, cnt = _moe_prep(g, x_mid, mods3, 3, 4, rwt, rb, tri, n_tiles)
        meta = _moe_meta(rank, grp, cnt, n_rows)
        x_all = _moe_group(g, xr, mods[l][:, 3 * d:], w1r, w3r, w2r, meta, n_rows, bsz + 1, l * N_EXPERTS)

    return x_all.reshape(bsz, t, d)
```

```python
import functools
import math

import jax
import jax.numpy as jnp
from jax import lax
from jax.experimental import pallas as pl
from jax.experimental.pallas import tpu as pltpu

F32 = jnp.float32
BF16 = jnp.bfloat16

GRID_W = 64
ROPE_BASE = 10000.0
EPS = 1e-6
N_BRANCH = 3
BRANCH_W = 1024
M_HEADS = 4
M_DH = BRANCH_W // M_HEADS
M_CHUNK = 64
M_CONV = 5
DA_HEADS = 4
DA_DV = BRANCH_W // DA_HEADS
DA_DH = DA_DV // 2
DA_QK_W = DA_HEADS * 2 * DA_DH
MLA_HEADS = 8
MLA_Q_RANK = 512
MLA_KV_RANK = 256
MLA_NOPE = 128
MLA_ROPE = 64
MLA_DV = BRANCH_W // MLA_HEADS
MLA_QK = MLA_NOPE + MLA_ROPE
N_EXPERTS = 16
N_GROUPS = 4
GROUP_SIZE = N_EXPERTS // N_GROUPS
D_EXPERT = 512

OFF_MQK = 0
OFF_MV = OFF_MQK + 2 * BRANCH_W
OFF_MO = OFF_MV + BRANCH_W
OFF_MG = OFF_MO + BRANCH_W
OFF_DQ = OFF_MG + 4 * M_HEADS
OFF_DK = OFF_DQ + DA_QK_W
OFF_DV = OFF_DK + DA_QK_W
OFF_CQ = OFF_DV + BRANCH_W
OFF_CKV = OFF_CQ + MLA_Q_RANK
OFF_KR = OFF_CKV + MLA_KV_RANK
OFF_G = OFF_KR + MLA_ROPE

LANES = 128
MOD_ROWS = 8
VMEM_LIMIT = 56 * 1024 * 1024
M_HPB = 2
MLA_HW = 256
MOE_TM = 256


def _cp(*sem):
    return pltpu.CompilerParams(dimension_semantics=sem, vmem_limit_bytes=VMEM_LIMIT)


def _rms(x):
    return x * lax.rsqrt(jnp.mean(x * x, axis=-1, keepdims=True) + EPS)


def _sigmoid(x):
    return 1.0 / (1.0 + jnp.exp(-x))


def _dot(a, b):
    return jnp.dot(a, b, preferred_element_type=F32)


def _dot_nt(a, b):
    return lax.dot_general(a, b, (((1,), (1,)), ((), ())), preferred_element_type=F32)


def _dot_tn(a, b):
    return lax.dot_general(a, b, (((0,), (0,)), ((), ())), preferred_element_type=F32)


def _mod_kernel(c_ref, w_ref, b_ref, o_ref):
    c = c_ref[...]
    s = (c * _sigmoid(c)).astype(BF16)
    o_ref[...] = _dot(s, w_ref[...].astype(BF16)) + b_ref[...]


def _mod_vectors(cvec, w_ada, b_ada):
    depth, d, n = w_ada.shape
    tn = 1024
    return pl.pallas_call(
        _mod_kernel,
        grid=(depth, n // tn),
        in_specs=[
            pl.BlockSpec((MOD_ROWS, d), lambda l, j: (0, 0)),
            pl.BlockSpec((None, d, tn), lambda l, j: (l, 0, j)),
            pl.BlockSpec((None, 1, tn), lambda l, j: (l, 0, j)),
        ],
        out_specs=pl.BlockSpec((None, MOD_ROWS, tn), lambda l, j: (l, 0, j)),
        out_shape=jax.ShapeDtypeStruct((depth, MOD_ROWS, n), F32),
        compiler_params=_cp("parallel", "parallel"),
        name="adaln_mod",
    )(cvec, w_ada, b_ada.reshape(depth, 1, n))


class _Geom:
    def __init__(self, b, t, lc, d, tm):
        assert t % tm == 0 and (b * lc) % tm == 0 and t % lc == 0 and t % GRID_W == 0
        assert lc % M_CHUNK == 0 and t % M_CHUNK == 0
        self.b, self.t, self.lc, self.d, self.tm = b, t, lc, d, tm
        self.rl, self.rc = b * t, b * lc
        self.r = self.rl + self.rc
        self.mblk = min(256, lc)
        assert lc % self.mblk == 0 and t % self.mblk == 0 and self.mblk % M_CHUNK == 0
        self.n_lat_tiles = self.rl // tm
        self.n_tiles = self.r // tm
        self.tiles_per_seq = t // tm

    def mod_row(self, i):
        return jnp.minimum(i // self.tiles_per_seq, self.b)

    def pos_block(self, i):
        return jnp.where(i < self.n_lat_tiles, i % self.tiles_per_seq, self.tiles_per_seq)


def _mod_spec(g, which, width=None, col_of=None):
    width = g.d if width is None else width
    per = g.d // width
    if col_of is None:
        return pl.BlockSpec((None, 1, width), lambda i, *_: (g.mod_row(i), 0, which * per))
    return pl.BlockSpec((None, 1, width), lambda i, j: (g.mod_row(i), 0, which * per + col_of(j)))


def _prenorm_kernel(x_ref, sh_ref, sc_ref, o_ref):
    o_ref[...] = (_rms(x_ref[...]) * (1.0 + sc_ref[...]) + sh_ref[...]).astype(o_ref.dtype)


def _prenorm(g, x_all, mods3, which_shift, which_scale, n_tiles):
    return pl.pallas_call(
        _prenorm_kernel,
        grid=(n_tiles,),
        in_specs=[
            pl.BlockSpec((g.tm, g.d), lambda i: (i, 0)),
            _mod_spec(g, which_shift),
            _mod_spec(g, which_scale),
        ],
        out_specs=pl.BlockSpec((g.tm, g.d), lambda i: (i, 0)),
        out_shape=jax.ShapeDtypeStruct((n_tiles * g.tm, g.d), BF16),
        compiler_params=_cp("parallel"),
        name="prenorm",
    )(x_all, mods3, mods3)


def _epi_raw(acc, o_ref):
    o_ref[...] = acc.astype(o_ref.dtype)


def _epi_sigmoid(acc, o_ref):
    o_ref[...] = _sigmoid(acc).astype(o_ref.dtype)


def _epi_normrope(acc, gain_ref, cos_ref, sin_ref, o_ref):
    cos = cos_ref[...]
    sin = sin_ref[...]
    for j in range(acc.shape[1] // LANES):
        sl = slice(j * LANES, (j + 1) * LANES)
        x = _rms(acc[:, sl]) * gain_ref[:, sl]
        o_ref[:, sl] = (x * cos + pltpu.roll(x, LANES // 2, 1) * sin).astype(o_ref.dtype)


def _epi_gate_perm(acc, perm_ref, o_ref):
    for hb in range(perm_ref.shape[0]):
        o_ref[:, hb * LANES:(hb + 1) * LANES] = jnp.dot(
            acc, perm_ref[hb], precision=lax.Precision.HIGHEST, preferred_element_type=F32)


def _proj_kernel(*refs, shift, epilogue):
    if shift:
        h_ref, wa_ref, wb_ref, ba_ref, bb_ref, *rest = refs
    else:
        h_ref, wa_ref, ba_ref, *rest = refs
    *extra, o_ref, w_s, b_s = rest
    tn = w_s.shape[0]

    @pl.when(pl.program_id(1) == 0)
    def _():
        if shift:
            w_s[0:tn - shift, :] = wa_ref[shift:tn, :].astype(BF16)
            w_s[tn - shift:tn, :] = wb_ref[0:shift, :].astype(BF16)
            bb = jnp.concatenate([ba_ref[...], bb_ref[...]], axis=1)
            bb = jnp.broadcast_to(bb, (b_s.shape[0], 2 * tn))
            b_s[...] = pltpu.roll(bb, 2 * tn - shift, 1)[:, :tn]
        else:
            w_s[...] = wa_ref[...].astype(BF16)
            b_s[...] = jnp.broadcast_to(ba_ref[...], b_s.shape)

    acc = _dot_nt(h_ref[...], w_s[...]) + b_s[0:1, :]
    epilogue(acc, *extra, o_ref)


def _proj(g, epilogue, h, w_in_t, b_in3, l, col0, width, out_dtype, *, tn=512, out_tn=None,
          extra=(), extra_specs=(), name="proj"):
    ncols, kdim = w_in_t.shape[1], w_in_t.shape[2]
    blk0 = col0 // tn
    shift = col0 - blk0 * tn
    assert shift % 16 == 0
    nj = pl.cdiv(width, tn)
    last_blk = pl.cdiv(ncols, tn) - 1
    out_tn = tn if out_tn is None else out_tn
    w_specs = [pl.BlockSpec((None, tn, kdim), lambda j, i: (l, blk0 + j, 0))]
    b_specs = [pl.BlockSpec((None, 1, tn), lambda j, i: (l, 0, blk0 + j))]
    if shift:
        nxt = lambda j: jnp.minimum(blk0 + j + 1, last_blk)
        w_specs.append(pl.BlockSpec((None, tn, kdim), lambda j, i: (l, nxt(j), 0)))
        b_specs.append(pl.BlockSpec((None, 1, tn), lambda j, i: (l, 0, nxt(j))))
    n_w = len(w_specs)
    return pl.pallas_call(
        functools.partial(_proj_kernel, shift=shift, epilogue=epilogue),
        grid=(nj, g.n_tiles),
        in_specs=[pl.BlockSpec((g.tm, kdim), lambda j, i: (i, 0)), *w_specs, *b_specs, *extra_specs],
        out_specs=pl.BlockSpec((g.tm, out_tn), lambda j, i: (i, j)),
        out_shape=jax.ShapeDtypeStruct((g.r, nj * out_tn), out_dtype),
        scratch_shapes=[pltpu.VMEM((tn, kdim), BF16), pltpu.VMEM((MOD_ROWS, tn), F32)],
        compiler_params=_cp("arbitrary", "arbitrary"),
        name=name,
    )(h, *([w_in_t] * n_w), *([b_in3] * n_w), *extra)


def _mlstm_kernel(ql_ref, kl_ref, vl_ref, ol_ref, gl_ref, qc_ref, kc_ref, vc_ref, oc_ref, gc_ref,
                  cwq_ref, cwk_ref, cbq_ref, cbk_ref, ng_ref, yl_ref, yc_ref,
                  qs, ks, vs, xs, bcs, hf, hb, ct, nv, *, lc, t):
    n = lc + t
    ncc = lc // M_CHUNK
    nch = n // M_CHUNK
    L = M_CHUNK

    def conv_silu(x_ref, w_ref, b_ref, scale):
        x = x_ref[...].astype(F32)
        rows = x.shape[0]
        row = lax.broadcasted_iota(jnp.int32, (rows, 1), 0)
        acc = x * w_ref[M_CONV // 2:M_CONV // 2 + 1, :] + b_ref[...]
        for j in range(M_CONV):
            s = M_CONV // 2 - j
            if s == 0:
                continue
            xs_ = pltpu.roll(x, s % rows, 0)
            ok = jnp.logical_and(row - s >= 0, row - s < rows)
            acc = acc + jnp.where(ok, xs_, 0.0) * w_ref[j:j + 1, :]
        return (acc * _sigmoid(acc) * scale).astype(BF16)

    qs[0:lc, :] = conv_silu(qc_ref, cwq_ref, cbq_ref, M_DH ** -0.5)
    qs[lc:n, :] = conv_silu(ql_ref, cwq_ref, cbq_ref, M_DH ** -0.5)
    ks[0:lc, :] = conv_silu(kc_ref, cwk_ref, cbk_ref, 1.0)
    ks[lc:n, :] = conv_silu(kl_ref, cwk_ref, cbk_ref, 1.0)
    vs[0:lc, :] = vc_ref[...]
    vs[lc:n, :] = vl_ref[...]

    lane = lax.broadcasted_iota(jnp.int32, (1, LANES), 1)
    is_f = jnp.logical_or(jnp.logical_and(lane >= M_HPB, lane < 2 * M_HPB),
                          jnp.logical_and(lane >= 3 * M_HPB, lane < 4 * M_HPB))

    def gate_prep(g_ref, lo, rows):
        gx = g_ref[...]
        lsg = jnp.minimum(gx, 0.0) - jnp.log(1.0 + jnp.exp(-jnp.abs(gx)))
        x = jnp.where(is_f, lsg, gx)
        xs[lo:lo + rows, :] = x
        pos = lax.broadcasted_iota(jnp.int32, (rows, 1), 0) % L
        pre = x
        suf = x
        k = 1
        while k < L:
            pre = pre + jnp.where(pos >= k, pltpu.roll(pre, k, 0), 0.0)
            suf = suf + jnp.where(pos < L - k, pltpu.roll(suf, rows - k, 0), 0.0)
            k *= 2
        bcs[lo:lo + rows, :] = jnp.where(lane >= 2 * M_HPB, suf, pre)

    gate_prep(gc_ref, 0, lc)
    gate_prep(gl_ref, lc, t)

    ct[...] = jnp.zeros_like(ct)
    nv[...] = jnp.zeros_like(nv)

    r_i = lax.broadcasted_iota(jnp.int32, (L, L), 0)
    c_i = lax.broadcasted_iota(jnp.int32, (L, L), 1)
    eye = r_i == c_i
    masks = (c_i <= r_i, c_i >= r_i)

    def chain_step(c, j, d, m):
        idx = d * M_HPB + j
        r0 = pl.multiple_of(c * L, L)
        hs = slice(j * M_DH, (j + 1) * M_DH)
        q = qs[pl.ds(r0, L), hs]
        k = ks[pl.ds(r0, L), hs]
        v = vs[pl.ds(r0, L), hs]
        xg = xs[pl.ds(r0, L), :]
        bg = bcs[pl.ds(r0, L), :]
        li = 2 * d * M_HPB + j
        lf = (2 * d + 1) * M_HPB + j
        ic = xg[:, li:li + 1]
        a = bg[:, lf:lf + 1]
        b_last = a[L - 1:L, :] if d == 0 else a[0:1, :]
        d_col = ic - a
        d_row = jnp.sum(jnp.where(eye, d_col, 0.0), axis=0, keepdims=True)
        logw = jnp.where(masks[d], a + d_row, -jnp.inf)
        inter = a + m
        m_t = jnp.maximum(inter, jnp.max(logw, axis=1, keepdims=True))
        w = jnp.exp(logw - m_t) * _dot_nt(q, k)
        decay = jnp.exp(inter - m_t)
        ctb = ct[idx].astype(BF16)
        num = _dot(w.astype(BF16), v) + decay * _dot(q, ctb)
        nrow = nv[idx]
        den = jnp.sum(w, axis=1, keepdims=True) + decay * jnp.sum(q.astype(F32) * nrow, axis=1, keepdims=True)
        h = num / jnp.maximum(jnp.abs(den), jnp.exp(-m_t))
        if d == 0:
            hf[pl.ds(r0, L), hs] = h
        else:
            hb[pl.ds(r0, L), hs] = h
        src = b_last - a + ic
        m_new = jnp.maximum(b_last + m, jnp.max(src, axis=0, keepdims=True))
        gsc = jnp.exp(src - m_new)
        keep = jnp.exp(b_last + m - m_new)
        gv = (gsc * v.astype(F32)).astype(BF16)
        ct[idx] = keep * ct[idx] + _dot_tn(k, gv)
        nv[idx] = keep * nrow + jnp.sum(gsc * k.astype(F32), axis=0, keepdims=True)
        return m_new

    def body(i, ms):
        cf = i
        cb = jnp.where(i < ncc, ncc - 1 - i, nch - 1 + ncc - i)
        out = []
        for d in range(2):
            for j in range(M_HPB):
                out.append(chain_step(cf if d == 0 else cb, j, d, ms[d * M_HPB + j]))
        return tuple(out)

    lax.fori_loop(0, nch, body, tuple(jnp.zeros((1, 1), F32) for _ in range(2 * M_HPB)))

    for j in range(M_HPB):
        hs = slice(j * M_DH, (j + 1) * M_DH)
        hn = _rms(hf[:, hs] + hb[:, hs]) * ng_ref[:, hs]
        yc_ref[:, hs] = (hn[0:lc] * _sigmoid(oc_ref[:, hs].astype(F32))).astype(BF16)
        yl_ref[:, hs] = (hn[lc:n] * _sigmoid(ol_ref[:, hs].astype(F32))).astype(BF16)


def _mlstm(g, pm, pg, conv_w, conv_b, norm_g):
    w = M_HPB * M_DH
    nhb = M_HEADS // M_HPB
    cb = BRANCH_W // w
    ctx0 = g.rl // g.lc
    gcol = 0
    n = g.lc + g.t

    def lat(seg):
        return pl.BlockSpec((g.t, w), lambda b, hb: (b, seg * cb + hb))

    def ctx(seg):
        return pl.BlockSpec((g.lc, w), lambda b, hb: (ctx0 + b, seg * cb + hb))

    return pl.pallas_call(
        functools.partial(_mlstm_kernel, lc=g.lc, t=g.t),
        grid=(g.b, nhb),
        in_specs=[
            lat(0), lat(1), lat(2), lat(3),
            pl.BlockSpec((g.t, LANES), lambda b, hb: (b, gcol + hb)),
            ctx(0), ctx(1), ctx(2), ctx(3),
            pl.BlockSpec((g.lc, LANES), lambda b, hb: (ctx0 + b, gcol + hb)),
            pl.BlockSpec((M_CONV, w), lambda b, hb: (0, hb)),
            pl.BlockSpec((M_CONV, w), lambda b, hb: (0, cb + hb)),
            pl.BlockSpec((1, w), lambda b, hb: (0, hb)),
            pl.BlockSpec((1, w), lambda b, hb: (0, cb + hb)),
            pl.BlockSpec((1, w), lambda b, hb: (0, hb)),
        ],
        out_specs=[
            pl.BlockSpec((g.t, w), lambda b, hb: (b, hb)),
            pl.BlockSpec((g.lc, w), lambda b, hb: (b, hb)),
        ],
        out_shape=[
            jax.ShapeDtypeStruct((g.rl, BRANCH_W), BF16),
            jax.ShapeDtypeStruct((g.rc, BRANCH_W), BF16),
        ],
        scratch_shapes=[
            pltpu.VMEM((n, w), BF16), pltpu.VMEM((n, w), BF16), pltpu.VMEM((n, w), BF16),
            pltpu.VMEM((n, LANES), F32), pltpu.VMEM((n, LANES), F32),
            pltpu.VMEM((n, w), F32), pltpu.VMEM((n, w), F32),
            pltpu.VMEM((2 * M_HPB, M_DH, M_DH), F32), pltpu.VMEM((2 * M_HPB, 1, M_DH), F32),
        ],
        compiler_params=_cp("parallel", "parallel"),
        name="mlstm",
    )(pm, pm, pm, pm, pg, pm, pm, pm, pm, pg, conv_w, conv_w, conv_b, conv_b, norm_g)


def _mconv_kernel(x_ref, prev_ref, next_ref, w_ref, b_ref, s_ref, o_ref, xe, *, blocks_per_seq, blocks_per_ctx,
                  n_lat_blocks):
    i = pl.program_id(0)
    rows = x_ref.shape[0]
    halo = prev_ref.shape[0]
    pad = M_CONV // 2
    in_lat = i < n_lat_blocks
    pos = jnp.where(in_lat, i % blocks_per_seq, (i - n_lat_blocks) % blocks_per_ctx)
    per_seq = jnp.where(in_lat, blocks_per_seq, blocks_per_ctx)
    has_prev = pos != 0
    has_next = pos != per_seq - 1
    xe[0:halo, :] = jnp.where(has_prev, prev_ref[...].astype(F32), 0.0)
    xe[halo:halo + rows, :] = x_ref[...].astype(F32)
    xe[halo + rows:halo + rows + halo, :] = jnp.where(has_next, next_ref[...].astype(F32), 0.0)
    acc = jnp.broadcast_to(b_ref[...], (rows, x_ref.shape[1]))
    for j in range(M_CONV):
        acc = acc + xe[halo - pad + j:halo - pad + j + rows, :] * w_ref[j:j + 1, :]
    o_ref[...] = (acc * _sigmoid(acc) * s_ref[...]).astype(o_ref.dtype)


def _mconv(g, pm, conv_w, conv_b, scale_row):
    rows = g.mblk
    halo = 16
    tn = 512
    per = rows // halo
    nblk = g.r // rows
    last = g.r // halo - 1
    return pl.pallas_call(
        functools.partial(_mconv_kernel, blocks_per_seq=g.t // rows, blocks_per_ctx=g.lc // rows,
                          n_lat_blocks=g.rl // rows),
        grid=(nblk, 2 * BRANCH_W // tn),
        in_specs=[
            pl.BlockSpec((rows, tn), lambda i, j: (i, j)),
            pl.BlockSpec((halo, tn), lambda i, j: (jnp.maximum(i * per - 1, 0), j)),
            pl.BlockSpec((halo, tn), lambda i, j: (jnp.minimum((i + 1) * per, last), j)),
            pl.BlockSpec((M_CONV, tn), lambda i, j: (0, j)),
            pl.BlockSpec((1, tn), lambda i, j: (0, j)),
            pl.BlockSpec((1, tn), lambda i, j: (0, j)),
        ],
        out_specs=pl.BlockSpec((rows, tn), lambda i, j: (i, j)),
        out_shape=jax.ShapeDtypeStruct((g.r, 2 * BRANCH_W), BF16),
        scratch_shapes=[pltpu.VMEM((rows + 2 * halo, tn), F32)],
        compiler_params=_cp("parallel", "parallel"),
        name="mlstm_conv",
    )(pm, pm, pm, conv_w, conv_b, scale_row)


def _mscan_kernel(qkf_ref, vf_ref, gf_ref, qkb_ref, vb_ref, gb_ref, ol_ref, oc_ref, ng_ref, yl_ref, yc_ref,
                  xsf, bcf, xsb, bcb, hf, hb, ct, nv, ms, *, lc, t, mblk):
    i = pl.program_id(1)
    nblk = pl.num_programs(1)
    L = M_CHUNK
    cpb = mblk // L
    nbc = lc // mblk
    nbl = t // mblk
    n = lc + t

    @pl.when(i == 0)
    def _():
        ct[...] = jnp.zeros_like(ct)
        nv[...] = jnp.zeros_like(nv)
        ms[...] = jnp.zeros_like(ms)

    lane = lax.broadcasted_iota(jnp.int32, (1, LANES), 1)
    is_f = jnp.logical_or(jnp.logical_and(lane >= M_HEADS, lane < 2 * M_HEADS),
                          jnp.logical_and(lane >= 3 * M_HEADS, lane < 4 * M_HEADS))
    pos = lax.broadcasted_iota(jnp.int32, (mblk, 1), 0) % L

    def gate_prep(g_ref, x_out, bc_out, forward):
        gx = g_ref[...]
        lsg = jnp.minimum(gx, 0.0) - jnp.log(1.0 + jnp.exp(-jnp.abs(gx)))
        x = jnp.where(is_f, lsg, gx)
        x_out[...] = x
        run = x
        k = 1
        while k < L:
            if forward:
                run = run + jnp.where(pos >= k, pltpu.roll(run, k, 0), 0.0)
            else:
                run = run + jnp.where(pos < L - k, pltpu.roll(run, mblk - k, 0), 0.0)
            k *= 2
        bc_out[...] = run

    gate_prep(gf_ref, xsf, bcf, True)
    gate_prep(gb_ref, xsb, bcb, False)

    pos_f = i * mblk
    pos_b = jnp.where(i < nbc, (nbc - 1 - i) * mblk, lc + (nbl - 1 - (i - nbc)) * mblk)

    r_i = lax.broadcasted_iota(jnp.int32, (L, L), 0)
    c_i = lax.broadcasted_iota(jnp.int32, (L, L), 1)
    eye = r_i == c_i
    masks = (c_i <= r_i, c_i >= r_i)

    def chain_step(c, j, d, m):
        idx = d * M_HEADS + j
        qk_ref, v_ref, x_ref, bc_ref, h_ref, base = (
            (qkf_ref, vf_ref, xsf, bcf, hf, pos_f), (qkb_ref, vb_ref, xsb, bcb, hb, pos_b))[d]
        r0 = pl.multiple_of(c * L, L)
        hs = slice(j * M_DH, (j + 1) * M_DH)
        q = qk_ref[pl.ds(r0, L), hs]
        k = qk_ref[pl.ds(r0, L), BRANCH_W + j * M_DH:BRANCH_W + (j + 1) * M_DH]
        v = v_ref[pl.ds(r0, L), hs]
        xg = x_ref[pl.ds(r0, L), :]
        bg = bc_ref[pl.ds(r0, L), :]
        li = 2 * d * M_HEADS + j
        lf = (2 * d + 1) * M_HEADS + j
        ic = xg[:, li:li + 1]
        a = bg[:, lf:lf + 1]
        b_last = a[L - 1:L, :] if d == 0 else a[0:1, :]
        d_col = ic - a
        d_row = jnp.sum(jnp.where(eye, d_col, 0.0), axis=0, keepdims=True)
        logw = jnp.where(masks[d], a + d_row, -jnp.inf)
        m_loc = jnp.max(logw, axis=1, keepdims=True)
        w = jnp.exp(logw - m_loc) * _dot_nt(q, k)
        num_loc = _dot(w.astype(BF16), v)
        den_loc = jnp.sum(w, axis=1, keepdims=True)
        src = b_last - a + ic
        m_src = jnp.max(src, axis=0, keepdims=True)
        gsc = jnp.exp(src - m_src)
        d_c = _dot_tn(k, (gsc * v.astype(F32)).astype(BF16))
        d_n = jnp.sum(gsc * k.astype(F32), axis=0, keepdims=True)
        nrow = nv[idx]
        inter = _dot(q, ct[idx].astype(BF16))
        qn = jnp.sum(q.astype(F32) * nrow, axis=1, keepdims=True)
        bm = a + m
        m_t = jnp.maximum(bm, m_loc)
        e_loc = jnp.exp(m_loc - m_t)
        e_int = jnp.exp(bm - m_t)
        num = e_loc * num_loc + e_int * inter
        den = e_loc * den_loc + e_int * qn
        h = num / jnp.maximum(jnp.abs(den), jnp.exp(-m_t))
        h_ref[pl.ds(pl.multiple_of(base + r0, L), L), hs] = h
        m_new = jnp.maximum(b_last + m, m_src)
        keep = jnp.exp(b_last + m - m_new)
        scale = jnp.exp(m_src - m_new)
        ct[idx] = keep * ct[idx] + scale * d_c
        nv[idx] = keep * nrow + scale * d_n
        return m_new

    def body(s, mvals):
        out = []
        for d in range(2):
            c = s if d == 0 else cpb - 1 - s
            for j in range(M_HEADS):
                out.append(chain_step(c, j, d, mvals[d * M_HEADS + j]))
        return tuple(out)

    m0 = tuple(ms[q_:q_ + 1, 0:1] for q_ in range(2 * M_HEADS))
    m1 = lax.fori_loop(0, cpb, body, m0)
    for q_ in range(2 * M_HEADS):
        ms[q_:q_ + 1, :] = jnp.broadcast_to(m1[q_], (1, LANES))

    @pl.when(i == nblk - 1)
    def _():
        for blk in range(n // mblk):
            rs = slice(blk * mblk, (blk + 1) * mblk)
            for j in range(M_HEADS):
                hs = slice(j * M_DH, (j + 1) * M_DH)
                hn = _rms(hf[rs, hs] + hb[rs, hs]) * ng_ref[:, hs]
                if blk < nbc:
                    yc_ref[rs, hs] = (hn * _sigmoid(oc_ref[rs, hs].astype(F32))).astype(BF16)
                else:
                    ls = slice(blk * mblk - lc, (blk + 1) * mblk - lc)
                    yl_ref[ls, hs] = (hn * _sigmoid(ol_ref[ls, hs].astype(F32))).astype(BF16)


def _mscan(g, qkc, pm, pg, norm_g):
    mblk = g.mblk
    nbc, nbl = g.lc // mblk, g.t // mblk
    lat_blocks = g.rl // mblk
    n = g.lc + g.t

    def fwd(b, i):
        return jnp.where(i < nbc, lat_blocks + b * nbc + i, b * nbl + (i - nbc))

    def bwd(b, i):
        return jnp.where(i < nbc, lat_blocks + b * nbc + (nbc - 1 - i), b * nbl + (nbl - 1 - (i - nbc)))

    v_col = 2 * BRANCH_W // BRANCH_W
    o_col = 3 * BRANCH_W // BRANCH_W
    ctx0 = g.rl // g.lc

    def side(blk_of):
        return [pl.BlockSpec((mblk, 2 * BRANCH_W), lambda b, i: (blk_of(b, i), 0)),
                pl.BlockSpec((mblk, BRANCH_W), lambda b, i: (blk_of(b, i), v_col)),
                pl.BlockSpec((mblk, LANES), lambda b, i: (blk_of(b, i), 0))]

    return pl.pallas_call(
        functools.partial(_mscan_kernel, lc=g.lc, t=g.t, mblk=mblk),
        grid=(g.b, nbc + nbl),
        in_specs=[
            *side(fwd), *side(bwd),
            pl.BlockSpec((g.t, BRANCH_W), lambda b, i: (b, o_col)),
            pl.BlockSpec((g.lc, BRANCH_W), lambda b, i: (ctx0 + b, o_col)),
            pl.BlockSpec((1, BRANCH_W), lambda b, i: (0, 0)),
        ],
        out_specs=[
            pl.BlockSpec((g.t, BRANCH_W), lambda b, i: (b, 0)),
            pl.BlockSpec((g.lc, BRANCH_W), lambda b, i: (b, 0)),
        ],
        out_shape=[
            jax.ShapeDtypeStruct((g.rl, BRANCH_W), BF16),
            jax.ShapeDtypeStruct((g.rc, BRANCH_W), BF16),
        ],
        scratch_shapes=[
            pltpu.VMEM((mblk, LANES), F32), pltpu.VMEM((mblk, LANES), F32),
            pltpu.VMEM((mblk, LANES), F32), pltpu.VMEM((mblk, LANES), F32),
            pltpu.VMEM((n, BRANCH_W), F32), pltpu.VMEM((n, BRANCH_W), F32),
            pltpu.VMEM((2 * M_HEADS, M_DH, M_DH), F32), pltpu.VMEM((2 * M_HEADS, 1, M_DH), F32),
            pltpu.VMEM((2 * M_HEADS, LANES), F32),
        ],
        compiler_params=_cp("parallel", "arbitrary"),
        name="mlstm_scan",
    )(qkc, pm, pg, qkc, pm, pg, pm, pm, norm_g)


def _mla_prep_kernel(cq_ref, ckv_ref, kr_ref, wuq_ref, wukv_ref, cqg_ref, ckvg_ref, qng_ref, qrg_ref,
                     kng_ref, krg_ref, cos_ref, sin_ref, aq_ref, ak_ref, av_ref, wuq_s, wukv_s):
    @pl.when(pl.program_id(0) == 0)
    def _():
        wuq_s[...] = wuq_ref[...].astype(BF16)
        wukv_s[...] = wukv_ref[...].astype(BF16)

    half = LANES // 2
    lane = lax.broadcasted_iota(jnp.int32, (1, LANES), 1)
    lo = lane < half
    first = (lane % half) < (half // 2)
    cos = cos_ref[...]
    sin = sin_ref[...]

    def rms_half(x):
        x2 = x * x
        s_lo = jnp.sum(jnp.where(lo, x2, 0.0), axis=-1, keepdims=True)
        s_hi = jnp.sum(jnp.where(lo, 0.0, x2), axis=-1, keepdims=True)
        ms = jnp.where(lo, s_lo, s_hi) * (1.0 / half)
        return x * lax.rsqrt(ms + EPS)

    def rope_half(x):
        partner = jnp.where(first, pltpu.roll(x, LANES - half // 2, 1), pltpu.roll(x, half // 2, 1))
        return x * cos + partner * sin

    cq = (_rms(cq_ref[...]) * cqg_ref[...]).astype(BF16)
    q = _dot(cq, wuq_s[...])
    ckv = (_rms(ckv_ref[...]) * ckvg_ref[...]).astype(BF16)
    kv = _dot(ckv, wukv_s[...])

    krn = rope_half(rms_half(kr_ref[...]) * krg_ref[...]).astype(BF16)

    rope0 = MLA_HEADS * MLA_NOPE
    for hp in range(MLA_HEADS // 2):
        slab = q[:, rope0 + hp * LANES: rope0 + (hp + 1) * LANES]
        r = rope_half(rms_half(slab) * qrg_ref[...])
        parts = (jnp.where(lo, r, 0.0), jnp.where(lo, pltpu.roll(r, half, 1), 0.0))
        for e in range(2):
            h = 2 * hp + e
            nope = _rms(q[:, h * MLA_NOPE:(h + 1) * MLA_NOPE]) * qng_ref[...]
            aq_ref[:, h * MLA_HW: h * MLA_HW + MLA_NOPE] = nope.astype(BF16)
            aq_ref[:, h * MLA_HW + MLA_NOPE:(h + 1) * MLA_HW] = parts[e].astype(BF16)
    kvw = MLA_NOPE + MLA_DV
    for h in range(MLA_HEADS):
        kn = _rms(kv[:, h * kvw: h * kvw + MLA_NOPE]) * kng_ref[...]
        ak_ref[:, h * MLA_HW: h * MLA_HW + MLA_NOPE] = kn.astype(BF16)
        ak_ref[:, h * MLA_HW + MLA_NOPE:(h + 1) * MLA_HW] = krn
        av_ref[:, h * MLA_DV:(h + 1) * MLA_DV] = kv[:, h * kvw + MLA_NOPE:(h + 1) * kvw].astype(BF16)


def _mla_prep(g, pc, wuq, wukv, gains, cos_t, sin_t):
    tm = g.tm
    full = lambda shape: pl.BlockSpec(shape, lambda i: (0, 0))
    kr_col = (MLA_Q_RANK + MLA_KV_RANK) // LANES
    return pl.pallas_call(
        _mla_prep_kernel,
        grid=(g.n_tiles,),
        in_specs=[
            pl.BlockSpec((tm, MLA_Q_RANK), lambda i: (i, 0)),
            pl.BlockSpec((tm, MLA_KV_RANK), lambda i: (i, MLA_Q_RANK // MLA_KV_RANK)),
            pl.BlockSpec((tm, LANES), lambda i: (i, kr_col)),
            full(wuq.shape), full(wukv.shape),
            full((1, MLA_Q_RANK)), full((1, MLA_KV_RANK)),
            full((1, LANES)), full((1, LANES)), full((1, LANES)), full((1, LANES)),
            pl.BlockSpec((tm, LANES), lambda i: (g.pos_block(i), 0)),
            pl.BlockSpec((tm, LANES), lambda i: (g.pos_block(i), 0)),
        ],
        out_specs=[
            pl.BlockSpec((tm, MLA_HEADS * MLA_HW), lambda i: (i, 0)),
            pl.BlockSpec((tm, MLA_HEADS * MLA_HW), lambda i: (i, 0)),
            pl.BlockSpec((tm, BRANCH_W), lambda i: (i, 0)),
        ],
        out_shape=[
            jax.ShapeDtypeStruct((g.r, MLA_HEADS * MLA_HW), BF16),
            jax.ShapeDtypeStruct((g.r, MLA_HEADS * MLA_HW), BF16),
            jax.ShapeDtypeStruct((g.r, BRANCH_W), BF16),
        ],
        scratch_shapes=[pltpu.VMEM(wuq.shape, BF16), pltpu.VMEM(wukv.shape, BF16)],
        compiler_params=_cp("arbitrary"),
        name="mla_prep",
    )(pc, pc, pc, wuq, wukv, *gains, cos_t, sin_t)


def _attn_kernel(*refs, n_soft, dh, has_lat, diff, lam_init):
    refs = list(refs)
    vs = refs.pop()
    kts = refs.pop()
    o_ref = refs.pop()
    if diff:
        lam_ref, sg_ref = refs[0], refs[1]
        refs = refs[2:]
    q_ref, kc_ref, vc_ref = refs[:3]
    lc = kc_ref.shape[0]

    @pl.when(pl.program_id(2) == 0)
    def _():
        kts[:, 0:lc] = kc_ref[...].T
        vs[0:lc, :] = vc_ref[...]
        if has_lat:
            kts[:, lc:] = refs[3][...].T
            vs[lc:, :] = refs[4][...]

    q = q_ref[...]
    tq = q.shape[0]
    n_sub = 4 if tq % 64 == 0 else 1
    rsub = tq // n_sub
    outs = []
    for s in range(n_soft):
        sl = slice(s * dh, (s + 1) * dh)
        scs = [_dot(q[u * rsub:(u + 1) * rsub, sl], kts[sl, :]) for u in range(n_sub)]
        ps, ls = [], []
        for sc in scs:
            m = jnp.max(sc, axis=-1, keepdims=True)
            p = jnp.exp(sc - m)
            ls.append(jnp.sum(p, axis=-1, keepdims=True))
            ps.append(p.astype(BF16))
        os_ = [_dot(p, vs[...]) / l for p, l in zip(ps, ls)]
        outs.append(jnp.concatenate(os_, axis=0))
    if diff:
        lp = lam_ref[...]
        lam = (jnp.exp(jnp.sum(lp[0:1] * lp[1:2], axis=-1, keepdims=True))
               - jnp.exp(jnp.sum(lp[2:3] * lp[3:4], axis=-1, keepdims=True)) + lam_init)
        o = outs[0] - lam * outs[1]
        o = _rms(o) * sg_ref[...] * (1.0 - lam_init)
    else:
        o = outs[0]
    o_ref[...] = o.astype(o_ref.dtype)


def _attention(g, q_arr, k_arr, v_arr, *, q_col0, k_col0, heads, qk_w, dv, n_soft, ctx_queries,
               lam=None, subln_g=None, lam_init=0.0):
    diff = lam is not None
    ctx0 = g.rl // g.lc
    n_keys = g.lc if ctx_queries else g.lc + g.t
    if ctx_queries:
        tq = g.lc
        nq = 1
        q_row = lambda b, qi: ctx0 + b
        out_rows = g.rc
        o_row = lambda b, qi: b
    else:
        tq = min(512, g.t)
        nq = g.t // tq
        q_row = lambda b, qi: b * nq + qi
        out_rows = g.rl
        o_row = q_row
    in_specs = []
    args = []
    if diff:
        in_specs += [pl.BlockSpec(lam.shape, lambda b, h, qi: (0, 0)),
                     pl.BlockSpec((1, dv), lambda b, h, qi: (0, 0))]
        args += [lam, subln_g]
    in_specs += [
        pl.BlockSpec((tq, qk_w), lambda b, h, qi: (q_row(b, qi), q_col0 + h)),
        pl.BlockSpec((g.lc, qk_w), lambda b, h, qi: (ctx0 + b, k_col0 + h)),
        pl.BlockSpec((g.lc, dv), lambda b, h, qi: (ctx0 + b, h)),
    ]
    args += [q_arr, k_arr, v_arr]
    if not ctx_queries:
        in_specs += [
            pl.BlockSpec((g.t, qk_w), lambda b, h, qi: (b, k_col0 + h)),
            pl.BlockSpec((g.t, dv), lambda b, h, qi: (b, h)),
        ]
        args += [k_arr, v_arr]
    return pl.pallas_call(
        functools.partial(_attn_kernel, n_soft=n_soft, dh=qk_w // n_soft, has_lat=not ctx_queries,
                          diff=diff, lam_init=lam_init),
        grid=(g.b, heads, nq),
        in_specs=in_specs,
        out_specs=pl.BlockSpec((tq, dv), lambda b, h, qi: (o_row(b, qi), h)),
        out_shape=jax.ShapeDtypeStruct((out_rows, heads * dv), BF16),
        scratch_shapes=[pltpu.VMEM((qk_w, n_keys), BF16), pltpu.VMEM((n_keys, dv), BF16)],
        compiler_params=_cp("parallel", "parallel", "arbitrary"),
        name="attn_diff" if diff else "attn_mla",
    )(*args)


def _branch_kernel(ym_ref, yd_ref, ya_ref, wb_ref, g0_ref, g1_ref, g2_ref, o_ref, wb_s):
    @pl.when(pl.program_id(1) == 0)
    def _():
        wb_s[...] = wb_ref[...].astype(BF16)

    acc = g0_ref[...].astype(F32) * _dot(ym_ref[...], wb_s[0])
    acc = acc + g1_ref[...].astype(F32) * _dot(yd_ref[...], wb_s[1])
    acc = acc + g2_ref[...].astype(F32) * _dot(ya_ref[...], wb_s[2])
    o_ref[...] = acc.astype(o_ref.dtype)


def _branch_merge(g, ym, yd, ya, w_branch, l, gt, n_tiles):
    tn = 512
    nj = g.d // tn
    y_spec = pl.BlockSpec((g.tm, BRANCH_W), lambda j, i: (i, 0))
    return pl.pallas_call(
        _branch_kernel,
        grid=(nj, n_tiles),
        in_specs=[
            y_spec, y_spec, y_spec,
            pl.BlockSpec((None, N_BRANCH, BRANCH_W, tn), lambda j, i: (l, 0, 0, j)),
            pl.BlockSpec((g.tm, tn), lambda j, i: (i, j)),
            pl.BlockSpec((g.tm, tn), lambda j, i: (i, nj + j)),
            pl.BlockSpec((g.tm, tn), lambda j, i: (i, 2 * nj + j)),
        ],
        out_specs=pl.BlockSpec((g.tm, tn), lambda j, i: (i, j)),
        out_shape=jax.ShapeDtypeStruct((n_tiles * g.tm, g.d), BF16),
        scratch_shapes=[pltpu.VMEM((N_BRANCH, BRANCH_W, tn), BF16)],
        compiler_params=_cp("arbitrary", "arbitrary"),
        name="branch_merge",
    )(ym, yd, ya, w_branch, gt, gt, gt)


def _outproj_kernel(z_ref, w_ref, x_ref, g_ref, o_ref, w_s):
    @pl.when(pl.program_id(1) == 0)
    def _():
        w_s[...] = w_ref[...].astype(BF16)

    o_ref[...] = x_ref[...] + g_ref[...] * _dot(z_ref[...], w_s[...])


def _outproj(g, z, w_out, l, x_all, mods3, which_gate, n_tiles):
    tn = 512
    per = g.d // tn
    return pl.pallas_call(
        _outproj_kernel,
        grid=(per, n_tiles),
        in_specs=[
            pl.BlockSpec((g.tm, g.d), lambda j, i: (i, 0)),
            pl.BlockSpec((None, g.d, tn), lambda j, i: (l, 0, j)),
            pl.BlockSpec((g.tm, tn), lambda j, i: (i, j)),
            pl.BlockSpec((None, 1, tn), lambda j, i: (g.mod_row(i), 0, which_gate * per + j)),
        ],
        out_specs=pl.BlockSpec((g.tm, tn), lambda j, i: (i, j)),
        out_shape=jax.ShapeDtypeStruct((n_tiles * g.tm, g.d), F32),
        scratch_shapes=[pltpu.VMEM((g.d, tn), BF16)],
        compiler_params=_cp("arbitrary", "arbitrary"),
        name="outproj",
    )(z, w_out, x_all, mods3)


def _moe_prep_kernel(x_ref, sh_ref, sc_ref, rwt_ref, rb_ref, tri_ref, xr_ref, rank_ref, grp_ref, cnt_ref,
                     carry, *, mod_row):
    i = pl.program_id(0)

    @pl.when(i == 0)
    def _():
        carry[...] = jnp.zeros_like(carry)

    x = x_ref[...]
    h = _rms(x) * (1.0 + sc_ref[...]) + sh_ref[...]
    logits = lax.dot_general(rwt_ref[...], h, (((1,), (1,)), ((), ())),
                             precision=lax.Precision.HIGHEST, preferred_element_type=F32)
    scores = _sigmoid(logits)
    selm = scores + rb_ref[...]
    sc_rows = [scores[e:e + 1, :] for e in range(N_EXPERTS)]
    sel = [selm[e:e + 1, :] for e in range(N_EXPERTS)]
    gscore = []
    for gi in range(N_GROUPS):
        v = sel[gi * GROUP_SIZE:(gi + 1) * GROUP_SIZE]
        pair = [v[a] + v[b] for a in range(GROUP_SIZE) for b in range(a + 1, GROUP_SIZE)]
        gscore.append(functools.reduce(jnp.maximum, pair))
    gmax = functools.reduce(jnp.maximum, gscore)
    taken = jnp.zeros_like(gmax, dtype=jnp.bool_)
    in_group = []
    for gi in range(N_GROUPS):
        hit = jnp.logical_and(gscore[gi] == gmax, jnp.logical_not(taken))
        taken = jnp.logical_or(taken, hit)
        in_group.append(hit)
    neg = -jnp.inf
    masked = [jnp.where(in_group[e // GROUP_SIZE], sel[e], neg) for e in range(N_EXPERTS)]

    def pick(vals):
        mx = functools.reduce(jnp.maximum, vals)
        seen = jnp.zeros_like(mx, dtype=jnp.bool_)
        hot = []
        for v in vals:
            hit = jnp.logical_and(v == mx, jnp.logical_not(seen))
            seen = jnp.logical_or(seen, hit)
            hot.append(hit)
        return hot

    hot1 = pick(masked)
    hot2 = pick([jnp.where(hot1[e], neg, masked[e]) for e in range(N_EXPERTS)])
    zero = jnp.zeros_like(gmax)
    s1 = functools.reduce(lambda a, b: a + b, [jnp.where(hot1[e], sc_rows[e], zero) for e in range(N_EXPERTS)])
    s2 = functools.reduce(lambda a, b: a + b, [jnp.where(hot2[e], sc_rows[e], zero) for e in range(N_EXPERTS)])
    tot = s1 + s2
    w1 = s1 / tot
    w2 = s2 / tot
    comb = [jnp.where(hot1[e], w1, zero) + jnp.where(hot2[e], w2, zero) for e in range(N_EXPERTS)]

    tm = x.shape[0]
    gmat = jnp.concatenate([jnp.where(in_group[gi], 1.0, zero) for gi in range(N_GROUPS)]
                           + [jnp.zeros((MOD_ROWS - N_GROUPS, tm), F32)], axis=0)
    incl = _dot(gmat.astype(BF16), tri_ref[...])
    excl = incl - gmat + carry[:, 0:1]
    rank = functools.reduce(lambda a, b: a + b,
                            [jnp.where(in_group[gi], excl[gi:gi + 1, :], zero) for gi in range(N_GROUPS)])
    grp = functools.reduce(lambda a, b: a + b,
                           [jnp.where(in_group[gi], float(gi), zero) for gi in range(N_GROUPS)])
    rank_ref[...] = rank.astype(jnp.int32)
    grp_ref[...] = grp.astype(jnp.int32)
    carry[...] = carry[...] + jnp.sum(gmat, axis=1, keepdims=True)
    cnt_ref[...] = carry[...]

    side = jnp.concatenate(comb + [jnp.zeros((LANES - N_EXPERTS, tm), F32)], axis=0).T
    lane = lax.broadcasted_iota(jnp.int32, (1, LANES), 1)
    side = jnp.where(lane == N_EXPERTS + mod_row(i), 1.0, side)
    d = x.shape[1]
    xr_ref[:, 0:d] = x
    xr_ref[:, d:d + LANES] = side


def _moe_prep(g, x_mid, mods3, which_shift, which_scale, rwt, rb, tri, n_tiles):
    n_rows = n_tiles * g.tm
    return pl.pallas_call(
        functools.partial(_moe_prep_kernel, mod_row=g.mod_row),
        grid=(n_tiles,),
        in_specs=[
            pl.BlockSpec((g.tm, g.d), lambda i: (i, 0)),
            _mod_spec(g, which_shift),
            _mod_spec(g, which_scale),
            pl.BlockSpec((N_EXPERTS, g.d), lambda i: (0, 0)),
            pl.BlockSpec((N_EXPERTS, 1), lambda i: (0, 0)),
            pl.BlockSpec((g.tm, g.tm), lambda i: (0, 0)),
        ],
        out_specs=[
            pl.BlockSpec((g.tm, g.d + LANES), lambda i: (i, 0)),
            pl.BlockSpec((1, g.tm), lambda i: (0, i)),
            pl.BlockSpec((1, g.tm), lambda i: (0, i)),
            pl.BlockSpec((MOD_ROWS, LANES), lambda i: (0, 0)),
        ],
        out_shape=[
            jax.ShapeDtypeStruct((n_rows, g.d + LANES), F32),
            jax.ShapeDtypeStruct((1, n_rows), jnp.int32),
            jax.ShapeDtypeStruct((1, n_rows), jnp.int32),
            jax.ShapeDtypeStruct((MOD_ROWS, LANES), F32),
        ],
        scratch_shapes=[pltpu.VMEM((MOD_ROWS, LANES), F32)],
        compiler_params=_cp("arbitrary"),
        name="moe_prep",
    )(x_mid, mods3, mods3, rwt, rb, tri)


def _moe_group_kernel(tg_ref, nv_ref, nu_ref, src_ref, xr_hbm, mods_ref, w1_hbm, w3_hbm, w2_hbm, out_hbm,
                      gbuf, obuf, w1s, w3s, w2s, stg_a, stg_b, gsem, ssem, wsem, *, n_mod_rows, e_base):
    k = pl.program_id(0)
    n_used = nu_ref[0]
    slot = k % 2
    d = obuf.shape[2]
    tme = obuf.shape[1]

    def gather_start(kk, sl):
        base = kk * tme

        def issue(r, c):
            tok = src_ref[base + r]
            pltpu.make_async_copy(xr_hbm.at[pl.ds(tok, 1), :], gbuf.at[sl, pl.ds(r, 1), :], gsem.at[sl]).start()
            return c

        lax.fori_loop(0, tme, issue, 0, unroll=8)

    def gather_wait(sl):
        def w(r, c):
            pltpu.make_async_copy(xr_hbm.at[pl.ds(0, 1), :], gbuf.at[sl, pl.ds(0, 1), :], gsem.at[sl]).wait()
            return c

        lax.fori_loop(0, tme, w, 0, unroll=8)

    def scatter_start(kk, sl):
        base = kk * tme

        def issue(r, c):
            tok = src_ref[base + r]
            pltpu.make_async_copy(obuf.at[sl, pl.ds(r, 1), :], out_hbm.at[pl.ds(tok, 1), :], ssem.at[sl]).start()
            return c

        lax.fori_loop(0, nv_ref[kk], issue, 0)

    def scatter_wait(kk, sl):
        def w(r, c):
            pltpu.make_async_copy(obuf.at[sl, pl.ds(0, 1), :], out_hbm.at[pl.ds(0, 1), :], ssem.at[sl]).wait()
            return c

        lax.fori_loop(0, nv_ref[kk], w, 0)

    def load_weights(grp):
        rows_a = stg_a.shape[1]
        rows_b = stg_b.shape[1]
        chunks = []
        n_a = n_b = 0
        for e in range(GROUP_SIZE):
            ex = e_base + grp * GROUP_SIZE + e
            for src, dst in ((w1_hbm, w1s), (w3_hbm, w3s)):
                for c0 in range(0, src.shape[1], rows_a):
                    sl = n_a % 2
                    n_a += 1
                    cp = pltpu.make_async_copy(src.at[ex, pl.ds(c0, rows_a), :], stg_a.at[sl], wsem.at[sl])
                    chunks.append((cp, stg_a.at[sl], dst, e, c0, rows_a))
            for c0 in range(0, w2_hbm.shape[1], rows_b):
                sl = n_b % 2
                n_b += 1
                cp = pltpu.make_async_copy(w2_hbm.at[ex, pl.ds(c0, rows_b), :], stg_b.at[sl], wsem.at[2 + sl])
                chunks.append((cp, stg_b.at[sl], w2s, e, c0, rows_b))
        chunks[0][0].start()
        for idx, (cp, stg, dst, e, c0, rows) in enumerate(chunks):
            if idx + 1 < len(chunks):
                chunks[idx + 1][0].start()
            cp.wait()
            dst[e, c0:c0 + rows, :] = stg[...].astype(BF16)

    @pl.when(k < n_used)
    def _():
        grp = tg_ref[k]

        @pl.when(k == 0)
        def _():
            gather_start(0, 0)

        @pl.when(k + 1 < n_used)
        def _():
            gather_start(k + 1, 1 - slot)

        @pl.when(jnp.logical_or(k == 0, grp != tg_ref[jnp.maximum(k - 1, 0)]))
        def _():
            load_weights(grp)

        gather_wait(slot)
        buf = gbuf[slot]
        x = buf[:, 0:d]
        side = buf[:, d:d + LANES]
        lane = lax.broadcasted_iota(jnp.int32, (1, LANES), 1)

        def cond_vec(which):
            out = jnp.zeros((tme, d), F32)
            for r in range(n_mod_rows):
                hot = side[:, N_EXPERTS + r:N_EXPERTS + r + 1] > 0.5
                out = jnp.where(hot, mods_ref[r:r + 1, which * d:(which + 1) * d], out)
            return out

        h = (_rms(x) * (1.0 + cond_vec(1)) + cond_vec(0)).astype(BF16)
        y = jnp.zeros((tme, d), F32)
        for e in range(GROUP_SIZE):
            a = _dot(h, w1s[e])
            b = _dot(h, w3s[e])
            cw = jnp.sum(jnp.where(lane == grp * GROUP_SIZE + e, side, 0.0), axis=1, keepdims=True)
            hid = (a * _sigmoid(a) * b * cw).astype(BF16)
            y = y + _dot(hid, w2s[e])
        res = x + cond_vec(2) * y

        @pl.when(k >= 2)
        def _():
            scatter_wait(k - 2, slot)

        obuf[slot] = res
        scatter_start(k, slot)

        @pl.when(k == n_used - 1)
        def _():
            @pl.when(k >= 1)
            def _():
                scatter_wait(k - 1, 1 - slot)

            scatter_wait(k, slot)


def _moe_group(g, xr, mods_tail, w1, w3, w2, meta, n_rows, n_mod_rows, e_base):
    tile_group, n_valid, n_used, src = meta
    kt = tile_group.shape[0]
    tme = MOE_TM
    any_spec = pl.BlockSpec(memory_space=pl.ANY)
    grid_spec = pltpu.PrefetchScalarGridSpec(
        num_scalar_prefetch=4,
        grid=(kt,),
        in_specs=[any_spec, pl.BlockSpec(mods_tail.shape, lambda k, *_: (0, 0)), any_spec, any_spec, any_spec],
        out_specs=any_spec,
        scratch_shapes=[
            pltpu.VMEM((2, tme, g.d + LANES), F32),
            pltpu.VMEM((2, tme, g.d), F32),
            pltpu.VMEM((GROUP_SIZE, g.d, D_EXPERT), BF16),
            pltpu.VMEM((GROUP_SIZE, g.d, D_EXPERT), BF16),
            pltpu.VMEM((GROUP_SIZE, D_EXPERT, g.d), BF16),
            pltpu.VMEM((2, g.d // 2, D_EXPERT), F32),
            pltpu.VMEM((2, D_EXPERT // 2, g.d), F32),
            pltpu.SemaphoreType.DMA((2,)),
            pltpu.SemaphoreType.DMA((2,)),
            pltpu.SemaphoreType.DMA((4,)),
        ],
    )
    return pl.pallas_call(
        functools.partial(_moe_group_kernel, n_mod_rows=n_mod_rows, e_base=e_base),
        grid_spec=grid_spec,
        out_shape=jax.ShapeDtypeStruct((n_rows, g.d), F32),
        compiler_params=_cp("arbitrary"),
        name="moe_group",
    )(tile_group, n_valid, n_used, src, xr, mods_tail, w1, w3, w2)


def _moe_meta(rank, grp, cnt, n_rows):
    tme = MOE_TM
    kt = n_rows // tme + N_GROUPS
    counts = cnt[:N_GROUPS, 0].astype(jnp.int32)
    ntile = (counts + tme - 1) // tme
    tile_end = jnp.cumsum(ntile)
    tile_off = tile_end - ntile
    pos = (tile_off * tme)[grp[0]] + rank[0]
    p = kt * tme
    src = jnp.zeros((p,), jnp.int32).at[pos].set(jnp.arange(n_rows, dtype=jnp.int32))
    ks = jnp.arange(kt, dtype=jnp.int32)
    tile_group = jnp.minimum(jnp.searchsorted(tile_end, ks, side="right").astype(jnp.int32), N_GROUPS - 1)
    n_used = tile_end[-1:]
    n_valid = jnp.clip(counts[tile_group] - (ks - tile_off[tile_group]) * tme, 0, tme)
    n_valid = jnp.where(ks < n_used[0], n_valid, 0)
    rows = jnp.arange(p, dtype=jnp.int32)
    in_tile = rows % tme
    valid = in_tile < n_valid[rows // tme]
    src = jnp.where(valid, src, src[rows - in_tile])
    return tile_group, n_valid, n_used, src


def _rope_tables(t, rot_dim, tile_rows, reps):
    rows = t // GRID_W
    r = jnp.repeat(jnp.arange(rows, dtype=F32), GRID_W)
    col = jnp.tile(jnp.arange(GRID_W, dtype=F32), rows)
    n_freq = rot_dim // 4
    inv = ROPE_BASE ** (-jnp.arange(n_freq, dtype=F32) / n_freq)
    ang = jnp.concatenate([r[:, None] * inv, col[:, None] * inv], axis=-1)
    cos = jnp.tile(jnp.concatenate([jnp.cos(ang), jnp.cos(ang)], axis=-1), (1, reps))
    sin = jnp.tile(jnp.concatenate([-jnp.sin(ang), jnp.sin(ang)], axis=-1), (1, reps))
    cos = jnp.concatenate([cos, jnp.ones((tile_rows, LANES), F32)], axis=0)
    sin = jnp.concatenate([sin, jnp.zeros((tile_rows, LANES), F32)], axis=0)
    return cos, sin


def _gate_perm():
    import numpy as np
    nhb = M_HEADS // M_HPB
    perm = np.zeros((nhb, LANES, LANES), np.float32)
    for hb in range(nhb):
        for gi in range(4):
            for j in range(M_HPB):
                perm[hb, gi * M_HEADS + hb * M_HPB + j, gi * M_HPB + j] = 1.0
    return jnp.asarray(perm)


def _uq_perm():
    nope = [h * MLA_QK + i for h in range(MLA_HEADS) for i in range(MLA_NOPE)]
    rope = [h * MLA_QK + MLA_NOPE + i for h in range(MLA_HEADS) for i in range(MLA_ROPE)]
    return jnp.asarray(nope + rope)


def kernel(x, c, ctx, c_ctx, w_ada, b_ada, w_in, b_in, m_conv_w, m_conv_b, m_norm_g, da_q_norm_g, da_k_norm_g,
           da_lambda, da_subln_g, mla_cq_norm_g, mla_ckv_norm_g, mla_w_uq, mla_w_ukv, mla_q_norm_g, mla_k_norm_g,
           w_branch, w_out, moe_w1, moe_w3, moe_w2, router_w, router_bias):
    bsz, t, d = x.shape
    lc = ctx.shape[1]
    depth = w_ada.shape[0]
    g = _Geom(bsz, t, lc, d, min(1024, bsz * lc, t))
    assert bsz + 1 <= MOD_ROWS

    cvec = jnp.concatenate([c, c_ctx[None, :], jnp.zeros((MOD_ROWS - bsz - 1, d), F32)], axis=0)
    mods = _mod_vectors(cvec, w_ada, b_ada)

    cos_d, sin_d = _rope_tables(t, DA_DH, g.tm, 1)
    cos_a, sin_a = _rope_tables(t, MLA_ROPE, g.tm, 2)
    rwt = router_w.T
    rb = router_bias.reshape(N_EXPERTS, 1)
    uq_perm = _uq_perm()
    x_all = jnp.concatenate([x.reshape(g.rl, d), ctx.reshape(g.rc, d)], axis=0)
    pos_spec = pl.BlockSpec((g.tm, LANES), lambda j, i: (g.pos_block(i), 0))
    b_in3 = b_in.reshape(depth, 1, -1)
    w_in_t = jnp.swapaxes(w_in, 1, 2)
    nhb = M_HEADS // M_HPB
    gate_perm = _gate_perm()
    tri = (jnp.arange(g.tm)[:, None] <= jnp.arange(g.tm)[None, :]).astype(BF16)
    w1r = moe_w1.reshape(depth * N_EXPERTS, d, D_EXPERT)
    w3r = moe_w3.reshape(depth * N_EXPERTS, d, D_EXPERT)
    w2r = moe_w2.reshape(depth * N_EXPERTS, D_EXPERT, d)

    for l in range(depth):
        last = l == depth - 1
        n_tiles = g.n_lat_tiles if last else g.n_tiles
        n_rows = n_tiles * g.tm
        lam_init = 0.8 - 0.6 * math.exp(-0.3 * l)
        mods3 = mods[l].reshape(MOD_ROWS, 1, 6 * d)

        h1 = _prenorm(g, x_all, mods3, 0, 1, g.n_tiles)

        row = lambda v: v.reshape(1, -1)
        proj = functools.partial(_proj, g, h=h1, w_in_t=w_in_t, b_in3=b_in3, l=l)
        pm = proj(_epi_raw, col0=OFF_MQK, width=OFF_MG - OFF_MQK, out_dtype=BF16, name="proj_mlstm")
        pg = proj(_epi_gate_perm, col0=OFF_MG, width=LANES, out_dtype=F32, tn=LANES, out_tn=nhb * LANES,
                  extra=(gate_perm,), extra_specs=(pl.BlockSpec(gate_perm.shape, lambda j, i: (0, 0, 0)),),
                  name="proj_mgates")
        da_gain = jnp.concatenate([jnp.tile(da_q_norm_g[l] * DA_DH ** -0.5, 2 * DA_HEADS),
                                   jnp.tile(da_k_norm_g[l], 2 * DA_HEADS)])
        dqk = proj(_epi_normrope, col0=OFF_DQ, width=2 * DA_QK_W, out_dtype=BF16,
                   extra=(row(da_gain), cos_d, sin_d),
                   extra_specs=(pl.BlockSpec((1, 512), lambda j, i: (0, j)), pos_spec, pos_spec),
                   name="proj_dqk")
        dv = proj(_epi_raw, col0=OFF_DV, width=BRANCH_W, out_dtype=BF16, name="proj_dv")
        pc = proj(_epi_raw, col0=OFF_CQ, width=OFF_G - OFF_CQ, out_dtype=F32, name="proj_small")
        gt = proj(_epi_sigmoid, col0=OFF_G, width=N_BRANCH * d, out_dtype=BF16, name="proj_gates")

        ym_l, ym_c = _mlstm(g, pm, pg, m_conv_w[l], row(m_conv_b[l]), row(m_norm_g[l]))

        qg, kg = mla_q_norm_g[l], mla_k_norm_g[l]
        a_scale = MLA_QK ** -0.5
        gains = (row(mla_cq_norm_g[l]), row(mla_ckv_norm_g[l]),
                 row(qg[:MLA_NOPE] * a_scale), row(jnp.tile(qg[MLA_NOPE:], 2) * a_scale),
                 row(kg[:MLA_NOPE]), row(jnp.concatenate([kg[MLA_NOPE:], jnp.zeros((LANES - MLA_ROPE,), F32)])))
        aq, ak, av = _mla_prep(g, pc, mla_w_uq[l][:, uq_perm], mla_w_ukv[l], gains, cos_a, sin_a)

        lam_p = da_lambda[l]
        sub_g = row(da_subln_g[l])
        da_kw = dict(q_col0=0, k_col0=DA_HEADS, heads=DA_HEADS, qk_w=2 * DA_DH, dv=DA_DV, n_soft=2,
                     lam=lam_p, subln_g=sub_g, lam_init=lam_init)
        mla_kw = dict(q_col0=0, k_col0=0, heads=MLA_HEADS, qk_w=MLA_HW, dv=MLA_DV, n_soft=1)
        yd_l = _attention(g, dqk, dqk, dv, ctx_queries=False, **da_kw)
        ya_l = _attention(g, aq, ak, av, ctx_queries=False, **mla_kw)
        if last:
            ym, yd, ya = ym_l, yd_l, ya_l
        else:
            yd_c = _attention(g, dqk, dqk, dv, ctx_queries=True, **da_kw)
            ya_c = _attention(g, aq, ak, av, ctx_queries=True, **mla_kw)
            ym = jnp.concatenate([ym_l, ym_c], axis=0)
            yd = jnp.concatenate([yd_l, yd_c], axis=0)
            ya = jnp.concatenate([ya_l, ya_c], axis=0)

        z = _branch_merge(g, ym, yd, ya, w_branch, l, gt, n_tiles)
        x_mid = _outproj(g, z, w_out, l, x_all, mods3, 2, n_tiles)

        xr, rank, grp, cnt = _moe_prep(g, x_mid, mods3, 3, 4, rwt, rb, tri, n_tiles)
        meta = _moe_meta(rank, grp, cnt, n_rows)
        x_all = _moe_group(g, xr, mods[l][:, 3 * d:], w1r, w3r, w2r, meta, n_rows, bsz + 1, l * N_EXPERTS)

    return x_all.reshape(bsz, t, d)
```

```python
import functools
import math

import jax
import jax.numpy as jnp
import numpy as np
from jax import lax
from jax.experimental import pallas as pl
from jax.experimental.pallas import tpu as pltpu

F32 = jnp.float32
BF16 = jnp.bfloat16

GRID_W = 64
ROPE_BASE = 10000.0
EPS = 1e-6
N_BRANCH = 3
BRANCH_W = 1024
M_HEADS = 4
M_DH = BRANCH_W // M_HEADS
M_CHUNK = 64
M_CONV = 5
DA_HEADS = 4
DA_DV = BRANCH_W // DA_HEADS
DA_DH = DA_DV // 2
DA_QK_W = DA_HEADS * 2 * DA_DH
MLA_HEADS = 8
MLA_Q_RANK = 512
MLA_KV_RANK = 256
MLA_NOPE = 128
MLA_ROPE = 64
MLA_DV = BRANCH_W // MLA_HEADS
MLA_QK = MLA_NOPE + MLA_ROPE
N_EXPERTS = 16
N_GROUPS = 4
GROUP_SIZE = N_EXPERTS // N_GROUPS
D_EXPERT = 512

OFF_MQK = 0
OFF_MV = OFF_MQK + 2 * BRANCH_W
OFF_MO = OFF_MV + BRANCH_W
OFF_MG = OFF_MO + BRANCH_W
OFF_DQ = OFF_MG + 4 * M_HEADS
OFF_DK = OFF_DQ + DA_QK_W
OFF_DV = OFF_DK + DA_QK_W
OFF_CQ = OFF_DV + BRANCH_W
OFF_CKV = OFF_CQ + MLA_Q_RANK
OFF_KR = OFF_CKV + MLA_KV_RANK
OFF_G = OFF_KR + MLA_ROPE

LANES = 128
MOD_ROWS = 8
VMEM_LIMIT = 56 * 1024 * 1024
M_HPB = 2
MLA_HW = 256
MOE_TM = 256
PROJ_ROW_GROUPS = 4
MOE_ROW_GROUPS = 1


def _cp(*sem):
    return pltpu.CompilerParams(dimension_semantics=sem, vmem_limit_bytes=VMEM_LIMIT)


def _rms(x):
    return x * lax.rsqrt(jnp.mean(x * x, axis=-1, keepdims=True) + EPS)


def _sigmoid(x):
    return 0.5 * jnp.tanh(0.5 * x) + 0.5


def _dot(a, b):
    return jnp.dot(a, b, preferred_element_type=F32)


def _dot_nt(a, b):
    return lax.dot_general(a, b, (((1,), (1,)), ((), ())), preferred_element_type=F32)


def _dot_tn(a, b):
    return lax.dot_general(a, b, (((0,), (0,)), ((), ())), preferred_element_type=F32)


def _mod_kernel(c_ref, w_ref, b_ref, o_ref):
    c = c_ref[...]
    s = (c * _sigmoid(c)).astype(BF16)
    o_ref[...] = _dot(s, w_ref[...].astype(BF16)) + b_ref[...]


def _mod_vectors(cvec, w_ada, b_ada):
    depth, d, n = w_ada.shape
    tn = 1024
    return pl.pallas_call(
        _mod_kernel,
        grid=(depth, n // tn),
        in_specs=[
            pl.BlockSpec((MOD_ROWS, d), lambda l, j: (0, 0)),
            pl.BlockSpec((None, d, tn), lambda l, j: (l, 0, j)),
            pl.BlockSpec((None, 1, tn), lambda l, j: (l, 0, j)),
        ],
        out_specs=pl.BlockSpec((None, MOD_ROWS, tn), lambda l, j: (l, 0, j)),
        out_shape=jax.ShapeDtypeStruct((depth, MOD_ROWS, n), F32),
        compiler_params=_cp("parallel", "parallel"),
        name="adaln_mod",
    )(cvec, w_ada, b_ada.reshape(depth, 1, n))


class _Geom:
    def __init__(self, b, t, lc, d, tm):
        assert t % tm == 0 and (b * lc) % tm == 0 and t % lc == 0 and t % GRID_W == 0
        assert lc % M_CHUNK == 0 and t % M_CHUNK == 0
        self.b, self.t, self.lc, self.d, self.tm = b, t, lc, d, tm
        self.rl, self.rc = b * t, b * lc
        self.r = self.rl + self.rc
        self.mblk = min(256, lc)
        assert lc % self.mblk == 0 and t % self.mblk == 0 and self.mblk % M_CHUNK == 0
        self.n_lat_tiles = self.rl // tm
        self.n_tiles = self.r // tm
        self.tiles_per_seq = t // tm

    def mod_row(self, i):
        return jnp.minimum(i // self.tiles_per_seq, self.b)

    def pos_block(self, i):
        return jnp.where(i < self.n_lat_tiles, i % self.tiles_per_seq, self.tiles_per_seq)


def _mod_spec(g, which, width=None, col_of=None):
    width = g.d if width is None else width
    per = g.d // width
    if col_of is None:
        return pl.BlockSpec((None, 1, width), lambda i, *_: (g.mod_row(i), 0, which * per))
    return pl.BlockSpec((None, 1, width), lambda i, j: (g.mod_row(i), 0, which * per + col_of(j)))


def _prenorm_kernel(x_ref, sh_ref, sc_ref, o_ref):
    o_ref[...] = (_rms(x_ref[...]) * (1.0 + sc_ref[...]) + sh_ref[...]).astype(o_ref.dtype)


def _prenorm(g, x_all, mods3, which_shift, which_scale, n_tiles):
    return pl.pallas_call(
        _prenorm_kernel,
        grid=(n_tiles,),
        in_specs=[
            pl.BlockSpec((g.tm, g.d), lambda i: (i, 0)),
            _mod_spec(g, which_shift),
            _mod_spec(g, which_scale),
        ],
        out_specs=pl.BlockSpec((g.tm, g.d), lambda i: (i, 0)),
        out_shape=jax.ShapeDtypeStruct((n_tiles * g.tm, g.d), BF16),
        compiler_params=_cp("parallel"),
        name="prenorm",
    )(x_all, mods3, mods3)


def _epi_raw(acc, rs, o_ref):
    o_ref[rs, :] = acc.astype(o_ref.dtype)


def _epi_sigmoid(acc, rs, o_ref):
    o_ref[rs, :] = _sigmoid(acc).astype(o_ref.dtype)


def _epi_normrope(acc, rs, gain_ref, cos_ref, sin_ref, o_ref):
    cos = cos_ref[rs, :]
    sin = sin_ref[rs, :]
    for j in range(acc.shape[1] // LANES):
        sl = slice(j * LANES, (j + 1) * LANES)
        x = _rms(acc[:, sl]) * gain_ref[:, sl]
        o_ref[rs, sl] = (x * cos + pltpu.roll(x, LANES // 2, 1) * sin).astype(o_ref.dtype)


def _epi_gate_perm(acc, rs, perm_ref, o_ref):
    for hb in range(perm_ref.shape[0]):
        o_ref[rs, hb * LANES:(hb + 1) * LANES] = jnp.dot(
            acc, perm_ref[hb], precision=lax.Precision.HIGHEST, preferred_element_type=F32)


def _proj_kernel(*refs, shift, epilogue):
    if shift:
        h_ref, wa_ref, wb_ref, ba_ref, bb_ref, *rest = refs
    else:
        h_ref, wa_ref, ba_ref, *rest = refs
    *extra, o_ref, w_s, b_s = rest
    tn = w_s.shape[0]

    @pl.when(pl.program_id(1) == 0)
    def _():
        if shift:
            w_s[0:tn - shift, :] = wa_ref[shift:tn, :].astype(BF16)
            w_s[tn - shift:tn, :] = wb_ref[0:shift, :].astype(BF16)
            bb = jnp.concatenate([ba_ref[...], bb_ref[...]], axis=1)
            bb = jnp.broadcast_to(bb, (b_s.shape[0], 2 * tn))
            b_s[...] = pltpu.roll(bb, 2 * tn - shift, 1)[:, :tn]
        else:
            w_s[...] = wa_ref[...].astype(BF16)
            b_s[...] = jnp.broadcast_to(ba_ref[...], b_s.shape)

    rows = h_ref.shape[0] // PROJ_ROW_GROUPS
    for u in range(PROJ_ROW_GROUPS):
        rs = slice(u * rows, (u + 1) * rows)
        acc = _dot_nt(h_ref[rs, :], w_s[...]) + b_s[0:1, :]
        epilogue(acc, rs, *extra, o_ref)


def _proj(g, epilogue, h, w_in_t, b_in3, l, col0, width, out_dtype, *, tn=512, out_tn=None,
          extra=(), extra_specs=(), name="proj"):
    ncols, kdim = w_in_t.shape[1], w_in_t.shape[2]
    blk0 = col0 // tn
    shift = col0 - blk0 * tn
    assert shift % 16 == 0
    nj = pl.cdiv(width, tn)
    last_blk = pl.cdiv(ncols, tn) - 1
    out_tn = tn if out_tn is None else out_tn
    w_specs = [pl.BlockSpec((None, tn, kdim), lambda j, i: (l, blk0 + j, 0))]
    b_specs = [pl.BlockSpec((None, 1, tn), lambda j, i: (l, 0, blk0 + j))]
    if shift:
        nxt = lambda j: jnp.minimum(blk0 + j + 1, last_blk)
        w_specs.append(pl.BlockSpec((None, tn, kdim), lambda j, i: (l, nxt(j), 0)))
        b_specs.append(pl.BlockSpec((None, 1, tn), lambda j, i: (l, 0, nxt(j))))
    n_w = len(w_specs)
    return pl.pallas_call(
        functools.partial(_proj_kernel, shift=shift, epilogue=epilogue),
        grid=(nj, g.n_tiles),
        in_specs=[pl.BlockSpec((g.tm, kdim), lambda j, i: (i, 0)), *w_specs, *b_specs, *extra_specs],
        out_specs=pl.BlockSpec((g.tm, out_tn), lambda j, i: (i, j)),
        out_shape=jax.ShapeDtypeStruct((g.r, nj * out_tn), out_dtype),
        scratch_shapes=[pltpu.VMEM((tn, kdim), BF16), pltpu.VMEM((MOD_ROWS, tn), F32)],
        compiler_params=_cp("arbitrary", "arbitrary"),
        name=name,
    )(h, *([w_in_t] * n_w), *([b_in3] * n_w), *extra)


def _mlstm_kernel(ql_ref, kl_ref, vl_ref, ol_ref, gl_ref, qc_ref, kc_ref, vc_ref, oc_ref, gc_ref,
                  cwq_ref, cwk_ref, cbq_ref, cbk_ref, ng_ref, yl_ref, yc_ref,
                  qs, ks, vs, xs, bcs, hf, hb, ct, nv, *, lc, t):
    n = lc + t
    ncc = lc // M_CHUNK
    nch = n // M_CHUNK
    L = M_CHUNK

    def conv_silu(x_ref, w_ref, b_ref, scale):
        x = x_ref[...].astype(F32)
        rows = x.shape[0]
        row = lax.broadcasted_iota(jnp.int32, (rows, 1), 0)
        acc = x * w_ref[M_CONV // 2:M_CONV // 2 + 1, :] + b_ref[...]
        for j in range(M_CONV):
            s = M_CONV // 2 - j
            if s == 0:
                continue
            xs_ = pltpu.roll(x, s % rows, 0)
            ok = jnp.logical_and(row - s >= 0, row - s < rows)
            acc = acc + jnp.where(ok, xs_, 0.0) * w_ref[j:j + 1, :]
        return (acc * _sigmoid(acc) * scale).astype(BF16)

    qs[0:lc, :] = conv_silu(qc_ref, cwq_ref, cbq_ref, M_DH ** -0.5)
    qs[lc:n, :] = conv_silu(ql_ref, cwq_ref, cbq_ref, M_DH ** -0.5)
    ks[0:lc, :] = conv_silu(kc_ref, cwk_ref, cbk_ref, 1.0)
    ks[lc:n, :] = conv_silu(kl_ref, cwk_ref, cbk_ref, 1.0)
    vs[0:lc, :] = vc_ref[...]
    vs[lc:n, :] = vl_ref[...]

    lane = lax.broadcasted_iota(jnp.int32, (1, LANES), 1)
    is_f = jnp.logical_or(jnp.logical_and(lane >= M_HPB, lane < 2 * M_HPB),
                          jnp.logical_and(lane >= 3 * M_HPB, lane < 4 * M_HPB))

    def gate_prep(g_ref, lo, rows):
        gx = g_ref[...]
        lsg = jnp.minimum(gx, 0.0) - jnp.log(1.0 + jnp.exp(-jnp.abs(gx)))
        x = jnp.where(is_f, lsg, gx)
        xs[lo:lo + rows, :] = x
        pos = lax.broadcasted_iota(jnp.int32, (rows, 1), 0) % L
        pre = x
        suf = x
        k = 1
        while k < L:
            pre = pre + jnp.where(pos >= k, pltpu.roll(pre, k, 0), 0.0)
            suf = suf + jnp.where(pos < L - k, pltpu.roll(suf, rows - k, 0), 0.0)
            k *= 2
        bcs[lo:lo + rows, :] = jnp.where(lane >= 2 * M_HPB, suf, pre)

    gate_prep(gc_ref, 0, lc)
    gate_prep(gl_ref, lc, t)

    ct[...] = jnp.zeros_like(ct)
    nv[...] = jnp.zeros_like(nv)

    r_i = lax.broadcasted_iota(jnp.int32, (L, L), 0)
    c_i = lax.broadcasted_iota(jnp.int32, (L, L), 1)
    eye = r_i == c_i
    masks = (c_i <= r_i, c_i >= r_i)

    def chain_step(c, j, d, m):
        idx = d * M_HPB + j
        r0 = pl.multiple_of(c * L, L)
        hs = slice(j * M_DH, (j + 1) * M_DH)
        q = qs[pl.ds(r0, L), hs]
        k = ks[pl.ds(r0, L), hs]
        v = vs[pl.ds(r0, L), hs]
        xg = xs[pl.ds(r0, L), :]
        bg = bcs[pl.ds(r0, L), :]
        li = 2 * d * M_HPB + j
        lf = (2 * d + 1) * M_HPB + j
        ic = xg[:, li:li + 1]
        a = bg[:, lf:lf + 1]
        b_last = a[L - 1:L, :] if d == 0 else a[0:1, :]
        d_col = ic - a
        d_row = jnp.sum(jnp.where(eye, d_col, 0.0), axis=0, keepdims=True)
        logw = jnp.where(masks[d], a + d_row, -jnp.inf)
        inter = a + m
        m_t = jnp.maximum(inter, jnp.max(logw, axis=1, keepdims=True))
        w = jnp.exp(logw - m_t) * _dot_nt(q, k)
        decay = jnp.exp(inter - m_t)
        ctb = ct[idx].astype(BF16)
        num = _dot(w.astype(BF16), v) + decay * _dot(q, ctb)
        nrow = nv[idx]
        den = jnp.sum(w, axis=1, keepdims=True) + decay * jnp.sum(q.astype(F32) * nrow, axis=1, keepdims=True)
        h = num / jnp.maximum(jnp.abs(den), jnp.exp(-m_t))
        if d == 0:
            hf[pl.ds(r0, L), hs] = h
        else:
            hb[pl.ds(r0, L), hs] = h
        src = b_last - a + ic
        m_new = jnp.maximum(b_last + m, jnp.max(src, axis=0, keepdims=True))
        gsc = jnp.exp(src - m_new)
        keep = jnp.exp(b_last + m - m_new)
        gv = (gsc * v.astype(F32)).astype(BF16)
        ct[idx] = keep * ct[idx] + _dot_tn(k, gv)
        nv[idx] = keep * nrow + jnp.sum(gsc * k.astype(F32), axis=0, keepdims=True)
        return m_new

    def body(i, ms):
        cf = i
        cb = jnp.where(i < ncc, ncc - 1 - i, nch - 1 + ncc - i)
        out = []
        for d in range(2):
            for j in range(M_HPB):
                out.append(chain_step(cf if d == 0 else cb, j, d, ms[d * M_HPB + j]))
        return tuple(out)

    lax.fori_loop(0, nch, body, tuple(jnp.zeros((1, 1), F32) for _ in range(2 * M_HPB)))

    for j in range(M_HPB):
        hs = slice(j * M_DH, (j + 1) * M_DH)
        hn = _rms(hf[:, hs] + hb[:, hs]) * ng_ref[:, hs]
        yc_ref[:, hs] = (hn[0:lc] * _sigmoid(oc_ref[:, hs].astype(F32))).astype(BF16)
        yl_ref[:, hs] = (hn[lc:n] * _sigmoid(ol_ref[:, hs].astype(F32))).astype(BF16)


def _mlstm(g, pm, pg, conv_w, conv_b, norm_g):
    w = M_HPB * M_DH
    nhb = M_HEADS // M_HPB
    cb = BRANCH_W // w
    ctx0 = g.rl // g.lc
    gcol = 0
    n = g.lc + g.t

    def lat(seg):
        return pl.BlockSpec((g.t, w), lambda b, hb: (b, seg * cb + hb))

    def ctx(seg):
        return pl.BlockSpec((g.lc, w), lambda b, hb: (ctx0 + b, seg * cb + hb))

    return pl.pallas_call(
        functools.partial(_mlstm_kernel, lc=g.lc, t=g.t),
        grid=(g.b, nhb),
        in_specs=[
            lat(0), lat(1), lat(2), lat(3),
            pl.BlockSpec((g.t, LANES), lambda b, hb: (b, gcol + hb)),
            ctx(0), ctx(1), ctx(2), ctx(3),
            pl.BlockSpec((g.lc, LANES), lambda b, hb: (ctx0 + b, gcol + hb)),
            pl.BlockSpec((M_CONV, w), lambda b, hb: (0, hb)),
            pl.BlockSpec((M_CONV, w), lambda b, hb: (0, cb + hb)),
            pl.BlockSpec((1, w), lambda b, hb: (0, hb)),
            pl.BlockSpec((1, w), lambda b, hb: (0, cb + hb)),
            pl.BlockSpec((1, w), lambda b, hb: (0, hb)),
        ],
        out_specs=[
            pl.BlockSpec((g.t, w), lambda b, hb: (b, hb)),
            pl.BlockSpec((g.lc, w), lambda b, hb: (b, hb)),
        ],
        out_shape=[
            jax.ShapeDtypeStruct((g.rl, BRANCH_W), BF16),
            jax.ShapeDtypeStruct((g.rc, BRANCH_W), BF16),
        ],
        scratch_shapes=[
            pltpu.VMEM((n, w), BF16), pltpu.VMEM((n, w), BF16), pltpu.VMEM((n, w), BF16),
            pltpu.VMEM((n, LANES), F32), pltpu.VMEM((n, LANES), F32),
            pltpu.VMEM((n, w), F32), pltpu.VMEM((n, w), F32),
            pltpu.VMEM((2 * M_HPB, M_DH, M_DH), F32), pltpu.VMEM((2 * M_HPB, 1, M_DH), F32),
        ],
        compiler_params=_cp("parallel", "parallel"),
        name="mlstm",
    )(pm, pm, pm, pm, pg, pm, pm, pm, pm, pg, conv_w, conv_w, conv_b, conv_b, norm_g)


def _mconv_kernel(x_ref, prev_ref, next_ref, w_ref, b_ref, s_ref, o_ref, xe, *, blocks_per_seq, blocks_per_ctx,
                  n_lat_blocks):
    i = pl.program_id(0)
    rows = x_ref.shape[0]
    halo = prev_ref.shape[0]
    pad = M_CONV // 2
    in_lat = i < n_lat_blocks
    pos = jnp.where(in_lat, i % blocks_per_seq, (i - n_lat_blocks) % blocks_per_ctx)
    per_seq = jnp.where(in_lat, blocks_per_seq, blocks_per_ctx)
    has_prev = pos != 0
    has_next = pos != per_seq - 1
    xe[0:halo, :] = jnp.where(has_prev, prev_ref[...].astype(F32), 0.0)
    xe[halo:halo + rows, :] = x_ref[...].astype(F32)
    xe[halo + rows:halo + rows + halo, :] = jnp.where(has_next, next_ref[...].astype(F32), 0.0)
    acc = jnp.broadcast_to(b_ref[...], (rows, x_ref.shape[1]))
    for j in range(M_CONV):
        acc = acc + xe[halo - pad + j:halo - pad + j + rows, :] * w_ref[j:j + 1, :]
    o_ref[...] = (acc * _sigmoid(acc) * s_ref[...]).astype(o_ref.dtype)


def _mconv(g, pm, conv_w, conv_b, scale_row):
    rows = g.mblk
    halo = 16
    tn = 512
    per = rows // halo
    nblk = g.r // rows
    last = g.r // halo - 1
    return pl.pallas_call(
        functools.partial(_mconv_kernel, blocks_per_seq=g.t // rows, blocks_per_ctx=g.lc // rows,
                          n_lat_blocks=g.rl // rows),
        grid=(nblk, 2 * BRANCH_W // tn),
        in_specs=[
            pl.BlockSpec((rows, tn), lambda i, j: (i, j)),
            pl.BlockSpec((halo, tn), lambda i, j: (jnp.maximum(i * per - 1, 0), j)),
            pl.BlockSpec((halo, tn), lambda i, j: (jnp.minimum((i + 1) * per, last), j)),
            pl.BlockSpec((M_CONV, tn), lambda i, j: (0, j)),
            pl.BlockSpec((1, tn), lambda i, j: (0, j)),
            pl.BlockSpec((1, tn), lambda i, j: (0, j)),
        ],
        out_specs=pl.BlockSpec((rows, tn), lambda i, j: (i, j)),
        out_shape=jax.ShapeDtypeStruct((g.r, 2 * BRANCH_W), BF16),
        scratch_shapes=[pltpu.VMEM((rows + 2 * halo, tn), F32)],
        compiler_params=_cp("parallel", "parallel"),
        name="mlstm_conv",
    )(pm, pm, pm, conv_w, conv_b, scale_row)


def _mscan_kernel(qkf_ref, vf_ref, gf_ref, qkb_ref, vb_ref, gb_ref, ol_ref, oc_ref, ng_ref, yl_ref, yc_ref,
                  xsf, bcf, xsb, bcb, hf, hb, ct, nv, ms, *, lc, t, mblk):
    i = pl.program_id(1)
    nblk = pl.num_programs(1)
    L = M_CHUNK
    cpb = mblk // L
    nbc = lc // mblk
    nbl = t // mblk
    n = lc + t

    @pl.when(i == 0)
    def _():
        ct[...] = jnp.zeros_like(ct)
        nv[...] = jnp.zeros_like(nv)
        ms[...] = jnp.zeros_like(ms)

    lane = lax.broadcasted_iota(jnp.int32, (1, LANES), 1)
    is_f = jnp.logical_or(jnp.logical_and(lane >= M_HEADS, lane < 2 * M_HEADS),
                          jnp.logical_and(lane >= 3 * M_HEADS, lane < 4 * M_HEADS))
    pos = lax.broadcasted_iota(jnp.int32, (mblk, 1), 0) % L

    def gate_prep(g_ref, x_out, bc_out, forward):
        gx = g_ref[...]
        lsg = jnp.minimum(gx, 0.0) - jnp.log(1.0 + jnp.exp(-jnp.abs(gx)))
        x = jnp.where(is_f, lsg, gx)
        x_out[...] = x
        run = x
        k = 1
        while k < L:
            if forward:
                run = run + jnp.where(pos >= k, pltpu.roll(run, k, 0), 0.0)
            else:
                run = run + jnp.where(pos < L - k, pltpu.roll(run, mblk - k, 0), 0.0)
            k *= 2
        bc_out[...] = run

    gate_prep(gf_ref, xsf, bcf, True)
    gate_prep(gb_ref, xsb, bcb, False)

    pos_f = i * mblk
    pos_b = jnp.where(i < nbc, (nbc - 1 - i) * mblk, lc + (nbl - 1 - (i - nbc)) * mblk)

    r_i = lax.broadcasted_iota(jnp.int32, (L, L), 0)
    c_i = lax.broadcasted_iota(jnp.int32, (L, L), 1)
    eye = r_i == c_i
    masks = (c_i <= r_i, c_i >= r_i)

    def chain_step(c, j, d, m):
        idx = d * M_HEADS + j
        qk_ref, v_ref, x_ref, bc_ref, h_ref, base = (
            (qkf_ref, vf_ref, xsf, bcf, hf, pos_f), (qkb_ref, vb_ref, xsb, bcb, hb, pos_b))[d]
        r0 = pl.multiple_of(c * L, L)
        hs = slice(j * M_DH, (j + 1) * M_DH)
        q = qk_ref[pl.ds(r0, L), hs]
        k = qk_ref[pl.ds(r0, L), BRANCH_W + j * M_DH:BRANCH_W + (j + 1) * M_DH]
        v = v_ref[pl.ds(r0, L), hs]
        xg = x_ref[pl.ds(r0, L), :]
        bg = bc_ref[pl.ds(r0, L), :]
        li = 2 * d * M_HEADS + j
        lf = (2 * d + 1) * M_HEADS + j
        ic = xg[:, li:li + 1]
        a = bg[:, lf:lf + 1]
        b_last = a[L - 1:L, :] if d == 0 else a[0:1, :]
        d_col = ic - a
        d_row = jnp.sum(jnp.where(eye, d_col, 0.0), axis=0, keepdims=True)
        logw = jnp.where(masks[d], a + d_row, -jnp.inf)
        m_loc = jnp.max(logw, axis=1, keepdims=True)
        w = jnp.exp(logw - m_loc) * _dot_nt(q, k)
        num_loc = _dot(w.astype(BF16), v)
        den_loc = jnp.sum(w, axis=1, keepdims=True)
        src = b_last - a + ic
        m_src = jnp.max(src, axis=0, keepdims=True)
        gsc = jnp.exp(src - m_src)
        d_c = _dot_tn(k, (gsc * v.astype(F32)).astype(BF16))
        d_n = jnp.sum(gsc * k.astype(F32), axis=0, keepdims=True)
        nrow = nv[idx]
        inter = _dot(q, ct[idx].astype(BF16))
        qn = jnp.sum(q.astype(F32) * nrow, axis=1, keepdims=True)
        bm = a + m
        m_t = jnp.maximum(bm, m_loc)
        e_loc = jnp.exp(m_loc - m_t)
        e_int = jnp.exp(bm - m_t)
        num = e_loc * num_loc + e_int * inter
        den = e_loc * den_loc + e_int * qn
        h = num / jnp.maximum(jnp.abs(den), jnp.exp(-m_t))
        h_ref[pl.ds(pl.multiple_of(base + r0, L), L), hs] = h
        m_new = jnp.maximum(b_last + m, m_src)
        keep = jnp.exp(b_last + m - m_new)
        scale = jnp.exp(m_src - m_new)
        ct[idx] = keep * ct[idx] + scale * d_c
        nv[idx] = keep * nrow + scale * d_n
        return m_new

    def body(s, mvals):
        out = []
        for d in range(2):
            c = s if d == 0 else cpb - 1 - s
            for j in range(M_HEADS):
                out.append(chain_step(c, j, d, mvals[d * M_HEADS + j]))
        return tuple(out)

    m0 = tuple(ms[q_:q_ + 1, 0:1] for q_ in range(2 * M_HEADS))
    m1 = lax.fori_loop(0, cpb, body, m0)
    for q_ in range(2 * M_HEADS):
        ms[q_:q_ + 1, :] = jnp.broadcast_to(m1[q_], (1, LANES))

    @pl.when(i == nblk - 1)
    def _():
        for blk in range(n // mblk):
            rs = slice(blk * mblk, (blk + 1) * mblk)
            for j in range(M_HEADS):
                hs = slice(j * M_DH, (j + 1) * M_DH)
                hn = _rms(hf[rs, hs] + hb[rs, hs]) * ng_ref[:, hs]
                if blk < nbc:
                    yc_ref[rs, hs] = (hn * _sigmoid(oc_ref[rs, hs].astype(F32))).astype(BF16)
                else:
                    ls = slice(blk * mblk - lc, (blk + 1) * mblk - lc)
                    yl_ref[ls, hs] = (hn * _sigmoid(ol_ref[ls, hs].astype(F32))).astype(BF16)


def _mscan(g, qkc, pm, pg, norm_g):
    mblk = g.mblk
    nbc, nbl = g.lc // mblk, g.t // mblk
    lat_blocks = g.rl // mblk
    n = g.lc + g.t

    def fwd(b, i):
        return jnp.where(i < nbc, lat_blocks + b * nbc + i, b * nbl + (i - nbc))

    def bwd(b, i):
        return jnp.where(i < nbc, lat_blocks + b * nbc + (nbc - 1 - i), b * nbl + (nbl - 1 - (i - nbc)))

    v_col = 2 * BRANCH_W // BRANCH_W
    o_col = 3 * BRANCH_W // BRANCH_W
    ctx0 = g.rl // g.lc

    def side(blk_of):
        return [pl.BlockSpec((mblk, 2 * BRANCH_W), lambda b, i: (blk_of(b, i), 0)),
                pl.BlockSpec((mblk, BRANCH_W), lambda b, i: (blk_of(b, i), v_col)),
                pl.BlockSpec((mblk, LANES), lambda b, i: (blk_of(b, i), 0))]

    return pl.pallas_call(
        functools.partial(_mscan_kernel, lc=g.lc, t=g.t, mblk=mblk),
        grid=(g.b, nbc + nbl),
        in_specs=[
            *side(fwd), *side(bwd),
            pl.BlockSpec((g.t, BRANCH_W), lambda b, i: (b, o_col)),
            pl.BlockSpec((g.lc, BRANCH_W), lambda b, i: (ctx0 + b, o_col)),
            pl.BlockSpec((1, BRANCH_W), lambda b, i: (0, 0)),
        ],
        out_specs=[
            pl.BlockSpec((g.t, BRANCH_W), lambda b, i: (b, 0)),
            pl.BlockSpec((g.lc, BRANCH_W), lambda b, i: (b, 0)),
        ],
        out_shape=[
            jax.ShapeDtypeStruct((g.rl, BRANCH_W), BF16),
            jax.ShapeDtypeStruct((g.rc, BRANCH_W), BF16),
        ],
        scratch_shapes=[
            pltpu.VMEM((mblk, LANES), F32), pltpu.VMEM((mblk, LANES), F32),
            pltpu.VMEM((mblk, LANES), F32), pltpu.VMEM((mblk, LANES), F32),
            pltpu.VMEM((n, BRANCH_W), F32), pltpu.VMEM((n, BRANCH_W), F32),
            pltpu.VMEM((2 * M_HEADS, M_DH, M_DH), F32), pltpu.VMEM((2 * M_HEADS, 1, M_DH), F32),
            pltpu.VMEM((2 * M_HEADS, LANES), F32),
        ],
        compiler_params=_cp("parallel", "arbitrary"),
        name="mlstm_scan",
    )(qkc, pm, pg, qkc, pm, pg, pm, pm, norm_g)


def _mla_prep_kernel(cq_ref, ckv_ref, kr_ref, wuq_ref, wukv_ref, cqg_ref, ckvg_ref, qng_ref, qrg_ref,
                     kng_ref, krg_ref, cos_ref, sin_ref, aq_ref, ak_ref, av_ref, wuq_s, wukv_s):
    @pl.when(pl.program_id(0) == 0)
    def _():
        wuq_s[...] = wuq_ref[...].astype(BF16)
        wukv_s[...] = wukv_ref[...].astype(BF16)

    half = LANES // 2
    lane = lax.broadcasted_iota(jnp.int32, (1, LANES), 1)
    lo = lane < half
    first = (lane % half) < (half // 2)

    def rms_half(x):
        x2 = x * x
        s_lo = jnp.sum(jnp.where(lo, x2, 0.0), axis=-1, keepdims=True)
        s_hi = jnp.sum(jnp.where(lo, 0.0, x2), axis=-1, keepdims=True)
        ms = jnp.where(lo, s_lo, s_hi) * (1.0 / half)
        return x * lax.rsqrt(ms + EPS)

    def rows_group(rs):
        cos = cos_ref[rs, :]
        sin = sin_ref[rs, :]

        def rope_half(x):
            partner = jnp.where(first, pltpu.roll(x, LANES - half // 2, 1), pltpu.roll(x, half // 2, 1))
            return x * cos + partner * sin

        cq = (_rms(cq_ref[rs, :]) * cqg_ref[...]).astype(BF16)
        q = _dot(cq, wuq_s[...])
        ckv = (_rms(ckv_ref[rs, :]) * ckvg_ref[...]).astype(BF16)
        kv = _dot(ckv, wukv_s[...])

        krn = rope_half(rms_half(kr_ref[rs, :]) * krg_ref[...]).astype(BF16)

        rope0 = MLA_HEADS * MLA_NOPE
        for hp in range(MLA_HEADS // 2):
            slab = q[:, rope0 + hp * LANES: rope0 + (hp + 1) * LANES]
            r = rope_half(rms_half(slab) * qrg_ref[...])
            parts = (jnp.where(lo, r, 0.0), jnp.where(lo, pltpu.roll(r, half, 1), 0.0))
            for e in range(2):
                h = 2 * hp + e
                nope = _rms(q[:, h * MLA_NOPE:(h + 1) * MLA_NOPE]) * qng_ref[...]
                aq_ref[rs, h * MLA_HW: h * MLA_HW + MLA_NOPE] = nope.astype(BF16)
                aq_ref[rs, h * MLA_HW + MLA_NOPE:(h + 1) * MLA_HW] = parts[e].astype(BF16)
        kvw = MLA_NOPE + MLA_DV
        for h in range(MLA_HEADS):
            kn = _rms(kv[:, h * kvw: h * kvw + MLA_NOPE]) * kng_ref[...]
            ak_ref[rs, h * MLA_HW: h * MLA_HW + MLA_NOPE] = kn.astype(BF16)
            ak_ref[rs, h * MLA_HW + MLA_NOPE:(h + 1) * MLA_HW] = krn
            av_ref[rs, h * MLA_DV:(h + 1) * MLA_DV] = kv[:, h * kvw + MLA_NOPE:(h + 1) * kvw].astype(BF16)

    rows_group(slice(0, cq_ref.shape[0]))


def _mla_prep(g, pc, wuq, wukv, gains, cos_t, sin_t):
    tm = g.tm
    full = lambda shape: pl.BlockSpec(shape, lambda i: (0, 0))
    kr_col = (MLA_Q_RANK + MLA_KV_RANK) // LANES
    return pl.pallas_call(
        _mla_prep_kernel,
        grid=(g.n_tiles,),
        in_specs=[
            pl.BlockSpec((tm, MLA_Q_RANK), lambda i: (i, 0)),
            pl.BlockSpec((tm, MLA_KV_RANK), lambda i: (i, MLA_Q_RANK // MLA_KV_RANK)),
            pl.BlockSpec((tm, LANES), lambda i: (i, kr_col)),
            full(wuq.shape), full(wukv.shape),
            full((1, MLA_Q_RANK)), full((1, MLA_KV_RANK)),
            full((1, LANES)), full((1, LANES)), full((1, LANES)), full((1, LANES)),
            pl.BlockSpec((tm, LANES), lambda i: (g.pos_block(i), 0)),
            pl.BlockSpec((tm, LANES), lambda i: (g.pos_block(i), 0)),
        ],
        out_specs=[
            pl.BlockSpec((tm, MLA_HEADS * MLA_HW), lambda i: (i, 0)),
            pl.BlockSpec((tm, MLA_HEADS * MLA_HW), lambda i: (i, 0)),
            pl.BlockSpec((tm, BRANCH_W), lambda i: (i, 0)),
        ],
        out_shape=[
            jax.ShapeDtypeStruct((g.r, MLA_HEADS * MLA_HW), BF16),
            jax.ShapeDtypeStruct((g.r, MLA_HEADS * MLA_HW), BF16),
            jax.ShapeDtypeStruct((g.r, BRANCH_W), BF16),
        ],
        scratch_shapes=[pltpu.VMEM(wuq.shape, BF16), pltpu.VMEM(wukv.shape, BF16)],
        compiler_params=_cp("arbitrary"),
        name="mla_prep",
    )(pc, pc, pc, wuq, wukv, *gains, cos_t, sin_t)


def _attn_kernel(*refs, n_soft, dh, has_lat, diff, lam_init):
    refs = list(refs)
    vs = refs.pop()
    kts = refs.pop()
    o_ref = refs.pop()
    if diff:
        lam_ref, sg_ref = refs[0], refs[1]
        refs = refs[2:]
    q_ref, kc_ref, vc_ref = refs[:3]
    lc = kc_ref.shape[0]

    @pl.when(pl.program_id(2) == 0)
    def _():
        kts[:, 0:lc] = kc_ref[...].T
        vs[0:lc, :] = vc_ref[...]
        if has_lat:
            kts[:, lc:] = refs[3][...].T
            vs[lc:, :] = refs[4][...]

    q = q_ref[...]
    tq = q.shape[0]
    n_sub = 4 if tq % 64 == 0 else 1
    rsub = tq // n_sub
    outs = []
    for s in range(n_soft):
        sl = slice(s * dh, (s + 1) * dh)
        scs = [_dot(q[u * rsub:(u + 1) * rsub, sl], kts[sl, :]) for u in range(n_sub)]
        ps, ls = [], []
        for sc in scs:
            m = jnp.max(sc, axis=-1, keepdims=True)
            p = jnp.exp(sc - m)
            ls.append(jnp.sum(p, axis=-1, keepdims=True))
            ps.append(p.astype(BF16))
        os_ = [_dot(p, vs[...]) / l for p, l in zip(ps, ls)]
        outs.append(jnp.concatenate(os_, axis=0))
    if diff:
        lp = lam_ref[...]
        lam = (jnp.exp(jnp.sum(lp[0:1] * lp[1:2], axis=-1, keepdims=True))
               - jnp.exp(jnp.sum(lp[2:3] * lp[3:4], axis=-1, keepdims=True)) + lam_init)
        o = outs[0] - lam * outs[1]
        o = _rms(o) * sg_ref[...] * (1.0 - lam_init)
    else:
        o = outs[0]
    o_ref[...] = o.astype(o_ref.dtype)


def _attention(g, q_arr, k_arr, v_arr, *, q_col0, k_col0, heads, qk_w, dv, n_soft, ctx_queries,
               lam=None, subln_g=None, lam_init=0.0):
    diff = lam is not None
    ctx0 = g.rl // g.lc
    n_keys = g.lc if ctx_queries else g.lc + g.t
    if ctx_queries:
        tq = g.lc
        nq = 1
        q_row = lambda b, qi: ctx0 + b
        out_rows = g.rc
        o_row = lambda b, qi: b
    else:
        tq = min(512, g.t)
        nq = g.t // tq
        q_row = lambda b, qi: b * nq + qi
        out_rows = g.rl
        o_row = q_row
    in_specs = []
    args = []
    if diff:
        in_specs += [pl.BlockSpec(lam.shape, lambda b, h, qi: (0, 0)),
                     pl.BlockSpec((1, dv), lambda b, h, qi: (0, 0))]
        args += [lam, subln_g]
    in_specs += [
        pl.BlockSpec((tq, qk_w), lambda b, h, qi: (q_row(b, qi), q_col0 + h)),
        pl.BlockSpec((g.lc, qk_w), lambda b, h, qi: (ctx0 + b, k_col0 + h)),
        pl.BlockSpec((g.lc, dv), lambda b, h, qi: (ctx0 + b, h)),
    ]
    args += [q_arr, k_arr, v_arr]
    if not ctx_queries:
        in_specs += [
            pl.BlockSpec((g.t, qk_w), lambda b, h, qi: (b, k_col0 + h)),
            pl.BlockSpec((g.t, dv), lambda b, h, qi: (b, h)),
        ]
        args += [k_arr, v_arr]
    return pl.pallas_call(
        functools.partial(_attn_kernel, n_soft=n_soft, dh=qk_w // n_soft, has_lat=not ctx_queries,
                          diff=diff, lam_init=lam_init),
        grid=(g.b, heads, nq),
        in_specs=in_specs,
        out_specs=pl.BlockSpec((tq, dv), lambda b, h, qi: (o_row(b, qi), h)),
        out_shape=jax.ShapeDtypeStruct((out_rows, heads * dv), BF16),
        scratch_shapes=[pltpu.VMEM((qk_w, n_keys), BF16), pltpu.VMEM((n_keys, dv), BF16)],
        compiler_params=_cp("parallel", "parallel", "arbitrary"),
        name="attn_diff" if diff else "attn_mla",
    )(*args)


def _branch_kernel(ym_ref, yd_ref, ya_ref, wb_ref, g0_ref, g1_ref, g2_ref, o_ref, wb_s):
    @pl.when(pl.program_id(1) == 0)
    def _():
        wb_s[...] = wb_ref[...].astype(BF16)

    acc = g0_ref[...].astype(F32) * _dot(ym_ref[...], wb_s[0])
    acc = acc + g1_ref[...].astype(F32) * _dot(yd_ref[...], wb_s[1])
    acc = acc + g2_ref[...].astype(F32) * _dot(ya_ref[...], wb_s[2])
    o_ref[...] = acc.astype(o_ref.dtype)


def _branch_merge(g, ym, yd, ya, w_branch, l, gt, n_tiles):
    tn = 512
    nj = g.d // tn
    y_spec = pl.BlockSpec((g.tm, BRANCH_W), lambda j, i: (i, 0))
    return pl.pallas_call(
        _branch_kernel,
        grid=(nj, n_tiles),
        in_specs=[
            y_spec, y_spec, y_spec,
            pl.BlockSpec((None, N_BRANCH, BRANCH_W, tn), lambda j, i: (l, 0, 0, j)),
            pl.BlockSpec((g.tm, tn), lambda j, i: (i, j)),
            pl.BlockSpec((g.tm, tn), lambda j, i: (i, nj + j)),
            pl.BlockSpec((g.tm, tn), lambda j, i: (i, 2 * nj + j)),
        ],
        out_specs=pl.BlockSpec((g.tm, tn), lambda j, i: (i, j)),
        out_shape=jax.ShapeDtypeStruct((n_tiles * g.tm, g.d), BF16),
        scratch_shapes=[pltpu.VMEM((N_BRANCH, BRANCH_W, tn), BF16)],
        compiler_params=_cp("arbitrary", "arbitrary"),
        name="branch_merge",
    )(ym, yd, ya, w_branch, gt, gt, gt)


def _outproj_kernel(z_ref, w_ref, x_ref, g_ref, o_ref, w_s):
    @pl.when(pl.program_id(1) == 0)
    def _():
        w_s[...] = w_ref[...].astype(BF16)

    o_ref[...] = x_ref[...] + g_ref[...] * _dot(z_ref[...], w_s[...])


def _outproj(g, z, w_out, l, x_all, mods3, which_gate, n_tiles):
    tn = 512
    per = g.d // tn
    return pl.pallas_call(
        _outproj_kernel,
        grid=(per, n_tiles),
        in_specs=[
            pl.BlockSpec((g.tm, g.d), lambda j, i: (i, 0)),
            pl.BlockSpec((None, g.d, tn), lambda j, i: (l, 0, j)),
            pl.BlockSpec((g.tm, tn), lambda j, i: (i, j)),
            pl.BlockSpec((None, 1, tn), lambda j, i: (g.mod_row(i), 0, which_gate * per + j)),
        ],
        out_specs=pl.BlockSpec((g.tm, tn), lambda j, i: (i, j)),
        out_shape=jax.ShapeDtypeStruct((n_tiles * g.tm, g.d), F32),
        scratch_shapes=[pltpu.VMEM((g.d, tn), BF16)],
        compiler_params=_cp("arbitrary", "arbitrary"),
        name="outproj",
    )(z, w_out, x_all, mods3)


def _moe_prep_kernel(x_ref, sh_ref, sc_ref, rwt_ref, rb_ref, tri_ref, xr_ref, rank_ref, grp_ref, cnt_ref,
                     carry, *, mod_row):
    i = pl.program_id(0)

    @pl.when(i == 0)
    def _():
        carry[...] = jnp.zeros_like(carry)

    x = x_ref[...]
    h = _rms(x) * (1.0 + sc_ref[...]) + sh_ref[...]
    logits = _dot_nt(rwt_ref[...].astype(BF16), h.astype(BF16))
    scores = _sigmoid(logits)
    selm = scores + rb_ref[...]
    sc_rows = [scores[e:e + 1, :] for e in range(N_EXPERTS)]
    sel = [selm[e:e + 1, :] for e in range(N_EXPERTS)]
    gscore = []
    for gi in range(N_GROUPS):
        v = sel[gi * GROUP_SIZE:(gi + 1) * GROUP_SIZE]
        pair = [v[a] + v[b] for a in range(GROUP_SIZE) for b in range(a + 1, GROUP_SIZE)]
        gscore.append(functools.reduce(jnp.maximum, pair))
    gmax = functools.reduce(jnp.maximum, gscore)
    taken = jnp.zeros_like(gmax, dtype=jnp.bool_)
    in_group = []
    for gi in range(N_GROUPS):
        hit = jnp.logical_and(gscore[gi] == gmax, jnp.logical_not(taken))
        taken = jnp.logical_or(taken, hit)
        in_group.append(hit)
    neg = -jnp.inf
    masked = [jnp.where(in_group[e // GROUP_SIZE], sel[e], neg) for e in range(N_EXPERTS)]

    def pick(vals):
        mx = functools.reduce(jnp.maximum, vals)
        seen = jnp.zeros_like(mx, dtype=jnp.bool_)
        hot = []
        for v in vals:
            hit = jnp.logical_and(v == mx, jnp.logical_not(seen))
            seen = jnp.logical_or(seen, hit)
            hot.append(hit)
        return hot

    hot1 = pick(masked)
    hot2 = pick([jnp.where(hot1[e], neg, masked[e]) for e in range(N_EXPERTS)])
    zero = jnp.zeros_like(gmax)
    s1 = functools.reduce(lambda a, b: a + b, [jnp.where(hot1[e], sc_rows[e], zero) for e in range(N_EXPERTS)])
    s2 = functools.reduce(lambda a, b: a + b, [jnp.where(hot2[e], sc_rows[e], zero) for e in range(N_EXPERTS)])
    tot = s1 + s2
    w1 = s1 / tot
    w2 = s2 / tot
    comb = [jnp.where(hot1[e], w1, zero) + jnp.where(hot2[e], w2, zero) for e in range(N_EXPERTS)]

    tm = x.shape[0]
    gmat = jnp.concatenate([jnp.where(in_group[gi], 1.0, zero) for gi in range(N_GROUPS)]
                           + [jnp.zeros((MOD_ROWS - N_GROUPS, tm), F32)], axis=0)
    incl = _dot(gmat.astype(BF16), tri_ref[...])
    excl = incl - gmat + carry[:, 0:1]
    rank = functools.reduce(lambda a, b: a + b,
                            [jnp.where(in_group[gi], excl[gi:gi + 1, :], zero) for gi in range(N_GROUPS)])
    grp = functools.reduce(lambda a, b: a + b,
                           [jnp.where(in_group[gi], float(gi), zero) for gi in range(N_GROUPS)])
    rank_ref[...] = rank.astype(jnp.int32)
    grp_ref[...] = grp.astype(jnp.int32)
    carry[...] = carry[...] + jnp.sum(gmat, axis=1, keepdims=True)
    cnt_ref[...] = carry[...]

    side = jnp.concatenate(comb + [jnp.zeros((LANES - N_EXPERTS, tm), F32)], axis=0).T
    lane = lax.broadcasted_iota(jnp.int32, (1, LANES), 1)
    side = jnp.where(lane == N_EXPERTS + mod_row(i), 1.0, side)
    d = x.shape[1]
    xr_ref[:, 0:d] = x
    xr_ref[:, d:d + LANES] = side


def _moe_prep(g, x_mid, mods3, which_shift, which_scale, rwt, rb, tri, n_tiles):
    n_rows = n_tiles * g.tm
    return pl.pallas_call(
        functools.partial(_moe_prep_kernel, mod_row=g.mod_row),
        grid=(n_tiles,),
        in_specs=[
            pl.BlockSpec((g.tm, g.d), lambda i: (i, 0)),
            _mod_spec(g, which_shift),
            _mod_spec(g, which_scale),
            pl.BlockSpec((N_EXPERTS, g.d), lambda i: (0, 0)),
            pl.BlockSpec((N_EXPERTS, 1), lambda i: (0, 0)),
            pl.BlockSpec((g.tm, g.tm), lambda i: (0, 0)),
        ],
        out_specs=[
            pl.BlockSpec((g.tm, g.d + LANES), lambda i: (i, 0)),
            pl.BlockSpec((1, g.tm), lambda i: (0, i)),
            pl.BlockSpec((1, g.tm), lambda i: (0, i)),
            pl.BlockSpec((MOD_ROWS, LANES), lambda i: (0, 0)),
        ],
        out_shape=[
            jax.ShapeDtypeStruct((n_rows, g.d + LANES), F32),
            jax.ShapeDtypeStruct((1, n_rows), jnp.int32),
            jax.ShapeDtypeStruct((1, n_rows), jnp.int32),
            jax.ShapeDtypeStruct((MOD_ROWS, LANES), F32),
        ],
        scratch_shapes=[pltpu.VMEM((MOD_ROWS, LANES), F32)],
        compiler_params=_cp("arbitrary"),
        name="moe_prep",
    )(x_mid, mods3, mods3, rwt, rb, tri)


def _moe_group_kernel(tg_ref, nv_ref, nu_ref, src_ref, xr_hbm, mods_ref, w1_hbm, w3_hbm, w2_hbm, out_hbm,
                      gbuf, obuf, w1s, w3s, w2s, stg_a, stg_b, gsem, ssem, wsem, *, n_mod_rows, e_base):
    k = pl.program_id(0)
    n_used = nu_ref[0]
    slot = k % 2
    d = obuf.shape[2]
    tme = obuf.shape[1]

    def gather_start(kk, sl):
        base = kk * tme

        def issue(r, c):
            tok = src_ref[base + r]
            pltpu.make_async_copy(xr_hbm.at[pl.ds(tok, 1), :], gbuf.at[sl, pl.ds(r, 1), :], gsem.at[sl]).start()
            return c

        lax.fori_loop(0, tme, issue, 0, unroll=8)

    def gather_wait(sl):
        def w(r, c):
            pltpu.make_async_copy(xr_hbm.at[pl.ds(0, 1), :], gbuf.at[sl, pl.ds(0, 1), :], gsem.at[sl]).wait()
            return c

        lax.fori_loop(0, tme, w, 0, unroll=8)

    def scatter_start(kk, sl):
        base = kk * tme

        def issue(r, c):
            tok = src_ref[base + r]
            pltpu.make_async_copy(obuf.at[sl, pl.ds(r, 1), :], out_hbm.at[pl.ds(tok, 1), :], ssem.at[sl]).start()
            return c

        rows_of(kk, issue)

    def scatter_wait(kk, sl):
        def w(r, c):
            pltpu.make_async_copy(obuf.at[sl, pl.ds(0, 1), :], out_hbm.at[pl.ds(0, 1), :], ssem.at[sl]).wait()
            return c

        rows_of(kk, w)

    def rows_of(kk, fn):
        n_rows = nv_ref[kk]

        @pl.when(n_rows == tme)
        def _():
            lax.fori_loop(0, tme, fn, 0, unroll=8)

        @pl.when(n_rows != tme)
        def _():
            lax.fori_loop(0, n_rows, fn, 0)

    def load_weights(grp):
        rows_a = stg_a.shape[1]
        rows_b = stg_b.shape[1]
        chunks = []
        n_a = n_b = 0
        for e in range(GROUP_SIZE):
            ex = e_base + grp * GROUP_SIZE + e
            for src, dst in ((w1_hbm, w1s), (w3_hbm, w3s)):
                for c0 in range(0, src.shape[1], rows_a):
                    sl = n_a % 2
                    n_a += 1
                    cp = pltpu.make_async_copy(src.at[ex, pl.ds(c0, rows_a), :], stg_a.at[sl], wsem.at[sl])
                    chunks.append((cp, stg_a.at[sl], dst, e, c0, rows_a))
            for c0 in range(0, w2_hbm.shape[1], rows_b):
                sl = n_b % 2
                n_b += 1
                cp = pltpu.make_async_copy(w2_hbm.at[ex, pl.ds(c0, rows_b), :], stg_b.at[sl], wsem.at[2 + sl])
                chunks.append((cp, stg_b.at[sl], w2s, e, c0, rows_b))
        chunks[0][0].start()
        for idx, (cp, stg, dst, e, c0, rows) in enumerate(chunks):
            if idx + 1 < len(chunks):
                chunks[idx + 1][0].start()
            cp.wait()
            dst[e, c0:c0 + rows, :] = stg[...].astype(BF16)

    @pl.when(k < n_used)
    def _():
        grp = tg_ref[k]

        @pl.when(k == 0)
        def _():
            gather_start(0, 0)

        @pl.when(k + 1 < n_used)
        def _():
            gather_start(k + 1, 1 - slot)

        @pl.when(jnp.logical_or(k == 0, grp != tg_ref[jnp.maximum(k - 1, 0)]))
        def _():
            load_weights(grp)

        gather_wait(slot)

        @pl.when(k >= 2)
        def _():
            scatter_wait(k - 2, slot)

        lane = lax.broadcasted_iota(jnp.int32, (1, LANES), 1)
        rows = tme // MOE_ROW_GROUPS
        for u in range(MOE_ROW_GROUPS):
            rs = slice(u * rows, (u + 1) * rows)
            x = gbuf[slot, rs, 0:d]
            side = gbuf[slot, rs, d:d + LANES]

            def cond_vec(which):
                out = jnp.zeros((rows, d), F32)
                for r in range(n_mod_rows):
                    hot = side[:, N_EXPERTS + r:N_EXPERTS + r + 1] > 0.5
                    out = jnp.where(hot, mods_ref[r:r + 1, which * d:(which + 1) * d], out)
                return out

            h = (_rms(x) * (1.0 + cond_vec(1)) + cond_vec(0)).astype(BF16)
            y = jnp.zeros((rows, d), F32)
            for e in range(GROUP_SIZE):
                a = _dot(h, w1s[e])
                b = _dot(h, w3s[e])
                cw = jnp.sum(jnp.where(lane == grp * GROUP_SIZE + e, side, 0.0), axis=1, keepdims=True)
                hid = (a * _sigmoid(a) * b * cw).astype(BF16)
                y = y + _dot(hid, w2s[e])
            obuf[slot, rs, :] = x + cond_vec(2) * y

        scatter_start(k, slot)

        @pl.when(k == n_used - 1)
        def _():
            @pl.when(k >= 1)
            def _():
                scatter_wait(k - 1, 1 - slot)

            scatter_wait(k, slot)


def _moe_group(g, xr, mods_tail, w1, w3, w2, meta, n_rows, n_mod_rows, e_base):
    tile_group, n_valid, n_used, src = meta
    kt = tile_group.shape[0]
    tme = MOE_TM
    any_spec = pl.BlockSpec(memory_space=pl.ANY)
    grid_spec = pltpu.PrefetchScalarGridSpec(
        num_scalar_prefetch=4,
        grid=(kt,),
        in_specs=[any_spec, pl.BlockSpec(mods_tail.shape, lambda k, *_: (0, 0)), any_spec, any_spec, any_spec],
        out_specs=any_spec,
        scratch_shapes=[
            pltpu.VMEM((2, tme, g.d + LANES), F32),
            pltpu.VMEM((2, tme, g.d), F32),
            pltpu.VMEM((GROUP_SIZE, g.d, D_EXPERT), BF16),
            pltpu.VMEM((GROUP_SIZE, g.d, D_EXPERT), BF16),
            pltpu.VMEM((GROUP_SIZE, D_EXPERT, g.d), BF16),
            pltpu.VMEM((2, g.d // 2, D_EXPERT), F32),
            pltpu.VMEM((2, D_EXPERT // 2, g.d), F32),
            pltpu.SemaphoreType.DMA((2,)),
            pltpu.SemaphoreType.DMA((2,)),
            pltpu.SemaphoreType.DMA((4,)),
        ],
    )
    return pl.pallas_call(
        functools.partial(_moe_group_kernel, n_mod_rows=n_mod_rows, e_base=e_base),
        grid_spec=grid_spec,
        out_shape=jax.ShapeDtypeStruct((n_rows, g.d), F32),
        compiler_params=_cp("arbitrary"),
        name="moe_group",
    )(tile_group, n_valid, n_used, src, xr, mods_tail, w1, w3, w2)


def _moe_meta(rank, grp, cnt, n_rows):
    tme = MOE_TM
    kt = n_rows // tme + N_GROUPS
    counts = cnt[:N_GROUPS, 0].astype(jnp.int32)
    ntile = (counts + tme - 1) // tme
    tile_end = jnp.cumsum(ntile)
    tile_off = tile_end - ntile
    pos = (tile_off * tme)[grp[0]] + rank[0]
    p = kt * tme
    src = jnp.zeros((p,), jnp.int32).at[pos].set(jnp.arange(n_rows, dtype=jnp.int32))
    ks = jnp.arange(kt, dtype=jnp.int32)
    tile_group = jnp.minimum(jnp.searchsorted(tile_end, ks, side="right").astype(jnp.int32), N_GROUPS - 1)
    n_used = tile_end[-1:]
    n_valid = jnp.clip(counts[tile_group] - (ks - tile_off[tile_group]) * tme, 0, tme)
    n_valid = jnp.where(ks < n_used[0], n_valid, 0)
    rows = jnp.arange(p, dtype=jnp.int32)
    in_tile = rows % tme
    valid = in_tile < n_valid[rows // tme]
    src = jnp.where(valid, src, src[rows - in_tile])
    return tile_group, n_valid, n_used, src


def _rope_tables(t, rot_dim, tile_rows, reps):
    f32 = np.float32
    rows = t // GRID_W
    r = np.repeat(np.arange(rows, dtype=f32), GRID_W)
    col = np.tile(np.arange(GRID_W, dtype=f32), rows)
    n_freq = rot_dim // 4
    inv = np.power(f32(ROPE_BASE), -np.arange(n_freq, dtype=f32) / f32(n_freq)).astype(f32)
    ang = np.concatenate([r[:, None] * inv, col[:, None] * inv], axis=-1).astype(f32)
    cos = np.tile(np.concatenate([np.cos(ang), np.cos(ang)], axis=-1), (1, reps))
    sin = np.tile(np.concatenate([-np.sin(ang), np.sin(ang)], axis=-1), (1, reps))
    cos = np.concatenate([cos, np.ones((tile_rows, LANES), f32)], axis=0).astype(f32)
    sin = np.concatenate([sin, np.zeros((tile_rows, LANES), f32)], axis=0).astype(f32)
    return jnp.asarray(cos), jnp.asarray(sin)


def _gate_perm():
    import numpy as np
    nhb = M_HEADS // M_HPB
    perm = np.zeros((nhb, LANES, LANES), np.float32)
    for hb in range(nhb):
        for gi in range(4):
            for j in range(M_HPB):
                perm[hb, gi * M_HEADS + hb * M_HPB + j, gi * M_HPB + j] = 1.0
    return jnp.asarray(perm)


def _uq_perm():
    nope = [h * MLA_QK + i for h in range(MLA_HEADS) for i in range(MLA_NOPE)]
    rope = [h * MLA_QK + MLA_NOPE + i for h in range(MLA_HEADS) for i in range(MLA_ROPE)]
    return jnp.asarray(nope + rope)


def kernel(x, c, ctx, c_ctx, w_ada, b_ada, w_in, b_in, m_conv_w, m_conv_b, m_norm_g, da_q_norm_g, da_k_norm_g,
           da_lambda, da_subln_g, mla_cq_norm_g, mla_ckv_norm_g, mla_w_uq, mla_w_ukv, mla_q_norm_g, mla_k_norm_g,
           w_branch, w_out, moe_w1, moe_w3, moe_w2, router_w, router_bias):
    bsz, t, d = x.shape
    lc = ctx.shape[1]
    depth = w_ada.shape[0]
    g = _Geom(bsz, t, lc, d, min(1024, bsz * lc, t))
    assert bsz + 1 <= MOD_ROWS

    cvec = jnp.concatenate([c, c_ctx[None, :], jnp.zeros((MOD_ROWS - bsz - 1, d), F32)], axis=0)
    mods = _mod_vectors(cvec, w_ada, b_ada)

    cos_d, sin_d = _rope_tables(t, DA_DH, g.tm, 1)
    cos_a, sin_a = _rope_tables(t, MLA_ROPE, g.tm, 2)
    rwt = router_w.T
    rb = router_bias.reshape(N_EXPERTS, 1)
    uq_perm = _uq_perm()
    x_all = jnp.concatenate([x.reshape(g.rl, d), ctx.reshape(g.rc, d)], axis=0)
    pos_spec = pl.BlockSpec((g.tm, LANES), lambda j, i: (g.pos_block(i), 0))
    b_in3 = b_in.reshape(depth, 1, -1)
    w_in_t = jnp.swapaxes(w_in, 1, 2)
    nhb = M_HEADS // M_HPB
    gate_perm = _gate_perm()
    tri = (jnp.arange(g.tm)[:, None] <= jnp.arange(g.tm)[None, :]).astype(BF16)
    w1r = moe_w1.reshape(depth * N_EXPERTS, d, D_EXPERT)
    w3r = moe_w3.reshape(depth * N_EXPERTS, d, D_EXPERT)
    w2r = moe_w2.reshape(depth * N_EXPERTS, D_EXPERT, d)

    for l in range(depth):
        last = l == depth - 1
        n_tiles = g.n_lat_tiles if last else g.n_tiles
        n_rows = n_tiles * g.tm
        lam_init = 0.8 - 0.6 * math.exp(-0.3 * l)
        mods3 = mods[l].reshape(MOD_ROWS, 1, 6 * d)

        h1 = _prenorm(g, x_all, mods3, 0, 1, g.n_tiles)

        row = lambda v: v.reshape(1, -1)
        proj = functools.partial(_proj, g, h=h1, w_in_t=w_in_t, b_in3=b_in3, l=l)
        pm = proj(_epi_raw, col0=OFF_MQK, width=OFF_MG - OFF_MQK, out_dtype=BF16, name="proj_mlstm")
        pg = proj(_epi_gate_perm, col0=OFF_MG, width=LANES, out_dtype=F32, tn=LANES, out_tn=nhb * LANES,
                  extra=(gate_perm,), extra_specs=(pl.BlockSpec(gate_perm.shape, lambda j, i: (0, 0, 0)),),
                  name="proj_mgates")
        da_gain = jnp.concatenate([jnp.tile(da_q_norm_g[l] * DA_DH ** -0.5, 2 * DA_HEADS),
                                   jnp.tile(da_k_norm_g[l], 2 * DA_HEADS)])
        dqk = proj(_epi_normrope, col0=OFF_DQ, width=2 * DA_QK_W, out_dtype=BF16,
                   extra=(row(da_gain), cos_d, sin_d),
                   extra_specs=(pl.BlockSpec((1, 512), lambda j, i: (0, j)), pos_spec, pos_spec),
                   name="proj_dqk")
        dv = proj(_epi_raw, col0=OFF_DV, width=BRANCH_W, out_dtype=BF16, name="proj_dv")
        pc = proj(_epi_raw, col0=OFF_CQ, width=OFF_G - OFF_CQ, out_dtype=F32, name="proj_small")
        gt = proj(_epi_sigmoid, col0=OFF_G, width=N_BRANCH * d, out_dtype=BF16, name="proj_gates")

        ym_l, ym_c = _mlstm(g, pm, pg, m_conv_w[l], row(m_conv_b[l]), row(m_norm_g[l]))

        qg, kg = mla_q_norm_g[l], mla_k_norm_g[l]
        a_scale = MLA_QK ** -0.5
        gains = (row(mla_cq_norm_g[l]), row(mla_ckv_norm_g[l]),
                 row(qg[:MLA_NOPE] * a_scale), row(jnp.tile(qg[MLA_NOPE:], 2) * a_scale),
                 row(kg[:MLA_NOPE]), row(jnp.concatenate([kg[MLA_NOPE:], jnp.zeros((LANES - MLA_ROPE,), F32)])))
        aq, ak, av = _mla_prep(g, pc, mla_w_uq[l][:, uq_perm], mla_w_ukv[l], gains, cos_a, sin_a)

        lam_p = da_lambda[l]
        sub_g = row(da_subln_g[l])
        da_kw = dict(q_col0=0, k_col0=DA_HEADS, heads=DA_HEADS, qk_w=2 * DA_DH, dv=DA_DV, n_soft=2,
                     lam=lam_p, subln_g=sub_g, lam_init=lam_init)
        mla_kw = dict(q_col0=0, k_col0=0, heads=MLA_HEADS, qk_w=MLA_HW, dv=MLA_DV, n_soft=1)
        yd_l = _attention(g, dqk, dqk, dv, ctx_queries=False, **da_kw)
        ya_l = _attention(g, aq, ak, av, ctx_queries=False, **mla_kw)
        if last:
            ym, yd, ya = ym_l, yd_l, ya_l
        else:
            yd_c = _attention(g, dqk, dqk, dv, ctx_queries=True, **da_kw)
            ya_c = _attention(g, aq, ak, av, ctx_queries=True, **mla_kw)
            ym = jnp.concatenate([ym_l, ym_c], axis=0)
            yd = jnp.concatenate([yd_l, yd_c], axis=0)
            ya = jnp.concatenate([ya_l, ya_c], axis=0)

        z = _branch_merge(g, ym, yd, ya, w_branch, l, gt, n_tiles)
        x_mid = _outproj(g, z, w_out, l, x_all, mods3, 2, n_tiles)

        xr, rank, grp, cnt = _moe_prep(g, x_mid, mods3, 3, 4, rwt, rb, tri, n_tiles)
        meta = _moe_meta(rank, grp, cnt, n_rows)
        x_all = _moe_group(g, xr, mods[l][:, 3 * d:], w1r, w3r, w2r, meta, n_rows, bsz + 1, l * N_EXPERTS)

    return x_all.reshape(bsz, t, d)
```

```python
import functools
import math

import jax
import jax.numpy as jnp
import numpy as np
from jax import lax
from jax.experimental import pallas as pl
from jax.experimental.pallas import tpu as pltpu

F32 = jnp.float32
BF16 = jnp.bfloat16

GRID_W = 64
ROPE_BASE = 10000.0
EPS = 1e-6
N_BRANCH = 3
BRANCH_W = 1024
M_HEADS = 4
M_DH = BRANCH_W // M_HEADS
M_CHUNK = 64
M_CONV = 5
DA_HEADS = 4
DA_DV = BRANCH_W // DA_HEADS
DA_DH = DA_DV // 2
DA_QK_W = DA_HEADS * 2 * DA_DH
MLA_HEADS = 8
MLA_Q_RANK = 512
MLA_KV_RANK = 256
MLA_NOPE = 128
MLA_ROPE = 64
MLA_DV = BRANCH_W // MLA_HEADS
MLA_QK = MLA_NOPE + MLA_ROPE
N_EXPERTS = 16
N_GROUPS = 4
GROUP_SIZE = N_EXPERTS // N_GROUPS
D_EXPERT = 512

OFF_MQK = 0
OFF_MV = OFF_MQK + 2 * BRANCH_W
OFF_MO = OFF_MV + BRANCH_W
OFF_MG = OFF_MO + BRANCH_W
OFF_DQ = OFF_MG + 4 * M_HEADS
OFF_DK = OFF_DQ + DA_QK_W
OFF_DV = OFF_DK + DA_QK_W
OFF_CQ = OFF_DV + BRANCH_W
OFF_CKV = OFF_CQ + MLA_Q_RANK
OFF_KR = OFF_CKV + MLA_KV_RANK
OFF_G = OFF_KR + MLA_ROPE

LANES = 128
MOD_ROWS = 8
VMEM_LIMIT = 56 * 1024 * 1024
M_HPB = 2
MLA_HW = 256
MOE_TM = 256
PROJ_ROW_GROUPS = 4
MOE_ROW_GROUPS = 1


def _cp(*sem):
    return pltpu.CompilerParams(dimension_semantics=sem, vmem_limit_bytes=VMEM_LIMIT)


def _rms(x):
    return x * lax.rsqrt(jnp.mean(x * x, axis=-1, keepdims=True) + EPS)


def _sigmoid(x):
    return 0.5 * jnp.tanh(0.5 * x) + 0.5


def _dot(a, b):
    return jnp.dot(a, b, preferred_element_type=F32)


def _dot_nt(a, b):
    return lax.dot_general(a, b, (((1,), (1,)), ((), ())), preferred_element_type=F32)


def _dot_tn(a, b):
    return lax.dot_general(a, b, (((0,), (0,)), ((), ())), preferred_element_type=F32)


def _mod_kernel(c_ref, w_ref, b_ref, o_ref):
    c = c_ref[...]
    s = (c * _sigmoid(c)).astype(BF16)
    o_ref[...] = _dot(s, w_ref[...].astype(BF16)) + b_ref[...]


def _mod_vectors(cvec, w_ada, b_ada):
    depth, d, n = w_ada.shape
    tn = 1024
    return pl.pallas_call(
        _mod_kernel,
        grid=(depth, n // tn),
        in_specs=[
            pl.BlockSpec((MOD_ROWS, d), lambda l, j: (0, 0)),
            pl.BlockSpec((None, d, tn), lambda l, j: (l, 0, j)),
            pl.BlockSpec((None, 1, tn), lambda l, j: (l, 0, j)),
        ],
        out_specs=pl.BlockSpec((None, MOD_ROWS, tn), lambda l, j: (l, 0, j)),
        out_shape=jax.ShapeDtypeStruct((depth, MOD_ROWS, n), F32),
        compiler_params=_cp("parallel", "parallel"),
        name="adaln_mod",
    )(cvec, w_ada, b_ada.reshape(depth, 1, n))


class _Geom:
    def __init__(self, b, t, lc, d, tm):
        assert t % tm == 0 and (b * lc) % tm == 0 and t % lc == 0 and t % GRID_W == 0
        assert lc % M_CHUNK == 0 and t % M_CHUNK == 0
        self.b, self.t, self.lc, self.d, self.tm = b, t, lc, d, tm
        self.rl, self.rc = b * t, b * lc
        self.r = self.rl + self.rc
        self.mblk = min(256, lc)
        assert lc % self.mblk == 0 and t % self.mblk == 0 and self.mblk % M_CHUNK == 0
        self.n_lat_tiles = self.rl // tm
        self.n_tiles = self.r // tm
        self.tiles_per_seq = t // tm

    def mod_row(self, i):
        return jnp.minimum(i // self.tiles_per_seq, self.b)

    def pos_block(self, i):
        return jnp.where(i < self.n_lat_tiles, i % self.tiles_per_seq, self.tiles_per_seq)


def _mod_spec(g, which, width=None, col_of=None):
    width = g.d if width is None else width
    per = g.d // width
    if col_of is None:
        return pl.BlockSpec((None, 1, width), lambda i, *_: (g.mod_row(i), 0, which * per))
    return pl.BlockSpec((None, 1, width), lambda i, j: (g.mod_row(i), 0, which * per + col_of(j)))


def _prenorm_kernel(x_ref, sh_ref, sc_ref, o_ref):
    o_ref[...] = (_rms(x_ref[...]) * (1.0 + sc_ref[...]) + sh_ref[...]).astype(o_ref.dtype)


def _prenorm(g, x_all, mods3, which_shift, which_scale, n_tiles):
    return pl.pallas_call(
        _prenorm_kernel,
        grid=(n_tiles,),
        in_specs=[
            pl.BlockSpec((g.tm, g.d), lambda i: (i, 0)),
            _mod_spec(g, which_shift),
            _mod_spec(g, which_scale),
        ],
        out_specs=pl.BlockSpec((g.tm, g.d), lambda i: (i, 0)),
        out_shape=jax.ShapeDtypeStruct((n_tiles * g.tm, g.d), BF16),
        compiler_params=_cp("parallel"),
        name="prenorm",
    )(x_all, mods3, mods3)


def _epi_raw(acc, rs, o_ref):
    o_ref[rs, :] = acc.astype(o_ref.dtype)


def _epi_sigmoid(acc, rs, o_ref):
    o_ref[rs, :] = _sigmoid(acc).astype(o_ref.dtype)


def _epi_normrope(acc, rs, gain_ref, cos_ref, sin_ref, o_ref):
    cos = cos_ref[rs, :]
    sin = sin_ref[rs, :]
    for j in range(acc.shape[1] // LANES):
        sl = slice(j * LANES, (j + 1) * LANES)
        x = _rms(acc[:, sl]) * gain_ref[:, sl]
        o_ref[rs, sl] = (x * cos + pltpu.roll(x, LANES // 2, 1) * sin).astype(o_ref.dtype)


def _epi_gate_perm(acc, rs, perm_ref, o_ref):
    for hb in range(perm_ref.shape[0]):
        o_ref[rs, hb * LANES:(hb + 1) * LANES] = jnp.dot(
            acc, perm_ref[hb], precision=lax.Precision.HIGHEST, preferred_element_type=F32)


def _proj_kernel(*refs, shift, epilogue):
    if shift:
        h_ref, wa_ref, wb_ref, ba_ref, bb_ref, *rest = refs
    else:
        h_ref, wa_ref, ba_ref, *rest = refs
    *extra, o_ref, w_s, b_s = rest
    tn = w_s.shape[0]

    @pl.when(pl.program_id(1) == 0)
    def _():
        if shift:
            w_s[0:tn - shift, :] = wa_ref[shift:tn, :].astype(BF16)
            w_s[tn - shift:tn, :] = wb_ref[0:shift, :].astype(BF16)
            bb = jnp.concatenate([ba_ref[...], bb_ref[...]], axis=1)
            bb = jnp.broadcast_to(bb, (b_s.shape[0], 2 * tn))
            b_s[...] = pltpu.roll(bb, 2 * tn - shift, 1)[:, :tn]
        else:
            w_s[...] = wa_ref[...].astype(BF16)
            b_s[...] = jnp.broadcast_to(ba_ref[...], b_s.shape)

    rows = h_ref.shape[0] // PROJ_ROW_GROUPS
    for u in range(PROJ_ROW_GROUPS):
        rs = slice(u * rows, (u + 1) * rows)
        acc = _dot_nt(h_ref[rs, :], w_s[...]) + b_s[0:1, :]
        epilogue(acc, rs, *extra, o_ref)


def _proj(g, epilogue, h, w_in_t, b_in3, l, col0, width, out_dtype, *, tn=512, out_tn=None,
          extra=(), extra_specs=(), name="proj"):
    ncols, kdim = w_in_t.shape[1], w_in_t.shape[2]
    blk0 = col0 // tn
    shift = col0 - blk0 * tn
    assert shift % 16 == 0
    nj = pl.cdiv(width, tn)
    last_blk = pl.cdiv(ncols, tn) - 1
    out_tn = tn if out_tn is None else out_tn
    w_specs = [pl.BlockSpec((None, tn, kdim), lambda j, i: (l, blk0 + j, 0))]
    b_specs = [pl.BlockSpec((None, 1, tn), lambda j, i: (l, 0, blk0 + j))]
    if shift:
        nxt = lambda j: jnp.minimum(blk0 + j + 1, last_blk)
        w_specs.append(pl.BlockSpec((None, tn, kdim), lambda j, i: (l, nxt(j), 0)))
        b_specs.append(pl.BlockSpec((None, 1, tn), lambda j, i: (l, 0, nxt(j))))
    n_w = len(w_specs)
    return pl.pallas_call(
        functools.partial(_proj_kernel, shift=shift, epilogue=epilogue),
        grid=(nj, g.n_tiles),
        in_specs=[pl.BlockSpec((g.tm, kdim), lambda j, i: (i, 0)), *w_specs, *b_specs, *extra_specs],
        out_specs=pl.BlockSpec((g.tm, out_tn), lambda j, i: (i, j)),
        out_shape=jax.ShapeDtypeStruct((g.r, nj * out_tn), out_dtype),
        scratch_shapes=[pltpu.VMEM((tn, kdim), BF16), pltpu.VMEM((MOD_ROWS, tn), F32)],
        compiler_params=_cp("arbitrary", "arbitrary"),
        name=name,
    )(h, *([w_in_t] * n_w), *([b_in3] * n_w), *extra)


def _mlstm_kernel(ql_ref, kl_ref, vl_ref, ol_ref, gl_ref, qc_ref, kc_ref, vc_ref, oc_ref, gc_ref,
                  cwq_ref, cwk_ref, cbq_ref, cbk_ref, ng_ref, yl_ref, yc_ref,
                  qs, ks, vs, xs, bcs, hf, hb, ct, nv, *, lc, t):
    n = lc + t
    ncc = lc // M_CHUNK
    nch = n // M_CHUNK
    L = M_CHUNK

    def conv_silu(x_ref, w_ref, b_ref, scale):
        x = x_ref[...].astype(F32)
        rows = x.shape[0]
        row = lax.broadcasted_iota(jnp.int32, (rows, 1), 0)
        acc = x * w_ref[M_CONV // 2:M_CONV // 2 + 1, :] + b_ref[...]
        for j in range(M_CONV):
            s = M_CONV // 2 - j
            if s == 0:
                continue
            xs_ = pltpu.roll(x, s % rows, 0)
            ok = jnp.logical_and(row - s >= 0, row - s < rows)
            acc = acc + jnp.where(ok, xs_, 0.0) * w_ref[j:j + 1, :]
        return (acc * _sigmoid(acc) * scale).astype(BF16)

    qs[0:lc, :] = conv_silu(qc_ref, cwq_ref, cbq_ref, M_DH ** -0.5)
    qs[lc:n, :] = conv_silu(ql_ref, cwq_ref, cbq_ref, M_DH ** -0.5)
    ks[0:lc, :] = conv_silu(kc_ref, cwk_ref, cbk_ref, 1.0)
    ks[lc:n, :] = conv_silu(kl_ref, cwk_ref, cbk_ref, 1.0)
    vs[0:lc, :] = vc_ref[...]
    vs[lc:n, :] = vl_ref[...]

    lane = lax.broadcasted_iota(jnp.int32, (1, LANES), 1)
    is_f = jnp.logical_or(jnp.logical_and(lane >= M_HPB, lane < 2 * M_HPB),
                          jnp.logical_and(lane >= 3 * M_HPB, lane < 4 * M_HPB))

    def gate_prep(g_ref, lo, rows):
        gx = g_ref[...]
        lsg = jnp.minimum(gx, 0.0) - jnp.log(1.0 + jnp.exp(-jnp.abs(gx)))
        x = jnp.where(is_f, lsg, gx)
        xs[lo:lo + rows, :] = x
        pos = lax.broadcasted_iota(jnp.int32, (rows, 1), 0) % L
        pre = x
        suf = x
        k = 1
        while k < L:
            pre = pre + jnp.where(pos >= k, pltpu.roll(pre, k, 0), 0.0)
            suf = suf + jnp.where(pos < L - k, pltpu.roll(suf, rows - k, 0), 0.0)
            k *= 2
        bcs[lo:lo + rows, :] = jnp.where(lane >= 2 * M_HPB, suf, pre)

    gate_prep(gc_ref, 0, lc)
    gate_prep(gl_ref, lc, t)

    ct[...] = jnp.zeros_like(ct)
    nv[...] = jnp.zeros_like(nv)

    r_i = lax.broadcasted_iota(jnp.int32, (L, L), 0)
    c_i = lax.broadcasted_iota(jnp.int32, (L, L), 1)
    eye = r_i == c_i
    masks = (c_i <= r_i, c_i >= r_i)

    def chain_step(c, j, d, m):
        idx = d * M_HPB + j
        r0 = pl.multiple_of(c * L, L)
        hs = slice(j * M_DH, (j + 1) * M_DH)
        q = qs[pl.ds(r0, L), hs]
        k = ks[pl.ds(r0, L), hs]
        v = vs[pl.ds(r0, L), hs]
        xg = xs[pl.ds(r0, L), :]
        bg = bcs[pl.ds(r0, L), :]
        li = 2 * d * M_HPB + j
        lf = (2 * d + 1) * M_HPB + j
        ic = xg[:, li:li + 1]
        a = bg[:, lf:lf + 1]
        b_last = a[L - 1:L, :] if d == 0 else a[0:1, :]
        d_col = ic - a
        d_row = jnp.sum(jnp.where(eye, d_col, 0.0), axis=0, keepdims=True)
        logw = jnp.where(masks[d], a + d_row, -jnp.inf)
        inter = a + m
        m_t = jnp.maximum(inter, jnp.max(logw, axis=1, keepdims=True))
        w = jnp.exp(logw - m_t) * _dot_nt(q, k)
        decay = jnp.exp(inter - m_t)
        ctb = ct[idx].astype(BF16)
        num = _dot(w.astype(BF16), v) + decay * _dot(q, ctb)
        nrow = nv[idx]
        den = jnp.sum(w, axis=1, keepdims=True) + decay * jnp.sum(q.astype(F32) * nrow, axis=1, keepdims=True)
        h = num / jnp.maximum(jnp.abs(den), jnp.exp(-m_t))
        if d == 0:
            hf[pl.ds(r0, L), hs] = h
        else:
            hb[pl.ds(r0, L), hs] = h
        src = b_last - a + ic
        m_new = jnp.maximum(b_last + m, jnp.max(src, axis=0, keepdims=True))
        gsc = jnp.exp(src - m_new)
        keep = jnp.exp(b_last + m - m_new)
        gv = (gsc * v.astype(F32)).astype(BF16)
        ct[idx] = keep * ct[idx] + _dot_tn(k, gv)
        nv[idx] = keep * nrow + jnp.sum(gsc * k.astype(F32), axis=0, keepdims=True)
        return m_new

    def body(i, ms):
        cf = i
        cb = jnp.where(i < ncc, ncc - 1 - i, nch - 1 + ncc - i)
        out = []
        for d in range(2):
            for j in range(M_HPB):
                out.append(chain_step(cf if d == 0 else cb, j, d, ms[d * M_HPB + j]))
        return tuple(out)

    lax.fori_loop(0, nch, body, tuple(jnp.zeros((1, 1), F32) for _ in range(2 * M_HPB)))

    for j in range(M_HPB):
        hs = slice(j * M_DH, (j + 1) * M_DH)
        hn = _rms(hf[:, hs] + hb[:, hs]) * ng_ref[:, hs]
        yc_ref[:, hs] = (hn[0:lc] * _sigmoid(oc_ref[:, hs].astype(F32))).astype(BF16)
        yl_ref[:, hs] = (hn[lc:n] * _sigmoid(ol_ref[:, hs].astype(F32))).astype(BF16)


def _mlstm(g, pm, pg, conv_w, conv_b, norm_g):
    w = M_HPB * M_DH
    nhb = M_HEADS // M_HPB
    cb = BRANCH_W // w
    ctx0 = g.rl // g.lc
    gcol = 0
    n = g.lc + g.t

    def lat(seg):
        return pl.BlockSpec((g.t, w), lambda b, hb: (b, seg * cb + hb))

    def ctx(seg):
        return pl.BlockSpec((g.lc, w), lambda b, hb: (ctx0 + b, seg * cb + hb))

    return pl.pallas_call(
        functools.partial(_mlstm_kernel, lc=g.lc, t=g.t),
        grid=(g.b, nhb),
        in_specs=[
            lat(0), lat(1), lat(2), lat(3),
            pl.BlockSpec((g.t, LANES), lambda b, hb: (b, gcol + hb)),
            ctx(0), ctx(1), ctx(2), ctx(3),
            pl.BlockSpec((g.lc, LANES), lambda b, hb: (ctx0 + b, gcol + hb)),
            pl.BlockSpec((M_CONV, w), lambda b, hb: (0, hb)),
            pl.BlockSpec((M_CONV, w), lambda b, hb: (0, cb + hb)),
            pl.BlockSpec((1, w), lambda b, hb: (0, hb)),
            pl.BlockSpec((1, w), lambda b, hb: (0, cb + hb)),
            pl.BlockSpec((1, w), lambda b, hb: (0, hb)),
        ],
        out_specs=[
            pl.BlockSpec((g.t, w), lambda b, hb: (b, hb)),
            pl.BlockSpec((g.lc, w), lambda b, hb: (b, hb)),
        ],
        out_shape=[
            jax.ShapeDtypeStruct((g.rl, BRANCH_W), BF16),
            jax.ShapeDtypeStruct((g.rc, BRANCH_W), BF16),
        ],
        scratch_shapes=[
            pltpu.VMEM((n, w), BF16), pltpu.VMEM((n, w), BF16), pltpu.VMEM((n, w), BF16),
            pltpu.VMEM((n, LANES), F32), pltpu.VMEM((n, LANES), F32),
            pltpu.VMEM((n, w), F32), pltpu.VMEM((n, w), F32),
            pltpu.VMEM((2 * M_HPB, M_DH, M_DH), F32), pltpu.VMEM((2 * M_HPB, 1, M_DH), F32),
        ],
        compiler_params=_cp("parallel", "parallel"),
        name="mlstm",
    )(pm, pm, pm, pm, pg, pm, pm, pm, pm, pg, conv_w, conv_w, conv_b, conv_b, norm_g)


def _mconv_kernel(x_ref, prev_ref, next_ref, w_ref, b_ref, s_ref, o_ref, xe, *, blocks_per_seq, blocks_per_ctx,
                  n_lat_blocks):
    i = pl.program_id(0)
    rows = x_ref.shape[0]
    halo = prev_ref.shape[0]
    pad = M_CONV // 2
    in_lat = i < n_lat_blocks
    pos = jnp.where(in_lat, i % blocks_per_seq, (i - n_lat_blocks) % blocks_per_ctx)
    per_seq = jnp.where(in_lat, blocks_per_seq, blocks_per_ctx)
    has_prev = pos != 0
    has_next = pos != per_seq - 1
    xe[0:halo, :] = jnp.where(has_prev, prev_ref[...].astype(F32), 0.0)
    xe[halo:halo + rows, :] = x_ref[...].astype(F32)
    xe[halo + rows:halo + rows + halo, :] = jnp.where(has_next, next_ref[...].astype(F32), 0.0)
    acc = jnp.broadcast_to(b_ref[...], (rows, x_ref.shape[1]))
    for j in range(M_CONV):
        acc = acc + xe[halo - pad + j:halo - pad + j + rows, :] * w_ref[j:j + 1, :]
    o_ref[...] = (acc * _sigmoid(acc) * s_ref[...]).astype(o_ref.dtype)


def _mconv(g, pm, conv_w, conv_b, scale_row):
    rows = g.mblk
    halo = 16
    tn = 512
    per = rows // halo
    nblk = g.r // rows
    last = g.r // halo - 1
    return pl.pallas_call(
        functools.partial(_mconv_kernel, blocks_per_seq=g.t // rows, blocks_per_ctx=g.lc // rows,
                          n_lat_blocks=g.rl // rows),
        grid=(nblk, 2 * BRANCH_W // tn),
        in_specs=[
            pl.BlockSpec((rows, tn), lambda i, j: (i, j)),
            pl.BlockSpec((halo, tn), lambda i, j: (jnp.maximum(i * per - 1, 0), j)),
            pl.BlockSpec((halo, tn), lambda i, j: (jnp.minimum((i + 1) * per, last), j)),
            pl.BlockSpec((M_CONV, tn), lambda i, j: (0, j)),
            pl.BlockSpec((1, tn), lambda i, j: (0, j)),
            pl.BlockSpec((1, tn), lambda i, j: (0, j)),
        ],
        out_specs=pl.BlockSpec((rows, tn), lambda i, j: (i, j)),
        out_shape=jax.ShapeDtypeStruct((g.r, 2 * BRANCH_W), BF16),
        scratch_shapes=[pltpu.VMEM((rows + 2 * halo, tn), F32)],
        compiler_params=_cp("parallel", "parallel"),
        name="mlstm_conv",
    )(pm, pm, pm, conv_w, conv_b, scale_row)


def _mscan_kernel(qkf_ref, vf_ref, gf_ref, qkb_ref, vb_ref, gb_ref, ol_ref, oc_ref, ng_ref, yl_ref, yc_ref,
                  xsf, bcf, xsb, bcb, dtf, dtb, hf, hb, ct, nv, ms, *, lc, t, mblk):
    i = pl.program_id(1)
    nblk = pl.num_programs(1)
    L = M_CHUNK
    cpb = mblk // L
    nbc = lc // mblk
    nbl = t // mblk
    n = lc + t

    @pl.when(i == 0)
    def _():
        ct[...] = jnp.zeros_like(ct)
        nv[...] = jnp.zeros_like(nv)
        ms[...] = jnp.zeros_like(ms)

    lane = lax.broadcasted_iota(jnp.int32, (1, LANES), 1)
    is_f = jnp.logical_or(jnp.logical_and(lane >= M_HEADS, lane < 2 * M_HEADS),
                          jnp.logical_and(lane >= 3 * M_HEADS, lane < 4 * M_HEADS))
    pos = lax.broadcasted_iota(jnp.int32, (mblk, 1), 0) % L

    def gate_prep(g_ref, x_out, bc_out, forward):
        gx = g_ref[...]
        lsg = jnp.minimum(gx, 0.0) - jnp.log(1.0 + jnp.exp(-jnp.abs(gx)))
        x = jnp.where(is_f, lsg, gx)
        x_out[...] = x
        run = x
        k = 1
        while k < L:
            if forward:
                run = run + jnp.where(pos >= k, pltpu.roll(run, k, 0), 0.0)
            else:
                run = run + jnp.where(pos < L - k, pltpu.roll(run, mblk - k, 0), 0.0)
            k *= 2
        bc_out[...] = run

    gate_prep(gf_ref, xsf, bcf, True)
    gate_prep(gb_ref, xsb, bcb, False)

    pos_f = i * mblk
    pos_b = jnp.where(i < nbc, (nbc - 1 - i) * mblk, lc + (nbl - 1 - (i - nbc)) * mblk)

    def row_table(x_ref, bc_ref, out_ref):
        dm = x_ref[...] - pltpu.roll(bc_ref[...], LANES - M_HEADS, 1)
        for c in range(cpb):
            out_ref[c] = dm[c * L:(c + 1) * L, :].T

    row_table(xsf, bcf, dtf)
    row_table(xsb, bcb, dtb)

    H = M_HEADS
    r_i = lax.broadcasted_iota(jnp.int32, (H * L, L), 0) % L
    c_i = lax.broadcasted_iota(jnp.int32, (H * L, L), 1)
    masks = (c_i <= r_i, c_i >= r_i)

    def stack(parts):
        return jnp.concatenate(parts, axis=0)

    def dir_step(c, d, mvals):
        qk_ref, v_ref, x_ref, bc_ref, dt_ref, h_ref, base = (
            (qkf_ref, vf_ref, xsf, bcf, dtf, hf, pos_f), (qkb_ref, vb_ref, xsb, bcb, dtb, hb, pos_b))[d]
        r0 = pl.multiple_of(c * L, L)
        xg = x_ref[pl.ds(r0, L), :]
        bg = bc_ref[pl.ds(r0, L), :]
        dt = dt_ref[c]
        qs = [qk_ref[pl.ds(r0, L), j * M_DH:(j + 1) * M_DH] for j in range(H)]
        ks = [qk_ref[pl.ds(r0, L), BRANCH_W + j * M_DH:BRANCH_W + (j + 1) * M_DH] for j in range(H)]
        vs_ = [v_ref[pl.ds(r0, L), j * M_DH:(j + 1) * M_DH] for j in range(H)]
        li0 = 2 * d * H
        lf0 = (2 * d + 1) * H
        a_h = [bg[:, lf0 + j:lf0 + j + 1] for j in range(H)]
        bl_h = [a[L - 1:L, :] if d == 0 else a[0:1, :] for a in a_h]
        a4 = stack(a_h)
        d_rows = [dt[li0 + j:li0 + j + 1, :] for j in range(H)]
        d4 = stack([jnp.broadcast_to(r_, (L, L)) for r_ in d_rows])
        s4 = stack([_dot_nt(qs[j], ks[j]) for j in range(H)])
        logw = jnp.where(masks[d], a4 + d4, -jnp.inf)
        m_loc = jnp.max(logw, axis=1, keepdims=True)
        w4b = (jnp.exp(logw - m_loc) * s4).astype(BF16)
        den_loc = _dot(w4b, jnp.ones((L, LANES), BF16))[:, 0:1]
        num_loc = [_dot(w4b[j * L:(j + 1) * L, :], vs_[j]) for j in range(H)]
        s_rows = [bl_h[j] + d_rows[j] for j in range(H)]
        msrc_h = [jnp.max(s_, axis=1, keepdims=True) for s_ in s_rows]
        g_rows = [jnp.exp(s_rows[j] - msrc_h[j]) for j in range(H)]
        k_t = [ks[j].astype(F32).T for j in range(H)]
        d_c = [_dot((k_t[j] * g_rows[j]).astype(BF16), vs_[j]) for j in range(H)]
        d_n = [_dot(jnp.broadcast_to(g_rows[j], (MOD_ROWS, L)).astype(BF16), ks[j])[0:1, :] for j in range(H)]
        idx = [d * H + j for j in range(H)]
        nrow = [nv[i_] for i_ in idx]
        inter = [_dot(qs[j], ct[idx[j]].astype(BF16)) for j in range(H)]
        qn4 = stack([_dot_nt(qs[j], jnp.broadcast_to(nrow[j], (MOD_ROWS, M_DH)).astype(BF16))[:, 0:1]
                     for j in range(H)])
        m4 = stack([jnp.broadcast_to(mvals[idx[j]], (L, 1)) for j in range(H)])
        bm4 = a4 + m4
        m_t = jnp.maximum(bm4, m_loc)
        e_loc = jnp.exp(m_loc - m_t)
        e_int = jnp.exp(bm4 - m_t)
        den4 = e_loc * den_loc + e_int * qn4
        r4 = 1.0 / jnp.maximum(jnp.abs(den4), jnp.exp(-m_t))
        el = e_loc * r4
        ei = e_int * r4
        rows_out = pl.ds(pl.multiple_of(base + r0, L), L)
        out = list(mvals)
        for j in range(H):
            rs = slice(j * L, (j + 1) * L)
            h_ref[rows_out, j * M_DH:(j + 1) * M_DH] = el[rs] * num_loc[j] + ei[rs] * inter[j]
            m_old = mvals[idx[j]]
            m_new = jnp.maximum(bl_h[j] + m_old, msrc_h[j])
            keep = jnp.exp(bl_h[j] + m_old - m_new)
            scale = jnp.exp(msrc_h[j] - m_new)
            ct[idx[j]] = keep * ct[idx[j]] + scale * d_c[j]
            nv[idx[j]] = keep * nrow[j] + scale * d_n[j]
            out[idx[j]] = m_new
        return tuple(out)

    def body(s, mvals):
        mvals = dir_step(s, 0, mvals)
        return dir_step(cpb - 1 - s, 1, mvals)

    m0 = tuple(ms[q_:q_ + 1, 0:1] for q_ in range(2 * M_HEADS))
    m1 = lax.fori_loop(0, cpb, body, m0)
    for q_ in range(2 * M_HEADS):
        ms[q_:q_ + 1, :] = jnp.broadcast_to(m1[q_], (1, LANES))

    @pl.when(i == nblk - 1)
    def _():
        for blk in range(n // mblk):
            rs = slice(blk * mblk, (blk + 1) * mblk)
            for j in range(M_HEADS):
                hs = slice(j * M_DH, (j + 1) * M_DH)
                hn = _rms(hf[rs, hs] + hb[rs, hs]) * ng_ref[:, hs]
                if blk < nbc:
                    yc_ref[rs, hs] = (hn * _sigmoid(oc_ref[rs, hs].astype(F32))).astype(BF16)
                else:
                    ls = slice(blk * mblk - lc, (blk + 1) * mblk - lc)
                    yl_ref[ls, hs] = (hn * _sigmoid(ol_ref[ls, hs].astype(F32))).astype(BF16)


def _mscan(g, qkc, pm, pg, norm_g):
    mblk = g.mblk
    nbc, nbl = g.lc // mblk, g.t // mblk
    lat_blocks = g.rl // mblk
    n = g.lc + g.t

    def fwd(b, i):
        return jnp.where(i < nbc, lat_blocks + b * nbc + i, b * nbl + (i - nbc))

    def bwd(b, i):
        return jnp.where(i < nbc, lat_blocks + b * nbc + (nbc - 1 - i), b * nbl + (nbl - 1 - (i - nbc)))

    v_col = 2 * BRANCH_W // BRANCH_W
    o_col = 3 * BRANCH_W // BRANCH_W
    ctx0 = g.rl // g.lc

    def side(blk_of):
        return [pl.BlockSpec((mblk, 2 * BRANCH_W), lambda b, i: (blk_of(b, i), 0)),
                pl.BlockSpec((mblk, BRANCH_W), lambda b, i: (blk_of(b, i), v_col)),
                pl.BlockSpec((mblk, LANES), lambda b, i: (blk_of(b, i), 0))]

    return pl.pallas_call(
        functools.partial(_mscan_kernel, lc=g.lc, t=g.t, mblk=mblk),
        grid=(g.b, nbc + nbl),
        in_specs=[
            *side(fwd), *side(bwd),
            pl.BlockSpec((g.t, BRANCH_W), lambda b, i: (b, o_col)),
            pl.BlockSpec((g.lc, BRANCH_W), lambda b, i: (ctx0 + b, o_col)),
            pl.BlockSpec((1, BRANCH_W), lambda b, i: (0, 0)),
        ],
        out_specs=[
            pl.BlockSpec((g.t, BRANCH_W), lambda b, i: (b, 0)),
            pl.BlockSpec((g.lc, BRANCH_W), lambda b, i: (b, 0)),
        ],
        out_shape=[
            jax.ShapeDtypeStruct((g.rl, BRANCH_W), BF16),
            jax.ShapeDtypeStruct((g.rc, BRANCH_W), BF16),
        ],
        scratch_shapes=[
            pltpu.VMEM((mblk, LANES), F32), pltpu.VMEM((mblk, LANES), F32),
            pltpu.VMEM((mblk, LANES), F32), pltpu.VMEM((mblk, LANES), F32),
            pltpu.VMEM((mblk // M_CHUNK, LANES, M_CHUNK), F32), pltpu.VMEM((mblk // M_CHUNK, LANES, M_CHUNK), F32),
            pltpu.VMEM((n, BRANCH_W), F32), pltpu.VMEM((n, BRANCH_W), F32),
            pltpu.VMEM((2 * M_HEADS, M_DH, M_DH), F32), pltpu.VMEM((2 * M_HEADS, 1, M_DH), F32),
            pltpu.VMEM((2 * M_HEADS, LANES), F32),
        ],
        compiler_params=_cp("parallel", "arbitrary"),
        name="mlstm_scan",
    )(qkc, pm, pg, qkc, pm, pg, pm, pm, norm_g)


def _mla_prep_kernel(cq_ref, ckv_ref, kr_ref, wuq_ref, wukv_ref, cqg_ref, ckvg_ref, qng_ref, qrg_ref,
                     kng_ref, krg_ref, cos_ref, sin_ref, aq_ref, ak_ref, av_ref, wuq_s, wukv_s):
    @pl.when(pl.program_id(0) == 0)
    def _():
        wuq_s[...] = wuq_ref[...].astype(BF16)
        wukv_s[...] = wukv_ref[...].astype(BF16)

    half = LANES // 2
    lane = lax.broadcasted_iota(jnp.int32, (1, LANES), 1)
    lo = lane < half
    first = (lane % half) < (half // 2)

    def rms_half(x):
        x2 = x * x
        s_lo = jnp.sum(jnp.where(lo, x2, 0.0), axis=-1, keepdims=True)
        s_hi = jnp.sum(jnp.where(lo, 0.0, x2), axis=-1, keepdims=True)
        ms = jnp.where(lo, s_lo, s_hi) * (1.0 / half)
        return x * lax.rsqrt(ms + EPS)

    def rows_group(rs):
        cos = cos_ref[rs, :]
        sin = sin_ref[rs, :]

        def rope_half(x):
            partner = jnp.where(first, pltpu.roll(x, LANES - half // 2, 1), pltpu.roll(x, half // 2, 1))
            return x * cos + partner * sin

        cq = (_rms(cq_ref[rs, :]) * cqg_ref[...]).astype(BF16)
        q = _dot(cq, wuq_s[...])
        ckv = (_rms(ckv_ref[rs, :]) * ckvg_ref[...]).astype(BF16)
        kv = _dot(ckv, wukv_s[...])

        krn = rope_half(rms_half(kr_ref[rs, :]) * krg_ref[...]).astype(BF16)

        rope0 = MLA_HEADS * MLA_NOPE
        for hp in range(MLA_HEADS // 2):
            slab = q[:, rope0 + hp * LANES: rope0 + (hp + 1) * LANES]
            r = rope_half(rms_half(slab) * qrg_ref[...])
            parts = (jnp.where(lo, r, 0.0), jnp.where(lo, pltpu.roll(r, half, 1), 0.0))
            for e in range(2):
                h = 2 * hp + e
                nope = _rms(q[:, h * MLA_NOPE:(h + 1) * MLA_NOPE]) * qng_ref[...]
                aq_ref[rs, h * MLA_HW: h * MLA_HW + MLA_NOPE] = nope.astype(BF16)
                aq_ref[rs, h * MLA_HW + MLA_NOPE:(h + 1) * MLA_HW] = parts[e].astype(BF16)
        kvw = MLA_NOPE + MLA_DV
        for h in range(MLA_HEADS):
            kn = _rms(kv[:, h * kvw: h * kvw + MLA_NOPE]) * kng_ref[...]
            ak_ref[rs, h * MLA_HW: h * MLA_HW + MLA_NOPE] = kn.astype(BF16)
            ak_ref[rs, h * MLA_HW + MLA_NOPE:(h + 1) * MLA_HW] = krn
            av_ref[rs, h * MLA_DV:(h + 1) * MLA_DV] = kv[:, h * kvw + MLA_NOPE:(h + 1) * kvw].astype(BF16)

    rows_group(slice(0, cq_ref.shape[0]))


def _mla_prep(g, pc, wuq, wukv, gains, cos_t, sin_t):
    tm = g.tm
    full = lambda shape: pl.BlockSpec(shape, lambda i: (0, 0))
    kr_col = (MLA_Q_RANK + MLA_KV_RANK) // LANES
    return pl.pallas_call(
        _mla_prep_kernel,
        grid=(g.n_tiles,),
        in_specs=[
            pl.BlockSpec((tm, MLA_Q_RANK), lambda i: (i, 0)),
            pl.BlockSpec((tm, MLA_KV_RANK), lambda i: (i, MLA_Q_RANK // MLA_KV_RANK)),
            pl.BlockSpec((tm, LANES), lambda i: (i, kr_col)),
            full(wuq.shape), full(wukv.shape),
            full((1, MLA_Q_RANK)), full((1, MLA_KV_RANK)),
            full((1, LANES)), full((1, LANES)), full((1, LANES)), full((1, LANES)),
            pl.BlockSpec((tm, LANES), lambda i: (g.pos_block(i), 0)),
            pl.BlockSpec((tm, LANES), lambda i: (g.pos_block(i), 0)),
        ],
        out_specs=[
            pl.BlockSpec((tm, MLA_HEADS * MLA_HW), lambda i: (i, 0)),
            pl.BlockSpec((tm, MLA_HEADS * MLA_HW), lambda i: (i, 0)),
            pl.BlockSpec((tm, BRANCH_W), lambda i: (i, 0)),
        ],
        out_shape=[
            jax.ShapeDtypeStruct((g.r, MLA_HEADS * MLA_HW), BF16),
            jax.ShapeDtypeStruct((g.r, MLA_HEADS * MLA_HW), BF16),
            jax.ShapeDtypeStruct((g.r, BRANCH_W), BF16),
        ],
        scratch_shapes=[pltpu.VMEM(wuq.shape, BF16), pltpu.VMEM(wukv.shape, BF16)],
        compiler_params=_cp("arbitrary"),
        name="mla_prep",
    )(pc, pc, pc, wuq, wukv, *gains, cos_t, sin_t)


def _attn_kernel(*refs, n_soft, dh, has_lat, diff, lam_init):
    refs = list(refs)
    vs = refs.pop()
    kts = refs.pop()
    o_ref = refs.pop()
    if diff:
        lam_ref, sg_ref = refs[0], refs[1]
        refs = refs[2:]
    q_ref, kc_ref, vc_ref = refs[:3]
    lc = kc_ref.shape[0]

    @pl.when(pl.program_id(2) == 0)
    def _():
        kts[:, 0:lc] = kc_ref[...].T
        vs[0:lc, :] = vc_ref[...]
        if has_lat:
            kts[:, lc:] = refs[3][...].T
            vs[lc:, :] = refs[4][...]

    q = q_ref[...]
    tq = q.shape[0]
    n_sub = 4 if tq % 64 == 0 else 1
    rsub = tq // n_sub
    outs = []
    for s in range(n_soft):
        sl = slice(s * dh, (s + 1) * dh)
        scs = [_dot(q[u * rsub:(u + 1) * rsub, sl], kts[sl, :]) for u in range(n_sub)]
        ps, ls = [], []
        for sc in scs:
            m = jnp.max(sc, axis=-1, keepdims=True)
            p = jnp.exp(sc - m)
            ls.append(jnp.sum(p, axis=-1, keepdims=True))
            ps.append(p.astype(BF16))
        os_ = [_dot(p, vs[...]) / l for p, l in zip(ps, ls)]
        outs.append(jnp.concatenate(os_, axis=0))
    if diff:
        lp = lam_ref[...]
        lam = (jnp.exp(jnp.sum(lp[0:1] * lp[1:2], axis=-1, keepdims=True))
               - jnp.exp(jnp.sum(lp[2:3] * lp[3:4], axis=-1, keepdims=True)) + lam_init)
        o = outs[0] - lam * outs[1]
        o = _rms(o) * sg_ref[...] * (1.0 - lam_init)
    else:
        o = outs[0]
    o_ref[...] = o.astype(o_ref.dtype)


def _attention(g, q_arr, k_arr, v_arr, *, q_col0, k_col0, heads, qk_w, dv, n_soft, ctx_queries,
               lam=None, subln_g=None, lam_init=0.0):
    diff = lam is not None
    ctx0 = g.rl // g.lc
    n_keys = g.lc if ctx_queries else g.lc + g.t
    if ctx_queries:
        tq = g.lc
        nq = 1
        q_row = lambda b, qi: ctx0 + b
        out_rows = g.rc
        o_row = lambda b, qi: b
    else:
        tq = min(512, g.t)
        nq = g.t // tq
        q_row = lambda b, qi: b * nq + qi
        out_rows = g.rl
        o_row = q_row
    in_specs = []
    args = []
    if diff:
        in_specs += [pl.BlockSpec(lam.shape, lambda b, h, qi: (0, 0)),
                     pl.BlockSpec((1, dv), lambda b, h, qi: (0, 0))]
        args += [lam, subln_g]
    in_specs += [
        pl.BlockSpec((tq, qk_w), lambda b, h, qi: (q_row(b, qi), q_col0 + h)),
        pl.BlockSpec((g.lc, qk_w), lambda b, h, qi: (ctx0 + b, k_col0 + h)),
        pl.BlockSpec((g.lc, dv), lambda b, h, qi: (ctx0 + b, h)),
    ]
    args += [q_arr, k_arr, v_arr]
    if not ctx_queries:
        in_specs += [
            pl.BlockSpec((g.t, qk_w), lambda b, h, qi: (b, k_col0 + h)),
            pl.BlockSpec((g.t, dv), lambda b, h, qi: (b, h)),
        ]
        args += [k_arr, v_arr]
    return pl.pallas_call(
        functools.partial(_attn_kernel, n_soft=n_soft, dh=qk_w // n_soft, has_lat=not ctx_queries,
                          diff=diff, lam_init=lam_init),
        grid=(g.b, heads, nq),
        in_specs=in_specs,
        out_specs=pl.BlockSpec((tq, dv), lambda b, h, qi: (o_row(b, qi), h)),
        out_shape=jax.ShapeDtypeStruct((out_rows, heads * dv), BF16),
        scratch_shapes=[pltpu.VMEM((qk_w, n_keys), BF16), pltpu.VMEM((n_keys, dv), BF16)],
        compiler_params=_cp("parallel", "parallel", "arbitrary"),
        name="attn_diff" if diff else "attn_mla",
    )(*args)


def _branch_kernel(ym_ref, yd_ref, ya_ref, wb_ref, g0_ref, g1_ref, g2_ref, o_ref, wb_s):
    @pl.when(pl.program_id(1) == 0)
    def _():
        wb_s[...] = wb_ref[...].astype(BF16)

    acc = g0_ref[...].astype(F32) * _dot(ym_ref[...], wb_s[0])
    acc = acc + g1_ref[...].astype(F32) * _dot(yd_ref[...], wb_s[1])
    acc = acc + g2_ref[...].astype(F32) * _dot(ya_ref[...], wb_s[2])
    o_ref[...] = acc.astype(o_ref.dtype)


def _branch_merge(g, ym, yd, ya, w_branch, l, gt, n_tiles):
    tn = 512
    nj = g.d // tn
    y_spec = pl.BlockSpec((g.tm, BRANCH_W), lambda j, i: (i, 0))
    return pl.pallas_call(
        _branch_kernel,
        grid=(nj, n_tiles),
        in_specs=[
            y_spec, y_spec, y_spec,
            pl.BlockSpec((None, N_BRANCH, BRANCH_W, tn), lambda j, i: (l, 0, 0, j)),
            pl.BlockSpec((g.tm, tn), lambda j, i: (i, j)),
            pl.BlockSpec((g.tm, tn), lambda j, i: (i, nj + j)),
            pl.BlockSpec((g.tm, tn), lambda j, i: (i, 2 * nj + j)),
        ],
        out_specs=pl.BlockSpec((g.tm, tn), lambda j, i: (i, j)),
        out_shape=jax.ShapeDtypeStruct((n_tiles * g.tm, g.d), BF16),
        scratch_shapes=[pltpu.VMEM((N_BRANCH, BRANCH_W, tn), BF16)],
        compiler_params=_cp("arbitrary", "arbitrary"),
        name="branch_merge",
    )(ym, yd, ya, w_branch, gt, gt, gt)


def _outproj_kernel(z_ref, w_ref, x_ref, g_ref, o_ref, w_s):
    @pl.when(pl.program_id(1) == 0)
    def _():
        w_s[...] = w_ref[...].astype(BF16)

    o_ref[...] = x_ref[...] + g_ref[...] * _dot(z_ref[...], w_s[...])


def _outproj(g, z, w_out, l, x_all, mods3, which_gate, n_tiles):
    tn = 512
    per = g.d // tn
    return pl.pallas_call(
        _outproj_kernel,
        grid=(per, n_tiles),
        in_specs=[
            pl.BlockSpec((g.tm, g.d), lambda j, i: (i, 0)),
            pl.BlockSpec((None, g.d, tn), lambda j, i: (l, 0, j)),
            pl.BlockSpec((g.tm, tn), lambda j, i: (i, j)),
            pl.BlockSpec((None, 1, tn), lambda j, i: (g.mod_row(i), 0, which_gate * per + j)),
        ],
        out_specs=pl.BlockSpec((g.tm, tn), lambda j, i: (i, j)),
        out_shape=jax.ShapeDtypeStruct((n_tiles * g.tm, g.d), F32),
        scratch_shapes=[pltpu.VMEM((g.d, tn), BF16)],
        compiler_params=_cp("arbitrary", "arbitrary"),
        name="outproj",
    )(z, w_out, x_all, mods3)


def _moe_prep_kernel(x_ref, sh_ref, sc_ref, rwt_ref, rb_ref, tri_ref, xr_ref, rank_ref, grp_ref, cnt_ref,
                     carry, *, mod_row):
    i = pl.program_id(0)

    @pl.when(i == 0)
    def _():
        carry[...] = jnp.zeros_like(carry)

    x = x_ref[...]
    h = _rms(x) * (1.0 + sc_ref[...]) + sh_ref[...]
    logits = _dot_nt(rwt_ref[...].astype(BF16), h.astype(BF16))
    scores = _sigmoid(logits)
    selm = scores + rb_ref[...]
    sc_rows = [scores[e:e + 1, :] for e in range(N_EXPERTS)]
    sel = [selm[e:e + 1, :] for e in range(N_EXPERTS)]
    gscore = []
    for gi in range(N_GROUPS):
        v = sel[gi * GROUP_SIZE:(gi + 1) * GROUP_SIZE]
        pair = [v[a] + v[b] for a in range(GROUP_SIZE) for b in range(a + 1, GROUP_SIZE)]
        gscore.append(functools.reduce(jnp.maximum, pair))
    gmax = functools.reduce(jnp.maximum, gscore)
    taken = jnp.zeros_like(gmax, dtype=jnp.bool_)
    in_group = []
    for gi in range(N_GROUPS):
        hit = jnp.logical_and(gscore[gi] == gmax, jnp.logical_not(taken))
        taken = jnp.logical_or(taken, hit)
        in_group.append(hit)
    neg = -jnp.inf
    masked = [jnp.where(in_group[e // GROUP_SIZE], sel[e], neg) for e in range(N_EXPERTS)]

    def pick(vals):
        mx = functools.reduce(jnp.maximum, vals)
        seen = jnp.zeros_like(mx, dtype=jnp.bool_)
        hot = []
        for v in vals:
            hit = jnp.logical_and(v == mx, jnp.logical_not(seen))
            seen = jnp.logical_or(seen, hit)
            hot.append(hit)
        return hot

    hot1 = pick(masked)
    hot2 = pick([jnp.where(hot1[e], neg, masked[e]) for e in range(N_EXPERTS)])
    zero = jnp.zeros_like(gmax)
    s1 = functools.reduce(lambda a, b: a + b, [jnp.where(hot1[e], sc_rows[e], zero) for e in range(N_EXPERTS)])
    s2 = functools.reduce(lambda a, b: a + b, [jnp.where(hot2[e], sc_rows[e], zero) for e in range(N_EXPERTS)])
    tot = s1 + s2
    w1 = s1 / tot
    w2 = s2 / tot
    comb = [jnp.where(hot1[e], w1, zero) + jnp.where(hot2[e], w2, zero) for e in range(N_EXPERTS)]

    tm = x.shape[0]
    gmat = jnp.concatenate([jnp.where(in_group[gi], 1.0, zero) for gi in range(N_GROUPS)]
                           + [jnp.zeros((MOD_ROWS - N_GROUPS, tm), F32)], axis=0)
    incl = _dot(gmat.astype(BF16), tri_ref[...])
    excl = incl - gmat + carry[:, 0:1]
    rank = functools.reduce(lambda a, b: a + b,
                            [jnp.where(in_group[gi], excl[gi:gi + 1, :], zero) for gi in range(N_GROUPS)])
    grp = functools.reduce(lambda a, b: a + b,
                           [jnp.where(in_group[gi], float(gi), zero) for gi in range(N_GROUPS)])
    rank_ref[...] = rank.astype(jnp.int32)
    grp_ref[...] = grp.astype(jnp.int32)
    carry[...] = carry[...] + jnp.sum(gmat, axis=1, keepdims=True)
    cnt_ref[...] = carry[...]

    side = jnp.concatenate(comb + [jnp.zeros((LANES - N_EXPERTS, tm), F32)], axis=0).T
    lane = lax.broadcasted_iota(jnp.int32, (1, LANES), 1)
    side = jnp.where(lane == N_EXPERTS + mod_row(i), 1.0, side)
    d = x.shape[1]
    xr_ref[:, 0:d] = x
    xr_ref[:, d:d + LANES] = side


def _moe_prep(g, x_mid, mods3, which_shift, which_scale, rwt, rb, tri, n_tiles):
    n_rows = n_tiles * g.tm
    return pl.pallas_call(
        functools.partial(_moe_prep_kernel, mod_row=g.mod_row),
        grid=(n_tiles,),
        in_specs=[
            pl.BlockSpec((g.tm, g.d), lambda i: (i, 0)),
            _mod_spec(g, which_shift),
            _mod_spec(g, which_scale),
            pl.BlockSpec((N_EXPERTS, g.d), lambda i: (0, 0)),
            pl.BlockSpec((N_EXPERTS, 1), lambda i: (0, 0)),
            pl.BlockSpec((g.tm, g.tm), lambda i: (0, 0)),
        ],
        out_specs=[
            pl.BlockSpec((g.tm, g.d + LANES), lambda i: (i, 0)),
            pl.BlockSpec((1, g.tm), lambda i: (0, i)),
            pl.BlockSpec((1, g.tm), lambda i: (0, i)),
            pl.BlockSpec((MOD_ROWS, LANES), lambda i: (0, 0)),
        ],
        out_shape=[
            jax.ShapeDtypeStruct((n_rows, g.d + LANES), F32),
            jax.ShapeDtypeStruct((1, n_rows), jnp.int32),
            jax.ShapeDtypeStruct((1, n_rows), jnp.int32),
            jax.ShapeDtypeStruct((MOD_ROWS, LANES), F32),
        ],
        scratch_shapes=[pltpu.VMEM((MOD_ROWS, LANES), F32)],
        compiler_params=_cp("arbitrary"),
        name="moe_prep",
    )(x_mid, mods3, mods3, rwt, rb, tri)


def _moe_group_kernel(tg_ref, nv_ref, nu_ref, src_ref, xr_hbm, mods_ref, w1_hbm, w3_hbm, w2_hbm, out_hbm,
                      gbuf, obuf, w1s, w3s, w2s, stg_a, stg_b, gsem, ssem, wsem, *, n_mod_rows, e_base):
    k = pl.program_id(0)
    n_used = nu_ref[0]
    slot = k % 2
    d = obuf.shape[2]
    tme = obuf.shape[1]

    def gather_start(kk, sl):
        base = kk * tme

        def issue(r, c):
            tok = src_ref[base + r]
            pltpu.make_async_copy(xr_hbm.at[pl.ds(tok, 1), :], gbuf.at[sl, pl.ds(r, 1), :], gsem.at[sl]).start()
            return c

        lax.fori_loop(0, tme, issue, 0, unroll=8)

    def gather_wait(sl):
        def w(r, c):
            pltpu.make_async_copy(xr_hbm.at[pl.ds(0, 1), :], gbuf.at[sl, pl.ds(0, 1), :], gsem.at[sl]).wait()
            return c

        lax.fori_loop(0, tme, w, 0, unroll=8)

    def scatter_start(kk, sl):
        base = kk * tme

        def issue(r, c):
            tok = src_ref[base + r]
            pltpu.make_async_copy(obuf.at[sl, pl.ds(r, 1), :], out_hbm.at[pl.ds(tok, 1), :], ssem.at[sl]).start()
            return c

        rows_of(kk, issue)

    def scatter_wait(kk, sl):
        def w(r, c):
            pltpu.make_async_copy(obuf.at[sl, pl.ds(0, 1), :], out_hbm.at[pl.ds(0, 1), :], ssem.at[sl]).wait()
            return c

        rows_of(kk, w)

    def rows_of(kk, fn):
        n_rows = nv_ref[kk]

        @pl.when(n_rows == tme)
        def _():
            lax.fori_loop(0, tme, fn, 0, unroll=8)

        @pl.when(n_rows != tme)
        def _():
            lax.fori_loop(0, n_rows, fn, 0)

    def load_weights(grp):
        rows_a = stg_a.shape[1]
        rows_b = stg_b.shape[1]
        chunks = []
        n_a = n_b = 0
        for e in range(GROUP_SIZE):
            ex = e_base + grp * GROUP_SIZE + e
            for src, dst in ((w1_hbm, w1s), (w3_hbm, w3s)):
                for c0 in range(0, src.shape[1], rows_a):
                    sl = n_a % 2
                    n_a += 1
                    cp = pltpu.make_async_copy(src.at[ex, pl.ds(c0, rows_a), :], stg_a.at[sl], wsem.at[sl])
                    chunks.append((cp, stg_a.at[sl], dst, e, c0, rows_a))
            for c0 in range(0, w2_hbm.shape[1], rows_b):
                sl = n_b % 2
                n_b += 1
                cp = pltpu.make_async_copy(w2_hbm.at[ex, pl.ds(c0, rows_b), :], stg_b.at[sl], wsem.at[2 + sl])
                chunks.append((cp, stg_b.at[sl], w2s, e, c0, rows_b))
        chunks[0][0].start()
        for idx, (cp, stg, dst, e, c0, rows) in enumerate(chunks):
            if idx + 1 < len(chunks):
                chunks[idx + 1][0].start()
            cp.wait()
            dst[e, c0:c0 + rows, :] = stg[...].astype(BF16)

    @pl.when(k < n_used)
    def _():
        grp = tg_ref[k]

        @pl.when(k == 0)
        def _():
            gather_start(0, 0)

        @pl.when(k + 1 < n_used)
        def _():
            gather_start(k + 1, 1 - slot)

        @pl.when(jnp.logical_or(k == 0, grp != tg_ref[jnp.maximum(k - 1, 0)]))
        def _():
            load_weights(grp)

        gather_wait(slot)

        @pl.when(k >= 2)
        def _():
            scatter_wait(k - 2, slot)

        lane = lax.broadcasted_iota(jnp.int32, (1, LANES), 1)
        rows = tme // MOE_ROW_GROUPS
        for u in range(MOE_ROW_GROUPS):
            rs = slice(u * rows, (u + 1) * rows)
            x = gbuf[slot, rs, 0:d]
            side = gbuf[slot, rs, d:d + LANES]

            def cond_vec(which):
                out = jnp.zeros((rows, d), F32)
                for r in range(n_mod_rows):
                    hot = side[:, N_EXPERTS + r:N_EXPERTS + r + 1] > 0.5
                    out = jnp.where(hot, mods_ref[r:r + 1, which * d:(which + 1) * d], out)
                return out

            h = (_rms(x) * (1.0 + cond_vec(1)) + cond_vec(0)).astype(BF16)
            y = jnp.zeros((rows, d), F32)
            for e in range(GROUP_SIZE):
                a = _dot(h, w1s[e])
                b = _dot(h, w3s[e])
                cw = jnp.sum(jnp.where(lane == grp * GROUP_SIZE + e, side, 0.0), axis=1, keepdims=True)
                hid = (a * _sigmoid(a) * b * cw).astype(BF16)
                y = y + _dot(hid, w2s[e])
            obuf[slot, rs, :] = x + cond_vec(2) * y

        scatter_start(k, slot)

        @pl.when(k == n_used - 1)
        def _():
            @pl.when(k >= 1)
            def _():
                scatter_wait(k - 1, 1 - slot)

            scatter_wait(k, slot)


def _moe_group(g, xr, mods_tail, w1, w3, w2, meta, n_rows, n_mod_rows, e_base):
    tile_group, n_valid, n_used, src = meta
    kt = tile_group.shape[0]
    tme = MOE_TM
    any_spec = pl.BlockSpec(memory_space=pl.ANY)
    grid_spec = pltpu.PrefetchScalarGridSpec(
        num_scalar_prefetch=4,
        grid=(kt,),
        in_specs=[any_spec, pl.BlockSpec(mods_tail.shape, lambda k, *_: (0, 0)), any_spec, any_spec, any_spec],
        out_specs=any_spec,
        scratch_shapes=[
            pltpu.VMEM((2, tme, g.d + LANES), F32),
            pltpu.VMEM((2, tme, g.d), F32),
            pltpu.VMEM((GROUP_SIZE, g.d, D_EXPERT), BF16),
            pltpu.VMEM((GROUP_SIZE, g.d, D_EXPERT), BF16),
            pltpu.VMEM((GROUP_SIZE, D_EXPERT, g.d), BF16),
            pltpu.VMEM((2, g.d // 2, D_EXPERT), F32),
            pltpu.VMEM((2, D_EXPERT // 2, g.d), F32),
            pltpu.SemaphoreType.DMA((2,)),
            pltpu.SemaphoreType.DMA((2,)),
            pltpu.SemaphoreType.DMA((4,)),
        ],
    )
    return pl.pallas_call(
        functools.partial(_moe_group_kernel, n_mod_rows=n_mod_rows, e_base=e_base),
        grid_spec=grid_spec,
        out_shape=jax.ShapeDtypeStruct((n_rows, g.d), F32),
        compiler_params=_cp("arbitrary"),
        name="moe_group",
    )(tile_group, n_valid, n_used, src, xr, mods_tail, w1, w3, w2)


def _moe_meta(rank, grp, cnt, n_rows):
    tme = MOE_TM
    kt = n_rows // tme + N_GROUPS
    counts = cnt[:N_GROUPS, 0].astype(jnp.int32)
    ntile = (counts + tme - 1) // tme
    tile_end = jnp.cumsum(ntile)
    tile_off = tile_end - ntile
    pos = (tile_off * tme)[grp[0]] + rank[0]
    p = kt * tme
    src = jnp.zeros((p,), jnp.int32).at[pos].set(jnp.arange(n_rows, dtype=jnp.int32))
    ks = jnp.arange(kt, dtype=jnp.int32)
    tile_group = jnp.minimum(jnp.searchsorted(tile_end, ks, side="right").astype(jnp.int32), N_GROUPS - 1)
    n_used = tile_end[-1:]
    n_valid = jnp.clip(counts[tile_group] - (ks - tile_off[tile_group]) * tme, 0, tme)
    n_valid = jnp.where(ks < n_used[0], n_valid, 0)
    rows = jnp.arange(p, dtype=jnp.int32)
    in_tile = rows % tme
    valid = in_tile < n_valid[rows // tme]
    src = jnp.where(valid, src, src[rows - in_tile])
    return tile_group, n_valid, n_used, src


def _rope_tables(t, rot_dim, tile_rows, reps):
    f32 = np.float32
    rows = t // GRID_W
    r = np.repeat(np.arange(rows, dtype=f32), GRID_W)
    col = np.tile(np.arange(GRID_W, dtype=f32), rows)
    n_freq = rot_dim // 4
    inv = np.power(f32(ROPE_BASE), -np.arange(n_freq, dtype=f32) / f32(n_freq)).astype(f32)
    ang = np.concatenate([r[:, None] * inv, col[:, None] * inv], axis=-1).astype(f32)
    cos = np.tile(np.concatenate([np.cos(ang), np.cos(ang)], axis=-1), (1, reps))
    sin = np.tile(np.concatenate([-np.sin(ang), np.sin(ang)], axis=-1), (1, reps))
    cos = np.concatenate([cos, np.ones((tile_rows, LANES), f32)], axis=0).astype(f32)
    sin = np.concatenate([sin, np.zeros((tile_rows, LANES), f32)], axis=0).astype(f32)
    return jnp.asarray(cos), jnp.asarray(sin)


def _gate_perm():
    import numpy as np
    nhb = M_HEADS // M_HPB
    perm = np.zeros((nhb, LANES, LANES), np.float32)
    for hb in range(nhb):
        for gi in range(4):
            for j in range(M_HPB):
                perm[hb, gi * M_HEADS + hb * M_HPB + j, gi * M_HPB + j] = 1.0
    return jnp.asarray(perm)


def _uq_perm():
    nope = [h * MLA_QK + i for h in range(MLA_HEADS) for i in range(MLA_NOPE)]
    rope = [h * MLA_QK + MLA_NOPE + i for h in range(MLA_HEADS) for i in range(MLA_ROPE)]
    return jnp.asarray(nope + rope)


def kernel(x, c, ctx, c_ctx, w_ada, b_ada, w_in, b_in, m_conv_w, m_conv_b, m_norm_g, da_q_norm_g, da_k_norm_g,
           da_lambda, da_subln_g, mla_cq_norm_g, mla_ckv_norm_g, mla_w_uq, mla_w_ukv, mla_q_norm_g, mla_k_norm_g,
           w_branch, w_out, moe_w1, moe_w3, moe_w2, router_w, router_bias):
    bsz, t, d = x.shape
    lc = ctx.shape[1]
    depth = w_ada.shape[0]
    g = _Geom(bsz, t, lc, d, min(1024, bsz * lc, t))
    assert bsz + 1 <= MOD_ROWS

    cvec = jnp.concatenate([c, c_ctx[None, :], jnp.zeros((MOD_ROWS - bsz - 1, d), F32)], axis=0)
    mods = _mod_vectors(cvec, w_ada, b_ada)

    cos_d, sin_d = _rope_tables(t, DA_DH, g.tm, 1)
    cos_a, sin_a = _rope_tables(t, MLA_ROPE, g.tm, 2)
    rwt = router_w.T
    rb = router_bias.reshape(N_EXPERTS, 1)
    uq_perm = _uq_perm()
    x_all = jnp.concatenate([x.reshape(g.rl, d), ctx.reshape(g.rc, d)], axis=0)
    pos_spec = pl.BlockSpec((g.tm, LANES), lambda j, i: (g.pos_block(i), 0))
    b_in3 = b_in.reshape(depth, 1, -1)
    w_in_t = jnp.swapaxes(w_in, 1, 2)
    qk_scale = jnp.concatenate([jnp.full((1, BRANCH_W), M_DH ** -0.5, F32), jnp.ones((1, BRANCH_W), F32)], axis=1)
    tri = (jnp.arange(g.tm)[:, None] <= jnp.arange(g.tm)[None, :]).astype(BF16)
    w1r = moe_w1.reshape(depth * N_EXPERTS, d, D_EXPERT)
    w3r = moe_w3.reshape(depth * N_EXPERTS, d, D_EXPERT)
    w2r = moe_w2.reshape(depth * N_EXPERTS, D_EXPERT, d)

    for l in range(depth):
        last = l == depth - 1
        n_tiles = g.n_lat_tiles if last else g.n_tiles
        n_rows = n_tiles * g.tm
        lam_init = 0.8 - 0.6 * math.exp(-0.3 * l)
        mods3 = mods[l].reshape(MOD_ROWS, 1, 6 * d)

        h1 = _prenorm(g, x_all, mods3, 0, 1, g.n_tiles)

        row = lambda v: v.reshape(1, -1)
        proj = functools.partial(_proj, g, h=h1, w_in_t=w_in_t, b_in3=b_in3, l=l)
        pm = proj(_epi_raw, col0=OFF_MQK, width=OFF_MG - OFF_MQK, out_dtype=BF16, name="proj_mlstm")
        pg = proj(_epi_raw, col0=OFF_MG, width=LANES, out_dtype=F32, tn=LANES, name="proj_mgates")
        da_gain = jnp.concatenate([jnp.tile(da_q_norm_g[l] * DA_DH ** -0.5, 2 * DA_HEADS),
                                   jnp.tile(da_k_norm_g[l], 2 * DA_HEADS)])
        dqk = proj(_epi_normrope, col0=OFF_DQ, width=2 * DA_QK_W, out_dtype=BF16,
                   extra=(row(da_gain), cos_d, sin_d),
                   extra_specs=(pl.BlockSpec((1, 512), lambda j, i: (0, j)), pos_spec, pos_spec),
                   name="proj_dqk")
        dv = proj(_epi_raw, col0=OFF_DV, width=BRANCH_W, out_dtype=BF16, name="proj_dv")
        pc = proj(_epi_raw, col0=OFF_CQ, width=OFF_G - OFF_CQ, out_dtype=F32, name="proj_small")
        gt = proj(_epi_sigmoid, col0=OFF_G, width=N_BRANCH * d, out_dtype=BF16, name="proj_gates")

        qkc = _mconv(g, pm, m_conv_w[l], row(m_conv_b[l]), qk_scale)
        ym_l, ym_c = _mscan(g, qkc, pm, pg, row(m_norm_g[l]))

        qg, kg = mla_q_norm_g[l], mla_k_norm_g[l]
        a_scale = MLA_QK ** -0.5
        gains = (row(mla_cq_norm_g[l]), row(mla_ckv_norm_g[l]),
                 row(qg[:MLA_NOPE] * a_scale), row(jnp.tile(qg[MLA_NOPE:], 2) * a_scale),
                 row(kg[:MLA_NOPE]), row(jnp.concatenate([kg[MLA_NOPE:], jnp.zeros((LANES - MLA_ROPE,), F32)])))
        aq, ak, av = _mla_prep(g, pc, mla_w_uq[l][:, uq_perm], mla_w_ukv[l], gains, cos_a, sin_a)

        lam_p = da_lambda[l]
        sub_g = row(da_subln_g[l])
        da_kw = dict(q_col0=0, k_col0=DA_HEADS, heads=DA_HEADS, qk_w=2 * DA_DH, dv=DA_DV, n_soft=2,
                     lam=lam_p, subln_g=sub_g, lam_init=lam_init)
        mla_kw = dict(q_col0=0, k_col0=0, heads=MLA_HEADS, qk_w=MLA_HW, dv=MLA_DV, n_soft=1)
        yd_l = _attention(g, dqk, dqk, dv, ctx_queries=False, **da_kw)
        ya_l = _attention(g, aq, ak, av, ctx_queries=False, **mla_kw)
        if last:
            ym, yd, ya = ym_l, yd_l, ya_l
        else:
            yd_c = _attention(g, dqk, dqk, dv, ctx_queries=True, **da_kw)
            ya_c = _attention(g, aq, ak, av, ctx_queries=True, **mla_kw)
            ym = jnp.concatenate([ym_l, ym_c], axis=0)
            yd = jnp.concatenate([yd_l, yd_c], axis=0)
            ya = jnp.concatenate([ya_l, ya_c], axis=0)

        z = _branch_merge(g, ym, yd, ya, w_branch, l, gt, n_tiles)
        x_mid = _outproj(g, z, w_out, l, x_all, mods3, 2, n_tiles)

        xr, rank, grp, cnt = _moe_prep(g, x_mid, mods3, 3, 4, rwt, rb, tri, n_tiles)
        meta = _moe_meta(rank, grp, cnt, n_rows)
        x_all = _moe_group(g, xr, mods[l][:, 3 * d:], w1r, w3r, w2r, meta, n_rows, bsz + 1, l * N_EXPERTS)

    return x_all.reshape(bsz, t, d)
```

```python
import functools
import math

import jax
import jax.numpy as jnp
import numpy as np
from jax import lax
from jax.experimental import pallas as pl
from jax.experimental.pallas import tpu as pltpu

F32 = jnp.float32
BF16 = jnp.bfloat16

GRID_W = 64
ROPE_BASE = 10000.0
EPS = 1e-6
N_BRANCH = 3
BRANCH_W = 1024
M_HEADS = 4
M_DH = BRANCH_W // M_HEADS
M_CHUNK = 64
M_CONV = 5
DA_HEADS = 4
DA_DV = BRANCH_W // DA_HEADS
DA_DH = DA_DV // 2
DA_QK_W = DA_HEADS * 2 * DA_DH
MLA_HEADS = 8
MLA_Q_RANK = 512
MLA_KV_RANK = 256
MLA_NOPE = 128
MLA_ROPE = 64
MLA_DV = BRANCH_W // MLA_HEADS
MLA_QK = MLA_NOPE + MLA_ROPE
N_EXPERTS = 16
N_GROUPS = 4
GROUP_SIZE = N_EXPERTS // N_GROUPS
D_EXPERT = 512

OFF_MQK = 0
OFF_MV = OFF_MQK + 2 * BRANCH_W
OFF_MO = OFF_MV + BRANCH_W
OFF_MG = OFF_MO + BRANCH_W
OFF_DQ = OFF_MG + 4 * M_HEADS
OFF_DK = OFF_DQ + DA_QK_W
OFF_DV = OFF_DK + DA_QK_W
OFF_CQ = OFF_DV + BRANCH_W
OFF_CKV = OFF_CQ + MLA_Q_RANK
OFF_KR = OFF_CKV + MLA_KV_RANK
OFF_G = OFF_KR + MLA_ROPE

LANES = 128
MOD_ROWS = 8
VMEM_LIMIT = 56 * 1024 * 1024
M_HPB = 2
MLA_HW = 256
MOE_TM = 256
PROJ_ROW_GROUPS = 4
MOE_ROW_GROUPS = 1


def _cp(*sem):
    return pltpu.CompilerParams(dimension_semantics=sem, vmem_limit_bytes=VMEM_LIMIT)


def _rms(x):
    return x * lax.rsqrt(jnp.mean(x * x, axis=-1, keepdims=True) + EPS)


def _sigmoid(x):
    return 0.5 * jnp.tanh(0.5 * x) + 0.5


def _dot(a, b):
    return jnp.dot(a, b, preferred_element_type=F32)


def _dot_nt(a, b):
    return lax.dot_general(a, b, (((1,), (1,)), ((), ())), preferred_element_type=F32)


def _dot_tn(a, b):
    return lax.dot_general(a, b, (((0,), (0,)), ((), ())), preferred_element_type=F32)


def _mod_kernel(c_ref, w_ref, b_ref, o_ref):
    c = c_ref[...]
    s = (c * _sigmoid(c)).astype(BF16)
    o_ref[...] = _dot(s, w_ref[...].astype(BF16)) + b_ref[...]


def _mod_vectors(cvec, w_ada, b_ada):
    depth, d, n = w_ada.shape
    tn = 1024
    return pl.pallas_call(
        _mod_kernel,
        grid=(depth, n // tn),
        in_specs=[
            pl.BlockSpec((MOD_ROWS, d), lambda l, j: (0, 0)),
            pl.BlockSpec((None, d, tn), lambda l, j: (l, 0, j)),
            pl.BlockSpec((None, 1, tn), lambda l, j: (l, 0, j)),
        ],
        out_specs=pl.BlockSpec((None, MOD_ROWS, tn), lambda l, j: (l, 0, j)),
        out_shape=jax.ShapeDtypeStruct((depth, MOD_ROWS, n), F32),
        compiler_params=_cp("parallel", "parallel"),
        name="adaln_mod",
    )(cvec, w_ada, b_ada.reshape(depth, 1, n))


class _Geom:
    def __init__(self, b, t, lc, d, tm):
        assert t % tm == 0 and (b * lc) % tm == 0 and t % lc == 0 and t % GRID_W == 0
        assert lc % M_CHUNK == 0 and t % M_CHUNK == 0
        self.b, self.t, self.lc, self.d, self.tm = b, t, lc, d, tm
        self.rl, self.rc = b * t, b * lc
        self.r = self.rl + self.rc
        self.mblk = min(256, lc)
        assert lc % self.mblk == 0 and t % self.mblk == 0 and self.mblk % M_CHUNK == 0
        self.n_lat_tiles = self.rl // tm
        self.n_tiles = self.r // tm
        self.tiles_per_seq = t // tm

    def mod_row(self, i):
        return jnp.minimum(i // self.tiles_per_seq, self.b)

    def pos_block(self, i):
        return jnp.where(i < self.n_lat_tiles, i % self.tiles_per_seq, self.tiles_per_seq)


def _mod_spec(g, which, width=None, col_of=None):
    width = g.d if width is None else width
    per = g.d // width
    if col_of is None:
        return pl.BlockSpec((None, 1, width), lambda i, *_: (g.mod_row(i), 0, which * per))
    return pl.BlockSpec((None, 1, width), lambda i, j: (g.mod_row(i), 0, which * per + col_of(j)))


def _row_sources(g, src, width, col_of, n_grid):
    tile = lambda i: i
    if n_grid == 2:
        spec = lambda rows_of: pl.BlockSpec((g.tm, width), lambda j, i: (rows_of(i), col_of(j)))
    else:
        spec = lambda rows_of: pl.BlockSpec((g.tm, width), lambda i: (rows_of(i), 0))
    if not isinstance(src, tuple):
        return [spec(tile)], [src]
    lat, ctx = src
    nl = g.n_lat_tiles
    return ([spec(lambda i: jnp.minimum(i, nl - 1)), spec(lambda i: jnp.maximum(i - nl, 0))], [lat, ctx])


def _on_rows(i, n_lat_tiles, refs, body):
    if len(refs) == 1:
        body(refs[0])
        return

    @pl.when(i < n_lat_tiles)
    def _():
        body(refs[0])

    @pl.when(i >= n_lat_tiles)
    def _():
        body(refs[1])


def _prenorm_kernel(*refs, n_lat_tiles):
    *x_refs, sh_ref, sc_ref, o_ref = refs

    def body(x_ref):
        o_ref[...] = (_rms(x_ref[...]) * (1.0 + sc_ref[...]) + sh_ref[...]).astype(o_ref.dtype)

    _on_rows(pl.program_id(0), n_lat_tiles, x_refs, body)


def _prenorm(g, x_src, mods3, which_shift, which_scale, n_tiles):
    x_specs, x_args = _row_sources(g, x_src, g.d, None, 1)
    return pl.pallas_call(
        functools.partial(_prenorm_kernel, n_lat_tiles=g.n_lat_tiles),
        grid=(n_tiles,),
        in_specs=[*x_specs, _mod_spec(g, which_shift), _mod_spec(g, which_scale)],
        out_specs=pl.BlockSpec((g.tm, g.d), lambda i: (i, 0)),
        out_shape=jax.ShapeDtypeStruct((n_tiles * g.tm, g.d), BF16),
        compiler_params=_cp("parallel"),
        name="prenorm",
    )(*x_args, mods3, mods3)


def _epi_raw(acc, rs, o_ref):
    o_ref[rs, :] = acc.astype(o_ref.dtype)


def _epi_sigmoid(acc, rs, o_ref):
    o_ref[rs, :] = _sigmoid(acc).astype(o_ref.dtype)


def _epi_normrope(acc, rs, gain_ref, cos_ref, sin_ref, o_ref):
    cos = cos_ref[rs, :]
    sin = sin_ref[rs, :]
    for j in range(acc.shape[1] // LANES):
        sl = slice(j * LANES, (j + 1) * LANES)
        x = _rms(acc[:, sl]) * gain_ref[:, sl]
        o_ref[rs, sl] = (x * cos + pltpu.roll(x, LANES // 2, 1) * sin).astype(o_ref.dtype)


def _epi_gate_perm(acc, rs, perm_ref, o_ref):
    for hb in range(perm_ref.shape[0]):
        o_ref[rs, hb * LANES:(hb + 1) * LANES] = jnp.dot(
            acc, perm_ref[hb], precision=lax.Precision.HIGHEST, preferred_element_type=F32)


def _proj_kernel(*refs, shift, epilogue):
    if shift:
        h_ref, wa_ref, wb_ref, ba_ref, bb_ref, *rest = refs
    else:
        h_ref, wa_ref, ba_ref, *rest = refs
    *extra, o_ref, w_s, b_s = rest
    tn = w_s.shape[0]

    @pl.when(pl.program_id(1) == 0)
    def _():
        if shift:
            w_s[0:tn - shift, :] = wa_ref[shift:tn, :].astype(BF16)
            w_s[tn - shift:tn, :] = wb_ref[0:shift, :].astype(BF16)
            bb = jnp.concatenate([ba_ref[...], bb_ref[...]], axis=1)
            bb = jnp.broadcast_to(bb, (b_s.shape[0], 2 * tn))
            b_s[...] = pltpu.roll(bb, 2 * tn - shift, 1)[:, :tn]
        else:
            w_s[...] = wa_ref[...].astype(BF16)
            b_s[...] = jnp.broadcast_to(ba_ref[...], b_s.shape)

    rows = h_ref.shape[0] // PROJ_ROW_GROUPS
    for u in range(PROJ_ROW_GROUPS):
        rs = slice(u * rows, (u + 1) * rows)
        acc = _dot_nt(h_ref[rs, :], w_s[...]) + b_s[0:1, :]
        epilogue(acc, rs, *extra, o_ref)


def _proj(g, epilogue, h, w_in_t, b_in3, l, col0, width, out_dtype, *, tn=512, out_tn=None,
          extra=(), extra_specs=(), name="proj"):
    ncols, kdim = w_in_t.shape[1], w_in_t.shape[2]
    blk0 = col0 // tn
    shift = col0 - blk0 * tn
    assert shift % 16 == 0
    nj = pl.cdiv(width, tn)
    last_blk = pl.cdiv(ncols, tn) - 1
    out_tn = tn if out_tn is None else out_tn
    w_specs = [pl.BlockSpec((None, tn, kdim), lambda j, i: (l, blk0 + j, 0))]
    b_specs = [pl.BlockSpec((None, 1, tn), lambda j, i: (l, 0, blk0 + j))]
    if shift:
        nxt = lambda j: jnp.minimum(blk0 + j + 1, last_blk)
        w_specs.append(pl.BlockSpec((None, tn, kdim), lambda j, i: (l, nxt(j), 0)))
        b_specs.append(pl.BlockSpec((None, 1, tn), lambda j, i: (l, 0, nxt(j))))
    n_w = len(w_specs)
    return pl.pallas_call(
        functools.partial(_proj_kernel, shift=shift, epilogue=epilogue),
        grid=(nj, g.n_tiles),
        in_specs=[pl.BlockSpec((g.tm, kdim), lambda j, i: (i, 0)), *w_specs, *b_specs, *extra_specs],
        out_specs=pl.BlockSpec((g.tm, out_tn), lambda j, i: (i, j)),
        out_shape=jax.ShapeDtypeStruct((g.r, nj * out_tn), out_dtype),
        scratch_shapes=[pltpu.VMEM((tn, kdim), BF16), pltpu.VMEM((MOD_ROWS, tn), F32)],
        compiler_params=_cp("arbitrary", "arbitrary"),
        name=name,
    )(h, *([w_in_t] * n_w), *([b_in3] * n_w), *extra)


def _mlstm_kernel(ql_ref, kl_ref, vl_ref, ol_ref, gl_ref, qc_ref, kc_ref, vc_ref, oc_ref, gc_ref,
                  cwq_ref, cwk_ref, cbq_ref, cbk_ref, ng_ref, yl_ref, yc_ref,
                  qs, ks, vs, xs, bcs, hf, hb, ct, nv, *, lc, t):
    n = lc + t
    ncc = lc // M_CHUNK
    nch = n // M_CHUNK
    L = M_CHUNK

    def conv_silu(x_ref, w_ref, b_ref, scale):
        x = x_ref[...].astype(F32)
        rows = x.shape[0]
        row = lax.broadcasted_iota(jnp.int32, (rows, 1), 0)
        acc = x * w_ref[M_CONV // 2:M_CONV // 2 + 1, :] + b_ref[...]
        for j in range(M_CONV):
            s = M_CONV // 2 - j
            if s == 0:
                continue
            xs_ = pltpu.roll(x, s % rows, 0)
            ok = jnp.logical_and(row - s >= 0, row - s < rows)
            acc = acc + jnp.where(ok, xs_, 0.0) * w_ref[j:j + 1, :]
        return (acc * _sigmoid(acc) * scale).astype(BF16)

    qs[0:lc, :] = conv_silu(qc_ref, cwq_ref, cbq_ref, M_DH ** -0.5)
    qs[lc:n, :] = conv_silu(ql_ref, cwq_ref, cbq_ref, M_DH ** -0.5)
    ks[0:lc, :] = conv_silu(kc_ref, cwk_ref, cbk_ref, 1.0)
    ks[lc:n, :] = conv_silu(kl_ref, cwk_ref, cbk_ref, 1.0)
    vs[0:lc, :] = vc_ref[...]
    vs[lc:n, :] = vl_ref[...]

    lane = lax.broadcasted_iota(jnp.int32, (1, LANES), 1)
    is_f = jnp.logical_or(jnp.logical_and(lane >= M_HPB, lane < 2 * M_HPB),
                          jnp.logical_and(lane >= 3 * M_HPB, lane < 4 * M_HPB))

    def gate_prep(g_ref, lo, rows):
        gx = g_ref[...]
        lsg = jnp.minimum(gx, 0.0) - jnp.log(1.0 + jnp.exp(-jnp.abs(gx)))
        x = jnp.where(is_f, lsg, gx)
        xs[lo:lo + rows, :] = x
        pos = lax.broadcasted_iota(jnp.int32, (rows, 1), 0) % L
        pre = x
        suf = x
        k = 1
        while k < L:
            pre = pre + jnp.where(pos >= k, pltpu.roll(pre, k, 0), 0.0)
            suf = suf + jnp.where(pos < L - k, pltpu.roll(suf, rows - k, 0), 0.0)
            k *= 2
        bcs[lo:lo + rows, :] = jnp.where(lane >= 2 * M_HPB, suf, pre)

    gate_prep(gc_ref, 0, lc)
    gate_prep(gl_ref, lc, t)

    ct[...] = jnp.zeros_like(ct)
    nv[...] = jnp.zeros_like(nv)

    r_i = lax.broadcasted_iota(jnp.int32, (L, L), 0)
    c_i = lax.broadcasted_iota(jnp.int32, (L, L), 1)
    eye = r_i == c_i
    masks = (c_i <= r_i, c_i >= r_i)

    def chain_step(c, j, d, m):
        idx = d * M_HPB + j
        r0 = pl.multiple_of(c * L, L)
        hs = slice(j * M_DH, (j + 1) * M_DH)
        q = qs[pl.ds(r0, L), hs]
        k = ks[pl.ds(r0, L), hs]
        v = vs[pl.ds(r0, L), hs]
        xg = xs[pl.ds(r0, L), :]
        bg = bcs[pl.ds(r0, L), :]
        li = 2 * d * M_HPB + j
        lf = (2 * d + 1) * M_HPB + j
        ic = xg[:, li:li + 1]
        a = bg[:, lf:lf + 1]
        b_last = a[L - 1:L, :] if d == 0 else a[0:1, :]
        d_col = ic - a
        d_row = jnp.sum(jnp.where(eye, d_col, 0.0), axis=0, keepdims=True)
        logw = jnp.where(masks[d], a + d_row, -jnp.inf)
        inter = a + m
        m_t = jnp.maximum(inter, jnp.max(logw, axis=1, keepdims=True))
        w = jnp.exp(logw - m_t) * _dot_nt(q, k)
        decay = jnp.exp(inter - m_t)
        ctb = ct[idx].astype(BF16)
        num = _dot(w.astype(BF16), v) + decay * _dot(q, ctb)
        nrow = nv[idx]
        den = jnp.sum(w, axis=1, keepdims=True) + decay * jnp.sum(q.astype(F32) * nrow, axis=1, keepdims=True)
        h = num / jnp.maximum(jnp.abs(den), jnp.exp(-m_t))
        if d == 0:
            hf[pl.ds(r0, L), hs] = h
        else:
            hb[pl.ds(r0, L), hs] = h
        src = b_last - a + ic
        m_new = jnp.maximum(b_last + m, jnp.max(src, axis=0, keepdims=True))
        gsc = jnp.exp(src - m_new)
        keep = jnp.exp(b_last + m - m_new)
        gv = (gsc * v.astype(F32)).astype(BF16)
        ct[idx] = keep * ct[idx] + _dot_tn(k, gv)
        nv[idx] = keep * nrow + jnp.sum(gsc * k.astype(F32), axis=0, keepdims=True)
        return m_new

    def body(i, ms):
        cf = i
        cb = jnp.where(i < ncc, ncc - 1 - i, nch - 1 + ncc - i)
        out = []
        for d in range(2):
            for j in range(M_HPB):
                out.append(chain_step(cf if d == 0 else cb, j, d, ms[d * M_HPB + j]))
        return tuple(out)

    lax.fori_loop(0, nch, body, tuple(jnp.zeros((1, 1), F32) for _ in range(2 * M_HPB)))

    for j in range(M_HPB):
        hs = slice(j * M_DH, (j + 1) * M_DH)
        hn = _rms(hf[:, hs] + hb[:, hs]) * ng_ref[:, hs]
        yc_ref[:, hs] = (hn[0:lc] * _sigmoid(oc_ref[:, hs].astype(F32))).astype(BF16)
        yl_ref[:, hs] = (hn[lc:n] * _sigmoid(ol_ref[:, hs].astype(F32))).astype(BF16)


def _mlstm(g, pm, pg, conv_w, conv_b, norm_g):
    w = M_HPB * M_DH
    nhb = M_HEADS // M_HPB
    cb = BRANCH_W // w
    ctx0 = g.rl // g.lc
    gcol = 0
    n = g.lc + g.t

    def lat(seg):
        return pl.BlockSpec((g.t, w), lambda b, hb: (b, seg * cb + hb))

    def ctx(seg):
        return pl.BlockSpec((g.lc, w), lambda b, hb: (ctx0 + b, seg * cb + hb))

    return pl.pallas_call(
        functools.partial(_mlstm_kernel, lc=g.lc, t=g.t),
        grid=(g.b, nhb),
        in_specs=[
            lat(0), lat(1), lat(2), lat(3),
            pl.BlockSpec((g.t, LANES), lambda b, hb: (b, gcol + hb)),
            ctx(0), ctx(1), ctx(2), ctx(3),
            pl.BlockSpec((g.lc, LANES), lambda b, hb: (ctx0 + b, gcol + hb)),
            pl.BlockSpec((M_CONV, w), lambda b, hb: (0, hb)),
            pl.BlockSpec((M_CONV, w), lambda b, hb: (0, cb + hb)),
            pl.BlockSpec((1, w), lambda b, hb: (0, hb)),
            pl.BlockSpec((1, w), lambda b, hb: (0, cb + hb)),
            pl.BlockSpec((1, w), lambda b, hb: (0, hb)),
        ],
        out_specs=[
            pl.BlockSpec((g.t, w), lambda b, hb: (b, hb)),
            pl.BlockSpec((g.lc, w), lambda b, hb: (b, hb)),
        ],
        out_shape=[
            jax.ShapeDtypeStruct((g.rl, BRANCH_W), BF16),
            jax.ShapeDtypeStruct((g.rc, BRANCH_W), BF16),
        ],
        scratch_shapes=[
            pltpu.VMEM((n, w), BF16), pltpu.VMEM((n, w), BF16), pltpu.VMEM((n, w), BF16),
            pltpu.VMEM((n, LANES), F32), pltpu.VMEM((n, LANES), F32),
            pltpu.VMEM((n, w), F32), pltpu.VMEM((n, w), F32),
            pltpu.VMEM((2 * M_HPB, M_DH, M_DH), F32), pltpu.VMEM((2 * M_HPB, 1, M_DH), F32),
        ],
        compiler_params=_cp("parallel", "parallel"),
        name="mlstm",
    )(pm, pm, pm, pm, pg, pm, pm, pm, pm, pg, conv_w, conv_w, conv_b, conv_b, norm_g)


def _mconv_kernel(x_ref, prev_ref, next_ref, w_ref, b_ref, s_ref, o_ref, xe, *, blocks_per_seq, blocks_per_ctx,
                  n_lat_blocks):
    i = pl.program_id(0)
    rows = x_ref.shape[0]
    halo = prev_ref.shape[0]
    pad = M_CONV // 2
    in_lat = i < n_lat_blocks
    pos = jnp.where(in_lat, i % blocks_per_seq, (i - n_lat_blocks) % blocks_per_ctx)
    per_seq = jnp.where(in_lat, blocks_per_seq, blocks_per_ctx)
    has_prev = pos != 0
    has_next = pos != per_seq - 1
    xe[0:halo, :] = jnp.where(has_prev, prev_ref[...].astype(F32), 0.0)
    xe[halo:halo + rows, :] = x_ref[...].astype(F32)
    xe[halo + rows:halo + rows + halo, :] = jnp.where(has_next, next_ref[...].astype(F32), 0.0)
    acc = jnp.broadcast_to(b_ref[...], (rows, x_ref.shape[1]))
    for j in range(M_CONV):
        acc = acc + xe[halo - pad + j:halo - pad + j + rows, :] * w_ref[j:j + 1, :]
    o_ref[...] = (acc * _sigmoid(acc) * s_ref[...]).astype(o_ref.dtype)


def _mconv(g, pm, conv_w, conv_b, scale_row):
    rows = g.mblk
    halo = 16
    tn = 512
    per = rows // halo
    nblk = g.r // rows
    last = g.r // halo - 1
    return pl.pallas_call(
        functools.partial(_mconv_kernel, blocks_per_seq=g.t // rows, blocks_per_ctx=g.lc // rows,
                          n_lat_blocks=g.rl // rows),
        grid=(nblk, 2 * BRANCH_W // tn),
        in_specs=[
            pl.BlockSpec((rows, tn), lambda i, j: (i, j)),
            pl.BlockSpec((halo, tn), lambda i, j: (jnp.maximum(i * per - 1, 0), j)),
            pl.BlockSpec((halo, tn), lambda i, j: (jnp.minimum((i + 1) * per, last), j)),
            pl.BlockSpec((M_CONV, tn), lambda i, j: (0, j)),
            pl.BlockSpec((1, tn), lambda i, j: (0, j)),
            pl.BlockSpec((1, tn), lambda i, j: (0, j)),
        ],
        out_specs=pl.BlockSpec((rows, tn), lambda i, j: (i, j)),
        out_shape=jax.ShapeDtypeStruct((g.r, 2 * BRANCH_W), BF16),
        scratch_shapes=[pltpu.VMEM((rows + 2 * halo, tn), F32)],
        compiler_params=_cp("parallel", "parallel"),
        name="mlstm_conv",
    )(pm, pm, pm, conv_w, conv_b, scale_row)


def _mscan_kernel(qkf_ref, vf_ref, gf_ref, qkb_ref, vb_ref, gb_ref, ol_ref, oc_ref, ng_ref, yl_ref, yc_ref,
                  xsf, bcf, xsb, bcb, dtf, dtb, hf, hb, ct, nv, ms, *, lc, t, mblk):
    i = pl.program_id(1)
    nblk = pl.num_programs(1)
    L = M_CHUNK
    cpb = mblk // L
    nbc = lc // mblk
    nbl = t // mblk
    n = lc + t

    @pl.when(i == 0)
    def _():
        ct[...] = jnp.zeros_like(ct)
        nv[...] = jnp.zeros_like(nv)
        ms[...] = jnp.zeros_like(ms)

    lane = lax.broadcasted_iota(jnp.int32, (1, LANES), 1)
    is_f = jnp.logical_or(jnp.logical_and(lane >= M_HEADS, lane < 2 * M_HEADS),
                          jnp.logical_and(lane >= 3 * M_HEADS, lane < 4 * M_HEADS))
    pos = lax.broadcasted_iota(jnp.int32, (mblk, 1), 0) % L

    def gate_prep(g_ref, x_out, bc_out, forward):
        gx = g_ref[...]
        lsg = jnp.minimum(gx, 0.0) - jnp.log(1.0 + jnp.exp(-jnp.abs(gx)))
        x = jnp.where(is_f, lsg, gx)
        x_out[...] = x
        run = x
        k = 1
        while k < L:
            if forward:
                run = run + jnp.where(pos >= k, pltpu.roll(run, k, 0), 0.0)
            else:
                run = run + jnp.where(pos < L - k, pltpu.roll(run, mblk - k, 0), 0.0)
            k *= 2
        bc_out[...] = run

    gate_prep(gf_ref, xsf, bcf, True)
    gate_prep(gb_ref, xsb, bcb, False)

    pos_f = i * mblk
    pos_b = jnp.where(i < nbc, (nbc - 1 - i) * mblk, lc + (nbl - 1 - (i - nbc)) * mblk)

    def row_table(x_ref, bc_ref, out_ref):
        dm = x_ref[...] - pltpu.roll(bc_ref[...], LANES - M_HEADS, 1)
        for c in range(cpb):
            out_ref[c] = dm[c * L:(c + 1) * L, :].T

    row_table(xsf, bcf, dtf)
    row_table(xsb, bcb, dtb)

    H = M_HEADS
    r_i = lax.broadcasted_iota(jnp.int32, (H * L, L), 0) % L
    c_i = lax.broadcasted_iota(jnp.int32, (H * L, L), 1)
    masks = (c_i <= r_i, c_i >= r_i)

    def stack(parts):
        return jnp.concatenate(parts, axis=0)

    def dir_step(c, d, mvals):
        qk_ref, v_ref, x_ref, bc_ref, dt_ref, h_ref, base = (
            (qkf_ref, vf_ref, xsf, bcf, dtf, hf, pos_f), (qkb_ref, vb_ref, xsb, bcb, dtb, hb, pos_b))[d]
        r0 = pl.multiple_of(c * L, L)
        xg = x_ref[pl.ds(r0, L), :]
        bg = bc_ref[pl.ds(r0, L), :]
        dt = dt_ref[c]
        qs = [qk_ref[pl.ds(r0, L), j * M_DH:(j + 1) * M_DH] for j in range(H)]
        ks = [qk_ref[pl.ds(r0, L), BRANCH_W + j * M_DH:BRANCH_W + (j + 1) * M_DH] for j in range(H)]
        vs_ = [v_ref[pl.ds(r0, L), j * M_DH:(j + 1) * M_DH] for j in range(H)]
        li0 = 2 * d * H
        lf0 = (2 * d + 1) * H
        a_h = [bg[:, lf0 + j:lf0 + j + 1] for j in range(H)]
        bl_h = [a[L - 1:L, :] if d == 0 else a[0:1, :] for a in a_h]
        a4 = stack(a_h)
        d_rows = [dt[li0 + j:li0 + j + 1, :] for j in range(H)]
        d4 = stack([jnp.broadcast_to(r_, (L, L)) for r_ in d_rows])
        s4 = stack([_dot_nt(qs[j], ks[j]) for j in range(H)])
        logw = jnp.where(masks[d], a4 + d4, -jnp.inf)
        m_loc = jnp.max(logw, axis=1, keepdims=True)
        w4b = (jnp.exp(logw - m_loc) * s4).astype(BF16)
        den_loc = _dot(w4b, jnp.ones((L, LANES), BF16))[:, 0:1]
        num_loc = [_dot(w4b[j * L:(j + 1) * L, :], vs_[j]) for j in range(H)]
        s_rows = [bl_h[j] + d_rows[j] for j in range(H)]
        msrc_h = [jnp.max(s_, axis=1, keepdims=True) for s_ in s_rows]
        g_rows = [jnp.exp(s_rows[j] - msrc_h[j]) for j in range(H)]
        k_t = [ks[j].astype(F32).T for j in range(H)]
        d_c = [_dot((k_t[j] * g_rows[j]).astype(BF16), vs_[j]) for j in range(H)]
        d_n = [_dot(jnp.broadcast_to(g_rows[j], (MOD_ROWS, L)).astype(BF16), ks[j])[0:1, :] for j in range(H)]
        idx = [d * H + j for j in range(H)]
        nrow = [nv[i_] for i_ in idx]
        inter = [_dot(qs[j], ct[idx[j]].astype(BF16)) for j in range(H)]
        qn4 = stack([_dot_nt(qs[j], jnp.broadcast_to(nrow[j], (MOD_ROWS, M_DH)).astype(BF16))[:, 0:1]
                     for j in range(H)])
        m4 = stack([jnp.broadcast_to(mvals[idx[j]], (L, 1)) for j in range(H)])
        bm4 = a4 + m4
        m_t = jnp.maximum(bm4, m_loc)
        e_loc = jnp.exp(m_loc - m_t)
        e_int = jnp.exp(bm4 - m_t)
        den4 = e_loc * den_loc + e_int * qn4
        r4 = 1.0 / jnp.maximum(jnp.abs(den4), jnp.exp(-m_t))
        el = e_loc * r4
        ei = e_int * r4
        rows_out = pl.ds(pl.multiple_of(base + r0, L), L)
        out = list(mvals)
        for j in range(H):
            rs = slice(j * L, (j + 1) * L)
            h_ref[rows_out, j * M_DH:(j + 1) * M_DH] = el[rs] * num_loc[j] + ei[rs] * inter[j]
            m_old = mvals[idx[j]]
            m_new = jnp.maximum(bl_h[j] + m_old, msrc_h[j])
            keep = jnp.exp(bl_h[j] + m_old - m_new)
            scale = jnp.exp(msrc_h[j] - m_new)
            ct[idx[j]] = keep * ct[idx[j]] + scale * d_c[j]
            nv[idx[j]] = keep * nrow[j] + scale * d_n[j]
            out[idx[j]] = m_new
        return tuple(out)

    def body(s, mvals):
        mvals = dir_step(s, 0, mvals)
        return dir_step(cpb - 1 - s, 1, mvals)

    m0 = tuple(ms[q_:q_ + 1, 0:1] for q_ in range(2 * M_HEADS))
    m1 = lax.fori_loop(0, cpb, body, m0, unroll=2)
    for q_ in range(2 * M_HEADS):
        ms[q_:q_ + 1, :] = jnp.broadcast_to(m1[q_], (1, LANES))

    @pl.when(i == nblk - 1)
    def _():
        for blk in range(n // mblk):
            rs = slice(blk * mblk, (blk + 1) * mblk)
            for j in range(M_HEADS):
                hs = slice(j * M_DH, (j + 1) * M_DH)
                hn = _rms(hf[rs, hs] + hb[rs, hs]) * ng_ref[:, hs]
                if blk < nbc:
                    yc_ref[rs, hs] = (hn * _sigmoid(oc_ref[rs, hs].astype(F32))).astype(BF16)
                else:
                    ls = slice(blk * mblk - lc, (blk + 1) * mblk - lc)
                    yl_ref[ls, hs] = (hn * _sigmoid(ol_ref[ls, hs].astype(F32))).astype(BF16)


def _mscan(g, qkc, pm, pg, norm_g):
    mblk = g.mblk
    nbc, nbl = g.lc // mblk, g.t // mblk
    lat_blocks = g.rl // mblk
    n = g.lc + g.t

    def fwd(b, i):
        return jnp.where(i < nbc, lat_blocks + b * nbc + i, b * nbl + (i - nbc))

    def bwd(b, i):
        return jnp.where(i < nbc, lat_blocks + b * nbc + (nbc - 1 - i), b * nbl + (nbl - 1 - (i - nbc)))

    v_col = 2 * BRANCH_W // BRANCH_W
    o_col = 3 * BRANCH_W // BRANCH_W
    ctx0 = g.rl // g.lc

    def side(blk_of):
        return [pl.BlockSpec((mblk, 2 * BRANCH_W), lambda b, i: (blk_of(b, i), 0)),
                pl.BlockSpec((mblk, BRANCH_W), lambda b, i: (blk_of(b, i), v_col)),
                pl.BlockSpec((mblk, LANES), lambda b, i: (blk_of(b, i), 0))]

    return pl.pallas_call(
        functools.partial(_mscan_kernel, lc=g.lc, t=g.t, mblk=mblk),
        grid=(g.b, nbc + nbl),
        in_specs=[
            *side(fwd), *side(bwd),
            pl.BlockSpec((g.t, BRANCH_W), lambda b, i: (b, o_col)),
            pl.BlockSpec((g.lc, BRANCH_W), lambda b, i: (ctx0 + b, o_col)),
            pl.BlockSpec((1, BRANCH_W), lambda b, i: (0, 0)),
        ],
        out_specs=[
            pl.BlockSpec((g.t, BRANCH_W), lambda b, i: (b, 0)),
            pl.BlockSpec((g.lc, BRANCH_W), lambda b, i: (b, 0)),
        ],
        out_shape=[
            jax.ShapeDtypeStruct((g.rl, BRANCH_W), BF16),
            jax.ShapeDtypeStruct((g.rc, BRANCH_W), BF16),
        ],
        scratch_shapes=[
            pltpu.VMEM((mblk, LANES), F32), pltpu.VMEM((mblk, LANES), F32),
            pltpu.VMEM((mblk, LANES), F32), pltpu.VMEM((mblk, LANES), F32),
            pltpu.VMEM((mblk // M_CHUNK, LANES, M_CHUNK), F32), pltpu.VMEM((mblk // M_CHUNK, LANES, M_CHUNK), F32),
            pltpu.VMEM((n, BRANCH_W), F32), pltpu.VMEM((n, BRANCH_W), F32),
            pltpu.VMEM((2 * M_HEADS, M_DH, M_DH), F32), pltpu.VMEM((2 * M_HEADS, 1, M_DH), F32),
            pltpu.VMEM((2 * M_HEADS, LANES), F32),
        ],
        compiler_params=_cp("parallel", "arbitrary"),
        name="mlstm_scan",
    )(qkc, pm, pg, qkc, pm, pg, pm, pm, norm_g)


def _mla_prep_kernel(cq_ref, ckv_ref, kr_ref, wuq_ref, wukv_ref, cqg_ref, ckvg_ref, qng_ref, qrg_ref,
                     kng_ref, krg_ref, cos_ref, sin_ref, aq_ref, ak_ref, av_ref, wuq_s, wukv_s):
    @pl.when(pl.program_id(0) == 0)
    def _():
        wuq_s[...] = wuq_ref[...].astype(BF16)
        wukv_s[...] = wukv_ref[...].astype(BF16)

    half = LANES // 2
    lane = lax.broadcasted_iota(jnp.int32, (1, LANES), 1)
    lo = lane < half
    first = (lane % half) < (half // 2)

    def rms_half(x):
        x2 = x * x
        s_lo = jnp.sum(jnp.where(lo, x2, 0.0), axis=-1, keepdims=True)
        s_hi = jnp.sum(jnp.where(lo, 0.0, x2), axis=-1, keepdims=True)
        ms = jnp.where(lo, s_lo, s_hi) * (1.0 / half)
        return x * lax.rsqrt(ms + EPS)

    def rows_group(rs):
        cos = cos_ref[rs, :]
        sin = sin_ref[rs, :]

        def rope_half(x):
            partner = jnp.where(first, pltpu.roll(x, LANES - half // 2, 1), pltpu.roll(x, half // 2, 1))
            return x * cos + partner * sin

        cq = (_rms(cq_ref[rs, :]) * cqg_ref[...]).astype(BF16)
        q = _dot(cq, wuq_s[...])
        ckv = (_rms(ckv_ref[rs, :]) * ckvg_ref[...]).astype(BF16)
        kv = _dot(ckv, wukv_s[...])

        krn = rope_half(rms_half(kr_ref[rs, :]) * krg_ref[...]).astype(BF16)

        rope0 = MLA_HEADS * MLA_NOPE
        for hp in range(MLA_HEADS // 2):
            slab = q[:, rope0 + hp * LANES: rope0 + (hp + 1) * LANES]
            r = rope_half(rms_half(slab) * qrg_ref[...])
            parts = (jnp.where(lo, r, 0.0), jnp.where(lo, pltpu.roll(r, half, 1), 0.0))
            for e in range(2):
                h = 2 * hp + e
                nope = _rms(q[:, h * MLA_NOPE:(h + 1) * MLA_NOPE]) * qng_ref[...]
                aq_ref[rs, h * MLA_HW: h * MLA_HW + MLA_NOPE] = nope.astype(BF16)
                aq_ref[rs, h * MLA_HW + MLA_NOPE:(h + 1) * MLA_HW] = parts[e].astype(BF16)
        kvw = MLA_NOPE + MLA_DV
        for h in range(MLA_HEADS):
            kn = _rms(kv[:, h * kvw: h * kvw + MLA_NOPE]) * kng_ref[...]
            ak_ref[rs, h * MLA_HW: h * MLA_HW + MLA_NOPE] = kn.astype(BF16)
            ak_ref[rs, h * MLA_HW + MLA_NOPE:(h + 1) * MLA_HW] = krn
            av_ref[rs, h * MLA_DV:(h + 1) * MLA_DV] = kv[:, h * kvw + MLA_NOPE:(h + 1) * kvw].astype(BF16)

    rows_group(slice(0, cq_ref.shape[0]))


def _mla_prep(g, pc, wuq, wukv, gains, cos_t, sin_t):
    tm = g.tm
    full = lambda shape: pl.BlockSpec(shape, lambda i: (0, 0))
    kr_col = (MLA_Q_RANK + MLA_KV_RANK) // LANES
    return pl.pallas_call(
        _mla_prep_kernel,
        grid=(g.n_tiles,),
        in_specs=[
            pl.BlockSpec((tm, MLA_Q_RANK), lambda i: (i, 0)),
            pl.BlockSpec((tm, MLA_KV_RANK), lambda i: (i, MLA_Q_RANK // MLA_KV_RANK)),
            pl.BlockSpec((tm, LANES), lambda i: (i, kr_col)),
            full(wuq.shape), full(wukv.shape),
            full((1, MLA_Q_RANK)), full((1, MLA_KV_RANK)),
            full((1, LANES)), full((1, LANES)), full((1, LANES)), full((1, LANES)),
            pl.BlockSpec((tm, LANES), lambda i: (g.pos_block(i), 0)),
            pl.BlockSpec((tm, LANES), lambda i: (g.pos_block(i), 0)),
        ],
        out_specs=[
            pl.BlockSpec((tm, MLA_HEADS * MLA_HW), lambda i: (i, 0)),
            pl.BlockSpec((tm, MLA_HEADS * MLA_HW), lambda i: (i, 0)),
            pl.BlockSpec((tm, BRANCH_W), lambda i: (i, 0)),
        ],
        out_shape=[
            jax.ShapeDtypeStruct((g.r, MLA_HEADS * MLA_HW), BF16),
            jax.ShapeDtypeStruct((g.r, MLA_HEADS * MLA_HW), BF16),
            jax.ShapeDtypeStruct((g.r, BRANCH_W), BF16),
        ],
        scratch_shapes=[pltpu.VMEM(wuq.shape, BF16), pltpu.VMEM(wukv.shape, BF16)],
        compiler_params=_cp("arbitrary"),
        name="mla_prep",
    )(pc, pc, pc, wuq, wukv, *gains, cos_t, sin_t)


def _attn_kernel(*refs, n_soft, dh, has_lat, diff, lam_init):
    refs = list(refs)
    vs = refs.pop()
    kts = refs.pop()
    o_ref = refs.pop()
    if diff:
        lam_ref, sg_ref = refs[0], refs[1]
        refs = refs[2:]
    q_ref, kc_ref, vc_ref = refs[:3]
    lc = kc_ref.shape[0]

    @pl.when(pl.program_id(2) == 0)
    def _():
        kts[:, 0:lc] = kc_ref[...].T
        vs[0:lc, :] = vc_ref[...]
        if has_lat:
            kts[:, lc:] = refs[3][...].T
            vs[lc:, :] = refs[4][...]

    q = q_ref[...]
    tq = q.shape[0]
    n_sub = 4 if tq % 64 == 0 else 1
    rsub = tq // n_sub
    outs = []
    for s in range(n_soft):
        sl = slice(s * dh, (s + 1) * dh)
        scs = [_dot(q[u * rsub:(u + 1) * rsub, sl], kts[sl, :]) for u in range(n_sub)]
        ps, ls = [], []
        for sc in scs:
            m = jnp.max(sc, axis=-1, keepdims=True)
            p = jnp.exp(sc - m)
            ls.append(jnp.sum(p, axis=-1, keepdims=True))
            ps.append(p.astype(BF16))
        os_ = [_dot(p, vs[...]) / l for p, l in zip(ps, ls)]
        outs.append(jnp.concatenate(os_, axis=0))
    if diff:
        lp = lam_ref[...]
        lam = (jnp.exp(jnp.sum(lp[0:1] * lp[1:2], axis=-1, keepdims=True))
               - jnp.exp(jnp.sum(lp[2:3] * lp[3:4], axis=-1, keepdims=True)) + lam_init)
        o = outs[0] - lam * outs[1]
        o = _rms(o) * sg_ref[...] * (1.0 - lam_init)
    else:
        o = outs[0]
    o_ref[...] = o.astype(o_ref.dtype)


def _attention(g, q_arr, k_arr, v_arr, *, q_col0, k_col0, heads, qk_w, dv, n_soft, ctx_queries,
               lam=None, subln_g=None, lam_init=0.0):
    diff = lam is not None
    ctx0 = g.rl // g.lc
    n_keys = g.lc if ctx_queries else g.lc + g.t
    if ctx_queries:
        tq = g.lc
        nq = 1
        q_row = lambda b, qi: ctx0 + b
        out_rows = g.rc
        o_row = lambda b, qi: b
    else:
        tq = min(512, g.t)
        nq = g.t // tq
        q_row = lambda b, qi: b * nq + qi
        out_rows = g.rl
        o_row = q_row
    in_specs = []
    args = []
    if diff:
        in_specs += [pl.BlockSpec(lam.shape, lambda b, h, qi: (0, 0)),
                     pl.BlockSpec((1, dv), lambda b, h, qi: (0, 0))]
        args += [lam, subln_g]
    in_specs += [
        pl.BlockSpec((tq, qk_w), lambda b, h, qi: (q_row(b, qi), q_col0 + h)),
        pl.BlockSpec((g.lc, qk_w), lambda b, h, qi: (ctx0 + b, k_col0 + h)),
        pl.BlockSpec((g.lc, dv), lambda b, h, qi: (ctx0 + b, h)),
    ]
    args += [q_arr, k_arr, v_arr]
    if not ctx_queries:
        in_specs += [
            pl.BlockSpec((g.t, qk_w), lambda b, h, qi: (b, k_col0 + h)),
            pl.BlockSpec((g.t, dv), lambda b, h, qi: (b, h)),
        ]
        args += [k_arr, v_arr]
    return pl.pallas_call(
        functools.partial(_attn_kernel, n_soft=n_soft, dh=qk_w // n_soft, has_lat=not ctx_queries,
                          diff=diff, lam_init=lam_init),
        grid=(g.b, heads, nq),
        in_specs=in_specs,
        out_specs=pl.BlockSpec((tq, dv), lambda b, h, qi: (o_row(b, qi), h)),
        out_shape=jax.ShapeDtypeStruct((out_rows, heads * dv), BF16),
        scratch_shapes=[pltpu.VMEM((qk_w, n_keys), BF16), pltpu.VMEM((n_keys, dv), BF16)],
        compiler_params=_cp("parallel", "parallel", "arbitrary"),
        name="attn_diff" if diff else "attn_mla",
    )(*args)


def _branch_kernel(*refs, n_src, n_lat_tiles):
    y_refs = refs[:N_BRANCH * n_src]
    wb_ref, g0_ref, g1_ref, g2_ref, o_ref, wb_s = refs[N_BRANCH * n_src:]
    gates = (g0_ref, g1_ref, g2_ref)

    @pl.when(pl.program_id(1) == 0)
    def _():
        wb_s[...] = wb_ref[...].astype(BF16)

    def body(which):
        def run(_):
            acc = None
            for n in range(N_BRANCH):
                term = gates[n][...].astype(F32) * _dot(y_refs[n * n_src + which][...], wb_s[n])
                acc = term if acc is None else acc + term
            o_ref[...] = acc.astype(o_ref.dtype)
        return run

    i = pl.program_id(1)
    if n_src == 1:
        body(0)(None)
    else:
        pl.when(i < n_lat_tiles)(lambda: body(0)(None))
        pl.when(i >= n_lat_tiles)(lambda: body(1)(None))


def _branch_merge(g, ys, w_branch, l, gt, n_tiles):
    tn = 512
    nj = g.d // tn
    y_specs, y_args = [], []
    for y in ys:
        sp, ar = _row_sources(g, y, BRANCH_W, lambda j: 0, 2)
        y_specs += sp
        y_args += ar
    n_src = len(y_args) // N_BRANCH
    return pl.pallas_call(
        functools.partial(_branch_kernel, n_src=n_src, n_lat_tiles=g.n_lat_tiles),
        grid=(nj, n_tiles),
        in_specs=[
            *y_specs,
            pl.BlockSpec((None, N_BRANCH, BRANCH_W, tn), lambda j, i: (l, 0, 0, j)),
            pl.BlockSpec((g.tm, tn), lambda j, i: (i, j)),
            pl.BlockSpec((g.tm, tn), lambda j, i: (i, nj + j)),
            pl.BlockSpec((g.tm, tn), lambda j, i: (i, 2 * nj + j)),
        ],
        out_specs=pl.BlockSpec((g.tm, tn), lambda j, i: (i, j)),
        out_shape=jax.ShapeDtypeStruct((n_tiles * g.tm, g.d), BF16),
        scratch_shapes=[pltpu.VMEM((N_BRANCH, BRANCH_W, tn), BF16)],
        compiler_params=_cp("arbitrary", "arbitrary"),
        name="branch_merge",
    )(*y_args, w_branch, gt, gt, gt)


def _outproj_kernel(*refs, n_lat_tiles):
    z_ref, w_ref, *x_refs, g_ref, o_ref, w_s = refs

    @pl.when(pl.program_id(1) == 0)
    def _():
        w_s[...] = w_ref[...].astype(BF16)

    def body(x_ref):
        o_ref[...] = x_ref[...] + g_ref[...] * _dot(z_ref[...], w_s[...])

    _on_rows(pl.program_id(1), n_lat_tiles, x_refs, body)


def _outproj(g, z, w_out, l, x_src, mods3, which_gate, n_tiles):
    tn = 512
    per = g.d // tn
    x_specs, x_args = _row_sources(g, x_src, tn, lambda j: j, 2)
    return pl.pallas_call(
        functools.partial(_outproj_kernel, n_lat_tiles=g.n_lat_tiles),
        grid=(per, n_tiles),
        in_specs=[
            pl.BlockSpec((g.tm, g.d), lambda j, i: (i, 0)),
            pl.BlockSpec((None, g.d, tn), lambda j, i: (l, 0, j)),
            *x_specs,
            pl.BlockSpec((None, 1, tn), lambda j, i: (g.mod_row(i), 0, which_gate * per + j)),
        ],
        out_specs=pl.BlockSpec((g.tm, tn), lambda j, i: (i, j)),
        out_shape=jax.ShapeDtypeStruct((n_tiles * g.tm, g.d), F32),
        scratch_shapes=[pltpu.VMEM((g.d, tn), BF16)],
        compiler_params=_cp("arbitrary", "arbitrary"),
        name="outproj",
    )(z, w_out, *x_args, mods3)


def _moe_prep_kernel(x_ref, sh_ref, sc_ref, rwt_ref, rb_ref, tri_ref, xr_ref, rank_ref, grp_ref, cnt_ref,
                     carry, *, mod_row):
    i = pl.program_id(0)

    @pl.when(i == 0)
    def _():
        carry[...] = jnp.zeros_like(carry)

    x = x_ref[...]
    h = _rms(x) * (1.0 + sc_ref[...]) + sh_ref[...]
    logits = _dot_nt(rwt_ref[...].astype(BF16), h.astype(BF16))
    scores = _sigmoid(logits)
    selm = scores + rb_ref[...]
    sc_rows = [scores[e:e + 1, :] for e in range(N_EXPERTS)]
    sel = [selm[e:e + 1, :] for e in range(N_EXPERTS)]
    gscore = []
    for gi in range(N_GROUPS):
        v = sel[gi * GROUP_SIZE:(gi + 1) * GROUP_SIZE]
        pair = [v[a] + v[b] for a in range(GROUP_SIZE) for b in range(a + 1, GROUP_SIZE)]
        gscore.append(functools.reduce(jnp.maximum, pair))
    gmax = functools.reduce(jnp.maximum, gscore)
    taken = jnp.zeros_like(gmax, dtype=jnp.bool_)
    in_group = []
    for gi in range(N_GROUPS):
        hit = jnp.logical_and(gscore[gi] == gmax, jnp.logical_not(taken))
        taken = jnp.logical_or(taken, hit)
        in_group.append(hit)
    neg = -jnp.inf
    masked = [jnp.where(in_group[e // GROUP_SIZE], sel[e], neg) for e in range(N_EXPERTS)]

    def pick(vals):
        mx = functools.reduce(jnp.maximum, vals)
        seen = jnp.zeros_like(mx, dtype=jnp.bool_)
        hot = []
        for v in vals:
            hit = jnp.logical_and(v == mx, jnp.logical_not(seen))
            seen = jnp.logical_or(seen, hit)
            hot.append(hit)
        return hot

    hot1 = pick(masked)
    hot2 = pick([jnp.where(hot1[e], neg, masked[e]) for e in range(N_EXPERTS)])
    zero = jnp.zeros_like(gmax)
    s1 = functools.reduce(lambda a, b: a + b, [jnp.where(hot1[e], sc_rows[e], zero) for e in range(N_EXPERTS)])
    s2 = functools.reduce(lambda a, b: a + b, [jnp.where(hot2[e], sc_rows[e], zero) for e in range(N_EXPERTS)])
    tot = s1 + s2
    w1 = s1 / tot
    w2 = s2 / tot
    comb = [jnp.where(hot1[e], w1, zero) + jnp.where(hot2[e], w2, zero) for e in range(N_EXPERTS)]

    tm = x.shape[0]
    gmat = jnp.concatenate([jnp.where(in_group[gi], 1.0, zero) for gi in range(N_GROUPS)]
                           + [jnp.zeros((MOD_ROWS - N_GROUPS, tm), F32)], axis=0)
    incl = _dot(gmat.astype(BF16), tri_ref[...])
    excl = incl - gmat + carry[:, 0:1]
    rank = functools.reduce(lambda a, b: a + b,
                            [jnp.where(in_group[gi], excl[gi:gi + 1, :], zero) for gi in range(N_GROUPS)])
    grp = functools.reduce(lambda a, b: a + b,
                           [jnp.where(in_group[gi], float(gi), zero) for gi in range(N_GROUPS)])
    rank_ref[...] = rank.astype(jnp.int32)
    grp_ref[...] = grp.astype(jnp.int32)
    carry[...] = carry[...] + jnp.sum(gmat, axis=1, keepdims=True)
    cnt_ref[...] = carry[...]

    side = jnp.concatenate(comb + [jnp.zeros((LANES - N_EXPERTS, tm), F32)], axis=0).T
    lane = lax.broadcasted_iota(jnp.int32, (1, LANES), 1)
    side = jnp.where(lane == N_EXPERTS + mod_row(i), 1.0, side)
    d = x.shape[1]
    xr_ref[:, 0:d] = x
    xr_ref[:, d:d + LANES] = side


def _moe_prep(g, x_mid, mods3, which_shift, which_scale, rwt, rb, tri, n_tiles):
    n_rows = n_tiles * g.tm
    return pl.pallas_call(
        functools.partial(_moe_prep_kernel, mod_row=g.mod_row),
        grid=(n_tiles,),
        in_specs=[
            pl.BlockSpec((g.tm, g.d), lambda i: (i, 0)),
            _mod_spec(g, which_shift),
            _mod_spec(g, which_scale),
            pl.BlockSpec((N_EXPERTS, g.d), lambda i: (0, 0)),
            pl.BlockSpec((N_EXPERTS, 1), lambda i: (0, 0)),
            pl.BlockSpec((g.tm, g.tm), lambda i: (0, 0)),
        ],
        out_specs=[
            pl.BlockSpec((g.tm, g.d + LANES), lambda i: (i, 0)),
            pl.BlockSpec((1, g.tm), lambda i: (0, i)),
            pl.BlockSpec((1, g.tm), lambda i: (0, i)),
            pl.BlockSpec((MOD_ROWS, LANES), lambda i: (0, 0)),
        ],
        out_shape=[
            jax.ShapeDtypeStruct((n_rows, g.d + LANES), F32),
            jax.ShapeDtypeStruct((1, n_rows), jnp.int32),
            jax.ShapeDtypeStruct((1, n_rows), jnp.int32),
            jax.ShapeDtypeStruct((MOD_ROWS, LANES), F32),
        ],
        scratch_shapes=[pltpu.VMEM((MOD_ROWS, LANES), F32)],
        compiler_params=_cp("arbitrary"),
        name="moe_prep",
    )(x_mid, mods3, mods3, rwt, rb, tri)


def _moe_group_kernel(tg_ref, nv_ref, nu_ref, pos_ref, xr_hbm, mods_ref, w1_hbm, w3_hbm, w2_hbm, out_hbm,
                      gbuf, obuf, w1s, w3s, w2s, stg_a, stg_b, src_ref, gsem, ssem, wsem, *, n_mod_rows, e_base):
    k = pl.program_id(0)
    n_used = nu_ref[0]
    slot = k % 2
    d = obuf.shape[2]
    tme = obuf.shape[1]

    def gather_start(kk, sl):
        base = kk * tme

        first = src_ref[base]

        def issue(r, c):
            tok = src_ref[base + r]
            tok = jnp.where(tok < 0, first, tok)
            pltpu.make_async_copy(xr_hbm.at[pl.ds(tok, 1), :], gbuf.at[sl, pl.ds(r, 1), :], gsem.at[sl]).start()
            return c

        lax.fori_loop(0, tme, issue, 0, unroll=8)

    def build_sorted_index():
        def clear(p, c):
            src_ref[p] = -1
            return c

        def put(t, c):
            src_ref[pos_ref[t]] = t
            return c

        lax.fori_loop(0, src_ref.shape[0], clear, 0, unroll=8)
        lax.fori_loop(0, pos_ref.shape[0], put, 0, unroll=8)

    def gather_wait(sl):
        def w(r, c):
            pltpu.make_async_copy(xr_hbm.at[pl.ds(0, 1), :], gbuf.at[sl, pl.ds(0, 1), :], gsem.at[sl]).wait()
            return c

        lax.fori_loop(0, tme, w, 0, unroll=8)

    def scatter_start(kk, sl):
        base = kk * tme

        def issue(r, c):
            tok = src_ref[base + r]
            pltpu.make_async_copy(obuf.at[sl, pl.ds(r, 1), :], out_hbm.at[pl.ds(tok, 1), :], ssem.at[sl]).start()
            return c

        rows_of(kk, issue)

    def scatter_wait(kk, sl):
        def w(r, c):
            pltpu.make_async_copy(obuf.at[sl, pl.ds(0, 1), :], out_hbm.at[pl.ds(0, 1), :], ssem.at[sl]).wait()
            return c

        rows_of(kk, w)

    def rows_of(kk, fn):
        n_rows = nv_ref[kk]

        @pl.when(n_rows == tme)
        def _():
            lax.fori_loop(0, tme, fn, 0, unroll=8)

        @pl.when(n_rows != tme)
        def _():
            lax.fori_loop(0, n_rows, fn, 0)

    def load_weights(grp):
        rows_a = stg_a.shape[1]
        rows_b = stg_b.shape[1]
        chunks = []
        n_a = n_b = 0
        for e in range(GROUP_SIZE):
            ex = e_base + grp * GROUP_SIZE + e
            for src, dst in ((w1_hbm, w1s), (w3_hbm, w3s)):
                for c0 in range(0, src.shape[1], rows_a):
                    sl = n_a % 2
                    n_a += 1
                    cp = pltpu.make_async_copy(src.at[ex, pl.ds(c0, rows_a), :], stg_a.at[sl], wsem.at[sl])
                    chunks.append((cp, stg_a.at[sl], dst, e, c0, rows_a))
            for c0 in range(0, w2_hbm.shape[1], rows_b):
                sl = n_b % 2
                n_b += 1
                cp = pltpu.make_async_copy(w2_hbm.at[ex, pl.ds(c0, rows_b), :], stg_b.at[sl], wsem.at[2 + sl])
                chunks.append((cp, stg_b.at[sl], w2s, e, c0, rows_b))
        chunks[0][0].start()
        for idx, (cp, stg, dst, e, c0, rows) in enumerate(chunks):
            if idx + 1 < len(chunks):
                chunks[idx + 1][0].start()
            cp.wait()
            dst[e, c0:c0 + rows, :] = stg[...].astype(BF16)

    @pl.when(k < n_used)
    def _():
        grp = tg_ref[k]

        @pl.when(k == 0)
        def _():
            build_sorted_index()
            gather_start(0, 0)

        @pl.when(k + 1 < n_used)
        def _():
            gather_start(k + 1, 1 - slot)

        @pl.when(jnp.logical_or(k == 0, grp != tg_ref[jnp.maximum(k - 1, 0)]))
        def _():
            load_weights(grp)

        gather_wait(slot)

        @pl.when(k >= 2)
        def _():
            scatter_wait(k - 2, slot)

        lane = lax.broadcasted_iota(jnp.int32, (1, LANES), 1)
        rows = tme // MOE_ROW_GROUPS
        for u in range(MOE_ROW_GROUPS):
            rs = slice(u * rows, (u + 1) * rows)
            x = gbuf[slot, rs, 0:d]
            side = gbuf[slot, rs, d:d + LANES]

            def cond_vec(which):
                out = jnp.zeros((rows, d), F32)
                for r in range(n_mod_rows):
                    hot = side[:, N_EXPERTS + r:N_EXPERTS + r + 1] > 0.5
                    out = jnp.where(hot, mods_ref[r:r + 1, which * d:(which + 1) * d], out)
                return out

            h = (_rms(x) * (1.0 + cond_vec(1)) + cond_vec(0)).astype(BF16)
            y = jnp.zeros((rows, d), F32)
            for e in range(GROUP_SIZE):
                a = _dot(h, w1s[e])
                b = _dot(h, w3s[e])
                cw = jnp.sum(jnp.where(lane == grp * GROUP_SIZE + e, side, 0.0), axis=1, keepdims=True)
                hid = (a * _sigmoid(a) * b * cw).astype(BF16)
                y = y + _dot(hid, w2s[e])
            obuf[slot, rs, :] = x + cond_vec(2) * y

        scatter_start(k, slot)

        @pl.when(k == n_used - 1)
        def _():
            @pl.when(k >= 1)
            def _():
                scatter_wait(k - 1, 1 - slot)

            scatter_wait(k, slot)


def _moe_group(g, xr, mods_tail, w1, w3, w2, meta, n_rows, n_mod_rows, e_base):
    tile_group, n_valid, n_used, src = meta
    kt = tile_group.shape[0]
    tme = MOE_TM
    any_spec = pl.BlockSpec(memory_space=pl.ANY)
    grid_spec = pltpu.PrefetchScalarGridSpec(
        num_scalar_prefetch=4,
        grid=(kt,),
        in_specs=[any_spec, pl.BlockSpec(mods_tail.shape, lambda k, *_: (0, 0)), any_spec, any_spec, any_spec],
        out_specs=any_spec,
        scratch_shapes=[
            pltpu.VMEM((2, tme, g.d + LANES), F32),
            pltpu.VMEM((2, tme, g.d), F32),
            pltpu.VMEM((GROUP_SIZE, g.d, D_EXPERT), BF16),
            pltpu.VMEM((GROUP_SIZE, g.d, D_EXPERT), BF16),
            pltpu.VMEM((GROUP_SIZE, D_EXPERT, g.d), BF16),
            pltpu.VMEM((2, g.d // 2, D_EXPERT), F32),
            pltpu.VMEM((2, D_EXPERT // 2, g.d), F32),
            pltpu.SMEM((kt * tme,), jnp.int32),
            pltpu.SemaphoreType.DMA((2,)),
            pltpu.SemaphoreType.DMA((2,)),
            pltpu.SemaphoreType.DMA((4,)),
        ],
    )
    return pl.pallas_call(
        functools.partial(_moe_group_kernel, n_mod_rows=n_mod_rows, e_base=e_base),
        grid_spec=grid_spec,
        out_shape=jax.ShapeDtypeStruct((n_rows, g.d), F32),
        compiler_params=_cp("arbitrary"),
        name="moe_group",
    )(tile_group, n_valid, n_used, src, xr, mods_tail, w1, w3, w2)


def _moe_meta(rank, grp, cnt, n_rows):
    tme = MOE_TM
    kt = n_rows // tme + N_GROUPS
    counts = cnt[:N_GROUPS, 0].astype(jnp.int32)
    ntile = (counts + tme - 1) // tme
    tile_end = jnp.cumsum(ntile)
    tile_off = tile_end - ntile
    gsel = [grp[0] == gi for gi in range(N_GROUPS)]
    row_off = functools.reduce(lambda a, b: a + b,
                               [jnp.where(gsel[gi], tile_off[gi] * tme, 0) for gi in range(N_GROUPS)])
    pos = row_off + rank[0]
    ks = jnp.arange(kt, dtype=jnp.int32)
    tile_group = jnp.minimum(functools.reduce(lambda a, b: a + b,
                                              [(ks >= tile_end[gi]).astype(jnp.int32) for gi in range(N_GROUPS)]),
                             N_GROUPS - 1)
    n_used = tile_end[-1:]
    cnt_k = functools.reduce(lambda a, b: a + b,
                             [jnp.where(tile_group == gi, counts[gi] - (ks - tile_off[gi]) * tme, 0)
                              for gi in range(N_GROUPS)])
    n_valid = jnp.where(ks < n_used[0], jnp.clip(cnt_k, 0, tme), 0)
    return tile_group, n_valid, n_used, pos


def _rope_tables(t, rot_dim, tile_rows, reps):
    f32 = np.float32
    rows = t // GRID_W
    r = np.repeat(np.arange(rows, dtype=f32), GRID_W)
    col = np.tile(np.arange(GRID_W, dtype=f32), rows)
    n_freq = rot_dim // 4
    inv = np.power(f32(ROPE_BASE), -np.arange(n_freq, dtype=f32) / f32(n_freq)).astype(f32)
    ang = np.concatenate([r[:, None] * inv, col[:, None] * inv], axis=-1).astype(f32)
    cos = np.tile(np.concatenate([np.cos(ang), np.cos(ang)], axis=-1), (1, reps))
    sin = np.tile(np.concatenate([-np.sin(ang), np.sin(ang)], axis=-1), (1, reps))
    cos = np.concatenate([cos, np.ones((tile_rows, LANES), f32)], axis=0).astype(f32)
    sin = np.concatenate([sin, np.zeros((tile_rows, LANES), f32)], axis=0).astype(f32)
    return jnp.asarray(cos), jnp.asarray(sin)


def _gate_perm():
    import numpy as np
    nhb = M_HEADS // M_HPB
    perm = np.zeros((nhb, LANES, LANES), np.float32)
    for hb in range(nhb):
        for gi in range(4):
            for j in range(M_HPB):
                perm[hb, gi * M_HEADS + hb * M_HPB + j, gi * M_HPB + j] = 1.0
    return jnp.asarray(perm)


def _uq_perm():
    nope = [h * MLA_QK + i for h in range(MLA_HEADS) for i in range(MLA_NOPE)]
    rope = [h * MLA_QK + MLA_NOPE + i for h in range(MLA_HEADS) for i in range(MLA_ROPE)]
    return jnp.asarray(nope + rope)


def kernel(x, c, ctx, c_ctx, w_ada, b_ada, w_in, b_in, m_conv_w, m_conv_b, m_norm_g, da_q_norm_g, da_k_norm_g,
           da_lambda, da_subln_g, mla_cq_norm_g, mla_ckv_norm_g, mla_w_uq, mla_w_ukv, mla_q_norm_g, mla_k_norm_g,
           w_branch, w_out, moe_w1, moe_w3, moe_w2, router_w, router_bias):
    bsz, t, d = x.shape
    lc = ctx.shape[1]
    depth = w_ada.shape[0]
    g = _Geom(bsz, t, lc, d, min(1024, bsz * lc, t))
    assert bsz + 1 <= MOD_ROWS

    cvec = jnp.concatenate([c, c_ctx[None, :], jnp.zeros((MOD_ROWS - bsz - 1, d), F32)], axis=0)
    mods = _mod_vectors(cvec, w_ada, b_ada)

    cos_d, sin_d = _rope_tables(t, DA_DH, g.tm, 1)
    cos_a, sin_a = _rope_tables(t, MLA_ROPE, g.tm, 2)
    rwt = router_w.T
    rb = router_bias.reshape(N_EXPERTS, 1)
    uq_perm = _uq_perm()
    x_all = (x.reshape(g.rl, d), ctx.reshape(g.rc, d))
    pos_spec = pl.BlockSpec((g.tm, LANES), lambda j, i: (g.pos_block(i), 0))
    b_in3 = b_in.reshape(depth, 1, -1)
    w_in_t = jnp.swapaxes(w_in, 1, 2)
    qk_scale = jnp.concatenate([jnp.full((1, BRANCH_W), M_DH ** -0.5, F32), jnp.ones((1, BRANCH_W), F32)], axis=1)
    tri = (jnp.arange(g.tm)[:, None] <= jnp.arange(g.tm)[None, :]).astype(BF16)
    w1r = moe_w1.reshape(depth * N_EXPERTS, d, D_EXPERT)
    w3r = moe_w3.reshape(depth * N_EXPERTS, d, D_EXPERT)
    w2r = moe_w2.reshape(depth * N_EXPERTS, D_EXPERT, d)

    for l in range(depth):
        last = l == depth - 1
        n_tiles = g.n_lat_tiles if last else g.n_tiles
        n_rows = n_tiles * g.tm
        lam_init = 0.8 - 0.6 * math.exp(-0.3 * l)
        mods3 = mods[l].reshape(MOD_ROWS, 1, 6 * d)

        h1 = _prenorm(g, x_all, mods3, 0, 1, g.n_tiles)

        row = lambda v: v.reshape(1, -1)
        proj = functools.partial(_proj, g, h=h1, w_in_t=w_in_t, b_in3=b_in3, l=l)
        pm = proj(_epi_raw, col0=OFF_MQK, width=OFF_MG - OFF_MQK, out_dtype=BF16, name="proj_mlstm")
        pg = proj(_epi_raw, col0=OFF_MG, width=LANES, out_dtype=F32, tn=LANES, name="proj_mgates")
        da_gain = jnp.concatenate([jnp.tile(da_q_norm_g[l] * DA_DH ** -0.5, 2 * DA_HEADS),
                                   jnp.tile(da_k_norm_g[l], 2 * DA_HEADS)])
        dqk = proj(_epi_normrope, col0=OFF_DQ, width=2 * DA_QK_W, out_dtype=BF16,
                   extra=(row(da_gain), cos_d, sin_d),
                   extra_specs=(pl.BlockSpec((1, 512), lambda j, i: (0, j)), pos_spec, pos_spec),
                   name="proj_dqk")
        dv = proj(_epi_raw, col0=OFF_DV, width=BRANCH_W, out_dtype=BF16, name="proj_dv")
        pc = proj(_epi_raw, col0=OFF_CQ, width=OFF_G - OFF_CQ, out_dtype=F32, name="proj_small")
        gt = proj(_epi_sigmoid, col0=OFF_G, width=N_BRANCH * d, out_dtype=BF16, name="proj_gates")

        qkc = _mconv(g, pm, m_conv_w[l], row(m_conv_b[l]), qk_scale)
        ym_l, ym_c = _mscan(g, qkc, pm, pg, row(m_norm_g[l]))

        qg, kg = mla_q_norm_g[l], mla_k_norm_g[l]
        a_scale = MLA_QK ** -0.5
        gains = (row(mla_cq_norm_g[l]), row(mla_ckv_norm_g[l]),
                 row(qg[:MLA_NOPE] * a_scale), row(jnp.tile(qg[MLA_NOPE:], 2) * a_scale),
                 row(kg[:MLA_NOPE]), row(jnp.concatenate([kg[MLA_NOPE:], jnp.zeros((LANES - MLA_ROPE,), F32)])))
        aq, ak, av = _mla_prep(g, pc, mla_w_uq[l][:, uq_perm], mla_w_ukv[l], gains, cos_a, sin_a)

        lam_p = da_lambda[l]
        sub_g = row(da_subln_g[l])
        da_kw = dict(q_col0=0, k_col0=DA_HEADS, heads=DA_HEADS, qk_w=2 * DA_DH, dv=DA_DV, n_soft=2,
                     lam=lam_p, subln_g=sub_g, lam_init=lam_init)
        mla_kw = dict(q_col0=0, k_col0=0, heads=MLA_HEADS, qk_w=MLA_HW, dv=MLA_DV, n_soft=1)
        yd_l = _attention(g, dqk, dqk, dv, ctx_queries=False, **da_kw)
        ya_l = _attention(g, aq, ak, av, ctx_queries=False, **mla_kw)
        if last:
            ys = (ym_l, yd_l, ya_l)
        else:
            yd_c = _attention(g, dqk, dqk, dv, ctx_queries=True, **da_kw)
            ya_c = _attention(g, aq, ak, av, ctx_queries=True, **mla_kw)
            ys = ((ym_l, ym_c), (yd_l, yd_c), (ya_l, ya_c))

        z = _branch_merge(g, ys, w_branch, l, gt, n_tiles)
        x_mid = _outproj(g, z, w_out, l, x_all, mods3, 2, n_tiles)

        xr, rank, grp, cnt = _moe_prep(g, x_mid, mods3, 3, 4, rwt, rb, tri, n_tiles)
        meta = _moe_meta(rank, grp, cnt, n_rows)
        x_all = _moe_group(g, xr, mods[l][:, 3 * d:], w1r, w3r, w2r, meta, n_rows, bsz + 1, l * N_EXPERTS)

    return x_all.reshape(bsz, t, d)
```

```python
import functools
import math

import jax
import jax.numpy as jnp
import numpy as np
from jax import lax
from jax.experimental import pallas as pl
from jax.experimental.pallas import tpu as pltpu

F32 = jnp.float32
BF16 = jnp.bfloat16

GRID_W = 64
ROPE_BASE = 10000.0
EPS = 1e-6
N_BRANCH = 3
BRANCH_W = 1024
M_HEADS = 4
M_DH = BRANCH_W // M_HEADS
M_CHUNK = 64
M_CONV = 5
DA_HEADS = 4
DA_DV = BRANCH_W // DA_HEADS
DA_DH = DA_DV // 2
DA_QK_W = DA_HEADS * 2 * DA_DH
MLA_HEADS = 8
MLA_Q_RANK = 512
MLA_KV_RANK = 256
MLA_NOPE = 128
MLA_ROPE = 64
MLA_DV = BRANCH_W // MLA_HEADS
MLA_QK = MLA_NOPE + MLA_ROPE
N_EXPERTS = 16
N_GROUPS = 4
GROUP_SIZE = N_EXPERTS // N_GROUPS
D_EXPERT = 512

OFF_MQK = 0
OFF_MV = OFF_MQK + 2 * BRANCH_W
OFF_MO = OFF_MV + BRANCH_W
OFF_MG = OFF_MO + BRANCH_W
OFF_DQ = OFF_MG + 4 * M_HEADS
OFF_DK = OFF_DQ + DA_QK_W
OFF_DV = OFF_DK + DA_QK_W
OFF_CQ = OFF_DV + BRANCH_W
OFF_CKV = OFF_CQ + MLA_Q_RANK
OFF_KR = OFF_CKV + MLA_KV_RANK
OFF_G = OFF_KR + MLA_ROPE

LANES = 128
MOD_ROWS = 8
VMEM_LIMIT = 56 * 1024 * 1024
M_HPB = 2
MLA_HW = 256
MOE_TM = 256
ATT_TQ = 2048
PROJ_ROW_GROUPS = 4
MOE_ROW_GROUPS = 1


def _cp(*sem):
    return pltpu.CompilerParams(dimension_semantics=sem, vmem_limit_bytes=VMEM_LIMIT)


def _rms(x):
    return x * lax.rsqrt(jnp.mean(x * x, axis=-1, keepdims=True) + EPS)


def _sigmoid(x):
    return 0.5 * jnp.tanh(0.5 * x) + 0.5


def _dot(a, b):
    return jnp.dot(a, b, preferred_element_type=F32)


def _dot_nt(a, b):
    return lax.dot_general(a, b, (((1,), (1,)), ((), ())), preferred_element_type=F32)


def _dot_tn(a, b):
    return lax.dot_general(a, b, (((0,), (0,)), ((), ())), preferred_element_type=F32)


def _mod_kernel(c_ref, w_ref, b_ref, o_ref):
    c = c_ref[...]
    s = (c * _sigmoid(c)).astype(BF16)
    o_ref[...] = _dot(s, w_ref[...].astype(BF16)) + b_ref[...]


def _mod_vectors(cvec, w_ada, b_ada):
    depth, d, n = w_ada.shape
    tn = 1024
    return pl.pallas_call(
        _mod_kernel,
        grid=(depth, n // tn),
        in_specs=[
            pl.BlockSpec((MOD_ROWS, d), lambda l, j: (0, 0)),
            pl.BlockSpec((None, d, tn), lambda l, j: (l, 0, j)),
            pl.BlockSpec((None, 1, tn), lambda l, j: (l, 0, j)),
        ],
        out_specs=pl.BlockSpec((None, MOD_ROWS, tn), lambda l, j: (l, 0, j)),
        out_shape=jax.ShapeDtypeStruct((depth, MOD_ROWS, n), F32),
        compiler_params=_cp("parallel", "parallel"),
        name="adaln_mod",
    )(cvec, w_ada, b_ada.reshape(depth, 1, n))


class _Geom:
    def __init__(self, b, t, lc, d, tm):
        assert t % tm == 0 and (b * lc) % tm == 0 and t % lc == 0 and t % GRID_W == 0
        assert lc % M_CHUNK == 0 and t % M_CHUNK == 0
        self.b, self.t, self.lc, self.d, self.tm = b, t, lc, d, tm
        self.rl, self.rc = b * t, b * lc
        self.r = self.rl + self.rc
        self.mblk = min(256, lc)
        assert lc % self.mblk == 0 and t % self.mblk == 0 and self.mblk % M_CHUNK == 0
        self.n_lat_tiles = self.rl // tm
        self.n_tiles = self.r // tm
        self.tiles_per_seq = t // tm

    def mod_row(self, i):
        return jnp.minimum(i // self.tiles_per_seq, self.b)

    def pos_block(self, i):
        return jnp.where(i < self.n_lat_tiles, i % self.tiles_per_seq, self.tiles_per_seq)


def _mod_spec(g, which, width=None, col_of=None):
    width = g.d if width is None else width
    per = g.d // width
    if col_of is None:
        return pl.BlockSpec((None, 1, width), lambda i, *_: (g.mod_row(i), 0, which * per))
    return pl.BlockSpec((None, 1, width), lambda i, j: (g.mod_row(i), 0, which * per + col_of(j)))


def _row_sources(g, src, width, col_of, n_grid):
    tile = lambda i: i
    if n_grid == 2:
        spec = lambda rows_of: pl.BlockSpec((g.tm, width), lambda j, i: (rows_of(i), col_of(j)))
    else:
        spec = lambda rows_of: pl.BlockSpec((g.tm, width), lambda i: (rows_of(i), 0))
    if not isinstance(src, tuple):
        return [spec(tile)], [src]
    lat, ctx = src
    nl = g.n_lat_tiles
    return ([spec(lambda i: jnp.minimum(i, nl - 1)), spec(lambda i: jnp.maximum(i - nl, 0))], [lat, ctx])


def _on_rows(i, n_lat_tiles, refs, body):
    if len(refs) == 1:
        body(refs[0])
        return

    @pl.when(i < n_lat_tiles)
    def _():
        body(refs[0])

    @pl.when(i >= n_lat_tiles)
    def _():
        body(refs[1])


def _prenorm_kernel(*refs, n_lat_tiles):
    *x_refs, sh_ref, sc_ref, o_ref = refs

    def body(x_ref):
        o_ref[...] = (_rms(x_ref[...]) * (1.0 + sc_ref[...]) + sh_ref[...]).astype(o_ref.dtype)

    _on_rows(pl.program_id(0), n_lat_tiles, x_refs, body)


def _prenorm(g, x_src, mods3, which_shift, which_scale, n_tiles):
    x_specs, x_args = _row_sources(g, x_src, g.d, None, 1)
    return pl.pallas_call(
        functools.partial(_prenorm_kernel, n_lat_tiles=g.n_lat_tiles),
        grid=(n_tiles,),
        in_specs=[*x_specs, _mod_spec(g, which_shift), _mod_spec(g, which_scale)],
        out_specs=pl.BlockSpec((g.tm, g.d), lambda i: (i, 0)),
        out_shape=jax.ShapeDtypeStruct((n_tiles * g.tm, g.d), BF16),
        compiler_params=_cp("parallel"),
        name="prenorm",
    )(*x_args, mods3, mods3)


def _epi_raw(acc, rs, o_ref):
    o_ref[rs, :] = acc.astype(o_ref.dtype)


def _epi_sigmoid(acc, rs, o_ref):
    o_ref[rs, :] = _sigmoid(acc).astype(o_ref.dtype)


def _epi_normrope(acc, rs, gain_ref, cos_ref, sin_ref, o_ref):
    cos = cos_ref[rs, :]
    sin = sin_ref[rs, :]
    for j in range(acc.shape[1] // LANES):
        sl = slice(j * LANES, (j + 1) * LANES)
        x = _rms(acc[:, sl]) * gain_ref[:, sl]
        o_ref[rs, sl] = (x * cos + pltpu.roll(x, LANES // 2, 1) * sin).astype(o_ref.dtype)


def _epi_gate_perm(acc, rs, perm_ref, o_ref):
    for hb in range(perm_ref.shape[0]):
        o_ref[rs, hb * LANES:(hb + 1) * LANES] = jnp.dot(
            acc, perm_ref[hb], precision=lax.Precision.HIGHEST, preferred_element_type=F32)


def _proj_kernel(*refs, shift, epilogue):
    if shift:
        h_ref, wa_ref, wb_ref, ba_ref, bb_ref, *rest = refs
    else:
        h_ref, wa_ref, ba_ref, *rest = refs
    *extra, o_ref, w_s, b_s = rest
    tn = w_s.shape[0]

    @pl.when(pl.program_id(1) == 0)
    def _():
        if shift:
            w_s[0:tn - shift, :] = wa_ref[shift:tn, :].astype(BF16)
            w_s[tn - shift:tn, :] = wb_ref[0:shift, :].astype(BF16)
            bb = jnp.concatenate([ba_ref[...], bb_ref[...]], axis=1)
            bb = jnp.broadcast_to(bb, (b_s.shape[0], 2 * tn))
            b_s[...] = pltpu.roll(bb, 2 * tn - shift, 1)[:, :tn]
        else:
            w_s[...] = wa_ref[...].astype(BF16)
            b_s[...] = jnp.broadcast_to(ba_ref[...], b_s.shape)

    rows = h_ref.shape[0] // PROJ_ROW_GROUPS
    for u in range(PROJ_ROW_GROUPS):
        rs = slice(u * rows, (u + 1) * rows)
        acc = _dot_nt(h_ref[rs, :], w_s[...]) + b_s[0:1, :]
        epilogue(acc, rs, *extra, o_ref)


def _proj(g, epilogue, h, w_in_t, b_in3, l, col0, width, out_dtype, *, tn=512, out_tn=None,
          extra=(), extra_specs=(), name="proj"):
    ncols, kdim = w_in_t.shape[1], w_in_t.shape[2]
    blk0 = col0 // tn
    shift = col0 - blk0 * tn
    assert shift % 16 == 0
    nj = pl.cdiv(width, tn)
    last_blk = pl.cdiv(ncols, tn) - 1
    out_tn = tn if out_tn is None else out_tn
    w_specs = [pl.BlockSpec((None, tn, kdim), lambda j, i: (l, blk0 + j, 0))]
    b_specs = [pl.BlockSpec((None, 1, tn), lambda j, i: (l, 0, blk0 + j))]
    if shift:
        nxt = lambda j: jnp.minimum(blk0 + j + 1, last_blk)
        w_specs.append(pl.BlockSpec((None, tn, kdim), lambda j, i: (l, nxt(j), 0)))
        b_specs.append(pl.BlockSpec((None, 1, tn), lambda j, i: (l, 0, nxt(j))))
    n_w = len(w_specs)
    return pl.pallas_call(
        functools.partial(_proj_kernel, shift=shift, epilogue=epilogue),
        grid=(nj, g.n_tiles),
        in_specs=[pl.BlockSpec((g.tm, kdim), lambda j, i: (i, 0)), *w_specs, *b_specs, *extra_specs],
        out_specs=pl.BlockSpec((g.tm, out_tn), lambda j, i: (i, j)),
        out_shape=jax.ShapeDtypeStruct((g.r, nj * out_tn), out_dtype),
        scratch_shapes=[pltpu.VMEM((tn, kdim), BF16), pltpu.VMEM((MOD_ROWS, tn), F32)],
        compiler_params=_cp("arbitrary", "arbitrary"),
        name=name,
    )(h, *([w_in_t] * n_w), *([b_in3] * n_w), *extra)


def _mlstm_kernel(ql_ref, kl_ref, vl_ref, ol_ref, gl_ref, qc_ref, kc_ref, vc_ref, oc_ref, gc_ref,
                  cwq_ref, cwk_ref, cbq_ref, cbk_ref, ng_ref, yl_ref, yc_ref,
                  qs, ks, vs, xs, bcs, hf, hb, ct, nv, *, lc, t):
    n = lc + t
    ncc = lc // M_CHUNK
    nch = n // M_CHUNK
    L = M_CHUNK

    def conv_silu(x_ref, w_ref, b_ref, scale):
        x = x_ref[...].astype(F32)
        rows = x.shape[0]
        row = lax.broadcasted_iota(jnp.int32, (rows, 1), 0)
        acc = x * w_ref[M_CONV // 2:M_CONV // 2 + 1, :] + b_ref[...]
        for j in range(M_CONV):
            s = M_CONV // 2 - j
            if s == 0:
                continue
            xs_ = pltpu.roll(x, s % rows, 0)
            ok = jnp.logical_and(row - s >= 0, row - s < rows)
            acc = acc + jnp.where(ok, xs_, 0.0) * w_ref[j:j + 1, :]
        return (acc * _sigmoid(acc) * scale).astype(BF16)

    qs[0:lc, :] = conv_silu(qc_ref, cwq_ref, cbq_ref, M_DH ** -0.5)
    qs[lc:n, :] = conv_silu(ql_ref, cwq_ref, cbq_ref, M_DH ** -0.5)
    ks[0:lc, :] = conv_silu(kc_ref, cwk_ref, cbk_ref, 1.0)
    ks[lc:n, :] = conv_silu(kl_ref, cwk_ref, cbk_ref, 1.0)
    vs[0:lc, :] = vc_ref[...]
    vs[lc:n, :] = vl_ref[...]

    lane = lax.broadcasted_iota(jnp.int32, (1, LANES), 1)
    is_f = jnp.logical_or(jnp.logical_and(lane >= M_HPB, lane < 2 * M_HPB),
                          jnp.logical_and(lane >= 3 * M_HPB, lane < 4 * M_HPB))

    def gate_prep(g_ref, lo, rows):
        gx = g_ref[...]
        lsg = jnp.minimum(gx, 0.0) - jnp.log(1.0 + jnp.exp(-jnp.abs(gx)))
        x = jnp.where(is_f, lsg, gx)
        xs[lo:lo + rows, :] = x
        pos = lax.broadcasted_iota(jnp.int32, (rows, 1), 0) % L
        pre = x
        suf = x
        k = 1
        while k < L:
            pre = pre + jnp.where(pos >= k, pltpu.roll(pre, k, 0), 0.0)
            suf = suf + jnp.where(pos < L - k, pltpu.roll(suf, rows - k, 0), 0.0)
            k *= 2
        bcs[lo:lo + rows, :] = jnp.where(lane >= 2 * M_HPB, suf, pre)

    gate_prep(gc_ref, 0, lc)
    gate_prep(gl_ref, lc, t)

    ct[...] = jnp.zeros_like(ct)
    nv[...] = jnp.zeros_like(nv)

    r_i = lax.broadcasted_iota(jnp.int32, (L, L), 0)
    c_i = lax.broadcasted_iota(jnp.int32, (L, L), 1)
    eye = r_i == c_i
    masks = (c_i <= r_i, c_i >= r_i)

    def chain_step(c, j, d, m):
        idx = d * M_HPB + j
        r0 = pl.multiple_of(c * L, L)
        hs = slice(j * M_DH, (j + 1) * M_DH)
        q = qs[pl.ds(r0, L), hs]
        k = ks[pl.ds(r0, L), hs]
        v = vs[pl.ds(r0, L), hs]
        xg = xs[pl.ds(r0, L), :]
        bg = bcs[pl.ds(r0, L), :]
        li = 2 * d * M_HPB + j
        lf = (2 * d + 1) * M_HPB + j
        ic = xg[:, li:li + 1]
        a = bg[:, lf:lf + 1]
        b_last = a[L - 1:L, :] if d == 0 else a[0:1, :]
        d_col = ic - a
        d_row = jnp.sum(jnp.where(eye, d_col, 0.0), axis=0, keepdims=True)
        logw = jnp.where(masks[d], a + d_row, -jnp.inf)
        inter = a + m
        m_t = jnp.maximum(inter, jnp.max(logw, axis=1, keepdims=True))
        w = jnp.exp(logw - m_t) * _dot_nt(q, k)
        decay = jnp.exp(inter - m_t)
        ctb = ct[idx].astype(BF16)
        num = _dot(w.astype(BF16), v) + decay * _dot(q, ctb)
        nrow = nv[idx]
        den = jnp.sum(w, axis=1, keepdims=True) + decay * jnp.sum(q.astype(F32) * nrow, axis=1, keepdims=True)
        h = num / jnp.maximum(jnp.abs(den), jnp.exp(-m_t))
        if d == 0:
            hf[pl.ds(r0, L), hs] = h
        else:
            hb[pl.ds(r0, L), hs] = h
        src = b_last - a + ic
        m_new = jnp.maximum(b_last + m, jnp.max(src, axis=0, keepdims=True))
        gsc = jnp.exp(src - m_new)
        keep = jnp.exp(b_last + m - m_new)
        gv = (gsc * v.astype(F32)).astype(BF16)
        ct[idx] = keep * ct[idx] + _dot_tn(k, gv)
        nv[idx] = keep * nrow + jnp.sum(gsc * k.astype(F32), axis=0, keepdims=True)
        return m_new

    def body(i, ms):
        cf = i
        cb = jnp.where(i < ncc, ncc - 1 - i, nch - 1 + ncc - i)
        out = []
        for d in range(2):
            for j in range(M_HPB):
                out.append(chain_step(cf if d == 0 else cb, j, d, ms[d * M_HPB + j]))
        return tuple(out)

    lax.fori_loop(0, nch, body, tuple(jnp.zeros((1, 1), F32) for _ in range(2 * M_HPB)))

    for j in range(M_HPB):
        hs = slice(j * M_DH, (j + 1) * M_DH)
        hn = _rms(hf[:, hs] + hb[:, hs]) * ng_ref[:, hs]
        yc_ref[:, hs] = (hn[0:lc] * _sigmoid(oc_ref[:, hs].astype(F32))).astype(BF16)
        yl_ref[:, hs] = (hn[lc:n] * _sigmoid(ol_ref[:, hs].astype(F32))).astype(BF16)


def _mlstm(g, pm, pg, conv_w, conv_b, norm_g):
    w = M_HPB * M_DH
    nhb = M_HEADS // M_HPB
    cb = BRANCH_W // w
    ctx0 = g.rl // g.lc
    gcol = 0
    n = g.lc + g.t

    def lat(seg):
        return pl.BlockSpec((g.t, w), lambda b, hb: (b, seg * cb + hb))

    def ctx(seg):
        return pl.BlockSpec((g.lc, w), lambda b, hb: (ctx0 + b, seg * cb + hb))

    return pl.pallas_call(
        functools.partial(_mlstm_kernel, lc=g.lc, t=g.t),
        grid=(g.b, nhb),
        in_specs=[
            lat(0), lat(1), lat(2), lat(3),
            pl.BlockSpec((g.t, LANES), lambda b, hb: (b, gcol + hb)),
            ctx(0), ctx(1), ctx(2), ctx(3),
            pl.BlockSpec((g.lc, LANES), lambda b, hb: (ctx0 + b, gcol + hb)),
            pl.BlockSpec((M_CONV, w), lambda b, hb: (0, hb)),
            pl.BlockSpec((M_CONV, w), lambda b, hb: (0, cb + hb)),
            pl.BlockSpec((1, w), lambda b, hb: (0, hb)),
            pl.BlockSpec((1, w), lambda b, hb: (0, cb + hb)),
            pl.BlockSpec((1, w), lambda b, hb: (0, hb)),
        ],
        out_specs=[
            pl.BlockSpec((g.t, w), lambda b, hb: (b, hb)),
            pl.BlockSpec((g.lc, w), lambda b, hb: (b, hb)),
        ],
        out_shape=[
            jax.ShapeDtypeStruct((g.rl, BRANCH_W), BF16),
            jax.ShapeDtypeStruct((g.rc, BRANCH_W), BF16),
        ],
        scratch_shapes=[
            pltpu.VMEM((n, w), BF16), pltpu.VMEM((n, w), BF16), pltpu.VMEM((n, w), BF16),
            pltpu.VMEM((n, LANES), F32), pltpu.VMEM((n, LANES), F32),
            pltpu.VMEM((n, w), F32), pltpu.VMEM((n, w), F32),
            pltpu.VMEM((2 * M_HPB, M_DH, M_DH), F32), pltpu.VMEM((2 * M_HPB, 1, M_DH), F32),
        ],
        compiler_params=_cp("parallel", "parallel"),
        name="mlstm",
    )(pm, pm, pm, pm, pg, pm, pm, pm, pm, pg, conv_w, conv_w, conv_b, conv_b, norm_g)


def _mconv_kernel(x_ref, prev_ref, next_ref, w_ref, b_ref, s_ref, o_ref, xe, *, blocks_per_seq, blocks_per_ctx,
                  n_lat_blocks):
    i = pl.program_id(0)
    rows = x_ref.shape[0]
    halo = prev_ref.shape[0]
    pad = M_CONV // 2
    in_lat = i < n_lat_blocks
    pos = jnp.where(in_lat, i % blocks_per_seq, (i - n_lat_blocks) % blocks_per_ctx)
    per_seq = jnp.where(in_lat, blocks_per_seq, blocks_per_ctx)
    has_prev = pos != 0
    has_next = pos != per_seq - 1
    xe[0:halo, :] = jnp.where(has_prev, prev_ref[...].astype(F32), 0.0)
    xe[halo:halo + rows, :] = x_ref[...].astype(F32)
    xe[halo + rows:halo + rows + halo, :] = jnp.where(has_next, next_ref[...].astype(F32), 0.0)
    acc = jnp.broadcast_to(b_ref[...], (rows, x_ref.shape[1]))
    for j in range(M_CONV):
        acc = acc + xe[halo - pad + j:halo - pad + j + rows, :] * w_ref[j:j + 1, :]
    o_ref[...] = (acc * _sigmoid(acc) * s_ref[...]).astype(o_ref.dtype)


def _mconv(g, pm, conv_w, conv_b, scale_row):
    rows = g.mblk
    halo = 16
    tn = 512
    per = rows // halo
    nblk = g.r // rows
    last = g.r // halo - 1
    return pl.pallas_call(
        functools.partial(_mconv_kernel, blocks_per_seq=g.t // rows, blocks_per_ctx=g.lc // rows,
                          n_lat_blocks=g.rl // rows),
        grid=(nblk, 2 * BRANCH_W // tn),
        in_specs=[
            pl.BlockSpec((rows, tn), lambda i, j: (i, j)),
            pl.BlockSpec((halo, tn), lambda i, j: (jnp.maximum(i * per - 1, 0), j)),
            pl.BlockSpec((halo, tn), lambda i, j: (jnp.minimum((i + 1) * per, last), j)),
            pl.BlockSpec((M_CONV, tn), lambda i, j: (0, j)),
            pl.BlockSpec((1, tn), lambda i, j: (0, j)),
            pl.BlockSpec((1, tn), lambda i, j: (0, j)),
        ],
        out_specs=pl.BlockSpec((rows, tn), lambda i, j: (i, j)),
        out_shape=jax.ShapeDtypeStruct((g.r, 2 * BRANCH_W), BF16),
        scratch_shapes=[pltpu.VMEM((rows + 2 * halo, tn), F32)],
        compiler_params=_cp("parallel", "parallel"),
        name="mlstm_conv",
    )(pm, pm, pm, conv_w, conv_b, scale_row)


def _mscan_kernel(qkf_ref, vf_ref, gf_ref, qkb_ref, vb_ref, gb_ref, ol_ref, oc_ref, ng_ref, yl_ref, yc_ref,
                  xsf, bcf, xsb, bcb, dtf, dtb, hf, hb, ct, nv, ms, *, lc, t, mblk):
    i = pl.program_id(1)
    nblk = pl.num_programs(1)
    L = M_CHUNK
    cpb = mblk // L
    nbc = lc // mblk
    nbl = t // mblk
    n = lc + t

    @pl.when(i == 0)
    def _():
        ct[...] = jnp.zeros_like(ct)
        nv[...] = jnp.zeros_like(nv)
        ms[...] = jnp.zeros_like(ms)

    lane = lax.broadcasted_iota(jnp.int32, (1, LANES), 1)
    is_f = jnp.logical_or(jnp.logical_and(lane >= M_HEADS, lane < 2 * M_HEADS),
                          jnp.logical_and(lane >= 3 * M_HEADS, lane < 4 * M_HEADS))
    pos = lax.broadcasted_iota(jnp.int32, (mblk, 1), 0) % L

    def gate_prep(g_ref, x_out, bc_out, forward):
        gx = g_ref[...]
        lsg = jnp.minimum(gx, 0.0) - jnp.log(1.0 + jnp.exp(-jnp.abs(gx)))
        x = jnp.where(is_f, lsg, gx)
        x_out[...] = x
        run = x
        k = 1
        while k < L:
            if forward:
                run = run + jnp.where(pos >= k, pltpu.roll(run, k, 0), 0.0)
            else:
                run = run + jnp.where(pos < L - k, pltpu.roll(run, mblk - k, 0), 0.0)
            k *= 2
        bc_out[...] = run

    gate_prep(gf_ref, xsf, bcf, True)
    gate_prep(gb_ref, xsb, bcb, False)

    pos_f = i * mblk
    pos_b = jnp.where(i < nbc, (nbc - 1 - i) * mblk, lc + (nbl - 1 - (i - nbc)) * mblk)

    def row_table(x_ref, bc_ref, out_ref):
        dm = x_ref[...] - pltpu.roll(bc_ref[...], LANES - M_HEADS, 1)
        for c in range(cpb):
            out_ref[c] = dm[c * L:(c + 1) * L, :].T

    row_table(xsf, bcf, dtf)
    row_table(xsb, bcb, dtb)

    H = M_HEADS
    r_i = lax.broadcasted_iota(jnp.int32, (H * L, L), 0) % L
    c_i = lax.broadcasted_iota(jnp.int32, (H * L, L), 1)
    masks = (c_i <= r_i, c_i >= r_i)

    def stack(parts):
        return jnp.concatenate(parts, axis=0)

    def dir_step(c, d, mvals):
        qk_ref, v_ref, x_ref, bc_ref, dt_ref, h_ref, base = (
            (qkf_ref, vf_ref, xsf, bcf, dtf, hf, pos_f), (qkb_ref, vb_ref, xsb, bcb, dtb, hb, pos_b))[d]
        r0 = pl.multiple_of(c * L, L)
        xg = x_ref[pl.ds(r0, L), :]
        bg = bc_ref[pl.ds(r0, L), :]
        dt = dt_ref[c]
        qs = [qk_ref[pl.ds(r0, L), j * M_DH:(j + 1) * M_DH] for j in range(H)]
        ks = [qk_ref[pl.ds(r0, L), BRANCH_W + j * M_DH:BRANCH_W + (j + 1) * M_DH] for j in range(H)]
        vs_ = [v_ref[pl.ds(r0, L), j * M_DH:(j + 1) * M_DH] for j in range(H)]
        li0 = 2 * d * H
        lf0 = (2 * d + 1) * H
        a_h = [bg[:, lf0 + j:lf0 + j + 1] for j in range(H)]
        bl_h = [a[L - 1:L, :] if d == 0 else a[0:1, :] for a in a_h]
        a4 = stack(a_h)
        d_rows = [dt[li0 + j:li0 + j + 1, :] for j in range(H)]
        d4 = stack([jnp.broadcast_to(r_, (L, L)) for r_ in d_rows])
        s4 = stack([_dot_nt(qs[j], ks[j]) for j in range(H)])
        logw = jnp.where(masks[d], a4 + d4, -jnp.inf)
        m_loc = jnp.max(logw, axis=1, keepdims=True)
        w4b = (jnp.exp(logw - m_loc) * s4).astype(BF16)
        den_loc = _dot(w4b, jnp.ones((L, LANES), BF16))[:, 0:1]
        num_loc = [_dot(w4b[j * L:(j + 1) * L, :], vs_[j]) for j in range(H)]
        s_rows = [bl_h[j] + d_rows[j] for j in range(H)]
        msrc_h = [jnp.max(s_, axis=1, keepdims=True) for s_ in s_rows]
        g_rows = [jnp.exp(s_rows[j] - msrc_h[j]) for j in range(H)]
        k_t = [ks[j].astype(F32).T for j in range(H)]
        d_c = [_dot((k_t[j] * g_rows[j]).astype(BF16), vs_[j]) for j in range(H)]
        d_n = [_dot(jnp.broadcast_to(g_rows[j], (MOD_ROWS, L)).astype(BF16), ks[j])[0:1, :] for j in range(H)]
        idx = [d * H + j for j in range(H)]
        nrow = [nv[i_] for i_ in idx]
        inter = [_dot(qs[j], ct[idx[j]].astype(BF16)) for j in range(H)]
        qn4 = stack([_dot_nt(qs[j], jnp.broadcast_to(nrow[j], (MOD_ROWS, M_DH)).astype(BF16))[:, 0:1]
                     for j in range(H)])
        m4 = stack([jnp.broadcast_to(mvals[idx[j]], (L, 1)) for j in range(H)])
        bm4 = a4 + m4
        m_t = jnp.maximum(bm4, m_loc)
        e_loc = jnp.exp(m_loc - m_t)
        e_int = jnp.exp(bm4 - m_t)
        den4 = e_loc * den_loc + e_int * qn4
        r4 = 1.0 / jnp.maximum(jnp.abs(den4), jnp.exp(-m_t))
        el = e_loc * r4
        ei = e_int * r4
        rows_out = pl.ds(pl.multiple_of(base + r0, L), L)
        out = list(mvals)
        for j in range(H):
            rs = slice(j * L, (j + 1) * L)
            h_ref[rows_out, j * M_DH:(j + 1) * M_DH] = el[rs] * num_loc[j] + ei[rs] * inter[j]
            m_old = mvals[idx[j]]
            m_new = jnp.maximum(bl_h[j] + m_old, msrc_h[j])
            keep = jnp.exp(bl_h[j] + m_old - m_new)
            scale = jnp.exp(msrc_h[j] - m_new)
            ct[idx[j]] = keep * ct[idx[j]] + scale * d_c[j]
            nv[idx[j]] = keep * nrow[j] + scale * d_n[j]
            out[idx[j]] = m_new
        return tuple(out)

    def body(s, mvals):
        mvals = dir_step(s, 0, mvals)
        return dir_step(cpb - 1 - s, 1, mvals)

    m0 = tuple(ms[q_:q_ + 1, 0:1] for q_ in range(2 * M_HEADS))
    m1 = lax.fori_loop(0, cpb, body, m0, unroll=2)
    for q_ in range(2 * M_HEADS):
        ms[q_:q_ + 1, :] = jnp.broadcast_to(m1[q_], (1, LANES))

    @pl.when(i == nblk - 1)
    def _():
        for blk in range(n // mblk):
            rs = slice(blk * mblk, (blk + 1) * mblk)
            for j in range(M_HEADS):
                hs = slice(j * M_DH, (j + 1) * M_DH)
                hn = _rms(hf[rs, hs] + hb[rs, hs]) * ng_ref[:, hs]
                if blk < nbc:
                    yc_ref[rs, hs] = (hn * _sigmoid(oc_ref[rs, hs].astype(F32))).astype(BF16)
                else:
                    ls = slice(blk * mblk - lc, (blk + 1) * mblk - lc)
                    yl_ref[ls, hs] = (hn * _sigmoid(ol_ref[ls, hs].astype(F32))).astype(BF16)


def _mscan(g, qkc, pm, pg, norm_g):
    mblk = g.mblk
    nbc, nbl = g.lc // mblk, g.t // mblk
    lat_blocks = g.rl // mblk
    n = g.lc + g.t

    def fwd(b, i):
        return jnp.where(i < nbc, lat_blocks + b * nbc + i, b * nbl + (i - nbc))

    def bwd(b, i):
        return jnp.where(i < nbc, lat_blocks + b * nbc + (nbc - 1 - i), b * nbl + (nbl - 1 - (i - nbc)))

    v_col = 2 * BRANCH_W // BRANCH_W
    o_col = 3 * BRANCH_W // BRANCH_W
    ctx0 = g.rl // g.lc

    def side(blk_of):
        return [pl.BlockSpec((mblk, 2 * BRANCH_W), lambda b, i: (blk_of(b, i), 0)),
                pl.BlockSpec((mblk, BRANCH_W), lambda b, i: (blk_of(b, i), v_col)),
                pl.BlockSpec((mblk, LANES), lambda b, i: (blk_of(b, i), 0))]

    return pl.pallas_call(
        functools.partial(_mscan_kernel, lc=g.lc, t=g.t, mblk=mblk),
        grid=(g.b, nbc + nbl),
        in_specs=[
            *side(fwd), *side(bwd),
            pl.BlockSpec((g.t, BRANCH_W), lambda b, i: (b, o_col)),
            pl.BlockSpec((g.lc, BRANCH_W), lambda b, i: (ctx0 + b, o_col)),
            pl.BlockSpec((1, BRANCH_W), lambda b, i: (0, 0)),
        ],
        out_specs=[
            pl.BlockSpec((g.t, BRANCH_W), lambda b, i: (b, 0)),
            pl.BlockSpec((g.lc, BRANCH_W), lambda b, i: (b, 0)),
        ],
        out_shape=[
            jax.ShapeDtypeStruct((g.rl, BRANCH_W), BF16),
            jax.ShapeDtypeStruct((g.rc, BRANCH_W), BF16),
        ],
        scratch_shapes=[
            pltpu.VMEM((mblk, LANES), F32), pltpu.VMEM((mblk, LANES), F32),
            pltpu.VMEM((mblk, LANES), F32), pltpu.VMEM((mblk, LANES), F32),
            pltpu.VMEM((mblk // M_CHUNK, LANES, M_CHUNK), F32), pltpu.VMEM((mblk // M_CHUNK, LANES, M_CHUNK), F32),
            pltpu.VMEM((n, BRANCH_W), F32), pltpu.VMEM((n, BRANCH_W), F32),
            pltpu.VMEM((2 * M_HEADS, M_DH, M_DH), F32), pltpu.VMEM((2 * M_HEADS, 1, M_DH), F32),
            pltpu.VMEM((2 * M_HEADS, LANES), F32),
        ],
        compiler_params=_cp("parallel", "arbitrary"),
        name="mlstm_scan",
    )(qkc, pm, pg, qkc, pm, pg, pm, pm, norm_g)


def _mla_prep_kernel(cq_ref, ckv_ref, kr_ref, wuq_ref, wukv_ref, cqg_ref, ckvg_ref, qng_ref, qrg_ref,
                     kng_ref, krg_ref, cos_ref, sin_ref, aq_ref, ak_ref, av_ref, wuq_s, wukv_s):
    @pl.when(pl.program_id(0) == 0)
    def _():
        wuq_s[...] = wuq_ref[...].astype(BF16)
        wukv_s[...] = wukv_ref[...].astype(BF16)

    half = LANES // 2
    lane = lax.broadcasted_iota(jnp.int32, (1, LANES), 1)
    lo = lane < half
    first = (lane % half) < (half // 2)

    def rms_half(x):
        x2 = x * x
        s_lo = jnp.sum(jnp.where(lo, x2, 0.0), axis=-1, keepdims=True)
        s_hi = jnp.sum(jnp.where(lo, 0.0, x2), axis=-1, keepdims=True)
        ms = jnp.where(lo, s_lo, s_hi) * (1.0 / half)
        return x * lax.rsqrt(ms + EPS)

    def rows_group(rs):
        cos = cos_ref[rs, :]
        sin = sin_ref[rs, :]

        def rope_half(x):
            partner = jnp.where(first, pltpu.roll(x, LANES - half // 2, 1), pltpu.roll(x, half // 2, 1))
            return x * cos + partner * sin

        cq = (_rms(cq_ref[rs, :]) * cqg_ref[...]).astype(BF16)
        q = _dot(cq, wuq_s[...])
        ckv = (_rms(ckv_ref[rs, :]) * ckvg_ref[...]).astype(BF16)
        kv = _dot(ckv, wukv_s[...])

        krn = rope_half(rms_half(kr_ref[rs, :]) * krg_ref[...]).astype(BF16)

        rope0 = MLA_HEADS * MLA_NOPE
        for hp in range(MLA_HEADS // 2):
            slab = q[:, rope0 + hp * LANES: rope0 + (hp + 1) * LANES]
            r = rope_half(rms_half(slab) * qrg_ref[...])
            parts = (jnp.where(lo, r, 0.0), jnp.where(lo, pltpu.roll(r, half, 1), 0.0))
            for e in range(2):
                h = 2 * hp + e
                nope = _rms(q[:, h * MLA_NOPE:(h + 1) * MLA_NOPE]) * qng_ref[...]
                aq_ref[rs, h * MLA_HW: h * MLA_HW + MLA_NOPE] = nope.astype(BF16)
                aq_ref[rs, h * MLA_HW + MLA_NOPE:(h + 1) * MLA_HW] = parts[e].astype(BF16)
        kvw = MLA_NOPE + MLA_DV
        for h in range(MLA_HEADS):
            kn = _rms(kv[:, h * kvw: h * kvw + MLA_NOPE]) * kng_ref[...]
            ak_ref[rs, h * MLA_HW: h * MLA_HW + MLA_NOPE] = kn.astype(BF16)
            ak_ref[rs, h * MLA_HW + MLA_NOPE:(h + 1) * MLA_HW] = krn
            av_ref[rs, h * MLA_DV:(h + 1) * MLA_DV] = kv[:, h * kvw + MLA_NOPE:(h + 1) * kvw].astype(BF16)

    rows_group(slice(0, cq_ref.shape[0]))


def _mla_prep(g, pc, wuq, wukv, gains, cos_t, sin_t):
    tm = g.tm
    full = lambda shape: pl.BlockSpec(shape, lambda i: (0, 0))
    kr_col = (MLA_Q_RANK + MLA_KV_RANK) // LANES
    return pl.pallas_call(
        _mla_prep_kernel,
        grid=(g.n_tiles,),
        in_specs=[
            pl.BlockSpec((tm, MLA_Q_RANK), lambda i: (i, 0)),
            pl.BlockSpec((tm, MLA_KV_RANK), lambda i: (i, MLA_Q_RANK // MLA_KV_RANK)),
            pl.BlockSpec((tm, LANES), lambda i: (i, kr_col)),
            full(wuq.shape), full(wukv.shape),
            full((1, MLA_Q_RANK)), full((1, MLA_KV_RANK)),
            full((1, LANES)), full((1, LANES)), full((1, LANES)), full((1, LANES)),
            pl.BlockSpec((tm, LANES), lambda i: (g.pos_block(i), 0)),
            pl.BlockSpec((tm, LANES), lambda i: (g.pos_block(i), 0)),
        ],
        out_specs=[
            pl.BlockSpec((tm, MLA_HEADS * MLA_HW), lambda i: (i, 0)),
            pl.BlockSpec((tm, MLA_HEADS * MLA_HW), lambda i: (i, 0)),
            pl.BlockSpec((tm, BRANCH_W), lambda i: (i, 0)),
        ],
        out_shape=[
            jax.ShapeDtypeStruct((g.r, MLA_HEADS * MLA_HW), BF16),
            jax.ShapeDtypeStruct((g.r, MLA_HEADS * MLA_HW), BF16),
            jax.ShapeDtypeStruct((g.r, BRANCH_W), BF16),
        ],
        scratch_shapes=[pltpu.VMEM(wuq.shape, BF16), pltpu.VMEM(wukv.shape, BF16)],
        compiler_params=_cp("arbitrary"),
        name="mla_prep",
    )(pc, pc, pc, wuq, wukv, *gains, cos_t, sin_t)


def _attn_kernel(*refs, n_soft, dh, has_lat, diff, lam_init):
    refs = list(refs)
    vs = refs.pop()
    kts = refs.pop()
    o_ref = refs.pop()
    if diff:
        lam_ref, sg_ref = refs[0], refs[1]
        refs = refs[2:]
    q_ref, kc_ref, vc_ref = refs[:3]
    lc = kc_ref.shape[0]

    @pl.when(pl.program_id(2) == 0)
    def _():
        kts[:, 0:lc] = kc_ref[...].T
        vs[0:lc, :] = vc_ref[...]
        if has_lat:
            kts[:, lc:] = refs[3][...].T
            vs[lc:, :] = refs[4][...]

    q = q_ref[...]
    tq = q.shape[0]
    n_sub = 4 if tq % 64 == 0 else 1
    rsub = tq // n_sub
    outs = []
    for s in range(n_soft):
        sl = slice(s * dh, (s + 1) * dh)
        scs = [_dot(q[u * rsub:(u + 1) * rsub, sl], kts[sl, :]) for u in range(n_sub)]
        ps, ls = [], []
        for sc in scs:
            m = jnp.max(sc, axis=-1, keepdims=True)
            p = jnp.exp(sc - m)
            ls.append(jnp.sum(p, axis=-1, keepdims=True))
            ps.append(p.astype(BF16))
        os_ = [_dot(p, vs[...]) / l for p, l in zip(ps, ls)]
        outs.append(jnp.concatenate(os_, axis=0))
    if diff:
        lp = lam_ref[...]
        lam = (jnp.exp(jnp.sum(lp[0:1] * lp[1:2], axis=-1, keepdims=True))
               - jnp.exp(jnp.sum(lp[2:3] * lp[3:4], axis=-1, keepdims=True)) + lam_init)
        o = outs[0] - lam * outs[1]
        o = _rms(o) * sg_ref[...] * (1.0 - lam_init)
    else:
        o = outs[0]
    o_ref[...] = o.astype(o_ref.dtype)


def _attention(g, q_arr, k_arr, v_arr, *, q_col0, k_col0, heads, qk_w, dv, n_soft, ctx_queries,
               lam=None, subln_g=None, lam_init=0.0):
    diff = lam is not None
    ctx0 = g.rl // g.lc
    n_keys = g.lc if ctx_queries else g.lc + g.t
    if ctx_queries:
        tq = g.lc
        nq = 1
        q_row = lambda b, qi: ctx0 + b
        out_rows = g.rc
        o_row = lambda b, qi: b
    else:
        tq = min(ATT_TQ, g.t)
        nq = g.t // tq
        q_row = lambda b, qi: b * nq + qi
        out_rows = g.rl
        o_row = q_row
    in_specs = []
    args = []
    if diff:
        in_specs += [pl.BlockSpec(lam.shape, lambda b, h, qi: (0, 0)),
                     pl.BlockSpec((1, dv), lambda b, h, qi: (0, 0))]
        args += [lam, subln_g]
    in_specs += [
        pl.BlockSpec((tq, qk_w), lambda b, h, qi: (q_row(b, qi), q_col0 + h)),
        pl.BlockSpec((g.lc, qk_w), lambda b, h, qi: (ctx0 + b, k_col0 + h)),
        pl.BlockSpec((g.lc, dv), lambda b, h, qi: (ctx0 + b, h)),
    ]
    args += [q_arr, k_arr, v_arr]
    if not ctx_queries:
        in_specs += [
            pl.BlockSpec((g.t, qk_w), lambda b, h, qi: (b, k_col0 + h)),
            pl.BlockSpec((g.t, dv), lambda b, h, qi: (b, h)),
        ]
        args += [k_arr, v_arr]
    return pl.pallas_call(
        functools.partial(_attn_kernel, n_soft=n_soft, dh=qk_w // n_soft, has_lat=not ctx_queries,
                          diff=diff, lam_init=lam_init),
        grid=(g.b, heads, nq),
        in_specs=in_specs,
        out_specs=pl.BlockSpec((tq, dv), lambda b, h, qi: (o_row(b, qi), h)),
        out_shape=jax.ShapeDtypeStruct((out_rows, heads * dv), BF16),
        scratch_shapes=[pltpu.VMEM((qk_w, n_keys), BF16), pltpu.VMEM((n_keys, dv), BF16)],
        compiler_params=_cp("parallel", "parallel", "arbitrary"),
        name="attn_diff" if diff else "attn_mla",
    )(*args)


def _branch_kernel(*refs, n_src, n_lat_tiles):
    y_refs = refs[:N_BRANCH * n_src]
    wb_ref, g0_ref, g1_ref, g2_ref, o_ref, wb_s = refs[N_BRANCH * n_src:]
    gates = (g0_ref, g1_ref, g2_ref)

    @pl.when(pl.program_id(1) == 0)
    def _():
        wb_s[...] = wb_ref[...].astype(BF16)

    def body(which):
        def run(_):
            acc = None
            for n in range(N_BRANCH):
                term = gates[n][...].astype(F32) * _dot(y_refs[n * n_src + which][...], wb_s[n])
                acc = term if acc is None else acc + term
            o_ref[...] = acc.astype(o_ref.dtype)
        return run

    i = pl.program_id(1)
    if n_src == 1:
        body(0)(None)
    else:
        pl.when(i < n_lat_tiles)(lambda: body(0)(None))
        pl.when(i >= n_lat_tiles)(lambda: body(1)(None))


def _branch_merge(g, ys, w_branch, l, gt, n_tiles):
    tn = 512
    nj = g.d // tn
    y_specs, y_args = [], []
    for y in ys:
        sp, ar = _row_sources(g, y, BRANCH_W, lambda j: 0, 2)
        y_specs += sp
        y_args += ar
    n_src = len(y_args) // N_BRANCH
    return pl.pallas_call(
        functools.partial(_branch_kernel, n_src=n_src, n_lat_tiles=g.n_lat_tiles),
        grid=(nj, n_tiles),
        in_specs=[
            *y_specs,
            pl.BlockSpec((None, N_BRANCH, BRANCH_W, tn), lambda j, i: (l, 0, 0, j)),
            pl.BlockSpec((g.tm, tn), lambda j, i: (i, j)),
            pl.BlockSpec((g.tm, tn), lambda j, i: (i, nj + j)),
            pl.BlockSpec((g.tm, tn), lambda j, i: (i, 2 * nj + j)),
        ],
        out_specs=pl.BlockSpec((g.tm, tn), lambda j, i: (i, j)),
        out_shape=jax.ShapeDtypeStruct((n_tiles * g.tm, g.d), BF16),
        scratch_shapes=[pltpu.VMEM((N_BRANCH, BRANCH_W, tn), BF16)],
        compiler_params=_cp("arbitrary", "arbitrary"),
        name="branch_merge",
    )(*y_args, w_branch, gt, gt, gt)


def _outproj_kernel(*refs, n_lat_tiles):
    z_ref, w_ref, *x_refs, g_ref, o_ref, w_s = refs

    @pl.when(pl.program_id(1) == 0)
    def _():
        w_s[...] = w_ref[...].astype(BF16)

    def body(x_ref):
        o_ref[...] = x_ref[...] + g_ref[...] * _dot(z_ref[...], w_s[...])

    _on_rows(pl.program_id(1), n_lat_tiles, x_refs, body)


def _outproj(g, z, w_out, l, x_src, mods3, which_gate, n_tiles):
    tn = 512
    per = g.d // tn
    x_specs, x_args = _row_sources(g, x_src, tn, lambda j: j, 2)
    return pl.pallas_call(
        functools.partial(_outproj_kernel, n_lat_tiles=g.n_lat_tiles),
        grid=(per, n_tiles),
        in_specs=[
            pl.BlockSpec((g.tm, g.d), lambda j, i: (i, 0)),
            pl.BlockSpec((None, g.d, tn), lambda j, i: (l, 0, j)),
            *x_specs,
            pl.BlockSpec((None, 1, tn), lambda j, i: (g.mod_row(i), 0, which_gate * per + j)),
        ],
        out_specs=pl.BlockSpec((g.tm, tn), lambda j, i: (i, j)),
        out_shape=jax.ShapeDtypeStruct((n_tiles * g.tm, g.d), F32),
        scratch_shapes=[pltpu.VMEM((g.d, tn), BF16)],
        compiler_params=_cp("arbitrary", "arbitrary"),
        name="outproj",
    )(z, w_out, *x_args, mods3)


def _moe_prep_kernel(x_ref, sh_ref, sc_ref, rwt_ref, rb_ref, tri_ref, xr_ref, rank_ref, grp_ref, cnt_ref,
                     carry, *, mod_row):
    i = pl.program_id(0)

    @pl.when(i == 0)
    def _():
        carry[...] = jnp.zeros_like(carry)

    x = x_ref[...]
    h = _rms(x) * (1.0 + sc_ref[...]) + sh_ref[...]
    logits = _dot_nt(rwt_ref[...].astype(BF16), h.astype(BF16))
    scores = _sigmoid(logits)
    selm = scores + rb_ref[...]
    sc_rows = [scores[e:e + 1, :] for e in range(N_EXPERTS)]
    sel = [selm[e:e + 1, :] for e in range(N_EXPERTS)]
    gscore = []
    for gi in range(N_GROUPS):
        v = sel[gi * GROUP_SIZE:(gi + 1) * GROUP_SIZE]
        pair = [v[a] + v[b] for a in range(GROUP_SIZE) for b in range(a + 1, GROUP_SIZE)]
        gscore.append(functools.reduce(jnp.maximum, pair))
    gmax = functools.reduce(jnp.maximum, gscore)
    taken = jnp.zeros_like(gmax, dtype=jnp.bool_)
    in_group = []
    for gi in range(N_GROUPS):
        hit = jnp.logical_and(gscore[gi] == gmax, jnp.logical_not(taken))
        taken = jnp.logical_or(taken, hit)
        in_group.append(hit)
    neg = -jnp.inf
    masked = [jnp.where(in_group[e // GROUP_SIZE], sel[e], neg) for e in range(N_EXPERTS)]

    def pick(vals):
        mx = functools.reduce(jnp.maximum, vals)
        seen = jnp.zeros_like(mx, dtype=jnp.bool_)
        hot = []
        for v in vals:
            hit = jnp.logical_and(v == mx, jnp.logical_not(seen))
            seen = jnp.logical_or(seen, hit)
            hot.append(hit)
        return hot

    hot1 = pick(masked)
    hot2 = pick([jnp.where(hot1[e], neg, masked[e]) for e in range(N_EXPERTS)])
    zero = jnp.zeros_like(gmax)
    s1 = functools.reduce(lambda a, b: a + b, [jnp.where(hot1[e], sc_rows[e], zero) for e in range(N_EXPERTS)])
    s2 = functools.reduce(lambda a, b: a + b, [jnp.where(hot2[e], sc_rows[e], zero) for e in range(N_EXPERTS)])
    tot = s1 + s2
    w1 = s1 / tot
    w2 = s2 / tot
    comb = [jnp.where(hot1[e], w1, zero) + jnp.where(hot2[e], w2, zero) for e in range(N_EXPERTS)]

    tm = x.shape[0]
    gmat = jnp.concatenate([jnp.where(in_group[gi], 1.0, zero) for gi in range(N_GROUPS)]
                           + [jnp.zeros((MOD_ROWS - N_GROUPS, tm), F32)], axis=0)
    incl = _dot(gmat.astype(BF16), tri_ref[...])
    excl = incl - gmat + carry[:, 0:1]
    rank = functools.reduce(lambda a, b: a + b,
                            [jnp.where(in_group[gi], excl[gi:gi + 1, :], zero) for gi in range(N_GROUPS)])
    grp = functools.reduce(lambda a, b: a + b,
                           [jnp.where(in_group[gi], float(gi), zero) for gi in range(N_GROUPS)])
    rank_ref[...] = rank.astype(jnp.int32)
    grp_ref[...] = grp.astype(jnp.int32)
    carry[...] = carry[...] + jnp.sum(gmat, axis=1, keepdims=True)
    cnt_ref[...] = carry[...]

    side = jnp.concatenate(comb + [jnp.zeros((LANES - N_EXPERTS, tm), F32)], axis=0).T
    lane = lax.broadcasted_iota(jnp.int32, (1, LANES), 1)
    side = jnp.where(lane == N_EXPERTS + mod_row(i), 1.0, side)
    d = x.shape[1]
    xr_ref[:, 0:d] = x
    xr_ref[:, d:d + LANES] = side


def _moe_prep(g, x_mid, mods3, which_shift, which_scale, rwt, rb, tri, n_tiles):
    n_rows = n_tiles * g.tm
    return pl.pallas_call(
        functools.partial(_moe_prep_kernel, mod_row=g.mod_row),
        grid=(n_tiles,),
        in_specs=[
            pl.BlockSpec((g.tm, g.d), lambda i: (i, 0)),
            _mod_spec(g, which_shift),
            _mod_spec(g, which_scale),
            pl.BlockSpec((N_EXPERTS, g.d), lambda i: (0, 0)),
            pl.BlockSpec((N_EXPERTS, 1), lambda i: (0, 0)),
            pl.BlockSpec((g.tm, g.tm), lambda i: (0, 0)),
        ],
        out_specs=[
            pl.BlockSpec((g.tm, g.d + LANES), lambda i: (i, 0)),
            pl.BlockSpec((1, g.tm), lambda i: (0, i)),
            pl.BlockSpec((1, g.tm), lambda i: (0, i)),
            pl.BlockSpec((MOD_ROWS, LANES), lambda i: (0, 0)),
        ],
        out_shape=[
            jax.ShapeDtypeStruct((n_rows, g.d + LANES), F32),
            jax.ShapeDtypeStruct((1, n_rows), jnp.int32),
            jax.ShapeDtypeStruct((1, n_rows), jnp.int32),
            jax.ShapeDtypeStruct((MOD_ROWS, LANES), F32),
        ],
        scratch_shapes=[pltpu.VMEM((MOD_ROWS, LANES), F32)],
        compiler_params=_cp("arbitrary"),
        name="moe_prep",
    )(x_mid, mods3, mods3, rwt, rb, tri)


def _moe_group_kernel(tg_ref, nv_ref, nu_ref, pos_ref, xr_hbm, mods_ref, w1_hbm, w3_hbm, w2_hbm, out_hbm,
                      gbuf, obuf, w1s, w3s, w2s, stg_a, stg_b, src_ref, gsem, ssem, wsem, *, n_mod_rows, e_base):
    k = pl.program_id(0)
    n_used = nu_ref[0]
    slot = k % 2
    d = obuf.shape[2]
    tme = obuf.shape[1]

    def gather_start(kk, sl):
        base = kk * tme

        first = src_ref[base]

        def copy_row(r, tok):
            pltpu.make_async_copy(xr_hbm.at[pl.ds(tok, 1), :], gbuf.at[sl, pl.ds(r, 1), :], gsem.at[sl]).start()

        def issue(r, c):
            copy_row(r, src_ref[base + r])
            return c

        def issue_pad(r, c):
            copy_row(r, first)
            return c

        rows_of(kk, issue)
        lax.fori_loop(nv_ref[kk], tme, issue_pad, 0)

    def build_sorted_index():
        def put(t, c):
            src_ref[pos_ref[t]] = t
            return c

        lax.fori_loop(0, pos_ref.shape[0], put, 0, unroll=8)

    def gather_wait(sl):
        def w(r, c):
            pltpu.make_async_copy(xr_hbm.at[pl.ds(0, 1), :], gbuf.at[sl, pl.ds(0, 1), :], gsem.at[sl]).wait()
            return c

        lax.fori_loop(0, tme, w, 0, unroll=8)

    def scatter_start(kk, sl):
        base = kk * tme

        def issue(r, c):
            tok = src_ref[base + r]
            pltpu.make_async_copy(obuf.at[sl, pl.ds(r, 1), :], out_hbm.at[pl.ds(tok, 1), :], ssem.at[sl]).start()
            return c

        rows_of(kk, issue)

    def scatter_wait(kk, sl):
        def w(r, c):
            pltpu.make_async_copy(obuf.at[sl, pl.ds(0, 1), :], out_hbm.at[pl.ds(0, 1), :], ssem.at[sl]).wait()
            return c

        rows_of(kk, w)

    def rows_of(kk, fn):
        n_rows = nv_ref[kk]

        @pl.when(n_rows == tme)
        def _():
            lax.fori_loop(0, tme, fn, 0, unroll=8)

        @pl.when(n_rows != tme)
        def _():
            lax.fori_loop(0, n_rows, fn, 0)

    def load_weights(grp):
        rows_a = stg_a.shape[1]
        rows_b = stg_b.shape[1]
        chunks = []
        n_a = n_b = 0
        for e in range(GROUP_SIZE):
            ex = e_base + grp * GROUP_SIZE + e
            for src, dst in ((w1_hbm, w1s), (w3_hbm, w3s)):
                for c0 in range(0, src.shape[1], rows_a):
                    sl = n_a % 2
                    n_a += 1
                    cp = pltpu.make_async_copy(src.at[ex, pl.ds(c0, rows_a), :], stg_a.at[sl], wsem.at[sl])
                    chunks.append((cp, stg_a.at[sl], dst, e, c0, rows_a))
            for c0 in range(0, w2_hbm.shape[1], rows_b):
                sl = n_b % 2
                n_b += 1
                cp = pltpu.make_async_copy(w2_hbm.at[ex, pl.ds(c0, rows_b), :], stg_b.at[sl], wsem.at[2 + sl])
                chunks.append((cp, stg_b.at[sl], w2s, e, c0, rows_b))
        chunks[0][0].start()
        for idx, (cp, stg, dst, e, c0, rows) in enumerate(chunks):
            if idx + 1 < len(chunks):
                chunks[idx + 1][0].start()
            cp.wait()
            dst[e, c0:c0 + rows, :] = stg[...].astype(BF16)

    @pl.when(k < n_used)
    def _():
        grp = tg_ref[k]

        @pl.when(k == 0)
        def _():
            build_sorted_index()
            gather_start(0, 0)

        @pl.when(k + 1 < n_used)
        def _():
            gather_start(k + 1, 1 - slot)

        @pl.when(jnp.logical_or(k == 0, grp != tg_ref[jnp.maximum(k - 1, 0)]))
        def _():
            load_weights(grp)

        gather_wait(slot)

        @pl.when(k >= 2)
        def _():
            scatter_wait(k - 2, slot)

        lane = lax.broadcasted_iota(jnp.int32, (1, LANES), 1)
        rows = tme // MOE_ROW_GROUPS
        for u in range(MOE_ROW_GROUPS):
            rs = slice(u * rows, (u + 1) * rows)
            x = gbuf[slot, rs, 0:d]
            side = gbuf[slot, rs, d:d + LANES]

            def cond_vec(which):
                out = jnp.zeros((rows, d), F32)
                for r in range(n_mod_rows):
                    hot = side[:, N_EXPERTS + r:N_EXPERTS + r + 1] > 0.5
                    out = jnp.where(hot, mods_ref[r:r + 1, which * d:(which + 1) * d], out)
                return out

            h = (_rms(x) * (1.0 + cond_vec(1)) + cond_vec(0)).astype(BF16)
            y = jnp.zeros((rows, d), F32)
            for e in range(GROUP_SIZE):
                a = _dot(h, w1s[e])
                b = _dot(h, w3s[e])
                cw = jnp.sum(jnp.where(lane == grp * GROUP_SIZE + e, side, 0.0), axis=1, keepdims=True)
                hid = (a * _sigmoid(a) * b * cw).astype(BF16)
                y = y + _dot(hid, w2s[e])
            obuf[slot, rs, :] = x + cond_vec(2) * y

        scatter_start(k, slot)

        @pl.when(k == n_used - 1)
        def _():
            @pl.when(k >= 1)
            def _():
                scatter_wait(k - 1, 1 - slot)

            scatter_wait(k, slot)


def _moe_group(g, xr, mods_tail, w1, w3, w2, meta, n_rows, n_mod_rows, e_base):
    tile_group, n_valid, n_used, src = meta
    kt = tile_group.shape[0]
    tme = MOE_TM
    any_spec = pl.BlockSpec(memory_space=pl.ANY)
    grid_spec = pltpu.PrefetchScalarGridSpec(
        num_scalar_prefetch=4,
        grid=(kt,),
        in_specs=[any_spec, pl.BlockSpec(mods_tail.shape, lambda k, *_: (0, 0)), any_spec, any_spec, any_spec],
        out_specs=any_spec,
        scratch_shapes=[
            pltpu.VMEM((2, tme, g.d + LANES), F32),
            pltpu.VMEM((2, tme, g.d), F32),
            pltpu.VMEM((GROUP_SIZE, g.d, D_EXPERT), BF16),
            pltpu.VMEM((GROUP_SIZE, g.d, D_EXPERT), BF16),
            pltpu.VMEM((GROUP_SIZE, D_EXPERT, g.d), BF16),
            pltpu.VMEM((2, g.d // 2, D_EXPERT), F32),
            pltpu.VMEM((2, D_EXPERT // 2, g.d), F32),
            pltpu.SMEM((kt * tme,), jnp.int32),
            pltpu.SemaphoreType.DMA((2,)),
            pltpu.SemaphoreType.DMA((2,)),
            pltpu.SemaphoreType.DMA((4,)),
        ],
    )
    return pl.pallas_call(
        functools.partial(_moe_group_kernel, n_mod_rows=n_mod_rows, e_base=e_base),
        grid_spec=grid_spec,
        out_shape=jax.ShapeDtypeStruct((n_rows, g.d), F32),
        compiler_params=_cp("arbitrary"),
        name="moe_group",
    )(tile_group, n_valid, n_used, src, xr, mods_tail, w1, w3, w2)


def _moe_meta(rank, grp, cnt, n_rows):
    tme = MOE_TM
    kt = n_rows // tme + N_GROUPS
    counts = cnt[:N_GROUPS, 0].astype(jnp.int32)
    ntile = (counts + tme - 1) // tme
    tile_end = jnp.cumsum(ntile)
    tile_off = tile_end - ntile
    gsel = [grp[0] == gi for gi in range(N_GROUPS)]
    row_off = functools.reduce(lambda a, b: a + b,
                               [jnp.where(gsel[gi], tile_off[gi] * tme, 0) for gi in range(N_GROUPS)])
    pos = row_off + rank[0]
    ks = jnp.arange(kt, dtype=jnp.int32)
    tile_group = jnp.minimum(functools.reduce(lambda a, b: a + b,
                                              [(ks >= tile_end[gi]).astype(jnp.int32) for gi in range(N_GROUPS)]),
                             N_GROUPS - 1)
    n_used = tile_end[-1:]
    cnt_k = functools.reduce(lambda a, b: a + b,
                             [jnp.where(tile_group == gi, counts[gi] - (ks - tile_off[gi]) * tme, 0)
                              for gi in range(N_GROUPS)])
    n_valid = jnp.where(ks < n_used[0], jnp.clip(cnt_k, 0, tme), 0)
    return tile_group, n_valid, n_used, pos


def _rope_tables(t, rot_dim, tile_rows, reps):
    f32 = np.float32
    rows = t // GRID_W
    r = np.repeat(np.arange(rows, dtype=f32), GRID_W)
    col = np.tile(np.arange(GRID_W, dtype=f32), rows)
    n_freq = rot_dim // 4
    inv = np.power(f32(ROPE_BASE), -np.arange(n_freq, dtype=f32) / f32(n_freq)).astype(f32)
    ang = np.concatenate([r[:, None] * inv, col[:, None] * inv], axis=-1).astype(f32)
    cos = np.tile(np.concatenate([np.cos(ang), np.cos(ang)], axis=-1), (1, reps))
    sin = np.tile(np.concatenate([-np.sin(ang), np.sin(ang)], axis=-1), (1, reps))
    cos = np.concatenate([cos, np.ones((tile_rows, LANES), f32)], axis=0).astype(f32)
    sin = np.concatenate([sin, np.zeros((tile_rows, LANES), f32)], axis=0).astype(f32)
    return jnp.asarray(cos), jnp.asarray(sin)


def _gate_perm():
    import numpy as np
    nhb = M_HEADS // M_HPB
    perm = np.zeros((nhb, LANES, LANES), np.float32)
    for hb in range(nhb):
        for gi in range(4):
            for j in range(M_HPB):
                perm[hb, gi * M_HEADS + hb * M_HPB + j, gi * M_HPB + j] = 1.0
    return jnp.asarray(perm)


def _uq_perm():
    nope = [h * MLA_QK + i for h in range(MLA_HEADS) for i in range(MLA_NOPE)]
    rope = [h * MLA_QK + MLA_NOPE + i for h in range(MLA_HEADS) for i in range(MLA_ROPE)]
    return jnp.asarray(nope + rope)


def kernel(x, c, ctx, c_ctx, w_ada, b_ada, w_in, b_in, m_conv_w, m_conv_b, m_norm_g, da_q_norm_g, da_k_norm_g,
           da_lambda, da_subln_g, mla_cq_norm_g, mla_ckv_norm_g, mla_w_uq, mla_w_ukv, mla_q_norm_g, mla_k_norm_g,
           w_branch, w_out, moe_w1, moe_w3, moe_w2, router_w, router_bias):
    bsz, t, d = x.shape
    lc = ctx.shape[1]
    depth = w_ada.shape[0]
    g = _Geom(bsz, t, lc, d, min(1024, bsz * lc, t))
    assert bsz + 1 <= MOD_ROWS

    cvec = jnp.concatenate([c, c_ctx[None, :], jnp.zeros((MOD_ROWS - bsz - 1, d), F32)], axis=0)
    mods = _mod_vectors(cvec, w_ada, b_ada)

    cos_d, sin_d = _rope_tables(t, DA_DH, g.tm, 1)
    cos_a, sin_a = _rope_tables(t, MLA_ROPE, g.tm, 2)
    rwt = router_w.T
    rb = router_bias.reshape(N_EXPERTS, 1)
    uq_perm = _uq_perm()
    x_all = (x.reshape(g.rl, d), ctx.reshape(g.rc, d))
    pos_spec = pl.BlockSpec((g.tm, LANES), lambda j, i: (g.pos_block(i), 0))
    b_in3 = b_in.reshape(depth, 1, -1)
    w_in_t = jnp.swapaxes(w_in, 1, 2)
    qk_scale = jnp.concatenate([jnp.full((1, BRANCH_W), M_DH ** -0.5, F32), jnp.ones((1, BRANCH_W), F32)], axis=1)
    tri = (jnp.arange(g.tm)[:, None] <= jnp.arange(g.tm)[None, :]).astype(BF16)
    w1r = moe_w1.reshape(depth * N_EXPERTS, d, D_EXPERT)
    w3r = moe_w3.reshape(depth * N_EXPERTS, d, D_EXPERT)
    w2r = moe_w2.reshape(depth * N_EXPERTS, D_EXPERT, d)

    for l in range(depth):
        last = l == depth - 1
        n_tiles = g.n_lat_tiles if last else g.n_tiles
        n_rows = n_tiles * g.tm
        lam_init = 0.8 - 0.6 * math.exp(-0.3 * l)
        mods3 = mods[l].reshape(MOD_ROWS, 1, 6 * d)

        h1 = _prenorm(g, x_all, mods3, 0, 1, g.n_tiles)

        row = lambda v: v.reshape(1, -1)
        proj = functools.partial(_proj, g, h=h1, w_in_t=w_in_t, b_in3=b_in3, l=l)
        pm = proj(_epi_raw, col0=OFF_MQK, width=OFF_MG - OFF_MQK, out_dtype=BF16, name="proj_mlstm")
        pg = proj(_epi_raw, col0=OFF_MG, width=LANES, out_dtype=F32, tn=LANES, name="proj_mgates")
        da_gain = jnp.concatenate([jnp.tile(da_q_norm_g[l] * DA_DH ** -0.5, 2 * DA_HEADS),
                                   jnp.tile(da_k_norm_g[l], 2 * DA_HEADS)])
        dqk = proj(_epi_normrope, col0=OFF_DQ, width=2 * DA_QK_W, out_dtype=BF16,
                   extra=(row(da_gain), cos_d, sin_d),
                   extra_specs=(pl.BlockSpec((1, 512), lambda j, i: (0, j)), pos_spec, pos_spec),
                   name="proj_dqk")
        dv = proj(_epi_raw, col0=OFF_DV, width=BRANCH_W, out_dtype=BF16, name="proj_dv")
        pc = proj(_epi_raw, col0=OFF_CQ, width=OFF_G - OFF_CQ, out_dtype=F32, name="proj_small")
        gt = proj(_epi_sigmoid, col0=OFF_G, width=N_BRANCH * d, out_dtype=BF16, name="proj_gates")

        qkc = _mconv(g, pm, m_conv_w[l], row(m_conv_b[l]), qk_scale)
        ym_l, ym_c = _mscan(g, qkc, pm, pg, row(m_norm_g[l]))

        qg, kg = mla_q_norm_g[l], mla_k_norm_g[l]
        a_scale = MLA_QK ** -0.5
        gains = (row(mla_cq_norm_g[l]), row(mla_ckv_norm_g[l]),
                 row(qg[:MLA_NOPE] * a_scale), row(jnp.tile(qg[MLA_NOPE:], 2) * a_scale),
                 row(kg[:MLA_NOPE]), row(jnp.concatenate([kg[MLA_NOPE:], jnp.zeros((LANES - MLA_ROPE,), F32)])))
        aq, ak, av = _mla_prep(g, pc, mla_w_uq[l][:, uq_perm], mla_w_ukv[l], gains, cos_a, sin_a)

        lam_p = da_lambda[l]
        sub_g = row(da_subln_g[l])
        da_kw = dict(q_col0=0, k_col0=DA_HEADS, heads=DA_HEADS, qk_w=2 * DA_DH, dv=DA_DV, n_soft=2,
                     lam=lam_p, subln_g=sub_g, lam_init=lam_init)
        mla_kw = dict(q_col0=0, k_col0=0, heads=MLA_HEADS, qk_w=MLA_HW, dv=MLA_DV, n_soft=1)
        yd_l = _attention(g, dqk, dqk, dv, ctx_queries=False, **da_kw)
        ya_l = _attention(g, aq, ak, av, ctx_queries=False, **mla_kw)
        if last:
            ys = (ym_l, yd_l, ya_l)
        else:
            yd_c = _attention(g, dqk, dqk, dv, ctx_queries=True, **da_kw)
            ya_c = _attention(g, aq, ak, av, ctx_queries=True, **mla_kw)
            ys = ((ym_l, ym_c), (yd_l, yd_c), (ya_l, ya_c))

        z = _branch_merge(g, ys, w_branch, l, gt, n_tiles)
        x_mid = _outproj(g, z, w_out, l, x_all, mods3, 2, n_tiles)

        xr, rank, grp, cnt = _moe_prep(g, x_mid, mods3, 3, 4, rwt, rb, tri, n_tiles)
        meta = _moe_meta(rank, grp, cnt, n_rows)
        x_all = _moe_group(g, xr, mods[l][:, 3 * d:], w1r, w3r, w2r, meta, n_rows, bsz + 1, l * N_EXPERTS)

    return x_all.reshape(bsz, t, d)
```

```python
import functools
import math

import jax
import jax.numpy as jnp
import numpy as np
from jax import lax
from jax.experimental import pallas as pl
from jax.experimental.pallas import tpu as pltpu

F32 = jnp.float32
BF16 = jnp.bfloat16

GRID_W = 64
ROPE_BASE = 10000.0
EPS = 1e-6
N_BRANCH = 3
BRANCH_W = 1024
M_HEADS = 4
M_DH = BRANCH_W // M_HEADS
M_CHUNK = 64
M_CONV = 5
DA_HEADS = 4
DA_DV = BRANCH_W // DA_HEADS
DA_DH = DA_DV // 2
DA_QK_W = DA_HEADS * 2 * DA_DH
MLA_HEADS = 8
MLA_Q_RANK = 512
MLA_KV_RANK = 256
MLA_NOPE = 128
MLA_ROPE = 64
MLA_DV = BRANCH_W // MLA_HEADS
MLA_QK = MLA_NOPE + MLA_ROPE
N_EXPERTS = 16
N_GROUPS = 4
GROUP_SIZE = N_EXPERTS // N_GROUPS
D_EXPERT = 512

OFF_MQK = 0
OFF_MV = OFF_MQK + 2 * BRANCH_W
OFF_MO = OFF_MV + BRANCH_W
OFF_MG = OFF_MO + BRANCH_W
OFF_DQ = OFF_MG + 4 * M_HEADS
OFF_DK = OFF_DQ + DA_QK_W
OFF_DV = OFF_DK + DA_QK_W
OFF_CQ = OFF_DV + BRANCH_W
OFF_CKV = OFF_CQ + MLA_Q_RANK
OFF_KR = OFF_CKV + MLA_KV_RANK
OFF_G = OFF_KR + MLA_ROPE

LANES = 128
MOD_ROWS = 8
VMEM_LIMIT = 56 * 1024 * 1024
MLA_HW = 256
MOE_TM = 256
ATT_TQ = 2048
PC_GATE_COL = 896
PROJ_ROW_GROUPS = 4
MOE_ROW_GROUPS = 1


def _cp(*sem):
    return pltpu.CompilerParams(dimension_semantics=sem, vmem_limit_bytes=VMEM_LIMIT)


def _rms(x):
    return x * lax.rsqrt(jnp.mean(x * x, axis=-1, keepdims=True) + EPS)


def _sigmoid(x):
    return 0.5 * jnp.tanh(0.5 * x) + 0.5


def _dot(a, b):
    return jnp.dot(a, b, preferred_element_type=F32)


def _dot_nt(a, b):
    return lax.dot_general(a, b, (((1,), (1,)), ((), ())), preferred_element_type=F32)


def _mod_kernel(c_ref, w_ref, b_ref, o_ref):
    c = c_ref[...]
    s = (c * _sigmoid(c)).astype(BF16)
    o_ref[...] = _dot(s, w_ref[...].astype(BF16)) + b_ref[...]


def _mod_vectors(cvec, w_ada, b_ada):
    depth, d, n = w_ada.shape
    tn = 1024
    return pl.pallas_call(
        _mod_kernel,
        grid=(depth, n // tn),
        in_specs=[
            pl.BlockSpec((MOD_ROWS, d), lambda l, j: (0, 0)),
            pl.BlockSpec((None, d, tn), lambda l, j: (l, 0, j)),
            pl.BlockSpec((None, 1, tn), lambda l, j: (l, 0, j)),
        ],
        out_specs=pl.BlockSpec((None, MOD_ROWS, tn), lambda l, j: (l, 0, j)),
        out_shape=jax.ShapeDtypeStruct((depth, MOD_ROWS, n), F32),
        compiler_params=_cp("parallel", "parallel"),
        name="adaln_mod",
    )(cvec, w_ada, b_ada.reshape(depth, 1, n))


class _Geom:
    def __init__(self, b, t, lc, d, tm):
        assert t % tm == 0 and (b * lc) % tm == 0 and t % lc == 0 and t % GRID_W == 0
        assert lc % M_CHUNK == 0 and t % M_CHUNK == 0
        self.b, self.t, self.lc, self.d, self.tm = b, t, lc, d, tm
        self.rl, self.rc = b * t, b * lc
        self.r = self.rl + self.rc
        self.mblk = min(256, lc)
        assert lc % self.mblk == 0 and t % self.mblk == 0 and self.mblk % M_CHUNK == 0
        self.n_lat_tiles = self.rl // tm
        self.n_tiles = self.r // tm
        self.tiles_per_seq = t // tm

    def mod_row(self, i):
        return jnp.minimum(i // self.tiles_per_seq, self.b)

    def pos_block(self, i):
        return jnp.where(i < self.n_lat_tiles, i % self.tiles_per_seq, self.tiles_per_seq)


def _mod_spec(g, which):
    return pl.BlockSpec((None, 1, g.d), lambda i: (g.mod_row(i), 0, which))


def _row_sources(g, src, width, col_of, n_grid):
    tile = lambda i: i
    if n_grid == 2:
        spec = lambda rows_of: pl.BlockSpec((g.tm, width), lambda j, i: (rows_of(i), col_of(j)))
    else:
        spec = lambda rows_of: pl.BlockSpec((g.tm, width), lambda i: (rows_of(i), 0))
    if not isinstance(src, tuple):
        return [spec(tile)], [src]
    lat, ctx = src
    nl = g.n_lat_tiles
    return ([spec(lambda i: jnp.minimum(i, nl - 1)), spec(lambda i: jnp.maximum(i - nl, 0))], [lat, ctx])


def _on_rows(i, n_lat_tiles, refs, body):
    if len(refs) == 1:
        body(refs[0])
        return

    @pl.when(i < n_lat_tiles)
    def _():
        body(refs[0])

    @pl.when(i >= n_lat_tiles)
    def _():
        body(refs[1])


def _prenorm_kernel(*refs, n_lat_tiles):
    *x_refs, sh_ref, sc_ref, o_ref = refs

    def body(x_ref):
        o_ref[...] = (_rms(x_ref[...]) * (1.0 + sc_ref[...]) + sh_ref[...]).astype(o_ref.dtype)

    _on_rows(pl.program_id(0), n_lat_tiles, x_refs, body)


def _prenorm(g, x_src, mods3, which_shift, which_scale, n_tiles):
    x_specs, x_args = _row_sources(g, x_src, g.d, None, 1)
    return pl.pallas_call(
        functools.partial(_prenorm_kernel, n_lat_tiles=g.n_lat_tiles),
        grid=(n_tiles,),
        in_specs=[*x_specs, _mod_spec(g, which_shift), _mod_spec(g, which_scale)],
        out_specs=pl.BlockSpec((g.tm, g.d), lambda i: (i, 0)),
        out_shape=jax.ShapeDtypeStruct((n_tiles * g.tm, g.d), BF16),
        compiler_params=_cp("parallel"),
        name="prenorm",
    )(*x_args, mods3, mods3)


def _epi_raw(acc, rs, o_ref):
    o_ref[rs, :] = acc.astype(o_ref.dtype)


def _epi_sigmoid(acc, rs, o_ref):
    o_ref[rs, :] = _sigmoid(acc).astype(o_ref.dtype)


def _epi_normrope(acc, rs, gain_ref, cos_ref, sin_ref, o_ref):
    cos = cos_ref[rs, :]
    sin = sin_ref[rs, :]
    for j in range(acc.shape[1] // LANES):
        sl = slice(j * LANES, (j + 1) * LANES)
        x = _rms(acc[:, sl]) * gain_ref[:, sl]
        o_ref[rs, sl] = (x * cos + pltpu.roll(x, LANES // 2, 1) * sin).astype(o_ref.dtype)


def _proj_kernel(*refs, shift, epilogue, patch):
    if shift:
        h_ref, wa_ref, wb_ref, ba_ref, bb_ref, *rest = refs
    else:
        h_ref, wa_ref, ba_ref, *rest = refs
    if patch is not None:
        wp_ref, bp_ref, *rest = rest
    *extra, o_ref, w_s, b_s = rest
    tn = w_s.shape[0]

    @pl.when(pl.program_id(1) == 0)
    def _():
        if shift:
            w_s[0:tn - shift, :] = wa_ref[shift:tn, :].astype(BF16)
            w_s[tn - shift:tn, :] = wb_ref[0:shift, :].astype(BF16)
            bb = jnp.concatenate([ba_ref[...], bb_ref[...]], axis=1)
            bb = jnp.broadcast_to(bb, (b_s.shape[0], 2 * tn))
            b_s[...] = pltpu.roll(bb, 2 * tn - shift, 1)[:, :tn]
        else:
            w_s[...] = wa_ref[...].astype(BF16)
            b_s[...] = jnp.broadcast_to(ba_ref[...], b_s.shape)
        if patch is not None:
            p_tile, p_off = patch

            @pl.when(pl.program_id(0) == p_tile)
            def _():
                w_s[p_off:p_off + LANES, :] = wp_ref[...].astype(BF16)
                b_s[:, p_off:p_off + LANES] = jnp.broadcast_to(bp_ref[...], (b_s.shape[0], LANES))

    rows = h_ref.shape[0] // PROJ_ROW_GROUPS
    for u in range(PROJ_ROW_GROUPS):
        rs = slice(u * rows, (u + 1) * rows)
        acc = _dot_nt(h_ref[rs, :], w_s[...]) + b_s[0:1, :]
        epilogue(acc, rs, *extra, o_ref)


def _proj(g, epilogue, h, w_in_t, b_in3, l, col0, width, out_dtype, *, tn=512, out_tn=None,
          extra=(), extra_specs=(), patch=None, name="proj"):
    ncols, kdim = w_in_t.shape[1], w_in_t.shape[2]
    blk0 = col0 // tn
    shift = col0 - blk0 * tn
    assert shift % 16 == 0
    nj = pl.cdiv(width, tn)
    last_blk = pl.cdiv(ncols, tn) - 1
    out_tn = tn if out_tn is None else out_tn
    w_specs = [pl.BlockSpec((None, tn, kdim), lambda j, i: (l, blk0 + j, 0))]
    b_specs = [pl.BlockSpec((None, 1, tn), lambda j, i: (l, 0, blk0 + j))]
    if shift:
        nxt = lambda j: jnp.minimum(blk0 + j + 1, last_blk)
        w_specs.append(pl.BlockSpec((None, tn, kdim), lambda j, i: (l, nxt(j), 0)))
        b_specs.append(pl.BlockSpec((None, 1, tn), lambda j, i: (l, 0, nxt(j))))
    n_w = len(w_specs)
    p_specs, p_args, k_patch = [], [], None
    if patch is not None:
        src_col, dst_col = patch
        assert src_col % LANES == 0 and dst_col % LANES == 0 and dst_col >= width
        p_specs = [pl.BlockSpec((None, LANES, kdim), lambda j, i: (l, src_col // LANES, 0)),
                   pl.BlockSpec((None, 1, LANES), lambda j, i: (l, 0, src_col // LANES))]
        p_args = [w_in_t, b_in3]
        k_patch = (dst_col // tn, dst_col % tn)
    return pl.pallas_call(
        functools.partial(_proj_kernel, shift=shift, epilogue=epilogue, patch=k_patch),
        grid=(nj, g.n_tiles),
        in_specs=[pl.BlockSpec((g.tm, kdim), lambda j, i: (i, 0)), *w_specs, *b_specs, *p_specs, *extra_specs],
        out_specs=pl.BlockSpec((g.tm, out_tn), lambda j, i: (i, j)),
        out_shape=jax.ShapeDtypeStruct((g.r, nj * out_tn), out_dtype),
        scratch_shapes=[pltpu.VMEM((tn, kdim), BF16), pltpu.VMEM((MOD_ROWS, tn), F32)],
        compiler_params=_cp("arbitrary", "arbitrary"),
        name=name,
    )(h, *([w_in_t] * n_w), *([b_in3] * n_w), *p_args, *extra)


def _mconv_kernel(x_ref, prev_ref, next_ref, w_ref, b_ref, s_ref, sh_ref, o_ref, *, blocks_per_seq, blocks_per_ctx,
                  n_lat_blocks):
    i = pl.program_id(0)
    rows = x_ref.shape[0]
    halo = prev_ref.shape[0]
    edge = MOD_ROWS
    in_lat = i < n_lat_blocks
    pos = jnp.where(in_lat, i % blocks_per_seq, (i - n_lat_blocks) % blocks_per_ctx)
    per_seq = jnp.where(in_lat, blocks_per_seq, blocks_per_ctx)
    x = x_ref[...]
    w = [w_ref[j:j + 1, :] for j in range(M_CONV)]
    acc = x.astype(F32) * w[M_CONV // 2] + b_ref[...]
    taps = [j for j in range(M_CONV) if j != M_CONV // 2]
    for n_, j in enumerate(taps):
        acc = acc + _dot(sh_ref[n_], x) * w[j]
    p = jnp.where(pos != 0, prev_ref[...].astype(F32), 0.0)
    nx = jnp.where(pos != per_seq - 1, next_ref[...].astype(F32), 0.0)
    r8 = lax.broadcasted_iota(jnp.int32, (edge, 1), 0)
    top = (jnp.where(r8 < 2, pltpu.roll(p, 2, 0)[0:edge], 0.0) * w[0]
           + jnp.where(r8 < 1, pltpu.roll(p, 1, 0)[0:edge], 0.0) * w[1])
    bot = (jnp.where(r8 >= edge - 1, pltpu.roll(nx, edge - 1, 0)[0:edge], 0.0) * w[3]
           + jnp.where(r8 >= edge - 2, pltpu.roll(nx, edge - 2, 0)[0:edge], 0.0) * w[4])
    zeros = jnp.zeros((halo - edge, x.shape[1]), F32)

    def finish(a):
        return (a * _sigmoid(a) * s_ref[...]).astype(o_ref.dtype)

    o_ref[0:halo, :] = finish(acc[0:halo] + jnp.concatenate([top, zeros], axis=0))
    o_ref[halo:rows - halo, :] = finish(acc[halo:rows - halo])
    o_ref[rows - halo:rows, :] = finish(acc[rows - halo:rows] + jnp.concatenate([zeros, bot], axis=0))


def _mconv(g, pm, conv_w, conv_b, scale_row):
    rows = g.mblk
    halo = 16
    tn = 512
    per = rows // halo
    nblk = g.r // rows
    last = g.r // halo - 1
    taps = [j for j in range(M_CONV) if j != M_CONV // 2]
    shifts = np.zeros((len(taps), rows, rows), np.float32)
    for n_, j in enumerate(taps):
        for t_ in range(rows):
            if 0 <= t_ + j - M_CONV // 2 < rows:
                shifts[n_, t_, t_ + j - M_CONV // 2] = 1.0
    shifts = jnp.asarray(shifts, BF16)
    return pl.pallas_call(
        functools.partial(_mconv_kernel, blocks_per_seq=g.t // rows, blocks_per_ctx=g.lc // rows,
                          n_lat_blocks=g.rl // rows),
        grid=(nblk, 2 * BRANCH_W // tn),
        in_specs=[
            pl.BlockSpec((rows, tn), lambda i, j: (i, j)),
            pl.BlockSpec((halo, tn), lambda i, j: (jnp.maximum(i * per - 1, 0), j)),
            pl.BlockSpec((halo, tn), lambda i, j: (jnp.minimum((i + 1) * per, last), j)),
            pl.BlockSpec((M_CONV, tn), lambda i, j: (0, j)),
            pl.BlockSpec((1, tn), lambda i, j: (0, j)),
            pl.BlockSpec((1, tn), lambda i, j: (0, j)),
            pl.BlockSpec(shifts.shape, lambda i, j: (0, 0, 0)),
        ],
        out_specs=pl.BlockSpec((rows, tn), lambda i, j: (i, j)),
        out_shape=jax.ShapeDtypeStruct((g.r, 2 * BRANCH_W), BF16),
        compiler_params=_cp("parallel", "parallel"),
        name="mlstm_conv",
    )(pm, pm, pm, conv_w, conv_b, scale_row, shifts)


def _mscan_kernel(qkf_ref, vf_ref, gf_ref, qkb_ref, vb_ref, gb_ref, ol_ref, oc_ref, ng_ref, yl_ref, yc_ref,
                  xsf, bcf, xsb, bcb, dtf, dtb, hf, hb, ct, nv, ms, *, lc, t, mblk):
    i = pl.program_id(1)
    nblk = pl.num_programs(1)
    L = M_CHUNK
    cpb = mblk // L
    nbc = lc // mblk
    nbl = t // mblk
    n = lc + t

    @pl.when(i == 0)
    def _():
        ct[...] = jnp.zeros_like(ct)
        nv[...] = jnp.zeros_like(nv)
        ms[...] = jnp.zeros_like(ms)

    lane = lax.broadcasted_iota(jnp.int32, (1, LANES), 1)
    is_f = jnp.logical_or(jnp.logical_and(lane >= M_HEADS, lane < 2 * M_HEADS),
                          jnp.logical_and(lane >= 3 * M_HEADS, lane < 4 * M_HEADS))
    pos = lax.broadcasted_iota(jnp.int32, (mblk, 1), 0) % L

    def gate_prep(g_ref, x_out, bc_out, forward):
        gx = g_ref[...]
        lsg = jnp.minimum(gx, 0.0) - jnp.log(1.0 + jnp.exp(-jnp.abs(gx)))
        x = jnp.where(is_f, lsg, gx)
        x_out[...] = x
        run = x
        k = 1
        while k < L:
            if forward:
                run = run + jnp.where(pos >= k, pltpu.roll(run, k, 0), 0.0)
            else:
                run = run + jnp.where(pos < L - k, pltpu.roll(run, mblk - k, 0), 0.0)
            k *= 2
        bc_out[...] = run

    gate_prep(gf_ref, xsf, bcf, True)
    gate_prep(gb_ref, xsb, bcb, False)

    pos_f = i * mblk
    pos_b = jnp.where(i < nbc, (nbc - 1 - i) * mblk, lc + (nbl - 1 - (i - nbc)) * mblk)

    def row_table(x_ref, bc_ref, out_ref):
        dm = x_ref[...] - pltpu.roll(bc_ref[...], LANES - M_HEADS, 1)
        for c in range(cpb):
            out_ref[c] = dm[c * L:(c + 1) * L, :].T

    row_table(xsf, bcf, dtf)
    row_table(xsb, bcb, dtb)

    H = M_HEADS
    r_i = lax.broadcasted_iota(jnp.int32, (H * L, L), 0) % L
    c_i = lax.broadcasted_iota(jnp.int32, (H * L, L), 1)
    masks = (c_i <= r_i, c_i >= r_i)

    def stack(parts):
        return jnp.concatenate(parts, axis=0)

    def dir_step(c, d, mvals):
        qk_ref, v_ref, x_ref, bc_ref, dt_ref, h_ref, base = (
            (qkf_ref, vf_ref, xsf, bcf, dtf, hf, pos_f), (qkb_ref, vb_ref, xsb, bcb, dtb, hb, pos_b))[d]
        r0 = pl.multiple_of(c * L, L)
        bg = bc_ref[pl.ds(r0, L), :]
        dt = dt_ref[c]
        qs = [qk_ref[pl.ds(r0, L), j * M_DH:(j + 1) * M_DH] for j in range(H)]
        ks = [qk_ref[pl.ds(r0, L), BRANCH_W + j * M_DH:BRANCH_W + (j + 1) * M_DH] for j in range(H)]
        vs_ = [v_ref[pl.ds(r0, L), j * M_DH:(j + 1) * M_DH] for j in range(H)]
        li0 = 2 * d * H
        lf0 = (2 * d + 1) * H
        a_h = [bg[:, lf0 + j:lf0 + j + 1] for j in range(H)]
        bl_h = [a[L - 1:L, :] if d == 0 else a[0:1, :] for a in a_h]
        a4 = stack(a_h)
        d_rows = [dt[li0 + j:li0 + j + 1, :] for j in range(H)]
        d4 = stack([jnp.broadcast_to(r_, (L, L)) for r_ in d_rows])
        s4 = stack([_dot_nt(qs[j], ks[j]) for j in range(H)])
        logw = jnp.where(masks[d], a4 + d4, -jnp.inf)
        m_loc = jnp.max(logw, axis=1, keepdims=True)
        w4b = (jnp.exp(logw - m_loc) * s4).astype(BF16)
        den_loc = _dot(w4b, jnp.ones((L, LANES), BF16))[:, 0:1]
        num_loc = [_dot(w4b[j * L:(j + 1) * L, :], vs_[j]) for j in range(H)]
        s_rows = [bl_h[j] + d_rows[j] for j in range(H)]
        msrc_h = [jnp.max(s_, axis=1, keepdims=True) for s_ in s_rows]
        g_rows = [jnp.exp(s_rows[j] - msrc_h[j]) for j in range(H)]
        k_t = [ks[j].astype(F32).T for j in range(H)]
        d_c = [_dot((k_t[j] * g_rows[j]).astype(BF16), vs_[j]) for j in range(H)]
        d_n = [_dot(jnp.broadcast_to(g_rows[j], (MOD_ROWS, L)).astype(BF16), ks[j])[0:1, :] for j in range(H)]
        idx = [d * H + j for j in range(H)]
        nrow = [nv[i_] for i_ in idx]
        inter = [_dot(qs[j], ct[idx[j]].astype(BF16)) for j in range(H)]
        qn4 = stack([_dot_nt(qs[j], jnp.broadcast_to(nrow[j], (MOD_ROWS, M_DH)).astype(BF16))[:, 0:1]
                     for j in range(H)])
        m4 = stack([jnp.broadcast_to(mvals[idx[j]], (L, 1)) for j in range(H)])
        bm4 = a4 + m4
        m_t = jnp.maximum(bm4, m_loc)
        e_loc = jnp.exp(m_loc - m_t)
        e_int = jnp.exp(bm4 - m_t)
        den4 = e_loc * den_loc + e_int * qn4
        r4 = 1.0 / jnp.maximum(jnp.abs(den4), jnp.exp(-m_t))
        el = e_loc * r4
        ei = e_int * r4
        rows_out = pl.ds(pl.multiple_of(base + r0, L), L)
        out = list(mvals)
        for j in range(H):
            rs = slice(j * L, (j + 1) * L)
            h_ref[rows_out, j * M_DH:(j + 1) * M_DH] = el[rs] * num_loc[j] + ei[rs] * inter[j]
            m_old = mvals[idx[j]]
            m_new = jnp.maximum(bl_h[j] + m_old, msrc_h[j])
            keep = jnp.exp(bl_h[j] + m_old - m_new)
            scale = jnp.exp(msrc_h[j] - m_new)
            ct[idx[j]] = keep * ct[idx[j]] + scale * d_c[j]
            nv[idx[j]] = keep * nrow[j] + scale * d_n[j]
            out[idx[j]] = m_new
        return tuple(out)

    def body(s, mvals):
        mvals = dir_step(s, 0, mvals)
        return dir_step(cpb - 1 - s, 1, mvals)

    m0 = tuple(ms[q_:q_ + 1, 0:1] for q_ in range(2 * M_HEADS))
    m1 = lax.fori_loop(0, cpb, body, m0, unroll=2)
    for q_ in range(2 * M_HEADS):
        ms[q_:q_ + 1, :] = jnp.broadcast_to(m1[q_], (1, LANES))

    @pl.when(i == nblk - 1)
    def _():
        for blk in range(n // mblk):
            rs = slice(blk * mblk, (blk + 1) * mblk)
            for j in range(M_HEADS):
                hs = slice(j * M_DH, (j + 1) * M_DH)
                hn = _rms(hf[rs, hs] + hb[rs, hs]) * ng_ref[:, hs]
                if blk < nbc:
                    yc_ref[rs, hs] = (hn * _sigmoid(oc_ref[rs, hs].astype(F32))).astype(BF16)
                else:
                    ls = slice(blk * mblk - lc, (blk + 1) * mblk - lc)
                    yl_ref[ls, hs] = (hn * _sigmoid(ol_ref[ls, hs].astype(F32))).astype(BF16)


def _mscan(g, qkc, pm, pg, norm_g):
    mblk = g.mblk
    nbc, nbl = g.lc // mblk, g.t // mblk
    lat_blocks = g.rl // mblk
    n = g.lc + g.t

    def fwd(b, i):
        return jnp.where(i < nbc, lat_blocks + b * nbc + i, b * nbl + (i - nbc))

    def bwd(b, i):
        return jnp.where(i < nbc, lat_blocks + b * nbc + (nbc - 1 - i), b * nbl + (nbl - 1 - (i - nbc)))

    v_col = 2 * BRANCH_W // BRANCH_W
    o_col = 3 * BRANCH_W // BRANCH_W
    ctx0 = g.rl // g.lc

    def side(blk_of):
        return [pl.BlockSpec((mblk, 2 * BRANCH_W), lambda b, i: (blk_of(b, i), 0)),
                pl.BlockSpec((mblk, BRANCH_W), lambda b, i: (blk_of(b, i), v_col)),
                pl.BlockSpec((mblk, LANES), lambda b, i: (blk_of(b, i), PC_GATE_COL // LANES))]

    return pl.pallas_call(
        functools.partial(_mscan_kernel, lc=g.lc, t=g.t, mblk=mblk),
        grid=(g.b, nbc + nbl),
        in_specs=[
            *side(fwd), *side(bwd),
            pl.BlockSpec((g.t, BRANCH_W), lambda b, i: (b, o_col)),
            pl.BlockSpec((g.lc, BRANCH_W), lambda b, i: (ctx0 + b, o_col)),
            pl.BlockSpec((1, BRANCH_W), lambda b, i: (0, 0)),
        ],
        out_specs=[
            pl.BlockSpec((g.t, BRANCH_W), lambda b, i: (b, 0)),
            pl.BlockSpec((g.lc, BRANCH_W), lambda b, i: (b, 0)),
        ],
        out_shape=[
            jax.ShapeDtypeStruct((g.rl, BRANCH_W), BF16),
            jax.ShapeDtypeStruct((g.rc, BRANCH_W), BF16),
        ],
        scratch_shapes=[
            pltpu.VMEM((mblk, LANES), F32), pltpu.VMEM((mblk, LANES), F32),
            pltpu.VMEM((mblk, LANES), F32), pltpu.VMEM((mblk, LANES), F32),
            pltpu.VMEM((mblk // M_CHUNK, LANES, M_CHUNK), F32), pltpu.VMEM((mblk // M_CHUNK, LANES, M_CHUNK), F32),
            pltpu.VMEM((n, BRANCH_W), F32), pltpu.VMEM((n, BRANCH_W), F32),
            pltpu.VMEM((2 * M_HEADS, M_DH, M_DH), F32), pltpu.VMEM((2 * M_HEADS, 1, M_DH), F32),
            pltpu.VMEM((2 * M_HEADS, LANES), F32),
        ],
        compiler_params=_cp("parallel", "arbitrary"),
        name="mlstm_scan",
    )(qkc, pm, pg, qkc, pm, pg, pm, pm, norm_g)


def _mla_prep_kernel(cq_ref, ckv_ref, kr_ref, wuq_ref, wukv_ref, cqg_ref, ckvg_ref, qng_ref, qrg_ref,
                     kng_ref, krg_ref, cos_ref, sin_ref, aq_ref, ak_ref, av_ref, wuq_s, wukv_s):
    @pl.when(pl.program_id(0) == 0)
    def _():
        wuq_s[...] = wuq_ref[...].astype(BF16)
        wukv_s[...] = wukv_ref[...].astype(BF16)

    half = LANES // 2
    lane = lax.broadcasted_iota(jnp.int32, (1, LANES), 1)
    lo = lane < half
    first = (lane % half) < (half // 2)

    def rms_half(x):
        x2 = x * x
        s_lo = jnp.sum(jnp.where(lo, x2, 0.0), axis=-1, keepdims=True)
        s_hi = jnp.sum(jnp.where(lo, 0.0, x2), axis=-1, keepdims=True)
        ms = jnp.where(lo, s_lo, s_hi) * (1.0 / half)
        return x * lax.rsqrt(ms + EPS)

    def rows_group(rs):
        cos = cos_ref[rs, :]
        sin = sin_ref[rs, :]

        def rope_half(x):
            partner = jnp.where(first, pltpu.roll(x, LANES - half // 2, 1), pltpu.roll(x, half // 2, 1))
            return x * cos + partner * sin

        cq = (_rms(cq_ref[rs, :]) * cqg_ref[...]).astype(BF16)
        q = _dot(cq, wuq_s[...])
        ckv = (_rms(ckv_ref[rs, :]) * ckvg_ref[...]).astype(BF16)
        kv = _dot(ckv, wukv_s[...])

        krn = rope_half(rms_half(kr_ref[rs, :]) * krg_ref[...]).astype(BF16)

        rope0 = MLA_HEADS * MLA_NOPE
        for hp in range(MLA_HEADS // 2):
            slab = q[:, rope0 + hp * LANES: rope0 + (hp + 1) * LANES]
            r = rope_half(rms_half(slab) * qrg_ref[...])
            parts = (jnp.where(lo, r, 0.0), jnp.where(lo, pltpu.roll(r, half, 1), 0.0))
            for e in range(2):
                h = 2 * hp + e
                nope = _rms(q[:, h * MLA_NOPE:(h + 1) * MLA_NOPE]) * qng_ref[...]
                aq_ref[rs, h * MLA_HW: h * MLA_HW + MLA_NOPE] = nope.astype(BF16)
                aq_ref[rs, h * MLA_HW + MLA_NOPE:(h + 1) * MLA_HW] = parts[e].astype(BF16)
        kvw = MLA_NOPE + MLA_DV
        for h in range(MLA_HEADS):
            kn = _rms(kv[:, h * kvw: h * kvw + MLA_NOPE]) * kng_ref[...]
            ak_ref[rs, h * MLA_HW: h * MLA_HW + MLA_NOPE] = kn.astype(BF16)
            ak_ref[rs, h * MLA_HW + MLA_NOPE:(h + 1) * MLA_HW] = krn
            av_ref[rs, h * MLA_DV:(h + 1) * MLA_DV] = kv[:, h * kvw + MLA_NOPE:(h + 1) * kvw].astype(BF16)

    rows_group(slice(0, cq_ref.shape[0]))


def _mla_prep(g, pc, wuq, wukv, gains, cos_t, sin_t):
    tm = g.tm
    full = lambda shape: pl.BlockSpec(shape, lambda i: (0, 0))
    kr_col = (MLA_Q_RANK + MLA_KV_RANK) // LANES
    return pl.pallas_call(
        _mla_prep_kernel,
        grid=(g.n_tiles,),
        in_specs=[
            pl.BlockSpec((tm, MLA_Q_RANK), lambda i: (i, 0)),
            pl.BlockSpec((tm, MLA_KV_RANK), lambda i: (i, MLA_Q_RANK // MLA_KV_RANK)),
            pl.BlockSpec((tm, LANES), lambda i: (i, kr_col)),
            full(wuq.shape), full(wukv.shape),
            full((1, MLA_Q_RANK)), full((1, MLA_KV_RANK)),
            full((1, LANES)), full((1, LANES)), full((1, LANES)), full((1, LANES)),
            pl.BlockSpec((tm, LANES), lambda i: (g.pos_block(i), 0)),
            pl.BlockSpec((tm, LANES), lambda i: (g.pos_block(i), 0)),
        ],
        out_specs=[
            pl.BlockSpec((tm, MLA_HEADS * MLA_HW), lambda i: (i, 0)),
            pl.BlockSpec((tm, MLA_HEADS * MLA_HW), lambda i: (i, 0)),
            pl.BlockSpec((tm, BRANCH_W), lambda i: (i, 0)),
        ],
        out_shape=[
            jax.ShapeDtypeStruct((g.r, MLA_HEADS * MLA_HW), BF16),
            jax.ShapeDtypeStruct((g.r, MLA_HEADS * MLA_HW), BF16),
            jax.ShapeDtypeStruct((g.r, BRANCH_W), BF16),
        ],
        scratch_shapes=[pltpu.VMEM(wuq.shape, BF16), pltpu.VMEM(wukv.shape, BF16)],
        compiler_params=_cp("arbitrary"),
        name="mla_prep",
    )(pc, pc, pc, wuq, wukv, *gains, cos_t, sin_t)


def _attn_kernel(*refs, n_soft, dh, has_lat, diff, lam_init):
    refs = list(refs)
    vs = refs.pop()
    kts = refs.pop()
    o_ref = refs.pop()
    if diff:
        lam_ref, sg_ref = refs[0], refs[1]
        refs = refs[2:]
    q_ref, kc_ref, vc_ref = refs[:3]
    lc = kc_ref.shape[0]

    @pl.when(pl.program_id(2) == 0)
    def _():
        kts[:, 0:lc] = kc_ref[...].T
        vs[0:lc, :] = vc_ref[...]
        if has_lat:
            kts[:, lc:] = refs[3][...].T
            vs[lc:, :] = refs[4][...]

    q = q_ref[...]
    tq = q.shape[0]
    n_sub = 4 if tq % 64 == 0 else 1
    rsub = tq // n_sub
    outs = []
    for s in range(n_soft):
        sl = slice(s * dh, (s + 1) * dh)
        scs = [_dot(q[u * rsub:(u + 1) * rsub, sl], kts[sl, :]) for u in range(n_sub)]
        ps, ls = [], []
        for sc in scs:
            m = jnp.max(sc, axis=-1, keepdims=True)
            p = jnp.exp(sc - m)
            ls.append(jnp.sum(p, axis=-1, keepdims=True))
            ps.append(p.astype(BF16))
        os_ = [_dot(p, vs[...]) / l for p, l in zip(ps, ls)]
        outs.append(jnp.concatenate(os_, axis=0))
    if diff:
        lp = lam_ref[...]
        lam = (jnp.exp(jnp.sum(lp[0:1] * lp[1:2], axis=-1, keepdims=True))
               - jnp.exp(jnp.sum(lp[2:3] * lp[3:4], axis=-1, keepdims=True)) + lam_init)
        o = outs[0] - lam * outs[1]
        o = _rms(o) * sg_ref[...] * (1.0 - lam_init)
    else:
        o = outs[0]
    o_ref[...] = o.astype(o_ref.dtype)


def _attention(g, q_arr, k_arr, v_arr, *, q_col0, k_col0, heads, qk_w, dv, n_soft, ctx_queries,
               lam=None, subln_g=None, lam_init=0.0):
    diff = lam is not None
    ctx0 = g.rl // g.lc
    n_keys = g.lc if ctx_queries else g.lc + g.t
    if ctx_queries:
        tq = g.lc
        nq = 1
        q_row = lambda b, qi: ctx0 + b
        out_rows = g.rc
        o_row = lambda b, qi: b
    else:
        tq = min(ATT_TQ, g.t)
        nq = g.t // tq
        q_row = lambda b, qi: b * nq + qi
        out_rows = g.rl
        o_row = q_row
    in_specs = []
    args = []
    if diff:
        in_specs += [pl.BlockSpec(lam.shape, lambda b, h, qi: (0, 0)),
                     pl.BlockSpec((1, dv), lambda b, h, qi: (0, 0))]
        args += [lam, subln_g]
    in_specs += [
        pl.BlockSpec((tq, qk_w), lambda b, h, qi: (q_row(b, qi), q_col0 + h)),
        pl.BlockSpec((g.lc, qk_w), lambda b, h, qi: (ctx0 + b, k_col0 + h)),
        pl.BlockSpec((g.lc, dv), lambda b, h, qi: (ctx0 + b, h)),
    ]
    args += [q_arr, k_arr, v_arr]
    if not ctx_queries:
        in_specs += [
            pl.BlockSpec((g.t, qk_w), lambda b, h, qi: (b, k_col0 + h)),
            pl.BlockSpec((g.t, dv), lambda b, h, qi: (b, h)),
        ]
        args += [k_arr, v_arr]
    return pl.pallas_call(
        functools.partial(_attn_kernel, n_soft=n_soft, dh=qk_w // n_soft, has_lat=not ctx_queries,
                          diff=diff, lam_init=lam_init),
        grid=(g.b, heads, nq),
        in_specs=in_specs,
        out_specs=pl.BlockSpec((tq, dv), lambda b, h, qi: (o_row(b, qi), h)),
        out_shape=jax.ShapeDtypeStruct((out_rows, heads * dv), BF16),
        scratch_shapes=[pltpu.VMEM((qk_w, n_keys), BF16), pltpu.VMEM((n_keys, dv), BF16)],
        compiler_params=_cp("parallel", "parallel", "arbitrary"),
        name="attn_diff" if diff else "attn_mla",
    )(*args)


def _branch_kernel(*refs, n_src, n_lat_tiles):
    y_refs = refs[:N_BRANCH * n_src]
    wb_ref, g0_ref, g1_ref, g2_ref, o_ref, wb_s = refs[N_BRANCH * n_src:]
    gates = (g0_ref, g1_ref, g2_ref)

    @pl.when(pl.program_id(1) == 0)
    def _():
        wb_s[...] = wb_ref[...].astype(BF16)

    def body(which):
        def run(_):
            acc = None
            for n in range(N_BRANCH):
                term = gates[n][...].astype(F32) * _dot(y_refs[n * n_src + which][...], wb_s[n])
                acc = term if acc is None else acc + term
            o_ref[...] = acc.astype(o_ref.dtype)
        return run

    i = pl.program_id(1)
    if n_src == 1:
        body(0)(None)
    else:
        pl.when(i < n_lat_tiles)(lambda: body(0)(None))
        pl.when(i >= n_lat_tiles)(lambda: body(1)(None))


def _branch_merge(g, ys, w_branch, l, gt, n_tiles):
    tn = 512
    nj = g.d // tn
    y_specs, y_args = [], []
    for y in ys:
        sp, ar = _row_sources(g, y, BRANCH_W, lambda j: 0, 2)
        y_specs += sp
        y_args += ar
    n_src = len(y_args) // N_BRANCH
    return pl.pallas_call(
        functools.partial(_branch_kernel, n_src=n_src, n_lat_tiles=g.n_lat_tiles),
        grid=(nj, n_tiles),
        in_specs=[
            *y_specs,
            pl.BlockSpec((None, N_BRANCH, BRANCH_W, tn), lambda j, i: (l, 0, 0, j)),
            pl.BlockSpec((g.tm, tn), lambda j, i: (i, j)),
            pl.BlockSpec((g.tm, tn), lambda j, i: (i, nj + j)),
            pl.BlockSpec((g.tm, tn), lambda j, i: (i, 2 * nj + j)),
        ],
        out_specs=pl.BlockSpec((g.tm, tn), lambda j, i: (i, j)),
        out_shape=jax.ShapeDtypeStruct((n_tiles * g.tm, g.d), BF16),
        scratch_shapes=[pltpu.VMEM((N_BRANCH, BRANCH_W, tn), BF16)],
        compiler_params=_cp("arbitrary", "arbitrary"),
        name="branch_merge",
    )(*y_args, w_branch, gt, gt, gt)


def _outproj_kernel(*refs, n_lat_tiles):
    z_ref, w_ref, *x_refs, g_ref, o_ref, w_s = refs

    @pl.when(pl.program_id(1) == 0)
    def _():
        w_s[...] = w_ref[...].astype(BF16)

    def body(x_ref):
        o_ref[...] = x_ref[...] + g_ref[...] * _dot(z_ref[...], w_s[...])

    _on_rows(pl.program_id(1), n_lat_tiles, x_refs, body)


def _outproj(g, z, w_out, l, x_src, mods3, which_gate, n_tiles):
    tn = 512
    per = g.d // tn
    x_specs, x_args = _row_sources(g, x_src, tn, lambda j: j, 2)
    return pl.pallas_call(
        functools.partial(_outproj_kernel, n_lat_tiles=g.n_lat_tiles),
        grid=(per, n_tiles),
        in_specs=[
            pl.BlockSpec((g.tm, g.d), lambda j, i: (i, 0)),
            pl.BlockSpec((None, g.d, tn), lambda j, i: (l, 0, j)),
            *x_specs,
            pl.BlockSpec((None, 1, tn), lambda j, i: (g.mod_row(i), 0, which_gate * per + j)),
        ],
        out_specs=pl.BlockSpec((g.tm, tn), lambda j, i: (i, j)),
        out_shape=jax.ShapeDtypeStruct((n_tiles * g.tm, g.d), F32),
        scratch_shapes=[pltpu.VMEM((g.d, tn), BF16)],
        compiler_params=_cp("arbitrary", "arbitrary"),
        name="outproj",
    )(z, w_out, *x_args, mods3)


def _moe_prep_kernel(x_ref, sh_ref, sc_ref, rwt_ref, rb_ref, tri_ref, xr_ref, rank_ref, grp_ref, cnt_ref,
                     carry, *, mod_row):
    i = pl.program_id(0)

    @pl.when(i == 0)
    def _():
        carry[...] = jnp.zeros_like(carry)

    x = x_ref[...]
    h = _rms(x) * (1.0 + sc_ref[...]) + sh_ref[...]
    logits = _dot_nt(rwt_ref[...].astype(BF16), h.astype(BF16))
    scores = _sigmoid(logits)
    selm = scores + rb_ref[...]
    sc_rows = [scores[e:e + 1, :] for e in range(N_EXPERTS)]
    sel = [selm[e:e + 1, :] for e in range(N_EXPERTS)]
    gscore = []
    for gi in range(N_GROUPS):
        v = sel[gi * GROUP_SIZE:(gi + 1) * GROUP_SIZE]
        pair = [v[a] + v[b] for a in range(GROUP_SIZE) for b in range(a + 1, GROUP_SIZE)]
        gscore.append(functools.reduce(jnp.maximum, pair))
    gmax = functools.reduce(jnp.maximum, gscore)
    taken = jnp.zeros_like(gmax, dtype=jnp.bool_)
    in_group = []
    for gi in range(N_GROUPS):
        hit = jnp.logical_and(gscore[gi] == gmax, jnp.logical_not(taken))
        taken = jnp.logical_or(taken, hit)
        in_group.append(hit)
    neg = -jnp.inf
    masked = [jnp.where(in_group[e // GROUP_SIZE], sel[e], neg) for e in range(N_EXPERTS)]

    def pick(vals):
        mx = functools.reduce(jnp.maximum, vals)
        seen = jnp.zeros_like(mx, dtype=jnp.bool_)
        hot = []
        for v in vals:
            hit = jnp.logical_and(v == mx, jnp.logical_not(seen))
            seen = jnp.logical_or(seen, hit)
            hot.append(hit)
        return hot

    hot1 = pick(masked)
    hot2 = pick([jnp.where(hot1[e], neg, masked[e]) for e in range(N_EXPERTS)])
    zero = jnp.zeros_like(gmax)
    s1 = functools.reduce(lambda a, b: a + b, [jnp.where(hot1[e], sc_rows[e], zero) for e in range(N_EXPERTS)])
    s2 = functools.reduce(lambda a, b: a + b, [jnp.where(hot2[e], sc_rows[e], zero) for e in range(N_EXPERTS)])
    tot = s1 + s2
    w1 = s1 / tot
    w2 = s2 / tot
    comb = [jnp.where(hot1[e], w1, zero) + jnp.where(hot2[e], w2, zero) for e in range(N_EXPERTS)]

    tm = x.shape[0]
    gmat = jnp.concatenate([jnp.where(in_group[gi], 1.0, zero) for gi in range(N_GROUPS)]
                           + [jnp.zeros((MOD_ROWS - N_GROUPS, tm), F32)], axis=0)
    incl = _dot(gmat.astype(BF16), tri_ref[...])
    excl = incl - gmat + carry[:, 0:1]
    rank = functools.reduce(lambda a, b: a + b,
                            [jnp.where(in_group[gi], excl[gi:gi + 1, :], zero) for gi in range(N_GROUPS)])
    grp = functools.reduce(lambda a, b: a + b,
                           [jnp.where(in_group[gi], float(gi), zero) for gi in range(N_GROUPS)])
    rank_ref[...] = rank.astype(jnp.int32)
    grp_ref[...] = grp.astype(jnp.int32)
    carry[...] = carry[...] + jnp.sum(gmat, axis=1, keepdims=True)
    cnt_ref[...] = carry[...]

    side = jnp.concatenate(comb + [jnp.zeros((LANES - N_EXPERTS, tm), F32)], axis=0).T
    lane = lax.broadcasted_iota(jnp.int32, (1, LANES), 1)
    side = jnp.where(lane == N_EXPERTS + mod_row(i), 1.0, side)
    d = x.shape[1]
    xr_ref[:, 0:d] = x
    xr_ref[:, d:d + LANES] = side


def _moe_prep(g, x_mid, mods3, which_shift, which_scale, rwt, rb, tri, n_tiles):
    n_rows = n_tiles * g.tm
    return pl.pallas_call(
        functools.partial(_moe_prep_kernel, mod_row=g.mod_row),
        grid=(n_tiles,),
        in_specs=[
            pl.BlockSpec((g.tm, g.d), lambda i: (i, 0)),
            _mod_spec(g, which_shift),
            _mod_spec(g, which_scale),
            pl.BlockSpec((N_EXPERTS, g.d), lambda i: (0, 0)),
            pl.BlockSpec((N_EXPERTS, 1), lambda i: (0, 0)),
            pl.BlockSpec((g.tm, g.tm), lambda i: (0, 0)),
        ],
        out_specs=[
            pl.BlockSpec((g.tm, g.d + LANES), lambda i: (i, 0)),
            pl.BlockSpec((1, g.tm), lambda i: (0, i)),
            pl.BlockSpec((1, g.tm), lambda i: (0, i)),
            pl.BlockSpec((MOD_ROWS, LANES), lambda i: (0, 0)),
        ],
        out_shape=[
            jax.ShapeDtypeStruct((n_rows, g.d + LANES), F32),
            jax.ShapeDtypeStruct((1, n_rows), jnp.int32),
            jax.ShapeDtypeStruct((1, n_rows), jnp.int32),
            jax.ShapeDtypeStruct((MOD_ROWS, LANES), F32),
        ],
        scratch_shapes=[pltpu.VMEM((MOD_ROWS, LANES), F32)],
        compiler_params=_cp("arbitrary"),
        name="moe_prep",
    )(x_mid, mods3, mods3, rwt, rb, tri)


def _moe_group_kernel(tg_ref, nv_ref, nu_ref, pos_ref, xr_hbm, mods_ref, w1_hbm, w3_hbm, w2_hbm, out_hbm,
                      gbuf, obuf, w1s, w3s, w2s, stg_a, stg_b, src_ref, gsem, ssem, wsem, *, n_mod_rows, e_base):
    k = pl.program_id(0)
    n_used = nu_ref[0]
    slot = k % 2
    d = obuf.shape[2]
    tme = obuf.shape[1]

    def gather_start(kk, sl):
        base = kk * tme

        first = src_ref[base]

        def copy_row(r, tok):
            pltpu.make_async_copy(xr_hbm.at[pl.ds(tok, 1), :], gbuf.at[sl, pl.ds(r, 1), :], gsem.at[sl]).start()

        def issue(r, c):
            copy_row(r, src_ref[base + r])
            return c

        def issue_pad(r, c):
            copy_row(r, first)
            return c

        rows_of(kk, issue)
        lax.fori_loop(nv_ref[kk], tme, issue_pad, 0)

    def build_sorted_index():
        def put(t, c):
            src_ref[pos_ref[t]] = t
            return c

        lax.fori_loop(0, pos_ref.shape[0], put, 0, unroll=8)

    def gather_wait(sl):
        def w(r, c):
            pltpu.make_async_copy(xr_hbm.at[pl.ds(0, 1), :], gbuf.at[sl, pl.ds(0, 1), :], gsem.at[sl]).wait()
            return c

        lax.fori_loop(0, tme, w, 0, unroll=8)

    def scatter_start(kk, sl):
        base = kk * tme

        def issue(r, c):
            tok = src_ref[base + r]
            pltpu.make_async_copy(obuf.at[sl, pl.ds(r, 1), :], out_hbm.at[pl.ds(tok, 1), :], ssem.at[sl]).start()
            return c

        rows_of(kk, issue)

    def scatter_wait(kk, sl):
        def w(r, c):
            pltpu.make_async_copy(obuf.at[sl, pl.ds(0, 1), :], out_hbm.at[pl.ds(0, 1), :], ssem.at[sl]).wait()
            return c

        rows_of(kk, w)

    def rows_of(kk, fn):
        n_rows = nv_ref[kk]

        @pl.when(n_rows == tme)
        def _():
            lax.fori_loop(0, tme, fn, 0, unroll=8)

        @pl.when(n_rows != tme)
        def _():
            lax.fori_loop(0, n_rows, fn, 0)

    def load_weights(grp):
        rows_a = stg_a.shape[1]
        rows_b = stg_b.shape[1]
        chunks = []
        n_a = n_b = 0
        for e in range(GROUP_SIZE):
            ex = e_base + grp * GROUP_SIZE + e
            for src, dst in ((w1_hbm, w1s), (w3_hbm, w3s)):
                for c0 in range(0, src.shape[1], rows_a):
                    sl = n_a % 2
                    n_a += 1
                    cp = pltpu.make_async_copy(src.at[ex, pl.ds(c0, rows_a), :], stg_a.at[sl], wsem.at[sl])
                    chunks.append((cp, stg_a.at[sl], dst, e, c0, rows_a))
            for c0 in range(0, w2_hbm.shape[1], rows_b):
                sl = n_b % 2
                n_b += 1
                cp = pltpu.make_async_copy(w2_hbm.at[ex, pl.ds(c0, rows_b), :], stg_b.at[sl], wsem.at[2 + sl])
                chunks.append((cp, stg_b.at[sl], w2s, e, c0, rows_b))
        chunks[0][0].start()
        for idx, (cp, stg, dst, e, c0, rows) in enumerate(chunks):
            if idx + 1 < len(chunks):
                chunks[idx + 1][0].start()
            cp.wait()
            dst[e, c0:c0 + rows, :] = stg[...].astype(BF16)

    @pl.when(k < n_used)
    def _():
        grp = tg_ref[k]

        @pl.when(k == 0)
        def _():
            build_sorted_index()
            gather_start(0, 0)

        @pl.when(k + 1 < n_used)
        def _():
            gather_start(k + 1, 1 - slot)

        @pl.when(jnp.logical_or(k == 0, grp != tg_ref[jnp.maximum(k - 1, 0)]))
        def _():
            load_weights(grp)

        gather_wait(slot)

        @pl.when(k >= 2)
        def _():
            scatter_wait(k - 2, slot)

        lane = lax.broadcasted_iota(jnp.int32, (1, LANES), 1)
        rows = tme // MOE_ROW_GROUPS
        for u in range(MOE_ROW_GROUPS):
            rs = slice(u * rows, (u + 1) * rows)
            x = gbuf[slot, rs, 0:d]
            side = gbuf[slot, rs, d:d + LANES]

            def cond_vec(which):
                out = jnp.broadcast_to(mods_ref[0:1, which * d:(which + 1) * d], (rows, d))
                for r in range(1, n_mod_rows):
                    hot = side[:, N_EXPERTS + r:N_EXPERTS + r + 1] > 0.5
                    out = jnp.where(hot, mods_ref[r:r + 1, which * d:(which + 1) * d], out)
                return out

            h = (_rms(x) * (1.0 + cond_vec(1)) + cond_vec(0)).astype(BF16)
            y = jnp.zeros((rows, d), F32)
            for e in range(GROUP_SIZE):
                a = _dot(h, w1s[e])
                b = _dot(h, w3s[e])
                cw = jnp.sum(jnp.where(lane == grp * GROUP_SIZE + e, side, 0.0), axis=1, keepdims=True)
                hid = (a * _sigmoid(a) * b * cw).astype(BF16)
                y = y + _dot(hid, w2s[e])
            obuf[slot, rs, :] = x + cond_vec(2) * y

        scatter_start(k, slot)

        @pl.when(k == n_used - 1)
        def _():
            @pl.when(k >= 1)
            def _():
                scatter_wait(k - 1, 1 - slot)

            scatter_wait(k, slot)


def _moe_group(g, xr, mods_tail, w1, w3, w2, meta, n_rows, n_mod_rows, e_base):
    tile_group, n_valid, n_used, src = meta
    kt = tile_group.shape[0]
    tme = MOE_TM
    any_spec = pl.BlockSpec(memory_space=pl.ANY)
    grid_spec = pltpu.PrefetchScalarGridSpec(
        num_scalar_prefetch=4,
        grid=(kt,),
        in_specs=[any_spec, pl.BlockSpec(mods_tail.shape, lambda k, *_: (0, 0)), any_spec, any_spec, any_spec],
        out_specs=any_spec,
        scratch_shapes=[
            pltpu.VMEM((2, tme, g.d + LANES), F32),
            pltpu.VMEM((2, tme, g.d), F32),
            pltpu.VMEM((GROUP_SIZE, g.d, D_EXPERT), BF16),
            pltpu.VMEM((GROUP_SIZE, g.d, D_EXPERT), BF16),
            pltpu.VMEM((GROUP_SIZE, D_EXPERT, g.d), BF16),
            pltpu.VMEM((2, g.d // 2, D_EXPERT), F32),
            pltpu.VMEM((2, D_EXPERT // 2, g.d), F32),
            pltpu.SMEM((kt * tme,), jnp.int32),
            pltpu.SemaphoreType.DMA((2,)),
            pltpu.SemaphoreType.DMA((2,)),
            pltpu.SemaphoreType.DMA((4,)),
        ],
    )
    return pl.pallas_call(
        functools.partial(_moe_group_kernel, n_mod_rows=n_mod_rows, e_base=e_base),
        grid_spec=grid_spec,
        out_shape=jax.ShapeDtypeStruct((n_rows, g.d), F32),
        compiler_params=_cp("arbitrary"),
        name="moe_group",
    )(tile_group, n_valid, n_used, src, xr, mods_tail, w1, w3, w2)


def _moe_meta(rank, grp, cnt, n_rows):
    tme = MOE_TM
    kt = n_rows // tme + N_GROUPS
    counts = cnt[:N_GROUPS, 0].astype(jnp.int32)
    ntile = (counts + tme - 1) // tme
    tile_end = jnp.cumsum(ntile)
    tile_off = tile_end - ntile
    gsel = [grp[0] == gi for gi in range(N_GROUPS)]
    row_off = functools.reduce(lambda a, b: a + b,
                               [jnp.where(gsel[gi], tile_off[gi] * tme, 0) for gi in range(N_GROUPS)])
    pos = row_off + rank[0]
    ks = jnp.arange(kt, dtype=jnp.int32)
    tile_group = jnp.minimum(functools.reduce(lambda a, b: a + b,
                                              [(ks >= tile_end[gi]).astype(jnp.int32) for gi in range(N_GROUPS)]),
                             N_GROUPS - 1)
    n_used = tile_end[-1:]
    cnt_k = functools.reduce(lambda a, b: a + b,
                             [jnp.where(tile_group == gi, counts[gi] - (ks - tile_off[gi]) * tme, 0)
                              for gi in range(N_GROUPS)])
    n_valid = jnp.where(ks < n_used[0], jnp.clip(cnt_k, 0, tme), 0)
    return tile_group, n_valid, n_used, pos


def _rope_tables(t, rot_dim, tile_rows, reps):
    f32 = np.float32
    rows = t // GRID_W
    r = np.repeat(np.arange(rows, dtype=f32), GRID_W)
    col = np.tile(np.arange(GRID_W, dtype=f32), rows)
    n_freq = rot_dim // 4
    inv = np.power(f32(ROPE_BASE), -np.arange(n_freq, dtype=f32) / f32(n_freq)).astype(f32)
    ang = np.concatenate([r[:, None] * inv, col[:, None] * inv], axis=-1).astype(f32)
    cos = np.tile(np.concatenate([np.cos(ang), np.cos(ang)], axis=-1), (1, reps))
    sin = np.tile(np.concatenate([-np.sin(ang), np.sin(ang)], axis=-1), (1, reps))
    cos = np.concatenate([cos, np.ones((tile_rows, LANES), f32)], axis=0).astype(f32)
    sin = np.concatenate([sin, np.zeros((tile_rows, LANES), f32)], axis=0).astype(f32)
    return jnp.asarray(cos), jnp.asarray(sin)


def _uq_perm():
    nope = [h * MLA_QK + i for h in range(MLA_HEADS) for i in range(MLA_NOPE)]
    rope = [h * MLA_QK + MLA_NOPE + i for h in range(MLA_HEADS) for i in range(MLA_ROPE)]
    return jnp.asarray(nope + rope)


def kernel(x, c, ctx, c_ctx, w_ada, b_ada, w_in, b_in, m_conv_w, m_conv_b, m_norm_g, da_q_norm_g, da_k_norm_g,
           da_lambda, da_subln_g, mla_cq_norm_g, mla_ckv_norm_g, mla_w_uq, mla_w_ukv, mla_q_norm_g, mla_k_norm_g,
           w_branch, w_out, moe_w1, moe_w3, moe_w2, router_w, router_bias):
    bsz, t, d = x.shape
    lc = ctx.shape[1]
    depth = w_ada.shape[0]
    g = _Geom(bsz, t, lc, d, min(1024, bsz * lc, t))
    assert bsz + 1 <= MOD_ROWS

    cvec = jnp.concatenate([c, c_ctx[None, :], jnp.zeros((MOD_ROWS - bsz - 1, d), F32)], axis=0)
    mods = _mod_vectors(cvec, w_ada, b_ada)

    cos_d, sin_d = _rope_tables(t, DA_DH, g.tm, 1)
    cos_a, sin_a = _rope_tables(t, MLA_ROPE, g.tm, 2)
    rwt = router_w.T
    rb = router_bias.reshape(N_EXPERTS, 1)
    uq_perm = _uq_perm()
    x_all = (x.reshape(g.rl, d), ctx.reshape(g.rc, d))
    pos_spec = pl.BlockSpec((g.tm, LANES), lambda j, i: (g.pos_block(i), 0))
    b_in3 = b_in.reshape(depth, 1, -1)
    w_in_t = jnp.swapaxes(w_in, 1, 2)
    qk_scale = jnp.concatenate([jnp.full((1, BRANCH_W), M_DH ** -0.5, F32), jnp.ones((1, BRANCH_W), F32)], axis=1)
    tri = (jnp.arange(g.tm)[:, None] <= jnp.arange(g.tm)[None, :]).astype(BF16)
    w1r = moe_w1.reshape(depth * N_EXPERTS, d, D_EXPERT)
    w3r = moe_w3.reshape(depth * N_EXPERTS, d, D_EXPERT)
    w2r = moe_w2.reshape(depth * N_EXPERTS, D_EXPERT, d)

    for l in range(depth):
        last = l == depth - 1
        n_tiles = g.n_lat_tiles if last else g.n_tiles
        n_rows = n_tiles * g.tm
        lam_init = 0.8 - 0.6 * math.exp(-0.3 * l)
        mods3 = mods[l].reshape(MOD_ROWS, 1, 6 * d)

        h1 = _prenorm(g, x_all, mods3, 0, 1, g.n_tiles)

        row = lambda v: v.reshape(1, -1)
        proj = functools.partial(_proj, g, h=h1, w_in_t=w_in_t, b_in3=b_in3, l=l)
        pm = proj(_epi_raw, col0=OFF_MQK, width=OFF_MG - OFF_MQK, out_dtype=BF16, name="proj_mlstm")
        da_gain = jnp.concatenate([jnp.tile(da_q_norm_g[l] * DA_DH ** -0.5, 2 * DA_HEADS),
                                   jnp.tile(da_k_norm_g[l], 2 * DA_HEADS)])
        dqk = proj(_epi_normrope, col0=OFF_DQ, width=2 * DA_QK_W, out_dtype=BF16,
                   extra=(row(da_gain), cos_d, sin_d),
                   extra_specs=(pl.BlockSpec((1, 512), lambda j, i: (0, j)), pos_spec, pos_spec),
                   name="proj_dqk")
        dv = proj(_epi_raw, col0=OFF_DV, width=BRANCH_W, out_dtype=BF16, name="proj_dv")
        pc = proj(_epi_raw, col0=OFF_CQ, width=OFF_G - OFF_CQ, out_dtype=F32, patch=(OFF_MG, PC_GATE_COL),
                  name="proj_small")
        gt = proj(_epi_sigmoid, col0=OFF_G, width=N_BRANCH * d, out_dtype=BF16, name="proj_gates")

        qkc = _mconv(g, pm, m_conv_w[l], row(m_conv_b[l]), qk_scale)
        ym_l, ym_c = _mscan(g, qkc, pm, pc, row(m_norm_g[l]))

        qg, kg = mla_q_norm_g[l], mla_k_norm_g[l]
        a_scale = MLA_QK ** -0.5
        gains = (row(mla_cq_norm_g[l]), row(mla_ckv_norm_g[l]),
                 row(qg[:MLA_NOPE] * a_scale), row(jnp.tile(qg[MLA_NOPE:], 2) * a_scale),
                 row(kg[:MLA_NOPE]), row(jnp.concatenate([kg[MLA_NOPE:], jnp.zeros((LANES - MLA_ROPE,), F32)])))
        aq, ak, av = _mla_prep(g, pc, mla_w_uq[l][:, uq_perm], mla_w_ukv[l], gains, cos_a, sin_a)

        lam_p = da_lambda[l]
        sub_g = row(da_subln_g[l])
        da_kw = dict(q_col0=0, k_col0=DA_HEADS, heads=DA_HEADS, qk_w=2 * DA_DH, dv=DA_DV, n_soft=2,
                     lam=lam_p, subln_g=sub_g, lam_init=lam_init)
        mla_kw = dict(q_col0=0, k_col0=0, heads=MLA_HEADS, qk_w=MLA_HW, dv=MLA_DV, n_soft=1)
        yd_l = _attention(g, dqk, dqk, dv, ctx_queries=False, **da_kw)
        ya_l = _attention(g, aq, ak, av, ctx_queries=False, **mla_kw)
        if last:
            ys = (ym_l, yd_l, ya_l)
        else:
            yd_c = _attention(g, dqk, dqk, dv, ctx_queries=True, **da_kw)
            ya_c = _attention(g, aq, ak, av, ctx_queries=True, **mla_kw)
            ys = ((ym_l, ym_c), (yd_l, yd_c), (ya_l, ya_c))

        z = _branch_merge(g, ys, w_branch, l, gt, n_tiles)
        x_mid = _outproj(g, z, w_out, l, x_all, mods3, 2, n_tiles)

        xr, rank, grp, cnt = _moe_prep(g, x_mid, mods3, 3, 4, rwt, rb, tri, n_tiles)
        meta = _moe_meta(rank, grp, cnt, n_rows)
        n_cond = bsz if last else bsz + 1
        x_all = _moe_group(g, xr, mods[l][:, 3 * d:], w1r, w3r, w2r, meta, n_rows, n_cond, l * N_EXPERTS)

    return x_all.reshape(bsz, t, d)
```

```python
import functools
import math

import jax
import jax.numpy as jnp
import numpy as np
from jax import lax
from jax.experimental import pallas as pl
from jax.experimental.pallas import tpu as pltpu

F32 = jnp.float32
BF16 = jnp.bfloat16

GRID_W = 64
ROPE_BASE = 10000.0
EPS = 1e-6
N_BRANCH = 3
BRANCH_W = 1024
M_HEADS = 4
M_DH = BRANCH_W // M_HEADS
M_CHUNK = 256
M_CONV = 5
DA_HEADS = 4
DA_DV = BRANCH_W // DA_HEADS
DA_DH = DA_DV // 2
DA_QK_W = DA_HEADS * 2 * DA_DH
MLA_HEADS = 8
MLA_Q_RANK = 512
MLA_KV_RANK = 256
MLA_NOPE = 128
MLA_ROPE = 64
MLA_DV = BRANCH_W // MLA_HEADS
MLA_QK = MLA_NOPE + MLA_ROPE
N_EXPERTS = 16
N_GROUPS = 4
GROUP_SIZE = N_EXPERTS // N_GROUPS
D_EXPERT = 512

OFF_MQK = 0
OFF_MV = OFF_MQK + 2 * BRANCH_W
OFF_MO = OFF_MV + BRANCH_W
OFF_MG = OFF_MO + BRANCH_W
OFF_DQ = OFF_MG + 4 * M_HEADS
OFF_DK = OFF_DQ + DA_QK_W
OFF_DV = OFF_DK + DA_QK_W
OFF_CQ = OFF_DV + BRANCH_W
OFF_CKV = OFF_CQ + MLA_Q_RANK
OFF_KR = OFF_CKV + MLA_KV_RANK
OFF_G = OFF_KR + MLA_ROPE

LANES = 128
MOD_ROWS = 8
VMEM_LIMIT = 56 * 1024 * 1024
MLA_HW = 256
MOE_TM = 256
ATT_TQ = 2048
PC_GATE_COL = 896
PROJ_ROW_GROUPS = 4
MOE_ROW_GROUPS = 1


def _cp(*sem):
    return pltpu.CompilerParams(dimension_semantics=sem, vmem_limit_bytes=VMEM_LIMIT)


def _rms(x):
    return x * lax.rsqrt(jnp.mean(x * x, axis=-1, keepdims=True) + EPS)


def _sigmoid(x):
    return 0.5 * jnp.tanh(0.5 * x) + 0.5


def _dot(a, b):
    return jnp.dot(a, b, preferred_element_type=F32)


def _dot_nt(a, b):
    return lax.dot_general(a, b, (((1,), (1,)), ((), ())), preferred_element_type=F32)


def _mod_kernel(c_ref, w_ref, b_ref, o_ref):
    c = c_ref[...]
    s = (c * _sigmoid(c)).astype(BF16)
    o_ref[...] = _dot(s, w_ref[...].astype(BF16)) + b_ref[...]


def _mod_vectors(cvec, w_ada, b_ada):
    depth, d, n = w_ada.shape
    tn = 1024
    return pl.pallas_call(
        _mod_kernel,
        grid=(depth, n // tn),
        in_specs=[
            pl.BlockSpec((MOD_ROWS, d), lambda l, j: (0, 0)),
            pl.BlockSpec((None, d, tn), lambda l, j: (l, 0, j)),
            pl.BlockSpec((None, 1, tn), lambda l, j: (l, 0, j)),
        ],
        out_specs=pl.BlockSpec((None, MOD_ROWS, tn), lambda l, j: (l, 0, j)),
        out_shape=jax.ShapeDtypeStruct((depth, MOD_ROWS, n), F32),
        compiler_params=_cp("parallel", "parallel"),
        name="adaln_mod",
    )(cvec, w_ada, b_ada.reshape(depth, 1, n))


class _Geom:
    def __init__(self, b, t, lc, d, tm):
        assert t % tm == 0 and (b * lc) % tm == 0 and t % lc == 0 and t % GRID_W == 0
        self.b, self.t, self.lc, self.d, self.tm = b, t, lc, d, tm
        self.rl, self.rc = b * t, b * lc
        self.r = self.rl + self.rc
        self.mblk = min(256, lc)
        self.chunk = min(M_CHUNK, self.mblk)
        assert lc % self.mblk == 0 and t % self.mblk == 0 and self.mblk % self.chunk == 0
        self.n_lat_tiles = self.rl // tm
        self.n_tiles = self.r // tm
        self.tiles_per_seq = t // tm

    def mod_row(self, i):
        return jnp.minimum(i // self.tiles_per_seq, self.b)

    def pos_block(self, i):
        return jnp.where(i < self.n_lat_tiles, i % self.tiles_per_seq, self.tiles_per_seq)


def _mod_spec(g, which):
    return pl.BlockSpec((None, 1, g.d), lambda i: (g.mod_row(i), 0, which))


def _row_sources(g, src, width, col_of, n_grid):
    tile = lambda i: i
    if n_grid == 2:
        spec = lambda rows_of: pl.BlockSpec((g.tm, width), lambda j, i: (rows_of(i), col_of(j)))
    else:
        spec = lambda rows_of: pl.BlockSpec((g.tm, width), lambda i: (rows_of(i), 0))
    if not isinstance(src, tuple):
        return [spec(tile)], [src]
    lat, ctx = src
    nl = g.n_lat_tiles
    return ([spec(lambda i: jnp.minimum(i, nl - 1)), spec(lambda i: jnp.maximum(i - nl, 0))], [lat, ctx])


def _on_rows(i, n_lat_tiles, refs, body):
    if len(refs) == 1:
        body(refs[0])
        return

    @pl.when(i < n_lat_tiles)
    def _():
        body(refs[0])

    @pl.when(i >= n_lat_tiles)
    def _():
        body(refs[1])


def _prenorm_kernel(*refs, n_lat_tiles):
    *x_refs, sh_ref, sc_ref, o_ref = refs

    def body(x_ref):
        o_ref[...] = (_rms(x_ref[...]) * (1.0 + sc_ref[...]) + sh_ref[...]).astype(o_ref.dtype)

    _on_rows(pl.program_id(0), n_lat_tiles, x_refs, body)


def _prenorm(g, x_src, mods3, which_shift, which_scale, n_tiles):
    x_specs, x_args = _row_sources(g, x_src, g.d, None, 1)
    return pl.pallas_call(
        functools.partial(_prenorm_kernel, n_lat_tiles=g.n_lat_tiles),
        grid=(n_tiles,),
        in_specs=[*x_specs, _mod_spec(g, which_shift), _mod_spec(g, which_scale)],
        out_specs=pl.BlockSpec((g.tm, g.d), lambda i: (i, 0)),
        out_shape=jax.ShapeDtypeStruct((n_tiles * g.tm, g.d), BF16),
        compiler_params=_cp("parallel"),
        name="prenorm",
    )(*x_args, mods3, mods3)


def _epi_raw(acc, rs, o_ref):
    o_ref[rs, :] = acc.astype(o_ref.dtype)


def _epi_sigmoid(acc, rs, o_ref):
    o_ref[rs, :] = _sigmoid(acc).astype(o_ref.dtype)


def _epi_normrope(acc, rs, gain_ref, cos_ref, sin_ref, o_ref):
    cos = cos_ref[rs, :]
    sin = sin_ref[rs, :]
    for j in range(acc.shape[1] // LANES):
        sl = slice(j * LANES, (j + 1) * LANES)
        x = _rms(acc[:, sl]) * gain_ref[:, sl]
        o_ref[rs, sl] = (x * cos + pltpu.roll(x, LANES // 2, 1) * sin).astype(o_ref.dtype)


def _proj_kernel(*refs, shift, epilogue, patch):
    if shift:
        h_ref, wa_ref, wb_ref, ba_ref, bb_ref, *rest = refs
    else:
        h_ref, wa_ref, ba_ref, *rest = refs
    if patch is not None:
        wp_ref, bp_ref, *rest = rest
    *extra, o_ref, w_s, b_s = rest
    tn = w_s.shape[0]

    @pl.when(pl.program_id(1) == 0)
    def _():
        if shift:
            w_s[0:tn - shift, :] = wa_ref[shift:tn, :].astype(BF16)
            w_s[tn - shift:tn, :] = wb_ref[0:shift, :].astype(BF16)
            bb = jnp.concatenate([ba_ref[...], bb_ref[...]], axis=1)
            bb = jnp.broadcast_to(bb, (b_s.shape[0], 2 * tn))
            b_s[...] = pltpu.roll(bb, 2 * tn - shift, 1)[:, :tn]
        else:
            w_s[...] = wa_ref[...].astype(BF16)
            b_s[...] = jnp.broadcast_to(ba_ref[...], b_s.shape)
        if patch is not None:
            p_tile, p_off = patch

            @pl.when(pl.program_id(0) == p_tile)
            def _():
                w_s[p_off:p_off + LANES, :] = wp_ref[...].astype(BF16)
                b_s[:, p_off:p_off + LANES] = jnp.broadcast_to(bp_ref[...], (b_s.shape[0], LANES))

    rows = h_ref.shape[0] // PROJ_ROW_GROUPS
    for u in range(PROJ_ROW_GROUPS):
        rs = slice(u * rows, (u + 1) * rows)
        acc = _dot_nt(h_ref[rs, :], w_s[...]) + b_s[0:1, :]
        epilogue(acc, rs, *extra, o_ref)


def _proj(g, epilogue, h, w_in_t, b_in3, l, col0, width, out_dtype, *, tn=512, out_tn=None,
          extra=(), extra_specs=(), patch=None, name="proj"):
    ncols, kdim = w_in_t.shape[1], w_in_t.shape[2]
    blk0 = col0 // tn
    shift = col0 - blk0 * tn
    assert shift % 16 == 0
    nj = pl.cdiv(width, tn)
    last_blk = pl.cdiv(ncols, tn) - 1
    out_tn = tn if out_tn is None else out_tn
    w_specs = [pl.BlockSpec((None, tn, kdim), lambda j, i: (l, blk0 + j, 0))]
    b_specs = [pl.BlockSpec((None, 1, tn), lambda j, i: (l, 0, blk0 + j))]
    if shift:
        nxt = lambda j: jnp.minimum(blk0 + j + 1, last_blk)
        w_specs.append(pl.BlockSpec((None, tn, kdim), lambda j, i: (l, nxt(j), 0)))
        b_specs.append(pl.BlockSpec((None, 1, tn), lambda j, i: (l, 0, nxt(j))))
    n_w = len(w_specs)
    p_specs, p_args, k_patch = [], [], None
    if patch is not None:
        src_col, dst_col = patch
        assert src_col % LANES == 0 and dst_col % LANES == 0 and dst_col >= width
        p_specs = [pl.BlockSpec((None, LANES, kdim), lambda j, i: (l, src_col // LANES, 0)),
                   pl.BlockSpec((None, 1, LANES), lambda j, i: (l, 0, src_col // LANES))]
        p_args = [w_in_t, b_in3]
        k_patch = (dst_col // tn, dst_col % tn)
    return pl.pallas_call(
        functools.partial(_proj_kernel, shift=shift, epilogue=epilogue, patch=k_patch),
        grid=(nj, g.n_tiles),
        in_specs=[pl.BlockSpec((g.tm, kdim), lambda j, i: (i, 0)), *w_specs, *b_specs, *p_specs, *extra_specs],
        out_specs=pl.BlockSpec((g.tm, out_tn), lambda j, i: (i, j)),
        out_shape=jax.ShapeDtypeStruct((g.r, nj * out_tn), out_dtype),
        scratch_shapes=[pltpu.VMEM((tn, kdim), BF16), pltpu.VMEM((MOD_ROWS, tn), F32)],
        compiler_params=_cp("arbitrary", "arbitrary"),
        name=name,
    )(h, *([w_in_t] * n_w), *([b_in3] * n_w), *p_args, *extra)


def _mconv_kernel(x_ref, prev_ref, next_ref, w_ref, b_ref, s_ref, sh_ref, o_ref, *, blocks_per_seq, blocks_per_ctx,
                  n_lat_blocks):
    i = pl.program_id(0)
    rows = x_ref.shape[0]
    halo = prev_ref.shape[0]
    edge = MOD_ROWS
    in_lat = i < n_lat_blocks
    pos = jnp.where(in_lat, i % blocks_per_seq, (i - n_lat_blocks) % blocks_per_ctx)
    per_seq = jnp.where(in_lat, blocks_per_seq, blocks_per_ctx)
    x = x_ref[...]
    w = [w_ref[j:j + 1, :] for j in range(M_CONV)]
    acc = x.astype(F32) * w[M_CONV // 2] + b_ref[...]
    taps = [j for j in range(M_CONV) if j != M_CONV // 2]
    for n_, j in enumerate(taps):
        acc = acc + _dot(sh_ref[n_], x) * w[j]
    p = jnp.where(pos != 0, prev_ref[...].astype(F32), 0.0)
    nx = jnp.where(pos != per_seq - 1, next_ref[...].astype(F32), 0.0)
    r8 = lax.broadcasted_iota(jnp.int32, (edge, 1), 0)
    top = (jnp.where(r8 < 2, pltpu.roll(p, 2, 0)[0:edge], 0.0) * w[0]
           + jnp.where(r8 < 1, pltpu.roll(p, 1, 0)[0:edge], 0.0) * w[1])
    bot = (jnp.where(r8 >= edge - 1, pltpu.roll(nx, edge - 1, 0)[0:edge], 0.0) * w[3]
           + jnp.where(r8 >= edge - 2, pltpu.roll(nx, edge - 2, 0)[0:edge], 0.0) * w[4])
    zeros = jnp.zeros((halo - edge, x.shape[1]), F32)

    def finish(a):
        return (a * _sigmoid(a) * s_ref[...]).astype(o_ref.dtype)

    o_ref[0:halo, :] = finish(acc[0:halo] + jnp.concatenate([top, zeros], axis=0))
    o_ref[halo:rows - halo, :] = finish(acc[halo:rows - halo])
    o_ref[rows - halo:rows, :] = finish(acc[rows - halo:rows] + jnp.concatenate([zeros, bot], axis=0))


def _mconv(g, pm, conv_w, conv_b, scale_row):
    rows = g.mblk
    halo = 16
    tn = 2 * BRANCH_W
    per = rows // halo
    nblk = g.r // rows
    last = g.r // halo - 1
    taps = [j for j in range(M_CONV) if j != M_CONV // 2]
    shifts = np.zeros((len(taps), rows, rows), np.float32)
    for n_, j in enumerate(taps):
        for t_ in range(rows):
            if 0 <= t_ + j - M_CONV // 2 < rows:
                shifts[n_, t_, t_ + j - M_CONV // 2] = 1.0
    shifts = jnp.asarray(shifts, BF16)
    return pl.pallas_call(
        functools.partial(_mconv_kernel, blocks_per_seq=g.t // rows, blocks_per_ctx=g.lc // rows,
                          n_lat_blocks=g.rl // rows),
        grid=(nblk, 2 * BRANCH_W // tn),
        in_specs=[
            pl.BlockSpec((rows, tn), lambda i, j: (i, j)),
            pl.BlockSpec((halo, tn), lambda i, j: (jnp.maximum(i * per - 1, 0), j)),
            pl.BlockSpec((halo, tn), lambda i, j: (jnp.minimum((i + 1) * per, last), j)),
            pl.BlockSpec((M_CONV, tn), lambda i, j: (0, j)),
            pl.BlockSpec((1, tn), lambda i, j: (0, j)),
            pl.BlockSpec((1, tn), lambda i, j: (0, j)),
            pl.BlockSpec(shifts.shape, lambda i, j: (0, 0, 0)),
        ],
        out_specs=pl.BlockSpec((rows, tn), lambda i, j: (i, j)),
        out_shape=jax.ShapeDtypeStruct((g.r, 2 * BRANCH_W), BF16),
        compiler_params=_cp("parallel", "parallel"),
        name="mlstm_conv",
    )(pm, pm, pm, conv_w, conv_b, scale_row, shifts)


def _mscan_kernel(qkf_ref, vf_ref, gf_ref, qkb_ref, vb_ref, gb_ref, ol_ref, oc_ref, ng_ref, yl_ref, yc_ref,
                  xsf, bcf, xsb, bcb, dtf, dtb, hf, hb, ct, nv, ms, *, lc, t, mblk, chunk):
    i = pl.program_id(1)
    nblk = pl.num_programs(1)
    L = chunk
    cpb = mblk // L
    nbc = lc // mblk
    nbl = t // mblk
    n = lc + t

    @pl.when(i == 0)
    def _():
        ct[...] = jnp.zeros_like(ct)
        nv[...] = jnp.zeros_like(nv)
        ms[...] = jnp.zeros_like(ms)

    lane = lax.broadcasted_iota(jnp.int32, (1, LANES), 1)
    is_f = jnp.logical_or(jnp.logical_and(lane >= M_HEADS, lane < 2 * M_HEADS),
                          jnp.logical_and(lane >= 3 * M_HEADS, lane < 4 * M_HEADS))
    pos = lax.broadcasted_iota(jnp.int32, (mblk, 1), 0) % L

    def gate_prep(g_ref, x_out, bc_out, forward):
        gx = g_ref[...]
        lsg = jnp.minimum(gx, 0.0) - jnp.log(1.0 + jnp.exp(-jnp.abs(gx)))
        x = jnp.where(is_f, lsg, gx)
        x_out[...] = x
        run = x
        k = 1
        while k < L:
            if forward:
                run = run + jnp.where(pos >= k, pltpu.roll(run, k, 0), 0.0)
            else:
                run = run + jnp.where(pos < L - k, pltpu.roll(run, mblk - k, 0), 0.0)
            k *= 2
        bc_out[...] = run

    gate_prep(gf_ref, xsf, bcf, True)
    gate_prep(gb_ref, xsb, bcb, False)

    pos_f = i * mblk
    pos_b = jnp.where(i < nbc, (nbc - 1 - i) * mblk, lc + (nbl - 1 - (i - nbc)) * mblk)

    def row_table(x_ref, bc_ref, out_ref):
        dm = x_ref[...] - pltpu.roll(bc_ref[...], LANES - M_HEADS, 1)
        for c in range(cpb):
            out_ref[c] = dm[c * L:(c + 1) * L, :].T

    row_table(xsf, bcf, dtf)
    row_table(xsb, bcb, dtb)

    H = M_HEADS
    r_i = lax.broadcasted_iota(jnp.int32, (H * L, L), 0) % L
    c_i = lax.broadcasted_iota(jnp.int32, (H * L, L), 1)
    masks = (c_i <= r_i, c_i >= r_i)

    def stack(parts):
        return jnp.concatenate(parts, axis=0)

    def dir_step(c, d, mvals):
        qk_ref, v_ref, x_ref, bc_ref, dt_ref, h_ref, base = (
            (qkf_ref, vf_ref, xsf, bcf, dtf, hf, pos_f), (qkb_ref, vb_ref, xsb, bcb, dtb, hb, pos_b))[d]
        r0 = pl.multiple_of(c * L, L)
        bg = bc_ref[pl.ds(r0, L), :]
        dt = dt_ref[c]
        qs = [qk_ref[pl.ds(r0, L), j * M_DH:(j + 1) * M_DH] for j in range(H)]
        ks = [qk_ref[pl.ds(r0, L), BRANCH_W + j * M_DH:BRANCH_W + (j + 1) * M_DH] for j in range(H)]
        vs_ = [v_ref[pl.ds(r0, L), j * M_DH:(j + 1) * M_DH] for j in range(H)]
        li0 = 2 * d * H
        lf0 = (2 * d + 1) * H
        a_h = [bg[:, lf0 + j:lf0 + j + 1] for j in range(H)]
        bl_h = [a[L - 1:L, :] if d == 0 else a[0:1, :] for a in a_h]
        a4 = stack(a_h)
        d_rows = [dt[li0 + j:li0 + j + 1, :] for j in range(H)]
        d4 = stack([jnp.broadcast_to(r_, (L, L)) for r_ in d_rows])
        s4 = stack([_dot_nt(qs[j], ks[j]) for j in range(H)])
        logw = jnp.where(masks[d], a4 + d4, -jnp.inf)
        m_loc = jnp.max(logw, axis=1, keepdims=True)
        w4b = (jnp.exp(logw - m_loc) * s4).astype(BF16)
        den_loc = _dot(w4b, jnp.ones((L, LANES), BF16))[:, 0:1]
        num_loc = [_dot(w4b[j * L:(j + 1) * L, :], vs_[j]) for j in range(H)]
        s_rows = [bl_h[j] + d_rows[j] for j in range(H)]
        msrc_h = [jnp.max(s_, axis=1, keepdims=True) for s_ in s_rows]
        g_rows = [jnp.exp(s_rows[j] - msrc_h[j]) for j in range(H)]
        k_t = [ks[j].astype(F32).T for j in range(H)]
        d_c = [_dot((k_t[j] * g_rows[j]).astype(BF16), vs_[j]) for j in range(H)]
        d_n = [_dot(jnp.broadcast_to(g_rows[j], (MOD_ROWS, L)).astype(BF16), ks[j])[0:1, :] for j in range(H)]
        idx = [d * H + j for j in range(H)]
        nrow = [nv[i_] for i_ in idx]
        inter = [_dot(qs[j], ct[idx[j]].astype(BF16)) for j in range(H)]
        qn4 = stack([_dot_nt(qs[j], jnp.broadcast_to(nrow[j], (MOD_ROWS, M_DH)).astype(BF16))[:, 0:1]
                     for j in range(H)])
        m4 = stack([jnp.broadcast_to(mvals[idx[j]], (L, 1)) for j in range(H)])
        bm4 = a4 + m4
        m_t = jnp.maximum(bm4, m_loc)
        e_loc = jnp.exp(m_loc - m_t)
        e_int = jnp.exp(bm4 - m_t)
        den4 = e_loc * den_loc + e_int * qn4
        r4 = 1.0 / jnp.maximum(jnp.abs(den4), jnp.exp(-m_t))
        el = e_loc * r4
        ei = e_int * r4
        rows_out = pl.ds(pl.multiple_of(base + r0, L), L)
        out = list(mvals)
        for j in range(H):
            rs = slice(j * L, (j + 1) * L)
            h_ref[rows_out, j * M_DH:(j + 1) * M_DH] = el[rs] * num_loc[j] + ei[rs] * inter[j]
            m_old = mvals[idx[j]]
            m_new = jnp.maximum(bl_h[j] + m_old, msrc_h[j])
            keep = jnp.exp(bl_h[j] + m_old - m_new)
            scale = jnp.exp(msrc_h[j] - m_new)
            ct[idx[j]] = keep * ct[idx[j]] + scale * d_c[j]
            nv[idx[j]] = keep * nrow[j] + scale * d_n[j]
            out[idx[j]] = m_new
        return tuple(out)

    def body(s, mvals):
        mvals = dir_step(s, 0, mvals)
        return dir_step(cpb - 1 - s, 1, mvals)

    m0 = tuple(ms[q_:q_ + 1, 0:1] for q_ in range(2 * M_HEADS))
    m1 = lax.fori_loop(0, cpb, body, m0, unroll=min(2, cpb))
    for q_ in range(2 * M_HEADS):
        ms[q_:q_ + 1, :] = jnp.broadcast_to(m1[q_], (1, LANES))

    @pl.when(i == nblk - 1)
    def _():
        for blk in range(n // mblk):
            rs = slice(blk * mblk, (blk + 1) * mblk)
            for j in range(M_HEADS):
                hs = slice(j * M_DH, (j + 1) * M_DH)
                hn = _rms(hf[rs, hs] + hb[rs, hs]) * ng_ref[:, hs]
                if blk < nbc:
                    yc_ref[rs, hs] = (hn * _sigmoid(oc_ref[rs, hs].astype(F32))).astype(BF16)
                else:
                    ls = slice(blk * mblk - lc, (blk + 1) * mblk - lc)
                    yl_ref[ls, hs] = (hn * _sigmoid(ol_ref[ls, hs].astype(F32))).astype(BF16)


def _mscan(g, qkc, pm, pg, norm_g):
    mblk = g.mblk
    nbc, nbl = g.lc // mblk, g.t // mblk
    lat_blocks = g.rl // mblk
    n = g.lc + g.t

    def fwd(b, i):
        return jnp.where(i < nbc, lat_blocks + b * nbc + i, b * nbl + (i - nbc))

    def bwd(b, i):
        return jnp.where(i < nbc, lat_blocks + b * nbc + (nbc - 1 - i), b * nbl + (nbl - 1 - (i - nbc)))

    v_col = 2 * BRANCH_W // BRANCH_W
    o_col = 3 * BRANCH_W // BRANCH_W
    ctx0 = g.rl // g.lc

    def side(blk_of):
        return [pl.BlockSpec((mblk, 2 * BRANCH_W), lambda b, i: (blk_of(b, i), 0)),
                pl.BlockSpec((mblk, BRANCH_W), lambda b, i: (blk_of(b, i), v_col)),
                pl.BlockSpec((mblk, LANES), lambda b, i: (blk_of(b, i), PC_GATE_COL // LANES))]

    return pl.pallas_call(
        functools.partial(_mscan_kernel, lc=g.lc, t=g.t, mblk=mblk, chunk=g.chunk),
        grid=(g.b, nbc + nbl),
        in_specs=[
            *side(fwd), *side(bwd),
            pl.BlockSpec((g.t, BRANCH_W), lambda b, i: (b, o_col)),
            pl.BlockSpec((g.lc, BRANCH_W), lambda b, i: (ctx0 + b, o_col)),
            pl.BlockSpec((1, BRANCH_W), lambda b, i: (0, 0)),
        ],
        out_specs=[
            pl.BlockSpec((g.t, BRANCH_W), lambda b, i: (b, 0)),
            pl.BlockSpec((g.lc, BRANCH_W), lambda b, i: (b, 0)),
        ],
        out_shape=[
            jax.ShapeDtypeStruct((g.rl, BRANCH_W), BF16),
            jax.ShapeDtypeStruct((g.rc, BRANCH_W), BF16),
        ],
        scratch_shapes=[
            pltpu.VMEM((mblk, LANES), F32), pltpu.VMEM((mblk, LANES), F32),
            pltpu.VMEM((mblk, LANES), F32), pltpu.VMEM((mblk, LANES), F32),
            pltpu.VMEM((mblk // g.chunk, LANES, g.chunk), F32), pltpu.VMEM((mblk // g.chunk, LANES, g.chunk), F32),
            pltpu.VMEM((n, BRANCH_W), F32), pltpu.VMEM((n, BRANCH_W), F32),
            pltpu.VMEM((2 * M_HEADS, M_DH, M_DH), F32), pltpu.VMEM((2 * M_HEADS, 1, M_DH), F32),
            pltpu.VMEM((2 * M_HEADS, LANES), F32),
        ],
        compiler_params=_cp("parallel", "arbitrary"),
        name="mlstm_scan",
    )(qkc, pm, pg, qkc, pm, pg, pm, pm, norm_g)


def _mla_prep_kernel(cq_ref, ckv_ref, kr_ref, wuq_ref, wukv_ref, cqg_ref, ckvg_ref, qng_ref, qrg_ref,
                     kng_ref, krg_ref, cos_ref, sin_ref, aq_ref, ak_ref, av_ref, wuq_s, wukv_s):
    @pl.when(pl.program_id(0) == 0)
    def _():
        wuq_s[...] = wuq_ref[...].astype(BF16)
        wukv_s[...] = wukv_ref[...].astype(BF16)

    half = LANES // 2
    lane = lax.broadcasted_iota(jnp.int32, (1, LANES), 1)
    lo = lane < half
    first = (lane % half) < (half // 2)

    def rms_half(x):
        x2 = x * x
        s_lo = jnp.sum(jnp.where(lo, x2, 0.0), axis=-1, keepdims=True)
        s_hi = jnp.sum(jnp.where(lo, 0.0, x2), axis=-1, keepdims=True)
        ms = jnp.where(lo, s_lo, s_hi) * (1.0 / half)
        return x * lax.rsqrt(ms + EPS)

    def rows_group(rs):
        cos = cos_ref[rs, :]
        sin = sin_ref[rs, :]

        def rope_half(x):
            partner = jnp.where(first, pltpu.roll(x, LANES - half // 2, 1), pltpu.roll(x, half // 2, 1))
            return x * cos + partner * sin

        cq = (_rms(cq_ref[rs, :]) * cqg_ref[...]).astype(BF16)
        q = _dot(cq, wuq_s[...])
        ckv = (_rms(ckv_ref[rs, :]) * ckvg_ref[...]).astype(BF16)
        kv = _dot(ckv, wukv_s[...])

        krn = rope_half(rms_half(kr_ref[rs, :]) * krg_ref[...]).astype(BF16)

        rope0 = MLA_HEADS * MLA_NOPE
        for hp in range(MLA_HEADS // 2):
            slab = q[:, rope0 + hp * LANES: rope0 + (hp + 1) * LANES]
            r = rope_half(rms_half(slab) * qrg_ref[...])
            parts = (jnp.where(lo, r, 0.0), jnp.where(lo, pltpu.roll(r, half, 1), 0.0))
            for e in range(2):
                h = 2 * hp + e
                nope = _rms(q[:, h * MLA_NOPE:(h + 1) * MLA_NOPE]) * qng_ref[...]
                aq_ref[rs, h * MLA_HW: h * MLA_HW + MLA_NOPE] = nope.astype(BF16)
                aq_ref[rs, h * MLA_HW + MLA_NOPE:(h + 1) * MLA_HW] = parts[e].astype(BF16)
        kvw = MLA_NOPE + MLA_DV
        for h in range(MLA_HEADS):
            kn = _rms(kv[:, h * kvw: h * kvw + MLA_NOPE]) * kng_ref[...]
            ak_ref[rs, h * MLA_HW: h * MLA_HW + MLA_NOPE] = kn.astype(BF16)
            ak_ref[rs, h * MLA_HW + MLA_NOPE:(h + 1) * MLA_HW] = krn
            av_ref[rs, h * MLA_DV:(h + 1) * MLA_DV] = kv[:, h * kvw + MLA_NOPE:(h + 1) * kvw].astype(BF16)

    rows_group(slice(0, cq_ref.shape[0]))


def _mla_prep(g, pc, wuq, wukv, gains, cos_t, sin_t):
    tm = g.tm
    full = lambda shape: pl.BlockSpec(shape, lambda i: (0, 0))
    kr_col = (MLA_Q_RANK + MLA_KV_RANK) // LANES
    return pl.pallas_call(
        _mla_prep_kernel,
        grid=(g.n_tiles,),
        in_specs=[
            pl.BlockSpec((tm, MLA_Q_RANK), lambda i: (i, 0)),
            pl.BlockSpec((tm, MLA_KV_RANK), lambda i: (i, MLA_Q_RANK // MLA_KV_RANK)),
            pl.BlockSpec((tm, LANES), lambda i: (i, kr_col)),
            full(wuq.shape), full(wukv.shape),
            full((1, MLA_Q_RANK)), full((1, MLA_KV_RANK)),
            full((1, LANES)), full((1, LANES)), full((1, LANES)), full((1, LANES)),
            pl.BlockSpec((tm, LANES), lambda i: (g.pos_block(i), 0)),
            pl.BlockSpec((tm, LANES), lambda i: (g.pos_block(i), 0)),
        ],
        out_specs=[
            pl.BlockSpec((tm, MLA_HEADS * MLA_HW), lambda i: (i, 0)),
            pl.BlockSpec((tm, MLA_HEADS * MLA_HW), lambda i: (i, 0)),
            pl.BlockSpec((tm, BRANCH_W), lambda i: (i, 0)),
        ],
        out_shape=[
            jax.ShapeDtypeStruct((g.r, MLA_HEADS * MLA_HW), BF16),
            jax.ShapeDtypeStruct((g.r, MLA_HEADS * MLA_HW), BF16),
            jax.ShapeDtypeStruct((g.r, BRANCH_W), BF16),
        ],
        scratch_shapes=[pltpu.VMEM(wuq.shape, BF16), pltpu.VMEM(wukv.shape, BF16)],
        compiler_params=_cp("arbitrary"),
        name="mla_prep",
    )(pc, pc, pc, wuq, wukv, *gains, cos_t, sin_t)


def _attn_kernel(*refs, n_soft, dh, has_lat, diff, lam_init):
    refs = list(refs)
    vs = refs.pop()
    kts = refs.pop()
    o_ref = refs.pop()
    if diff:
        lam_ref, sg_ref = refs[0], refs[1]
        refs = refs[2:]
    q_ref, kc_ref, vc_ref = refs[:3]
    lc = kc_ref.shape[0]

    @pl.when(pl.program_id(2) == 0)
    def _():
        kts[:, 0:lc] = kc_ref[...].T
        vs[0:lc, :] = vc_ref[...]
        if has_lat:
            kts[:, lc:] = refs[3][...].T
            vs[lc:, :] = refs[4][...]

    q = q_ref[...]
    tq = q.shape[0]
    n_sub = 4 if tq % 64 == 0 else 1
    rsub = tq // n_sub
    outs = []
    for s in range(n_soft):
        sl = slice(s * dh, (s + 1) * dh)
        scs = [_dot(q[u * rsub:(u + 1) * rsub, sl], kts[sl, :]) for u in range(n_sub)]
        ps, ls = [], []
        for sc in scs:
            m = jnp.max(sc, axis=-1, keepdims=True)
            p = jnp.exp(sc - m)
            ls.append(jnp.sum(p, axis=-1, keepdims=True))
            ps.append(p.astype(BF16))
        os_ = [_dot(p, vs[...]) / l for p, l in zip(ps, ls)]
        outs.append(jnp.concatenate(os_, axis=0))
    if diff:
        lp = lam_ref[...]
        lam = (jnp.exp(jnp.sum(lp[0:1] * lp[1:2], axis=-1, keepdims=True))
               - jnp.exp(jnp.sum(lp[2:3] * lp[3:4], axis=-1, keepdims=True)) + lam_init)
        o = outs[0] - lam * outs[1]
        o = _rms(o) * sg_ref[...] * (1.0 - lam_init)
    else:
        o = outs[0]
    o_ref[...] = o.astype(o_ref.dtype)


def _attention(g, q_arr, k_arr, v_arr, *, q_col0, k_col0, heads, qk_w, dv, n_soft, ctx_queries,
               lam=None, subln_g=None, lam_init=0.0):
    diff = lam is not None
    ctx0 = g.rl // g.lc
    n_keys = g.lc if ctx_queries else g.lc + g.t
    if ctx_queries:
        tq = g.lc
        nq = 1
        q_row = lambda b, qi: ctx0 + b
        out_rows = g.rc
        o_row = lambda b, qi: b
    else:
        tq = min(ATT_TQ, g.t)
        nq = g.t // tq
        q_row = lambda b, qi: b * nq + qi
        out_rows = g.rl
        o_row = q_row
    in_specs = []
    args = []
    if diff:
        in_specs += [pl.BlockSpec(lam.shape, lambda b, h, qi: (0, 0)),
                     pl.BlockSpec((1, dv), lambda b, h, qi: (0, 0))]
        args += [lam, subln_g]
    in_specs += [
        pl.BlockSpec((tq, qk_w), lambda b, h, qi: (q_row(b, qi), q_col0 + h)),
        pl.BlockSpec((g.lc, qk_w), lambda b, h, qi: (ctx0 + b, k_col0 + h)),
        pl.BlockSpec((g.lc, dv), lambda b, h, qi: (ctx0 + b, h)),
    ]
    args += [q_arr, k_arr, v_arr]
    if not ctx_queries:
        in_specs += [
            pl.BlockSpec((g.t, qk_w), lambda b, h, qi: (b, k_col0 + h)),
            pl.BlockSpec((g.t, dv), lambda b, h, qi: (b, h)),
        ]
        args += [k_arr, v_arr]
    return pl.pallas_call(
        functools.partial(_attn_kernel, n_soft=n_soft, dh=qk_w // n_soft, has_lat=not ctx_queries,
                          diff=diff, lam_init=lam_init),
        grid=(g.b, heads, nq),
        in_specs=in_specs,
        out_specs=pl.BlockSpec((tq, dv), lambda b, h, qi: (o_row(b, qi), h)),
        out_shape=jax.ShapeDtypeStruct((out_rows, heads * dv), BF16),
        scratch_shapes=[pltpu.VMEM((qk_w, n_keys), BF16), pltpu.VMEM((n_keys, dv), BF16)],
        compiler_params=_cp("parallel", "parallel", "arbitrary"),
        name="attn_diff" if diff else "attn_mla",
    )(*args)


def _branch_kernel(*refs, n_src, n_lat_tiles):
    y_refs = refs[:N_BRANCH * n_src]
    wb_ref, g0_ref, g1_ref, g2_ref, o_ref, wb_s = refs[N_BRANCH * n_src:]
    gates = (g0_ref, g1_ref, g2_ref)

    @pl.when(pl.program_id(1) == 0)
    def _():
        wb_s[...] = wb_ref[...].astype(BF16)

    def body(which):
        def run(_):
            acc = None
            for n in range(N_BRANCH):
                term = gates[n][...].astype(F32) * _dot(y_refs[n * n_src + which][...], wb_s[n])
                acc = term if acc is None else acc + term
            o_ref[...] = acc.astype(o_ref.dtype)
        return run

    i = pl.program_id(1)
    if n_src == 1:
        body(0)(None)
    else:
        pl.when(i < n_lat_tiles)(lambda: body(0)(None))
        pl.when(i >= n_lat_tiles)(lambda: body(1)(None))


def _branch_merge(g, ys, w_branch, l, gt, n_tiles):
    tn = 512
    nj = g.d // tn
    y_specs, y_args = [], []
    for y in ys:
        sp, ar = _row_sources(g, y, BRANCH_W, lambda j: 0, 2)
        y_specs += sp
        y_args += ar
    n_src = len(y_args) // N_BRANCH
    return pl.pallas_call(
        functools.partial(_branch_kernel, n_src=n_src, n_lat_tiles=g.n_lat_tiles),
        grid=(nj, n_tiles),
        in_specs=[
            *y_specs,
            pl.BlockSpec((None, N_BRANCH, BRANCH_W, tn), lambda j, i: (l, 0, 0, j)),
            pl.BlockSpec((g.tm, tn), lambda j, i: (i, j)),
            pl.BlockSpec((g.tm, tn), lambda j, i: (i, nj + j)),
            pl.BlockSpec((g.tm, tn), lambda j, i: (i, 2 * nj + j)),
        ],
        out_specs=pl.BlockSpec((g.tm, tn), lambda j, i: (i, j)),
        out_shape=jax.ShapeDtypeStruct((n_tiles * g.tm, g.d), BF16),
        scratch_shapes=[pltpu.VMEM((N_BRANCH, BRANCH_W, tn), BF16)],
        compiler_params=_cp("arbitrary", "arbitrary"),
        name="branch_merge",
    )(*y_args, w_branch, gt, gt, gt)


def _outproj_kernel(*refs, n_lat_tiles):
    z_ref, w_ref, *x_refs, g_ref, o_ref, w_s = refs

    @pl.when(pl.program_id(1) == 0)
    def _():
        w_s[...] = w_ref[...].astype(BF16)

    def body(x_ref):
        o_ref[...] = x_ref[...] + g_ref[...] * _dot(z_ref[...], w_s[...])

    _on_rows(pl.program_id(1), n_lat_tiles, x_refs, body)


def _outproj(g, z, w_out, l, x_src, mods3, which_gate, n_tiles):
    tn = 512
    per = g.d // tn
    x_specs, x_args = _row_sources(g, x_src, tn, lambda j: j, 2)
    return pl.pallas_call(
        functools.partial(_outproj_kernel, n_lat_tiles=g.n_lat_tiles),
        grid=(per, n_tiles),
        in_specs=[
            pl.BlockSpec((g.tm, g.d), lambda j, i: (i, 0)),
            pl.BlockSpec((None, g.d, tn), lambda j, i: (l, 0, j)),
            *x_specs,
            pl.BlockSpec((None, 1, tn), lambda j, i: (g.mod_row(i), 0, which_gate * per + j)),
        ],
        out_specs=pl.BlockSpec((g.tm, tn), lambda j, i: (i, j)),
        out_shape=jax.ShapeDtypeStruct((n_tiles * g.tm, g.d), F32),
        scratch_shapes=[pltpu.VMEM((g.d, tn), BF16)],
        compiler_params=_cp("arbitrary", "arbitrary"),
        name="outproj",
    )(z, w_out, *x_args, mods3)


def _moe_prep_kernel(x_ref, sh_ref, sc_ref, rwt_ref, rb_ref, tri_ref, xr_ref, rank_ref, grp_ref, cnt_ref,
                     carry, *, mod_row):
    i = pl.program_id(0)

    @pl.when(i == 0)
    def _():
        carry[...] = jnp.zeros_like(carry)

    x = x_ref[...]
    h = _rms(x) * (1.0 + sc_ref[...]) + sh_ref[...]
    logits = _dot_nt(rwt_ref[...].astype(BF16), h.astype(BF16))
    scores = _sigmoid(logits)
    selm = scores + rb_ref[...]
    sc_rows = [scores[e:e + 1, :] for e in range(N_EXPERTS)]
    sel = [selm[e:e + 1, :] for e in range(N_EXPERTS)]
    gscore = []
    for gi in range(N_GROUPS):
        v = sel[gi * GROUP_SIZE:(gi + 1) * GROUP_SIZE]
        pair = [v[a] + v[b] for a in range(GROUP_SIZE) for b in range(a + 1, GROUP_SIZE)]
        gscore.append(functools.reduce(jnp.maximum, pair))
    gmax = functools.reduce(jnp.maximum, gscore)
    taken = jnp.zeros_like(gmax, dtype=jnp.bool_)
    in_group = []
    for gi in range(N_GROUPS):
        hit = jnp.logical_and(gscore[gi] == gmax, jnp.logical_not(taken))
        taken = jnp.logical_or(taken, hit)
        in_group.append(hit)
    neg = -jnp.inf
    masked = [jnp.where(in_group[e // GROUP_SIZE], sel[e], neg) for e in range(N_EXPERTS)]

    def pick(vals):
        mx = functools.reduce(jnp.maximum, vals)
        seen = jnp.zeros_like(mx, dtype=jnp.bool_)
        hot = []
        for v in vals:
            hit = jnp.logical_and(v == mx, jnp.logical_not(seen))
            seen = jnp.logical_or(seen, hit)
            hot.append(hit)
        return hot

    hot1 = pick(masked)
    hot2 = pick([jnp.where(hot1[e], neg, masked[e]) for e in range(N_EXPERTS)])
    zero = jnp.zeros_like(gmax)
    s1 = functools.reduce(lambda a, b: a + b, [jnp.where(hot1[e], sc_rows[e], zero) for e in range(N_EXPERTS)])
    s2 = functools.reduce(lambda a, b: a + b, [jnp.where(hot2[e], sc_rows[e], zero) for e in range(N_EXPERTS)])
    tot = s1 + s2
    w1 = s1 / tot
    w2 = s2 / tot
    comb = [jnp.where(hot1[e], w1, zero) + jnp.where(hot2[e], w2, zero) for e in range(N_EXPERTS)]

    tm = x.shape[0]
    gmat = jnp.concatenate([jnp.where(in_group[gi], 1.0, zero) for gi in range(N_GROUPS)]
                           + [jnp.zeros((MOD_ROWS - N_GROUPS, tm), F32)], axis=0)
    incl = _dot(gmat.astype(BF16), tri_ref[...])
    excl = incl - gmat + carry[:, 0:1]
    rank = functools.reduce(lambda a, b: a + b,
                            [jnp.where(in_group[gi], excl[gi:gi + 1, :], zero) for gi in range(N_GROUPS)])
    grp = functools.reduce(lambda a, b: a + b,
                           [jnp.where(in_group[gi], float(gi), zero) for gi in range(N_GROUPS)])
    rank_ref[...] = rank.astype(jnp.int32)
    grp_ref[...] = grp.astype(jnp.int32)
    carry[...] = carry[...] + jnp.sum(gmat, axis=1, keepdims=True)
    cnt_ref[...] = carry[...]

    side = jnp.concatenate(comb + [jnp.zeros((LANES - N_EXPERTS, tm), F32)], axis=0).T
    lane = lax.broadcasted_iota(jnp.int32, (1, LANES), 1)
    side = jnp.where(lane == N_EXPERTS + mod_row(i), 1.0, side)
    d = x.shape[1]
    xr_ref[:, 0:d] = x
    xr_ref[:, d:d + LANES] = side


def _moe_prep(g, x_mid, mods3, which_shift, which_scale, rwt, rb, tri, n_tiles):
    n_rows = n_tiles * g.tm
    return pl.pallas_call(
        functools.partial(_moe_prep_kernel, mod_row=g.mod_row),
        grid=(n_tiles,),
        in_specs=[
            pl.BlockSpec((g.tm, g.d), lambda i: (i, 0)),
            _mod_spec(g, which_shift),
            _mod_spec(g, which_scale),
            pl.BlockSpec((N_EXPERTS, g.d), lambda i: (0, 0)),
            pl.BlockSpec((N_EXPERTS, 1), lambda i: (0, 0)),
            pl.BlockSpec((g.tm, g.tm), lambda i: (0, 0)),
        ],
        out_specs=[
            pl.BlockSpec((g.tm, g.d + LANES), lambda i: (i, 0)),
            pl.BlockSpec((1, g.tm), lambda i: (0, i)),
            pl.BlockSpec((1, g.tm), lambda i: (0, i)),
            pl.BlockSpec((MOD_ROWS, LANES), lambda i: (0, 0)),
        ],
        out_shape=[
            jax.ShapeDtypeStruct((n_rows, g.d + LANES), F32),
            jax.ShapeDtypeStruct((1, n_rows), jnp.int32),
            jax.ShapeDtypeStruct((1, n_rows), jnp.int32),
            jax.ShapeDtypeStruct((MOD_ROWS, LANES), F32),
        ],
        scratch_shapes=[pltpu.VMEM((MOD_ROWS, LANES), F32)],
        compiler_params=_cp("arbitrary"),
        name="moe_prep",
    )(x_mid, mods3, mods3, rwt, rb, tri)


def _moe_group_kernel(tg_ref, nv_ref, nu_ref, pos_ref, xr_hbm, mods_ref, w1_hbm, w3_hbm, w2_hbm, out_hbm,
                      gbuf, obuf, w1s, w3s, w2s, stg_a, stg_b, src_ref, gsem, ssem, wsem, *, n_mod_rows, e_base):
    k = pl.program_id(0)
    n_used = nu_ref[0]
    slot = k % 2
    d = obuf.shape[2]
    tme = obuf.shape[1]

    def gather_start(kk, sl):
        base = kk * tme

        first = src_ref[base]

        def copy_row(r, tok):
            pltpu.make_async_copy(xr_hbm.at[pl.ds(tok, 1), :], gbuf.at[sl, pl.ds(r, 1), :], gsem.at[sl]).start()

        def issue(r, c):
            copy_row(r, src_ref[base + r])
            return c

        def issue_pad(r, c):
            copy_row(r, first)
            return c

        rows_of(kk, issue)
        lax.fori_loop(nv_ref[kk], tme, issue_pad, 0)

    def build_sorted_index():
        def put(t, c):
            src_ref[pos_ref[t]] = t
            return c

        lax.fori_loop(0, pos_ref.shape[0], put, 0, unroll=8)

    def gather_wait(sl):
        def w(r, c):
            pltpu.make_async_copy(xr_hbm.at[pl.ds(0, 1), :], gbuf.at[sl, pl.ds(0, 1), :], gsem.at[sl]).wait()
            return c

        lax.fori_loop(0, tme, w, 0, unroll=8)

    def scatter_start(kk, sl):
        base = kk * tme

        def issue(r, c):
            tok = src_ref[base + r]
            pltpu.make_async_copy(obuf.at[sl, pl.ds(r, 1), :], out_hbm.at[pl.ds(tok, 1), :], ssem.at[sl]).start()
            return c

        rows_of(kk, issue)

    def scatter_wait(kk, sl):
        def w(r, c):
            pltpu.make_async_copy(obuf.at[sl, pl.ds(0, 1), :], out_hbm.at[pl.ds(0, 1), :], ssem.at[sl]).wait()
            return c

        rows_of(kk, w)

    def rows_of(kk, fn):
        n_rows = nv_ref[kk]

        @pl.when(n_rows == tme)
        def _():
            lax.fori_loop(0, tme, fn, 0, unroll=8)

        @pl.when(n_rows != tme)
        def _():
            lax.fori_loop(0, n_rows, fn, 0)

    def load_weights(grp):
        rows_a = stg_a.shape[1]
        rows_b = stg_b.shape[1]
        chunks = []
        n_a = n_b = 0
        for e in range(GROUP_SIZE):
            ex = e_base + grp * GROUP_SIZE + e
            for src, dst in ((w1_hbm, w1s), (w3_hbm, w3s)):
                for c0 in range(0, src.shape[1], rows_a):
                    sl = n_a % 2
                    n_a += 1
                    cp = pltpu.make_async_copy(src.at[ex, pl.ds(c0, rows_a), :], stg_a.at[sl], wsem.at[sl])
                    chunks.append((cp, stg_a.at[sl], dst, e, c0, rows_a))
            for c0 in range(0, w2_hbm.shape[1], rows_b):
                sl = n_b % 2
                n_b += 1
                cp = pltpu.make_async_copy(w2_hbm.at[ex, pl.ds(c0, rows_b), :], stg_b.at[sl], wsem.at[2 + sl])
                chunks.append((cp, stg_b.at[sl], w2s, e, c0, rows_b))
        chunks[0][0].start()
        for idx, (cp, stg, dst, e, c0, rows) in enumerate(chunks):
            if idx + 1 < len(chunks):
                chunks[idx + 1][0].start()
            cp.wait()
            dst[e, c0:c0 + rows, :] = stg[...].astype(BF16)

    @pl.when(k < n_used)
    def _():
        grp = tg_ref[k]

        @pl.when(k == 0)
        def _():
            build_sorted_index()
            gather_start(0, 0)

        @pl.when(k + 1 < n_used)
        def _():
            gather_start(k + 1, 1 - slot)

        @pl.when(jnp.logical_or(k == 0, grp != tg_ref[jnp.maximum(k - 1, 0)]))
        def _():
            load_weights(grp)

        gather_wait(slot)

        @pl.when(k >= 2)
        def _():
            scatter_wait(k - 2, slot)

        lane = lax.broadcasted_iota(jnp.int32, (1, LANES), 1)
        rows = tme // MOE_ROW_GROUPS
        for u in range(MOE_ROW_GROUPS):
            rs = slice(u * rows, (u + 1) * rows)
            x = gbuf[slot, rs, 0:d]
            side = gbuf[slot, rs, d:d + LANES]

            def cond_vec(which):
                out = jnp.broadcast_to(mods_ref[0:1, which * d:(which + 1) * d], (rows, d))
                for r in range(1, n_mod_rows):
                    hot = side[:, N_EXPERTS + r:N_EXPERTS + r + 1] > 0.5
                    out = jnp.where(hot, mods_ref[r:r + 1, which * d:(which + 1) * d], out)
                return out

            h = (_rms(x) * (1.0 + cond_vec(1)) + cond_vec(0)).astype(BF16)
            y = jnp.zeros((rows, d), F32)
            for e in range(GROUP_SIZE):
                a = _dot(h, w1s[e])
                b = _dot(h, w3s[e])
                cw = jnp.sum(jnp.where(lane == grp * GROUP_SIZE + e, side, 0.0), axis=1, keepdims=True)
                hid = (a * _sigmoid(a) * b * cw).astype(BF16)
                y = y + _dot(hid, w2s[e])
            obuf[slot, rs, :] = x + cond_vec(2) * y

        scatter_start(k, slot)

        @pl.when(k == n_used - 1)
        def _():
            @pl.when(k >= 1)
            def _():
                scatter_wait(k - 1, 1 - slot)

            scatter_wait(k, slot)


def _moe_group(g, xr, mods_tail, w1, w3, w2, meta, n_rows, n_mod_rows, e_base):
    tile_group, n_valid, n_used, src = meta
    kt = tile_group.shape[0]
    tme = MOE_TM
    any_spec = pl.BlockSpec(memory_space=pl.ANY)
    grid_spec = pltpu.PrefetchScalarGridSpec(
        num_scalar_prefetch=4,
        grid=(kt,),
        in_specs=[any_spec, pl.BlockSpec(mods_tail.shape, lambda k, *_: (0, 0)), any_spec, any_spec, any_spec],
        out_specs=any_spec,
        scratch_shapes=[
            pltpu.VMEM((2, tme, g.d + LANES), F32),
            pltpu.VMEM((2, tme, g.d), F32),
            pltpu.VMEM((GROUP_SIZE, g.d, D_EXPERT), BF16),
            pltpu.VMEM((GROUP_SIZE, g.d, D_EXPERT), BF16),
            pltpu.VMEM((GROUP_SIZE, D_EXPERT, g.d), BF16),
            pltpu.VMEM((2, g.d // 2, D_EXPERT), F32),
            pltpu.VMEM((2, D_EXPERT // 2, g.d), F32),
            pltpu.SMEM((kt * tme,), jnp.int32),
            pltpu.SemaphoreType.DMA((2,)),
            pltpu.SemaphoreType.DMA((2,)),
            pltpu.SemaphoreType.DMA((4,)),
        ],
    )
    return pl.pallas_call(
        functools.partial(_moe_group_kernel, n_mod_rows=n_mod_rows, e_base=e_base),
        grid_spec=grid_spec,
        out_shape=jax.ShapeDtypeStruct((n_rows, g.d), F32),
        compiler_params=_cp("arbitrary"),
        name="moe_group",
    )(tile_group, n_valid, n_used, src, xr, mods_tail, w1, w3, w2)


def _moe_meta(rank, grp, cnt, n_rows):
    tme = MOE_TM
    kt = n_rows // tme + N_GROUPS
    counts = cnt[:N_GROUPS, 0].astype(jnp.int32)
    ntile = (counts + tme - 1) // tme
    tile_end = jnp.cumsum(ntile)
    tile_off = tile_end - ntile
    gsel = [grp[0] == gi for gi in range(N_GROUPS)]
    row_off = functools.reduce(lambda a, b: a + b,
                               [jnp.where(gsel[gi], tile_off[gi] * tme, 0) for gi in range(N_GROUPS)])
    pos = row_off + rank[0]
    ks = jnp.arange(kt, dtype=jnp.int32)
    tile_group = jnp.minimum(functools.reduce(lambda a, b: a + b,
                                              [(ks >= tile_end[gi]).astype(jnp.int32) for gi in range(N_GROUPS)]),
                             N_GROUPS - 1)
    n_used = tile_end[-1:]
    cnt_k = functools.reduce(lambda a, b: a + b,
                             [jnp.where(tile_group == gi, counts[gi] - (ks - tile_off[gi]) * tme, 0)
                              for gi in range(N_GROUPS)])
    n_valid = jnp.where(ks < n_used[0], jnp.clip(cnt_k, 0, tme), 0)
    return tile_group, n_valid, n_used, pos


def _rope_tables(t, rot_dim, tile_rows, reps):
    f32 = np.float32
    rows = t // GRID_W
    r = np.repeat(np.arange(rows, dtype=f32), GRID_W)
    col = np.tile(np.arange(GRID_W, dtype=f32), rows)
    n_freq = rot_dim // 4
    inv = np.power(f32(ROPE_BASE), -np.arange(n_freq, dtype=f32) / f32(n_freq)).astype(f32)
    ang = np.concatenate([r[:, None] * inv, col[:, None] * inv], axis=-1).astype(f32)
    cos = np.tile(np.concatenate([np.cos(ang), np.cos(ang)], axis=-1), (1, reps))
    sin = np.tile(np.concatenate([-np.sin(ang), np.sin(ang)], axis=-1), (1, reps))
    cos = np.concatenate([cos, np.ones((tile_rows, LANES), f32)], axis=0).astype(f32)
    sin = np.concatenate([sin, np.zeros((tile_rows, LANES), f32)], axis=0).astype(f32)
    return jnp.asarray(cos), jnp.asarray(sin)


def _uq_perm():
    nope = [h * MLA_QK + i for h in range(MLA_HEADS) for i in range(MLA_NOPE)]
    rope = [h * MLA_QK + MLA_NOPE + i for h in range(MLA_HEADS) for i in range(MLA_ROPE)]
    return jnp.asarray(nope + rope)


def kernel(x, c, ctx, c_ctx, w_ada, b_ada, w_in, b_in, m_conv_w, m_conv_b, m_norm_g, da_q_norm_g, da_k_norm_g,
           da_lambda, da_subln_g, mla_cq_norm_g, mla_ckv_norm_g, mla_w_uq, mla_w_ukv, mla_q_norm_g, mla_k_norm_g,
           w_branch, w_out, moe_w1, moe_w3, moe_w2, router_w, router_bias):
    bsz, t, d = x.shape
    lc = ctx.shape[1]
    depth = w_ada.shape[0]
    g = _Geom(bsz, t, lc, d, min(1024, bsz * lc, t))
    assert bsz + 1 <= MOD_ROWS

    cvec = jnp.concatenate([c, c_ctx[None, :], jnp.zeros((MOD_ROWS - bsz - 1, d), F32)], axis=0)
    mods = _mod_vectors(cvec, w_ada, b_ada)

    cos_d, sin_d = _rope_tables(t, DA_DH, g.tm, 1)
    cos_a, sin_a = _rope_tables(t, MLA_ROPE, g.tm, 2)
    rwt = router_w.T
    rb = router_bias.reshape(N_EXPERTS, 1)
    uq_perm = _uq_perm()
    x_all = (x.reshape(g.rl, d), ctx.reshape(g.rc, d))
    pos_spec = pl.BlockSpec((g.tm, LANES), lambda j, i: (g.pos_block(i), 0))
    b_in3 = b_in.reshape(depth, 1, -1)
    w_in_t = jnp.swapaxes(w_in, 1, 2)
    qk_scale = jnp.concatenate([jnp.full((1, BRANCH_W), M_DH ** -0.5, F32), jnp.ones((1, BRANCH_W), F32)], axis=1)
    tri = (jnp.arange(g.tm)[:, None] <= jnp.arange(g.tm)[None, :]).astype(BF16)
    w1r = moe_w1.reshape(depth * N_EXPERTS, d, D_EXPERT)
    w3r = moe_w3.reshape(depth * N_EXPERTS, d, D_EXPERT)
    w2r = moe_w2.reshape(depth * N_EXPERTS, D_EXPERT, d)

    for l in range(depth):
        last = l == depth - 1
        n_tiles = g.n_lat_tiles if last else g.n_tiles
        n_rows = n_tiles * g.tm
        lam_init = 0.8 - 0.6 * math.exp(-0.3 * l)
        mods3 = mods[l].reshape(MOD_ROWS, 1, 6 * d)

        h1 = _prenorm(g, x_all, mods3, 0, 1, g.n_tiles)

        row = lambda v: v.reshape(1, -1)
        proj = functools.partial(_proj, g, h=h1, w_in_t=w_in_t, b_in3=b_in3, l=l)
        pm = proj(_epi_raw, col0=OFF_MQK, width=OFF_MG - OFF_MQK, out_dtype=BF16, name="proj_mlstm")
        da_gain = jnp.concatenate([jnp.tile(da_q_norm_g[l] * DA_DH ** -0.5, 2 * DA_HEADS),
                                   jnp.tile(da_k_norm_g[l], 2 * DA_HEADS)])
        dqk = proj(_epi_normrope, col0=OFF_DQ, width=2 * DA_QK_W, out_dtype=BF16,
                   extra=(row(da_gain), cos_d, sin_d),
                   extra_specs=(pl.BlockSpec((1, 512), lambda j, i: (0, j)), pos_spec, pos_spec),
                   name="proj_dqk")
        dv = proj(_epi_raw, col0=OFF_DV, width=BRANCH_W, out_dtype=BF16, name="proj_dv")
        pc = proj(_epi_raw, col0=OFF_CQ, width=OFF_G - OFF_CQ, out_dtype=F32, patch=(OFF_MG, PC_GATE_COL),
                  name="proj_small")
        gt = proj(_epi_sigmoid, col0=OFF_G, width=N_BRANCH * d, out_dtype=BF16, name="proj_gates")

        qkc = _mconv(g, pm, m_conv_w[l], row(m_conv_b[l]), qk_scale)
        ym_l, ym_c = _mscan(g, qkc, pm, pc, row(m_norm_g[l]))

        qg, kg = mla_q_norm_g[l], mla_k_norm_g[l]
        a_scale = MLA_QK ** -0.5
        gains = (row(mla_cq_norm_g[l]), row(mla_ckv_norm_g[l]),
                 row(qg[:MLA_NOPE] * a_scale), row(jnp.tile(qg[MLA_NOPE:], 2) * a_scale),
                 row(kg[:MLA_NOPE]), row(jnp.concatenate([kg[MLA_NOPE:], jnp.zeros((LANES - MLA_ROPE,), F32)])))
        aq, ak, av = _mla_prep(g, pc, mla_w_uq[l][:, uq_perm], mla_w_ukv[l], gains, cos_a, sin_a)

        lam_p = da_lambda[l]
        sub_g = row(da_subln_g[l])
        da_kw = dict(q_col0=0, k_col0=DA_HEADS, heads=DA_HEADS, qk_w=2 * DA_DH, dv=DA_DV, n_soft=2,
                     lam=lam_p, subln_g=sub_g, lam_init=lam_init)
        mla_kw = dict(q_col0=0, k_col0=0, heads=MLA_HEADS, qk_w=MLA_HW, dv=MLA_DV, n_soft=1)
        yd_l = _attention(g, dqk, dqk, dv, ctx_queries=False, **da_kw)
        ya_l = _attention(g, aq, ak, av, ctx_queries=False, **mla_kw)
        if last:
            ys = (ym_l, yd_l, ya_l)
        else:
            yd_c = _attention(g, dqk, dqk, dv, ctx_queries=True, **da_kw)
            ya_c = _attention(g, aq, ak, av, ctx_queries=True, **mla_kw)
            ys = ((ym_l, ym_c), (yd_l, yd_c), (ya_l, ya_c))

        z = _branch_merge(g, ys, w_branch, l, gt, n_tiles)
        x_mid = _outproj(g, z, w_out, l, x_all, mods3, 2, n_tiles)

        xr, rank, grp, cnt = _moe_prep(g, x_mid, mods3, 3, 4, rwt, rb, tri, n_tiles)
        meta = _moe_meta(rank, grp, cnt, n_rows)
        n_cond = bsz if last else bsz + 1
        x_all = _moe_group(g, xr, mods[l][:, 3 * d:], w1r, w3r, w2r, meta, n_rows, n_cond, l * N_EXPERTS)

    return x_all.reshape(bsz, t, d)
```

```python
import functools
import math

import jax
import jax.numpy as jnp
import numpy as np
from jax import lax
from jax.experimental import pallas as pl
from jax.experimental.pallas import tpu as pltpu

F32 = jnp.float32
BF16 = jnp.bfloat16

GRID_W = 64
ROPE_BASE = 10000.0
EPS = 1e-6
N_BRANCH = 3
BRANCH_W = 1024
M_HEADS = 4
M_DH = BRANCH_W // M_HEADS
M_CHUNK = 256
M_CONV = 5
DA_HEADS = 4
DA_DV = BRANCH_W // DA_HEADS
DA_DH = DA_DV // 2
DA_QK_W = DA_HEADS * 2 * DA_DH
MLA_HEADS = 8
MLA_Q_RANK = 512
MLA_KV_RANK = 256
MLA_NOPE = 128
MLA_ROPE = 64
MLA_DV = BRANCH_W // MLA_HEADS
MLA_QK = MLA_NOPE + MLA_ROPE
N_EXPERTS = 16
N_GROUPS = 4
GROUP_SIZE = N_EXPERTS // N_GROUPS
D_EXPERT = 512
EXPERT_PAIRS = [(a, b) for a in range(GROUP_SIZE) for b in range(a + 1, GROUP_SIZE)]
N_CLASSES = N_GROUPS * len(EXPERT_PAIRS)
CLS_ROWS = -(-N_CLASSES // 8) * 8

OFF_MQK = 0
OFF_MV = OFF_MQK + 2 * BRANCH_W
OFF_MO = OFF_MV + BRANCH_W
OFF_MG = OFF_MO + BRANCH_W
OFF_DQ = OFF_MG + 4 * M_HEADS
OFF_DK = OFF_DQ + DA_QK_W
OFF_DV = OFF_DK + DA_QK_W
OFF_CQ = OFF_DV + BRANCH_W
OFF_CKV = OFF_CQ + MLA_Q_RANK
OFF_KR = OFF_CKV + MLA_KV_RANK
OFF_G = OFF_KR + MLA_ROPE

LANES = 128
MOD_ROWS = 8
VMEM_LIMIT = 56 * 1024 * 1024
MLA_HW = 256
MOE_TM = 256
ATT_TQ = 2048
PC_GATE_COL = 896
PROJ_ROW_GROUPS = 4
MOE_ROW_GROUPS = 1


def _cp(*sem):
    return pltpu.CompilerParams(dimension_semantics=sem, vmem_limit_bytes=VMEM_LIMIT)


def _rms(x):
    return x * lax.rsqrt(jnp.mean(x * x, axis=-1, keepdims=True) + EPS)


def _sigmoid(x):
    return 0.5 * jnp.tanh(0.5 * x) + 0.5


def _dot(a, b):
    return jnp.dot(a, b, preferred_element_type=F32)


def _dot_nt(a, b):
    return lax.dot_general(a, b, (((1,), (1,)), ((), ())), preferred_element_type=F32)


def _mod_kernel(c_ref, w_ref, b_ref, o_ref):
    c = c_ref[...]
    s = (c * _sigmoid(c)).astype(BF16)
    o_ref[...] = _dot(s, w_ref[...].astype(BF16)) + b_ref[...]


def _mod_vectors(cvec, w_ada, b_ada):
    depth, d, n = w_ada.shape
    tn = 1024
    return pl.pallas_call(
        _mod_kernel,
        grid=(depth, n // tn),
        in_specs=[
            pl.BlockSpec((MOD_ROWS, d), lambda l, j: (0, 0)),
            pl.BlockSpec((None, d, tn), lambda l, j: (l, 0, j)),
            pl.BlockSpec((None, 1, tn), lambda l, j: (l, 0, j)),
        ],
        out_specs=pl.BlockSpec((None, MOD_ROWS, tn), lambda l, j: (l, 0, j)),
        out_shape=jax.ShapeDtypeStruct((depth, MOD_ROWS, n), F32),
        compiler_params=_cp("parallel", "parallel"),
        name="adaln_mod",
    )(cvec, w_ada, b_ada.reshape(depth, 1, n))


class _Geom:
    def __init__(self, b, t, lc, d, tm):
        assert t % tm == 0 and (b * lc) % tm == 0 and t % lc == 0 and t % GRID_W == 0
        self.b, self.t, self.lc, self.d, self.tm = b, t, lc, d, tm
        self.rl, self.rc = b * t, b * lc
        self.r = self.rl + self.rc
        self.mblk = min(256, lc)
        self.chunk = min(M_CHUNK, self.mblk)
        assert lc % self.mblk == 0 and t % self.mblk == 0 and self.mblk % self.chunk == 0
        self.n_lat_tiles = self.rl // tm
        self.n_tiles = self.r // tm
        self.tiles_per_seq = t // tm

    def mod_row(self, i):
        return jnp.minimum(i // self.tiles_per_seq, self.b)

    def pos_block(self, i):
        return jnp.where(i < self.n_lat_tiles, i % self.tiles_per_seq, self.tiles_per_seq)


def _mod_spec(g, which):
    return pl.BlockSpec((None, 1, g.d), lambda i: (g.mod_row(i), 0, which))


def _row_sources(g, src, width, col_of, n_grid):
    tile = lambda i: i
    if n_grid == 2:
        spec = lambda rows_of: pl.BlockSpec((g.tm, width), lambda j, i: (rows_of(i), col_of(j)))
    else:
        spec = lambda rows_of: pl.BlockSpec((g.tm, width), lambda i: (rows_of(i), 0))
    if not isinstance(src, tuple):
        return [spec(tile)], [src]
    lat, ctx = src
    nl = g.n_lat_tiles
    return ([spec(lambda i: jnp.minimum(i, nl - 1)), spec(lambda i: jnp.maximum(i - nl, 0))], [lat, ctx])


def _on_rows(i, n_lat_tiles, refs, body):
    if len(refs) == 1:
        body(refs[0])
        return

    @pl.when(i < n_lat_tiles)
    def _():
        body(refs[0])

    @pl.when(i >= n_lat_tiles)
    def _():
        body(refs[1])


def _prenorm_kernel(*refs, n_lat_tiles):
    *x_refs, sh_ref, sc_ref, o_ref = refs

    def body(x_ref):
        o_ref[...] = (_rms(x_ref[...]) * (1.0 + sc_ref[...]) + sh_ref[...]).astype(o_ref.dtype)

    _on_rows(pl.program_id(0), n_lat_tiles, x_refs, body)


def _prenorm(g, x_src, mods3, which_shift, which_scale, n_tiles):
    x_specs, x_args = _row_sources(g, x_src, g.d, None, 1)
    return pl.pallas_call(
        functools.partial(_prenorm_kernel, n_lat_tiles=g.n_lat_tiles),
        grid=(n_tiles,),
        in_specs=[*x_specs, _mod_spec(g, which_shift), _mod_spec(g, which_scale)],
        out_specs=pl.BlockSpec((g.tm, g.d), lambda i: (i, 0)),
        out_shape=jax.ShapeDtypeStruct((n_tiles * g.tm, g.d), BF16),
        compiler_params=_cp("parallel"),
        name="prenorm",
    )(*x_args, mods3, mods3)


def _epi_raw(acc, rs, o_ref):
    o_ref[rs, :] = acc.astype(o_ref.dtype)


def _epi_sigmoid(acc, rs, o_ref):
    o_ref[rs, :] = _sigmoid(acc).astype(o_ref.dtype)


def _epi_normrope(acc, rs, gain_ref, cos_ref, sin_ref, o_ref):
    cos = cos_ref[rs, :]
    sin = sin_ref[rs, :]
    for j in range(acc.shape[1] // LANES):
        sl = slice(j * LANES, (j + 1) * LANES)
        x = _rms(acc[:, sl]) * gain_ref[:, sl]
        o_ref[rs, sl] = (x * cos + pltpu.roll(x, LANES // 2, 1) * sin).astype(o_ref.dtype)


def _proj_kernel(*refs, shift, epilogue, patch):
    if shift:
        h_ref, wa_ref, wb_ref, ba_ref, bb_ref, *rest = refs
    else:
        h_ref, wa_ref, ba_ref, *rest = refs
    if patch is not None:
        wp_ref, bp_ref, *rest = rest
    *extra, o_ref, w_s, b_s = rest
    tn = w_s.shape[0]

    @pl.when(pl.program_id(1) == 0)
    def _():
        if shift:
            w_s[0:tn - shift, :] = wa_ref[shift:tn, :].astype(BF16)
            w_s[tn - shift:tn, :] = wb_ref[0:shift, :].astype(BF16)
            bb = jnp.concatenate([ba_ref[...], bb_ref[...]], axis=1)
            bb = jnp.broadcast_to(bb, (b_s.shape[0], 2 * tn))
            b_s[...] = pltpu.roll(bb, 2 * tn - shift, 1)[:, :tn]
        else:
            w_s[...] = wa_ref[...].astype(BF16)
            b_s[...] = jnp.broadcast_to(ba_ref[...], b_s.shape)
        if patch is not None:
            p_tile, p_off = patch

            @pl.when(pl.program_id(0) == p_tile)
            def _():
                w_s[p_off:p_off + LANES, :] = wp_ref[...].astype(BF16)
                b_s[:, p_off:p_off + LANES] = jnp.broadcast_to(bp_ref[...], (b_s.shape[0], LANES))

    rows = h_ref.shape[0] // PROJ_ROW_GROUPS
    for u in range(PROJ_ROW_GROUPS):
        rs = slice(u * rows, (u + 1) * rows)
        acc = _dot_nt(h_ref[rs, :], w_s[...]) + b_s[0:1, :]
        epilogue(acc, rs, *extra, o_ref)


def _proj(g, epilogue, h, w_in_t, b_in3, l, col0, width, out_dtype, *, tn=512, out_tn=None,
          extra=(), extra_specs=(), patch=None, name="proj"):
    ncols, kdim = w_in_t.shape[1], w_in_t.shape[2]
    blk0 = col0 // tn
    shift = col0 - blk0 * tn
    assert shift % 16 == 0
    nj = pl.cdiv(width, tn)
    last_blk = pl.cdiv(ncols, tn) - 1
    out_tn = tn if out_tn is None else out_tn
    w_specs = [pl.BlockSpec((None, tn, kdim), lambda j, i: (l, blk0 + j, 0))]
    b_specs = [pl.BlockSpec((None, 1, tn), lambda j, i: (l, 0, blk0 + j))]
    if shift:
        nxt = lambda j: jnp.minimum(blk0 + j + 1, last_blk)
        w_specs.append(pl.BlockSpec((None, tn, kdim), lambda j, i: (l, nxt(j), 0)))
        b_specs.append(pl.BlockSpec((None, 1, tn), lambda j, i: (l, 0, nxt(j))))
    n_w = len(w_specs)
    p_specs, p_args, k_patch = [], [], None
    if patch is not None:
        src_col, dst_col = patch
        assert src_col % LANES == 0 and dst_col % LANES == 0 and dst_col >= width
        p_specs = [pl.BlockSpec((None, LANES, kdim), lambda j, i: (l, src_col // LANES, 0)),
                   pl.BlockSpec((None, 1, LANES), lambda j, i: (l, 0, src_col // LANES))]
        p_args = [w_in_t, b_in3]
        k_patch = (dst_col // tn, dst_col % tn)
    return pl.pallas_call(
        functools.partial(_proj_kernel, shift=shift, epilogue=epilogue, patch=k_patch),
        grid=(nj, g.n_tiles),
        in_specs=[pl.BlockSpec((g.tm, kdim), lambda j, i: (i, 0)), *w_specs, *b_specs, *p_specs, *extra_specs],
        out_specs=pl.BlockSpec((g.tm, out_tn), lambda j, i: (i, j)),
        out_shape=jax.ShapeDtypeStruct((g.r, nj * out_tn), out_dtype),
        scratch_shapes=[pltpu.VMEM((tn, kdim), BF16), pltpu.VMEM((MOD_ROWS, tn), F32)],
        compiler_params=_cp("arbitrary", "arbitrary"),
        name=name,
    )(h, *([w_in_t] * n_w), *([b_in3] * n_w), *p_args, *extra)


def _mconv_kernel(x_ref, prev_ref, next_ref, w_ref, b_ref, s_ref, sh_ref, o_ref, *, blocks_per_seq, blocks_per_ctx,
                  n_lat_blocks):
    i = pl.program_id(0)
    rows = x_ref.shape[0]
    halo = prev_ref.shape[0]
    edge = MOD_ROWS
    in_lat = i < n_lat_blocks
    pos = jnp.where(in_lat, i % blocks_per_seq, (i - n_lat_blocks) % blocks_per_ctx)
    per_seq = jnp.where(in_lat, blocks_per_seq, blocks_per_ctx)
    x = x_ref[...]
    w = [w_ref[j:j + 1, :] for j in range(M_CONV)]
    acc = x.astype(F32) * w[M_CONV // 2] + b_ref[...]
    taps = [j for j in range(M_CONV) if j != M_CONV // 2]
    for n_, j in enumerate(taps):
        acc = acc + _dot(sh_ref[n_], x) * w[j]
    p = jnp.where(pos != 0, prev_ref[...].astype(F32), 0.0)
    nx = jnp.where(pos != per_seq - 1, next_ref[...].astype(F32), 0.0)
    r8 = lax.broadcasted_iota(jnp.int32, (edge, 1), 0)
    top = (jnp.where(r8 < 2, pltpu.roll(p, 2, 0)[0:edge], 0.0) * w[0]
           + jnp.where(r8 < 1, pltpu.roll(p, 1, 0)[0:edge], 0.0) * w[1])
    bot = (jnp.where(r8 >= edge - 1, pltpu.roll(nx, edge - 1, 0)[0:edge], 0.0) * w[3]
           + jnp.where(r8 >= edge - 2, pltpu.roll(nx, edge - 2, 0)[0:edge], 0.0) * w[4])
    zeros = jnp.zeros((halo - edge, x.shape[1]), F32)

    def finish(a):
        return (a * _sigmoid(a) * s_ref[...]).astype(o_ref.dtype)

    o_ref[0:halo, :] = finish(acc[0:halo] + jnp.concatenate([top, zeros], axis=0))
    o_ref[halo:rows - halo, :] = finish(acc[halo:rows - halo])
    o_ref[rows - halo:rows, :] = finish(acc[rows - halo:rows] + jnp.concatenate([zeros, bot], axis=0))


def _mconv(g, pm, conv_w, conv_b, scale_row):
    rows = g.mblk
    halo = 16
    tn = 2 * BRANCH_W
    per = rows // halo
    nblk = g.r // rows
    last = g.r // halo - 1
    taps = [j for j in range(M_CONV) if j != M_CONV // 2]
    shifts = np.zeros((len(taps), rows, rows), np.float32)
    for n_, j in enumerate(taps):
        for t_ in range(rows):
            if 0 <= t_ + j - M_CONV // 2 < rows:
                shifts[n_, t_, t_ + j - M_CONV // 2] = 1.0
    shifts = jnp.asarray(shifts, BF16)
    return pl.pallas_call(
        functools.partial(_mconv_kernel, blocks_per_seq=g.t // rows, blocks_per_ctx=g.lc // rows,
                          n_lat_blocks=g.rl // rows),
        grid=(nblk, 2 * BRANCH_W // tn),
        in_specs=[
            pl.BlockSpec((rows, tn), lambda i, j: (i, j)),
            pl.BlockSpec((halo, tn), lambda i, j: (jnp.maximum(i * per - 1, 0), j)),
            pl.BlockSpec((halo, tn), lambda i, j: (jnp.minimum((i + 1) * per, last), j)),
            pl.BlockSpec((M_CONV, tn), lambda i, j: (0, j)),
            pl.BlockSpec((1, tn), lambda i, j: (0, j)),
            pl.BlockSpec((1, tn), lambda i, j: (0, j)),
            pl.BlockSpec(shifts.shape, lambda i, j: (0, 0, 0)),
        ],
        out_specs=pl.BlockSpec((rows, tn), lambda i, j: (i, j)),
        out_shape=jax.ShapeDtypeStruct((g.r, 2 * BRANCH_W), BF16),
        compiler_params=_cp("parallel", "parallel"),
        name="mlstm_conv",
    )(pm, pm, pm, conv_w, conv_b, scale_row, shifts)


def _mscan_kernel(qkf_ref, vf_ref, gf_ref, qkb_ref, vb_ref, gb_ref, ol_ref, oc_ref, ng_ref, yl_ref, yc_ref,
                  xsf, bcf, xsb, bcb, dtf, dtb, hf, hb, ct, nv, ms, *, lc, t, mblk, chunk):
    i = pl.program_id(1)
    nblk = pl.num_programs(1)
    L = chunk
    cpb = mblk // L
    nbc = lc // mblk
    nbl = t // mblk
    n = lc + t

    @pl.when(i == 0)
    def _():
        ct[...] = jnp.zeros_like(ct)
        nv[...] = jnp.zeros_like(nv)
        ms[...] = jnp.zeros_like(ms)

    lane = lax.broadcasted_iota(jnp.int32, (1, LANES), 1)
    is_f = jnp.logical_or(jnp.logical_and(lane >= M_HEADS, lane < 2 * M_HEADS),
                          jnp.logical_and(lane >= 3 * M_HEADS, lane < 4 * M_HEADS))
    pos = lax.broadcasted_iota(jnp.int32, (mblk, 1), 0) % L

    def gate_prep(g_ref, x_out, bc_out, forward):
        gx = g_ref[...]
        lsg = jnp.minimum(gx, 0.0) - jnp.log(1.0 + jnp.exp(-jnp.abs(gx)))
        x = jnp.where(is_f, lsg, gx)
        x_out[...] = x
        run = x
        k = 1
        while k < L:
            if forward:
                run = run + jnp.where(pos >= k, pltpu.roll(run, k, 0), 0.0)
            else:
                run = run + jnp.where(pos < L - k, pltpu.roll(run, mblk - k, 0), 0.0)
            k *= 2
        bc_out[...] = run

    gate_prep(gf_ref, xsf, bcf, True)
    gate_prep(gb_ref, xsb, bcb, False)

    pos_f = i * mblk
    pos_b = jnp.where(i < nbc, (nbc - 1 - i) * mblk, lc + (nbl - 1 - (i - nbc)) * mblk)

    def row_table(x_ref, bc_ref, out_ref):
        dm = x_ref[...] - pltpu.roll(bc_ref[...], LANES - M_HEADS, 1)
        for c in range(cpb):
            out_ref[c] = dm[c * L:(c + 1) * L, :].T

    row_table(xsf, bcf, dtf)
    row_table(xsb, bcb, dtb)

    H = M_HEADS
    r_i = lax.broadcasted_iota(jnp.int32, (H * L, L), 0) % L
    c_i = lax.broadcasted_iota(jnp.int32, (H * L, L), 1)
    masks = (c_i <= r_i, c_i >= r_i)

    def stack(parts):
        return jnp.concatenate(parts, axis=0)

    def dir_step(c, d, mvals):
        qk_ref, v_ref, x_ref, bc_ref, dt_ref, h_ref, base = (
            (qkf_ref, vf_ref, xsf, bcf, dtf, hf, pos_f), (qkb_ref, vb_ref, xsb, bcb, dtb, hb, pos_b))[d]
        r0 = pl.multiple_of(c * L, L)
        bg = bc_ref[pl.ds(r0, L), :]
        dt = dt_ref[c]
        qs = [qk_ref[pl.ds(r0, L), j * M_DH:(j + 1) * M_DH] for j in range(H)]
        ks = [qk_ref[pl.ds(r0, L), BRANCH_W + j * M_DH:BRANCH_W + (j + 1) * M_DH] for j in range(H)]
        vs_ = [v_ref[pl.ds(r0, L), j * M_DH:(j + 1) * M_DH] for j in range(H)]
        li0 = 2 * d * H
        lf0 = (2 * d + 1) * H
        a_h = [bg[:, lf0 + j:lf0 + j + 1] for j in range(H)]
        bl_h = [a[L - 1:L, :] if d == 0 else a[0:1, :] for a in a_h]
        a4 = stack(a_h)
        d_rows = [dt[li0 + j:li0 + j + 1, :] for j in range(H)]
        d4 = stack([jnp.broadcast_to(r_, (L, L)) for r_ in d_rows])
        s4 = stack([_dot_nt(qs[j], ks[j]) for j in range(H)])
        logw = jnp.where(masks[d], a4 + d4, -jnp.inf)
        m_loc = jnp.max(logw, axis=1, keepdims=True)
        w4b = (jnp.exp(logw - m_loc) * s4).astype(BF16)
        den_loc = _dot(w4b, jnp.ones((L, LANES), BF16))[:, 0:1]
        num_loc = [_dot(w4b[j * L:(j + 1) * L, :], vs_[j]) for j in range(H)]
        s_rows = [bl_h[j] + d_rows[j] for j in range(H)]
        msrc_h = [jnp.max(s_, axis=1, keepdims=True) for s_ in s_rows]
        g_rows = [jnp.exp(s_rows[j] - msrc_h[j]) for j in range(H)]
        k_t = [ks[j].astype(F32).T for j in range(H)]
        d_c = [_dot((k_t[j] * g_rows[j]).astype(BF16), vs_[j]) for j in range(H)]
        d_n = [_dot(jnp.broadcast_to(g_rows[j], (MOD_ROWS, L)).astype(BF16), ks[j])[0:1, :] for j in range(H)]
        idx = [d * H + j for j in range(H)]
        nrow = [nv[i_] for i_ in idx]
        inter = [_dot(qs[j], ct[idx[j]].astype(BF16)) for j in range(H)]
        qn4 = stack([_dot_nt(qs[j], jnp.broadcast_to(nrow[j], (MOD_ROWS, M_DH)).astype(BF16))[:, 0:1]
                     for j in range(H)])
        m4 = stack([jnp.broadcast_to(mvals[idx[j]], (L, 1)) for j in range(H)])
        bm4 = a4 + m4
        m_t = jnp.maximum(bm4, m_loc)
        e_loc = jnp.exp(m_loc - m_t)
        e_int = jnp.exp(bm4 - m_t)
        den4 = e_loc * den_loc + e_int * qn4
        r4 = 1.0 / jnp.maximum(jnp.abs(den4), jnp.exp(-m_t))
        el = e_loc * r4
        ei = e_int * r4
        rows_out = pl.ds(pl.multiple_of(base + r0, L), L)
        out = list(mvals)
        for j in range(H):
            rs = slice(j * L, (j + 1) * L)
            h_ref[rows_out, j * M_DH:(j + 1) * M_DH] = el[rs] * num_loc[j] + ei[rs] * inter[j]
            m_old = mvals[idx[j]]
            m_new = jnp.maximum(bl_h[j] + m_old, msrc_h[j])
            keep = jnp.exp(bl_h[j] + m_old - m_new)
            scale = jnp.exp(msrc_h[j] - m_new)
            ct[idx[j]] = keep * ct[idx[j]] + scale * d_c[j]
            nv[idx[j]] = keep * nrow[j] + scale * d_n[j]
            out[idx[j]] = m_new
        return tuple(out)

    def body(s, mvals):
        mvals = dir_step(s, 0, mvals)
        return dir_step(cpb - 1 - s, 1, mvals)

    m0 = tuple(ms[q_:q_ + 1, 0:1] for q_ in range(2 * M_HEADS))
    m1 = lax.fori_loop(0, cpb, body, m0, unroll=min(2, cpb))
    for q_ in range(2 * M_HEADS):
        ms[q_:q_ + 1, :] = jnp.broadcast_to(m1[q_], (1, LANES))

    @pl.when(i == nblk - 1)
    def _():
        for blk in range(n // mblk):
            rs = slice(blk * mblk, (blk + 1) * mblk)
            for j in range(M_HEADS):
                hs = slice(j * M_DH, (j + 1) * M_DH)
                hn = _rms(hf[rs, hs] + hb[rs, hs]) * ng_ref[:, hs]
                if blk < nbc:
                    yc_ref[rs, hs] = (hn * _sigmoid(oc_ref[rs, hs].astype(F32))).astype(BF16)
                else:
                    ls = slice(blk * mblk - lc, (blk + 1) * mblk - lc)
                    yl_ref[ls, hs] = (hn * _sigmoid(ol_ref[ls, hs].astype(F32))).astype(BF16)


def _mscan(g, qkc, pm, pg, norm_g):
    mblk = g.mblk
    nbc, nbl = g.lc // mblk, g.t // mblk
    lat_blocks = g.rl // mblk
    n = g.lc + g.t

    def fwd(b, i):
        return jnp.where(i < nbc, lat_blocks + b * nbc + i, b * nbl + (i - nbc))

    def bwd(b, i):
        return jnp.where(i < nbc, lat_blocks + b * nbc + (nbc - 1 - i), b * nbl + (nbl - 1 - (i - nbc)))

    v_col = 2 * BRANCH_W // BRANCH_W
    o_col = 3 * BRANCH_W // BRANCH_W
    ctx0 = g.rl // g.lc

    def side(blk_of):
        return [pl.BlockSpec((mblk, 2 * BRANCH_W), lambda b, i: (blk_of(b, i), 0)),
                pl.BlockSpec((mblk, BRANCH_W), lambda b, i: (blk_of(b, i), v_col)),
                pl.BlockSpec((mblk, LANES), lambda b, i: (blk_of(b, i), PC_GATE_COL // LANES))]

    return pl.pallas_call(
        functools.partial(_mscan_kernel, lc=g.lc, t=g.t, mblk=mblk, chunk=g.chunk),
        grid=(g.b, nbc + nbl),
        in_specs=[
            *side(fwd), *side(bwd),
            pl.BlockSpec((g.t, BRANCH_W), lambda b, i: (b, o_col)),
            pl.BlockSpec((g.lc, BRANCH_W), lambda b, i: (ctx0 + b, o_col)),
            pl.BlockSpec((1, BRANCH_W), lambda b, i: (0, 0)),
        ],
        out_specs=[
            pl.BlockSpec((g.t, BRANCH_W), lambda b, i: (b, 0)),
            pl.BlockSpec((g.lc, BRANCH_W), lambda b, i: (b, 0)),
        ],
        out_shape=[
            jax.ShapeDtypeStruct((g.rl, BRANCH_W), BF16),
            jax.ShapeDtypeStruct((g.rc, BRANCH_W), BF16),
        ],
        scratch_shapes=[
            pltpu.VMEM((mblk, LANES), F32), pltpu.VMEM((mblk, LANES), F32),
            pltpu.VMEM((mblk, LANES), F32), pltpu.VMEM((mblk, LANES), F32),
            pltpu.VMEM((mblk // g.chunk, LANES, g.chunk), F32), pltpu.VMEM((mblk // g.chunk, LANES, g.chunk), F32),
            pltpu.VMEM((n, BRANCH_W), F32), pltpu.VMEM((n, BRANCH_W), F32),
            pltpu.VMEM((2 * M_HEADS, M_DH, M_DH), F32), pltpu.VMEM((2 * M_HEADS, 1, M_DH), F32),
            pltpu.VMEM((2 * M_HEADS, LANES), F32),
        ],
        compiler_params=_cp("parallel", "arbitrary"),
        name="mlstm_scan",
    )(qkc, pm, pg, qkc, pm, pg, pm, pm, norm_g)


def _mla_prep_kernel(cq_ref, ckv_ref, kr_ref, wuq_ref, wukv_ref, cqg_ref, ckvg_ref, qng_ref, qrg_ref,
                     kng_ref, krg_ref, cos_ref, sin_ref, aq_ref, ak_ref, av_ref, wuq_s, wukv_s):
    @pl.when(pl.program_id(0) == 0)
    def _():
        wuq_s[...] = wuq_ref[...].astype(BF16)
        wukv_s[...] = wukv_ref[...].astype(BF16)

    half = LANES // 2
    lane = lax.broadcasted_iota(jnp.int32, (1, LANES), 1)
    lo = lane < half
    first = (lane % half) < (half // 2)

    def rms_half(x):
        x2 = x * x
        s_lo = jnp.sum(jnp.where(lo, x2, 0.0), axis=-1, keepdims=True)
        s_hi = jnp.sum(jnp.where(lo, 0.0, x2), axis=-1, keepdims=True)
        ms = jnp.where(lo, s_lo, s_hi) * (1.0 / half)
        return x * lax.rsqrt(ms + EPS)

    def rows_group(rs):
        cos = cos_ref[rs, :]
        sin = sin_ref[rs, :]

        def rope_half(x):
            partner = jnp.where(first, pltpu.roll(x, LANES - half // 2, 1), pltpu.roll(x, half // 2, 1))
            return x * cos + partner * sin

        cq = (_rms(cq_ref[rs, :]) * cqg_ref[...]).astype(BF16)
        q = _dot(cq, wuq_s[...])
        ckv = (_rms(ckv_ref[rs, :]) * ckvg_ref[...]).astype(BF16)
        kv = _dot(ckv, wukv_s[...])

        krn = rope_half(rms_half(kr_ref[rs, :]) * krg_ref[...]).astype(BF16)

        rope0 = MLA_HEADS * MLA_NOPE
        for hp in range(MLA_HEADS // 2):
            slab = q[:, rope0 + hp * LANES: rope0 + (hp + 1) * LANES]
            r = rope_half(rms_half(slab) * qrg_ref[...])
            parts = (jnp.where(lo, r, 0.0), jnp.where(lo, pltpu.roll(r, half, 1), 0.0))
            for e in range(2):
                h = 2 * hp + e
                nope = _rms(q[:, h * MLA_NOPE:(h + 1) * MLA_NOPE]) * qng_ref[...]
                aq_ref[rs, h * MLA_HW: h * MLA_HW + MLA_NOPE] = nope.astype(BF16)
                aq_ref[rs, h * MLA_HW + MLA_NOPE:(h + 1) * MLA_HW] = parts[e].astype(BF16)
        kvw = MLA_NOPE + MLA_DV
        for h in range(MLA_HEADS):
            kn = _rms(kv[:, h * kvw: h * kvw + MLA_NOPE]) * kng_ref[...]
            ak_ref[rs, h * MLA_HW: h * MLA_HW + MLA_NOPE] = kn.astype(BF16)
            ak_ref[rs, h * MLA_HW + MLA_NOPE:(h + 1) * MLA_HW] = krn
            av_ref[rs, h * MLA_DV:(h + 1) * MLA_DV] = kv[:, h * kvw + MLA_NOPE:(h + 1) * kvw].astype(BF16)

    rows_group(slice(0, cq_ref.shape[0]))


def _mla_prep(g, pc, wuq, wukv, gains, cos_t, sin_t):
    tm = g.tm
    full = lambda shape: pl.BlockSpec(shape, lambda i: (0, 0))
    kr_col = (MLA_Q_RANK + MLA_KV_RANK) // LANES
    return pl.pallas_call(
        _mla_prep_kernel,
        grid=(g.n_tiles,),
        in_specs=[
            pl.BlockSpec((tm, MLA_Q_RANK), lambda i: (i, 0)),
            pl.BlockSpec((tm, MLA_KV_RANK), lambda i: (i, MLA_Q_RANK // MLA_KV_RANK)),
            pl.BlockSpec((tm, LANES), lambda i: (i, kr_col)),
            full(wuq.shape), full(wukv.shape),
            full((1, MLA_Q_RANK)), full((1, MLA_KV_RANK)),
            full((1, LANES)), full((1, LANES)), full((1, LANES)), full((1, LANES)),
            pl.BlockSpec((tm, LANES), lambda i: (g.pos_block(i), 0)),
            pl.BlockSpec((tm, LANES), lambda i: (g.pos_block(i), 0)),
        ],
        out_specs=[
            pl.BlockSpec((tm, MLA_HEADS * MLA_HW), lambda i: (i, 0)),
            pl.BlockSpec((tm, MLA_HEADS * MLA_HW), lambda i: (i, 0)),
            pl.BlockSpec((tm, BRANCH_W), lambda i: (i, 0)),
        ],
        out_shape=[
            jax.ShapeDtypeStruct((g.r, MLA_HEADS * MLA_HW), BF16),
            jax.ShapeDtypeStruct((g.r, MLA_HEADS * MLA_HW), BF16),
            jax.ShapeDtypeStruct((g.r, BRANCH_W), BF16),
        ],
        scratch_shapes=[pltpu.VMEM(wuq.shape, BF16), pltpu.VMEM(wukv.shape, BF16)],
        compiler_params=_cp("arbitrary"),
        name="mla_prep",
    )(pc, pc, pc, wuq, wukv, *gains, cos_t, sin_t)


def _attn_kernel(*refs, n_soft, dh, has_lat, diff, lam_init):
    refs = list(refs)
    vs = refs.pop()
    kts = refs.pop()
    o_ref = refs.pop()
    if diff:
        lam_ref, sg_ref = refs[0], refs[1]
        refs = refs[2:]
    q_ref, kc_ref, vc_ref = refs[:3]
    lc = kc_ref.shape[0]

    @pl.when(pl.program_id(2) == 0)
    def _():
        kts[:, 0:lc] = kc_ref[...].T
        vs[0:lc, :] = vc_ref[...]
        if has_lat:
            kts[:, lc:] = refs[3][...].T
            vs[lc:, :] = refs[4][...]

    q = q_ref[...]
    tq = q.shape[0]
    n_sub = 4 if tq % 64 == 0 else 1
    rsub = tq // n_sub
    outs = []
    for s in range(n_soft):
        sl = slice(s * dh, (s + 1) * dh)
        scs = [_dot(q[u * rsub:(u + 1) * rsub, sl], kts[sl, :]) for u in range(n_sub)]
        ps, ls = [], []
        for sc in scs:
            m = jnp.max(sc, axis=-1, keepdims=True)
            p = jnp.exp(sc - m)
            ls.append(jnp.sum(p, axis=-1, keepdims=True))
            ps.append(p.astype(BF16))
        os_ = [_dot(p, vs[...]) / l for p, l in zip(ps, ls)]
        outs.append(jnp.concatenate(os_, axis=0))
    if diff:
        lp = lam_ref[...]
        lam = (jnp.exp(jnp.sum(lp[0:1] * lp[1:2], axis=-1, keepdims=True))
               - jnp.exp(jnp.sum(lp[2:3] * lp[3:4], axis=-1, keepdims=True)) + lam_init)
        o = outs[0] - lam * outs[1]
        o = _rms(o) * sg_ref[...] * (1.0 - lam_init)
    else:
        o = outs[0]
    o_ref[...] = o.astype(o_ref.dtype)


def _attention(g, q_arr, k_arr, v_arr, *, q_col0, k_col0, heads, qk_w, dv, n_soft, ctx_queries,
               lam=None, subln_g=None, lam_init=0.0):
    diff = lam is not None
    ctx0 = g.rl // g.lc
    n_keys = g.lc if ctx_queries else g.lc + g.t
    if ctx_queries:
        tq = g.lc
        nq = 1
        q_row = lambda b, qi: ctx0 + b
        out_rows = g.rc
        o_row = lambda b, qi: b
    else:
        tq = min(ATT_TQ, g.t)
        nq = g.t // tq
        q_row = lambda b, qi: b * nq + qi
        out_rows = g.rl
        o_row = q_row
    in_specs = []
    args = []
    if diff:
        in_specs += [pl.BlockSpec(lam.shape, lambda b, h, qi: (0, 0)),
                     pl.BlockSpec((1, dv), lambda b, h, qi: (0, 0))]
        args += [lam, subln_g]
    in_specs += [
        pl.BlockSpec((tq, qk_w), lambda b, h, qi: (q_row(b, qi), q_col0 + h)),
        pl.BlockSpec((g.lc, qk_w), lambda b, h, qi: (ctx0 + b, k_col0 + h)),
        pl.BlockSpec((g.lc, dv), lambda b, h, qi: (ctx0 + b, h)),
    ]
    args += [q_arr, k_arr, v_arr]
    if not ctx_queries:
        in_specs += [
            pl.BlockSpec((g.t, qk_w), lambda b, h, qi: (b, k_col0 + h)),
            pl.BlockSpec((g.t, dv), lambda b, h, qi: (b, h)),
        ]
        args += [k_arr, v_arr]
    return pl.pallas_call(
        functools.partial(_attn_kernel, n_soft=n_soft, dh=qk_w // n_soft, has_lat=not ctx_queries,
                          diff=diff, lam_init=lam_init),
        grid=(g.b, heads, nq),
        in_specs=in_specs,
        out_specs=pl.BlockSpec((tq, dv), lambda b, h, qi: (o_row(b, qi), h)),
        out_shape=jax.ShapeDtypeStruct((out_rows, heads * dv), BF16),
        scratch_shapes=[pltpu.VMEM((qk_w, n_keys), BF16), pltpu.VMEM((n_keys, dv), BF16)],
        compiler_params=_cp("parallel", "parallel", "arbitrary"),
        name="attn_diff" if diff else "attn_mla",
    )(*args)


def _branch_kernel(*refs, n_src, n_lat_tiles):
    y_refs = refs[:N_BRANCH * n_src]
    wb_ref, g0_ref, g1_ref, g2_ref, o_ref, wb_s = refs[N_BRANCH * n_src:]
    gates = (g0_ref, g1_ref, g2_ref)

    @pl.when(pl.program_id(1) == 0)
    def _():
        wb_s[...] = wb_ref[...].astype(BF16)

    def body(which):
        def run(_):
            acc = None
            for n in range(N_BRANCH):
                term = gates[n][...].astype(F32) * _dot(y_refs[n * n_src + which][...], wb_s[n])
                acc = term if acc is None else acc + term
            o_ref[...] = acc.astype(o_ref.dtype)
        return run

    i = pl.program_id(1)
    if n_src == 1:
        body(0)(None)
    else:
        pl.when(i < n_lat_tiles)(lambda: body(0)(None))
        pl.when(i >= n_lat_tiles)(lambda: body(1)(None))


def _branch_merge(g, ys, w_branch, l, gt, n_tiles):
    tn = 512
    nj = g.d // tn
    y_specs, y_args = [], []
    for y in ys:
        sp, ar = _row_sources(g, y, BRANCH_W, lambda j: 0, 2)
        y_specs += sp
        y_args += ar
    n_src = len(y_args) // N_BRANCH
    return pl.pallas_call(
        functools.partial(_branch_kernel, n_src=n_src, n_lat_tiles=g.n_lat_tiles),
        grid=(nj, n_tiles),
        in_specs=[
            *y_specs,
            pl.BlockSpec((None, N_BRANCH, BRANCH_W, tn), lambda j, i: (l, 0, 0, j)),
            pl.BlockSpec((g.tm, tn), lambda j, i: (i, j)),
            pl.BlockSpec((g.tm, tn), lambda j, i: (i, nj + j)),
            pl.BlockSpec((g.tm, tn), lambda j, i: (i, 2 * nj + j)),
        ],
        out_specs=pl.BlockSpec((g.tm, tn), lambda j, i: (i, j)),
        out_shape=jax.ShapeDtypeStruct((n_tiles * g.tm, g.d), BF16),
        scratch_shapes=[pltpu.VMEM((N_BRANCH, BRANCH_W, tn), BF16)],
        compiler_params=_cp("arbitrary", "arbitrary"),
        name="branch_merge",
    )(*y_args, w_branch, gt, gt, gt)


def _outproj_kernel(*refs, n_lat_tiles):
    z_ref, w_ref, *x_refs, g_ref, o_ref, w_s = refs

    @pl.when(pl.program_id(1) == 0)
    def _():
        w_s[...] = w_ref[...].astype(BF16)

    def body(x_ref):
        o_ref[...] = x_ref[...] + g_ref[...] * _dot(z_ref[...], w_s[...])

    _on_rows(pl.program_id(1), n_lat_tiles, x_refs, body)


def _outproj(g, z, w_out, l, x_src, mods3, which_gate, n_tiles):
    tn = 512
    per = g.d // tn
    x_specs, x_args = _row_sources(g, x_src, tn, lambda j: j, 2)
    return pl.pallas_call(
        functools.partial(_outproj_kernel, n_lat_tiles=g.n_lat_tiles),
        grid=(per, n_tiles),
        in_specs=[
            pl.BlockSpec((g.tm, g.d), lambda j, i: (i, 0)),
            pl.BlockSpec((None, g.d, tn), lambda j, i: (l, 0, j)),
            *x_specs,
            pl.BlockSpec((None, 1, tn), lambda j, i: (g.mod_row(i), 0, which_gate * per + j)),
        ],
        out_specs=pl.BlockSpec((g.tm, tn), lambda j, i: (i, j)),
        out_shape=jax.ShapeDtypeStruct((n_tiles * g.tm, g.d), F32),
        scratch_shapes=[pltpu.VMEM((g.d, tn), BF16)],
        compiler_params=_cp("arbitrary", "arbitrary"),
        name="outproj",
    )(z, w_out, *x_args, mods3)


def _moe_prep_kernel(x_ref, sh_ref, sc_ref, rwt_ref, rb_ref, tri_ref, xr_ref, rank_ref, grp_ref, cnt_ref,
                     carry, *, mod_row):
    i = pl.program_id(0)

    @pl.when(i == 0)
    def _():
        carry[...] = jnp.zeros_like(carry)

    x = x_ref[...]
    h = _rms(x) * (1.0 + sc_ref[...]) + sh_ref[...]
    logits = _dot_nt(rwt_ref[...].astype(BF16), h.astype(BF16))
    scores = _sigmoid(logits)
    selm = scores + rb_ref[...]
    sc_rows = [scores[e:e + 1, :] for e in range(N_EXPERTS)]
    sel = [selm[e:e + 1, :] for e in range(N_EXPERTS)]
    gscore = []
    for gi in range(N_GROUPS):
        v = sel[gi * GROUP_SIZE:(gi + 1) * GROUP_SIZE]
        pair = [v[a] + v[b] for a in range(GROUP_SIZE) for b in range(a + 1, GROUP_SIZE)]
        gscore.append(functools.reduce(jnp.maximum, pair))
    gmax = functools.reduce(jnp.maximum, gscore)
    taken = jnp.zeros_like(gmax, dtype=jnp.bool_)
    in_group = []
    for gi in range(N_GROUPS):
        hit = jnp.logical_and(gscore[gi] == gmax, jnp.logical_not(taken))
        taken = jnp.logical_or(taken, hit)
        in_group.append(hit)
    neg = -jnp.inf
    masked = [jnp.where(in_group[e // GROUP_SIZE], sel[e], neg) for e in range(N_EXPERTS)]

    def pick(vals):
        mx = functools.reduce(jnp.maximum, vals)
        seen = jnp.zeros_like(mx, dtype=jnp.bool_)
        hot = []
        for v in vals:
            hit = jnp.logical_and(v == mx, jnp.logical_not(seen))
            seen = jnp.logical_or(seen, hit)
            hot.append(hit)
        return hot

    hot1 = pick(masked)
    hot2 = pick([jnp.where(hot1[e], neg, masked[e]) for e in range(N_EXPERTS)])
    zero = jnp.zeros_like(gmax)
    s1 = functools.reduce(lambda a, b: a + b, [jnp.where(hot1[e], sc_rows[e], zero) for e in range(N_EXPERTS)])
    s2 = functools.reduce(lambda a, b: a + b, [jnp.where(hot2[e], sc_rows[e], zero) for e in range(N_EXPERTS)])
    tot = s1 + s2
    w1 = s1 / tot
    w2 = s2 / tot
    comb = [jnp.where(hot1[e], w1, zero) + jnp.where(hot2[e], w2, zero) for e in range(N_EXPERTS)]

    tm = x.shape[0]
    chosen = [jnp.logical_or(hot1[e], hot2[e]) for e in range(N_EXPERTS)]
    cls_hit = [jnp.logical_and(chosen[gi * GROUP_SIZE + a], chosen[gi * GROUP_SIZE + b])
               for gi in range(N_GROUPS) for a, b in EXPERT_PAIRS]
    pad_rows = [jnp.zeros((CLS_ROWS - N_CLASSES, tm), F32)] if CLS_ROWS > N_CLASSES else []
    gmat = jnp.concatenate([jnp.where(hit, 1.0, zero) for hit in cls_hit] + pad_rows, axis=0)
    incl = _dot(gmat.astype(BF16), tri_ref[...])
    excl = incl - gmat + carry[:, 0:1]
    rank = functools.reduce(lambda a, b: a + b,
                            [jnp.where(cls_hit[c], excl[c:c + 1, :], zero) for c in range(N_CLASSES)])
    grp = functools.reduce(lambda a, b: a + b,
                           [jnp.where(cls_hit[c], float(c), zero) for c in range(N_CLASSES)])
    rank_ref[...] = rank.astype(jnp.int32)
    grp_ref[...] = grp.astype(jnp.int32)
    carry[...] = carry[...] + jnp.sum(gmat, axis=1, keepdims=True)
    cnt_ref[...] = carry[...]

    side = jnp.concatenate(comb + [jnp.zeros((LANES - N_EXPERTS, tm), F32)], axis=0).T
    lane = lax.broadcasted_iota(jnp.int32, (1, LANES), 1)
    side = jnp.where(lane == N_EXPERTS + mod_row(i), 1.0, side)
    d = x.shape[1]
    xr_ref[:, 0:d] = x
    xr_ref[:, d:d + LANES] = side


def _moe_prep(g, x_mid, mods3, which_shift, which_scale, rwt, rb, tri, n_tiles):
    n_rows = n_tiles * g.tm
    return pl.pallas_call(
        functools.partial(_moe_prep_kernel, mod_row=g.mod_row),
        grid=(n_tiles,),
        in_specs=[
            pl.BlockSpec((g.tm, g.d), lambda i: (i, 0)),
            _mod_spec(g, which_shift),
            _mod_spec(g, which_scale),
            pl.BlockSpec((N_EXPERTS, g.d), lambda i: (0, 0)),
            pl.BlockSpec((N_EXPERTS, 1), lambda i: (0, 0)),
            pl.BlockSpec((g.tm, g.tm), lambda i: (0, 0)),
        ],
        out_specs=[
            pl.BlockSpec((g.tm, g.d + LANES), lambda i: (i, 0)),
            pl.BlockSpec((1, g.tm), lambda i: (0, i)),
            pl.BlockSpec((1, g.tm), lambda i: (0, i)),
            pl.BlockSpec((CLS_ROWS, LANES), lambda i: (0, 0)),
        ],
        out_shape=[
            jax.ShapeDtypeStruct((n_rows, g.d + LANES), F32),
            jax.ShapeDtypeStruct((1, n_rows), jnp.int32),
            jax.ShapeDtypeStruct((1, n_rows), jnp.int32),
            jax.ShapeDtypeStruct((CLS_ROWS, LANES), F32),
        ],
        scratch_shapes=[pltpu.VMEM((CLS_ROWS, LANES), F32)],
        compiler_params=_cp("arbitrary"),
        name="moe_prep",
    )(x_mid, mods3, mods3, rwt, rb, tri)


def _moe_group_kernel(tg_ref, ea_ref, eb_ref, nv_ref, nu_ref, pos_ref, xr_hbm, mods_ref, w1_hbm, w3_hbm, w2_hbm, out_hbm,
                      gbuf, obuf, w1s, w3s, w2s, stg_a, stg_b, src_ref, gsem, ssem, wsem, *, n_mod_rows, e_base):
    k = pl.program_id(0)
    n_used = nu_ref[0]
    slot = k % 2
    d = obuf.shape[2]
    tme = obuf.shape[1]

    def gather_start(kk, sl):
        base = kk * tme

        first = src_ref[base]

        def copy_row(r, tok):
            pltpu.make_async_copy(xr_hbm.at[pl.ds(tok, 1), :], gbuf.at[sl, pl.ds(r, 1), :], gsem.at[sl]).start()

        def issue(r, c):
            copy_row(r, src_ref[base + r])
            return c

        def issue_pad(r, c):
            copy_row(r, first)
            return c

        rows_of(kk, issue)
        lax.fori_loop(nv_ref[kk], tme, issue_pad, 0)

    def build_sorted_index():
        def put(t, c):
            src_ref[pos_ref[t]] = t
            return c

        lax.fori_loop(0, pos_ref.shape[0], put, 0, unroll=8)

    def gather_wait(sl):
        def w(r, c):
            pltpu.make_async_copy(xr_hbm.at[pl.ds(0, 1), :], gbuf.at[sl, pl.ds(0, 1), :], gsem.at[sl]).wait()
            return c

        lax.fori_loop(0, tme, w, 0, unroll=8)

    def scatter_start(kk, sl):
        base = kk * tme

        def issue(r, c):
            tok = src_ref[base + r]
            pltpu.make_async_copy(obuf.at[sl, pl.ds(r, 1), :], out_hbm.at[pl.ds(tok, 1), :], ssem.at[sl]).start()
            return c

        rows_of(kk, issue)

    def scatter_wait(kk, sl):
        def w(r, c):
            pltpu.make_async_copy(obuf.at[sl, pl.ds(0, 1), :], out_hbm.at[pl.ds(0, 1), :], ssem.at[sl]).wait()
            return c

        rows_of(kk, w)

    def rows_of(kk, fn):
        n_rows = nv_ref[kk]

        @pl.when(n_rows == tme)
        def _():
            lax.fori_loop(0, tme, fn, 0, unroll=8)

        @pl.when(n_rows != tme)
        def _():
            lax.fori_loop(0, n_rows, fn, 0)

    def load_weights(grp):
        rows_a = stg_a.shape[1]
        rows_b = stg_b.shape[1]
        chunks = []
        n_a = n_b = 0
        for e in range(GROUP_SIZE):
            ex = e_base + grp * GROUP_SIZE + e
            for src, dst in ((w1_hbm, w1s), (w3_hbm, w3s)):
                for c0 in range(0, src.shape[1], rows_a):
                    sl = n_a % 2
                    n_a += 1
                    cp = pltpu.make_async_copy(src.at[ex, pl.ds(c0, rows_a), :], stg_a.at[sl], wsem.at[sl])
                    chunks.append((cp, stg_a.at[sl], dst, e, c0, rows_a))
            for c0 in range(0, w2_hbm.shape[1], rows_b):
                sl = n_b % 2
                n_b += 1
                cp = pltpu.make_async_copy(w2_hbm.at[ex, pl.ds(c0, rows_b), :], stg_b.at[sl], wsem.at[2 + sl])
                chunks.append((cp, stg_b.at[sl], w2s, e, c0, rows_b))
        chunks[0][0].start()
        for idx, (cp, stg, dst, e, c0, rows) in enumerate(chunks):
            if idx + 1 < len(chunks):
                chunks[idx + 1][0].start()
            cp.wait()
            dst[e, c0:c0 + rows, :] = stg[...].astype(BF16)

    @pl.when(k < n_used)
    def _():
        grp = tg_ref[k]

        @pl.when(k == 0)
        def _():
            build_sorted_index()
            gather_start(0, 0)

        @pl.when(k + 1 < n_used)
        def _():
            gather_start(k + 1, 1 - slot)

        @pl.when(jnp.logical_or(k == 0, grp != tg_ref[jnp.maximum(k - 1, 0)]))
        def _():
            load_weights(grp)

        gather_wait(slot)

        @pl.when(k >= 2)
        def _():
            scatter_wait(k - 2, slot)

        lane = lax.broadcasted_iota(jnp.int32, (1, LANES), 1)
        rows = tme // MOE_ROW_GROUPS
        for u in range(MOE_ROW_GROUPS):
            rs = slice(u * rows, (u + 1) * rows)
            x = gbuf[slot, rs, 0:d]
            side = gbuf[slot, rs, d:d + LANES]

            def cond_vec(which):
                out = jnp.broadcast_to(mods_ref[0:1, which * d:(which + 1) * d], (rows, d))
                for r in range(1, n_mod_rows):
                    hot = side[:, N_EXPERTS + r:N_EXPERTS + r + 1] > 0.5
                    out = jnp.where(hot, mods_ref[r:r + 1, which * d:(which + 1) * d], out)
                return out

            h = (_rms(x) * (1.0 + cond_vec(1)) + cond_vec(0)).astype(BF16)
            y = jnp.zeros((rows, d), F32)
            for e in (ea_ref[k], eb_ref[k]):
                a = _dot(h, w1s[e])
                b = _dot(h, w3s[e])
                cw = jnp.sum(jnp.where(lane == grp * GROUP_SIZE + e, side, 0.0), axis=1, keepdims=True)
                hid = (a * _sigmoid(a) * b * cw).astype(BF16)
                y = y + _dot(hid, w2s[e])
            obuf[slot, rs, :] = x + cond_vec(2) * y

        scatter_start(k, slot)

        @pl.when(k == n_used - 1)
        def _():
            @pl.when(k >= 1)
            def _():
                scatter_wait(k - 1, 1 - slot)

            scatter_wait(k, slot)


def _moe_group(g, xr, mods_tail, w1, w3, w2, meta, n_rows, n_mod_rows, e_base):
    kt = meta[0].shape[0]
    tme = MOE_TM
    any_spec = pl.BlockSpec(memory_space=pl.ANY)
    grid_spec = pltpu.PrefetchScalarGridSpec(
        num_scalar_prefetch=len(meta),
        grid=(kt,),
        in_specs=[any_spec, pl.BlockSpec(mods_tail.shape, lambda k, *_: (0, 0)), any_spec, any_spec, any_spec],
        out_specs=any_spec,
        scratch_shapes=[
            pltpu.VMEM((2, tme, g.d + LANES), F32),
            pltpu.VMEM((2, tme, g.d), F32),
            pltpu.VMEM((GROUP_SIZE, g.d, D_EXPERT), BF16),
            pltpu.VMEM((GROUP_SIZE, g.d, D_EXPERT), BF16),
            pltpu.VMEM((GROUP_SIZE, D_EXPERT, g.d), BF16),
            pltpu.VMEM((2, g.d // 2, D_EXPERT), F32),
            pltpu.VMEM((2, D_EXPERT // 2, g.d), F32),
            pltpu.SMEM((kt * tme,), jnp.int32),
            pltpu.SemaphoreType.DMA((2,)),
            pltpu.SemaphoreType.DMA((2,)),
            pltpu.SemaphoreType.DMA((4,)),
        ],
    )
    return pl.pallas_call(
        functools.partial(_moe_group_kernel, n_mod_rows=n_mod_rows, e_base=e_base),
        grid_spec=grid_spec,
        out_shape=jax.ShapeDtypeStruct((n_rows, g.d), F32),
        compiler_params=_cp("arbitrary"),
        name="moe_group",
    )(*meta, xr, mods_tail, w1, w3, w2)


def _moe_meta(rank, grp, cnt, n_rows):
    tme = MOE_TM
    kt = n_rows // tme + N_CLASSES
    counts = cnt[:N_CLASSES, 0].astype(jnp.int32)
    ntile = (counts + tme - 1) // tme
    tile_end = jnp.cumsum(ntile)
    tile_off = tile_end - ntile
    csel = [grp[0] == c for c in range(N_CLASSES)]
    row_off = functools.reduce(lambda a, b: a + b,
                               [jnp.where(csel[c], tile_off[c] * tme, 0) for c in range(N_CLASSES)])
    pos = row_off + rank[0]
    ks = jnp.arange(kt, dtype=jnp.int32)
    tile_class = jnp.minimum(functools.reduce(lambda a, b: a + b,
                                              [(ks >= tile_end[c]).astype(jnp.int32) for c in range(N_CLASSES)]),
                             N_CLASSES - 1)
    n_used = tile_end[-1:]
    cnt_k = functools.reduce(lambda a, b: a + b,
                             [jnp.where(tile_class == c, counts[c] - (ks - tile_off[c]) * tme, 0)
                              for c in range(N_CLASSES)])
    n_valid = jnp.where(ks < n_used[0], jnp.clip(cnt_k, 0, tme), 0)
    n_pairs = len(EXPERT_PAIRS)
    pair = tile_class % n_pairs
    tile_group = tile_class // n_pairs
    tile_ea = functools.reduce(lambda a, b: a + b,
                               [jnp.where(pair == p, EXPERT_PAIRS[p][0], 0) for p in range(n_pairs)])
    tile_eb = functools.reduce(lambda a, b: a + b,
                               [jnp.where(pair == p, EXPERT_PAIRS[p][1], 0) for p in range(n_pairs)])
    return tile_group, tile_ea.astype(jnp.int32), tile_eb.astype(jnp.int32), n_valid, n_used, pos


def _rope_tables(t, rot_dim, tile_rows, reps):
    f32 = np.float32
    rows = t // GRID_W
    r = np.repeat(np.arange(rows, dtype=f32), GRID_W)
    col = np.tile(np.arange(GRID_W, dtype=f32), rows)
    n_freq = rot_dim // 4
    inv = np.power(f32(ROPE_BASE), -np.arange(n_freq, dtype=f32) / f32(n_freq)).astype(f32)
    ang = np.concatenate([r[:, None] * inv, col[:, None] * inv], axis=-1).astype(f32)
    cos = np.tile(np.concatenate([np.cos(ang), np.cos(ang)], axis=-1), (1, reps))
    sin = np.tile(np.concatenate([-np.sin(ang), np.sin(ang)], axis=-1), (1, reps))
    cos = np.concatenate([cos, np.ones((tile_rows, LANES), f32)], axis=0).astype(f32)
    sin = np.concatenate([sin, np.zeros((tile_rows, LANES), f32)], axis=0).astype(f32)
    return jnp.asarray(cos), jnp.asarray(sin)


def _uq_perm():
    nope = [h * MLA_QK + i for h in range(MLA_HEADS) for i in range(MLA_NOPE)]
    rope = [h * MLA_QK + MLA_NOPE + i for h in range(MLA_HEADS) for i in range(MLA_ROPE)]
    return jnp.asarray(nope + rope)


def kernel(x, c, ctx, c_ctx, w_ada, b_ada, w_in, b_in, m_conv_w, m_conv_b, m_norm_g, da_q_norm_g, da_k_norm_g,
           da_lambda, da_subln_g, mla_cq_norm_g, mla_ckv_norm_g, mla_w_uq, mla_w_ukv, mla_q_norm_g, mla_k_norm_g,
           w_branch, w_out, moe_w1, moe_w3, moe_w2, router_w, router_bias):
    bsz, t, d = x.shape
    lc = ctx.shape[1]
    depth = w_ada.shape[0]
    g = _Geom(bsz, t, lc, d, min(1024, bsz * lc, t))
    assert bsz + 1 <= MOD_ROWS

    cvec = jnp.concatenate([c, c_ctx[None, :], jnp.zeros((MOD_ROWS - bsz - 1, d), F32)], axis=0)
    mods = _mod_vectors(cvec, w_ada, b_ada)

    cos_d, sin_d = _rope_tables(t, DA_DH, g.tm, 1)
    cos_a, sin_a = _rope_tables(t, MLA_ROPE, g.tm, 2)
    rwt = router_w.T
    rb = router_bias.reshape(N_EXPERTS, 1)
    uq_perm = _uq_perm()
    x_all = (x.reshape(g.rl, d), ctx.reshape(g.rc, d))
    pos_spec = pl.BlockSpec((g.tm, LANES), lambda j, i: (g.pos_block(i), 0))
    b_in3 = b_in.reshape(depth, 1, -1)
    w_in_t = jnp.swapaxes(w_in, 1, 2)
    qk_scale = jnp.concatenate([jnp.full((1, BRANCH_W), M_DH ** -0.5, F32), jnp.ones((1, BRANCH_W), F32)], axis=1)
    tri = (jnp.arange(g.tm)[:, None] <= jnp.arange(g.tm)[None, :]).astype(BF16)
    w1r = moe_w1.reshape(depth * N_EXPERTS, d, D_EXPERT)
    w3r = moe_w3.reshape(depth * N_EXPERTS, d, D_EXPERT)
    w2r = moe_w2.reshape(depth * N_EXPERTS, D_EXPERT, d)

    for l in range(depth):
        last = l == depth - 1
        n_tiles = g.n_lat_tiles if last else g.n_tiles
        n_rows = n_tiles * g.tm
        lam_init = 0.8 - 0.6 * math.exp(-0.3 * l)
        mods3 = mods[l].reshape(MOD_ROWS, 1, 6 * d)

        h1 = _prenorm(g, x_all, mods3, 0, 1, g.n_tiles)

        row = lambda v: v.reshape(1, -1)
        proj = functools.partial(_proj, g, h=h1, w_in_t=w_in_t, b_in3=b_in3, l=l)
        pm = proj(_epi_raw, col0=OFF_MQK, width=OFF_MG - OFF_MQK, out_dtype=BF16, tn=1024, name="proj_mlstm")
        da_gain = jnp.concatenate([jnp.tile(da_q_norm_g[l] * DA_DH ** -0.5, 2 * DA_HEADS),
                                   jnp.tile(da_k_norm_g[l], 2 * DA_HEADS)])
        dqk = proj(_epi_normrope, col0=OFF_DQ, width=2 * DA_QK_W, out_dtype=BF16,
                   extra=(row(da_gain), cos_d, sin_d),
                   extra_specs=(pl.BlockSpec((1, 512), lambda j, i: (0, j)), pos_spec, pos_spec),
                   name="proj_dqk")
        dv = proj(_epi_raw, col0=OFF_DV, width=BRANCH_W, out_dtype=BF16, name="proj_dv")
        pc = proj(_epi_raw, col0=OFF_CQ, width=OFF_G - OFF_CQ, out_dtype=F32, patch=(OFF_MG, PC_GATE_COL),
                  name="proj_small")
        gt = proj(_epi_sigmoid, col0=OFF_G, width=N_BRANCH * d, out_dtype=BF16, name="proj_gates")

        qkc = _mconv(g, pm, m_conv_w[l], row(m_conv_b[l]), qk_scale)
        ym_l, ym_c = _mscan(g, qkc, pm, pc, row(m_norm_g[l]))

        qg, kg = mla_q_norm_g[l], mla_k_norm_g[l]
        a_scale = MLA_QK ** -0.5
        gains = (row(mla_cq_norm_g[l]), row(mla_ckv_norm_g[l]),
                 row(qg[:MLA_NOPE] * a_scale), row(jnp.tile(qg[MLA_NOPE:], 2) * a_scale),
                 row(kg[:MLA_NOPE]), row(jnp.concatenate([kg[MLA_NOPE:], jnp.zeros((LANES - MLA_ROPE,), F32)])))
        aq, ak, av = _mla_prep(g, pc, mla_w_uq[l][:, uq_perm], mla_w_ukv[l], gains, cos_a, sin_a)

        lam_p = da_lambda[l]
        sub_g = row(da_subln_g[l])
        da_kw = dict(q_col0=0, k_col0=DA_HEADS, heads=DA_HEADS, qk_w=2 * DA_DH, dv=DA_DV, n_soft=2,
                     lam=lam_p, subln_g=sub_g, lam_init=lam_init)
        mla_kw = dict(q_col0=0, k_col0=0, heads=MLA_HEADS, qk_w=MLA_HW, dv=MLA_DV, n_soft=1)
        yd_l = _attention(g, dqk, dqk, dv, ctx_queries=False, **da_kw)
        ya_l = _attention(g, aq, ak, av, ctx_queries=False, **mla_kw)
        if last:
            ys = (ym_l, yd_l, ya_l)
        else:
            yd_c = _attention(g, dqk, dqk, dv, ctx_queries=True, **da_kw)
            ya_c = _attention(g, aq, ak, av, ctx_queries=True, **mla_kw)
            ys = ((ym_l, ym_c), (yd_l, yd_c), (ya_l, ya_c))

        z = _branch_merge(g, ys, w_branch, l, gt, n_tiles)
        x_mid = _outproj(g, z, w_out, l, x_all, mods3, 2, n_tiles)

        xr, rank, grp, cnt = _moe_prep(g, x_mid, mods3, 3, 4, rwt, rb, tri, n_tiles)
        meta = _moe_meta(rank, grp, cnt, n_rows)
        n_cond = bsz if last else bsz + 1
        x_all = _moe_group(g, xr, mods[l][:, 3 * d:], w1r, w3r, w2r, meta, n_rows, n_cond, l * N_EXPERTS)

    return x_all.reshape(bsz, t, d)
```

```python
import functools
import math

import jax
import jax.numpy as jnp
import numpy as np
from jax import lax
from jax.experimental import pallas as pl
from jax.experimental.pallas import tpu as pltpu

F32 = jnp.float32
BF16 = jnp.bfloat16

GRID_W = 64
ROPE_BASE = 10000.0
EPS = 1e-6
N_BRANCH = 3
BRANCH_W = 1024
M_HEADS = 4
M_DH = BRANCH_W // M_HEADS
M_CHUNK = 256
M_CONV = 5
DA_HEADS = 4
DA_DV = BRANCH_W // DA_HEADS
DA_DH = DA_DV // 2
DA_QK_W = DA_HEADS * 2 * DA_DH
MLA_HEADS = 8
MLA_Q_RANK = 512
MLA_KV_RANK = 256
MLA_NOPE = 128
MLA_ROPE = 64
MLA_DV = BRANCH_W // MLA_HEADS
MLA_QK = MLA_NOPE + MLA_ROPE
N_EXPERTS = 16
N_GROUPS = 4
GROUP_SIZE = N_EXPERTS // N_GROUPS
D_EXPERT = 512

OFF_MQK = 0
OFF_MV = OFF_MQK + 2 * BRANCH_W
OFF_MO = OFF_MV + BRANCH_W
OFF_MG = OFF_MO + BRANCH_W
OFF_DQ = OFF_MG + 4 * M_HEADS
OFF_DK = OFF_DQ + DA_QK_W
OFF_DV = OFF_DK + DA_QK_W
OFF_CQ = OFF_DV + BRANCH_W
OFF_CKV = OFF_CQ + MLA_Q_RANK
OFF_KR = OFF_CKV + MLA_KV_RANK
OFF_G = OFF_KR + MLA_ROPE

LANES = 128
MOD_ROWS = 8
VMEM_LIMIT = 56 * 1024 * 1024
MLA_HW = 256
MOE_TM = 256
ATT_TQ = 2048
PC_GATE_COL = 896
PROJ_ROW_GROUPS = 4
MOE_ROW_GROUPS = 1


def _cp(*sem):
    return pltpu.CompilerParams(dimension_semantics=sem, vmem_limit_bytes=VMEM_LIMIT)


def _rms(x):
    return x * lax.rsqrt(jnp.mean(x * x, axis=-1, keepdims=True) + EPS)


def _sigmoid(x):
    return 0.5 * jnp.tanh(0.5 * x) + 0.5


def _dot(a, b):
    return jnp.dot(a, b, preferred_element_type=F32)


def _dot_nt(a, b):
    return lax.dot_general(a, b, (((1,), (1,)), ((), ())), preferred_element_type=F32)


def _mod_kernel(c_ref, w_ref, b_ref, o_ref):
    c = c_ref[...]
    s = (c * _sigmoid(c)).astype(BF16)
    o_ref[...] = _dot(s, w_ref[...].astype(BF16)) + b_ref[...]


def _mod_vectors(cvec, w_ada, b_ada):
    depth, d, n = w_ada.shape
    tn = 1024
    return pl.pallas_call(
        _mod_kernel,
        grid=(depth, n // tn),
        in_specs=[
            pl.BlockSpec((MOD_ROWS, d), lambda l, j: (0, 0)),
            pl.BlockSpec((None, d, tn), lambda l, j: (l, 0, j)),
            pl.BlockSpec((None, 1, tn), lambda l, j: (l, 0, j)),
        ],
        out_specs=pl.BlockSpec((None, MOD_ROWS, tn), lambda l, j: (l, 0, j)),
        out_shape=jax.ShapeDtypeStruct((depth, MOD_ROWS, n), F32),
        compiler_params=_cp("parallel", "parallel"),
        name="adaln_mod",
    )(cvec, w_ada, b_ada.reshape(depth, 1, n))


class _Geom:
    def __init__(self, b, t, lc, d, tm):
        assert t % tm == 0 and (b * lc) % tm == 0 and t % lc == 0 and t % GRID_W == 0
        self.b, self.t, self.lc, self.d, self.tm = b, t, lc, d, tm
        self.rl, self.rc = b * t, b * lc
        self.r = self.rl + self.rc
        self.mblk = min(256, lc)
        self.chunk = min(M_CHUNK, self.mblk)
        assert lc % self.mblk == 0 and t % self.mblk == 0 and self.mblk % self.chunk == 0
        self.n_lat_tiles = self.rl // tm
        self.n_tiles = self.r // tm
        self.tiles_per_seq = t // tm

    def mod_row(self, i):
        return jnp.minimum(i // self.tiles_per_seq, self.b)

    def pos_block(self, i):
        return jnp.where(i < self.n_lat_tiles, i % self.tiles_per_seq, self.tiles_per_seq)


def _mod_spec(g, which):
    return pl.BlockSpec((None, 1, g.d), lambda i: (g.mod_row(i), 0, which))


def _row_sources(g, src, width, col_of, n_grid):
    tile = lambda i: i
    if n_grid == 2:
        spec = lambda rows_of: pl.BlockSpec((g.tm, width), lambda j, i: (rows_of(i), col_of(j)))
    else:
        spec = lambda rows_of: pl.BlockSpec((g.tm, width), lambda i: (rows_of(i), 0))
    if not isinstance(src, tuple):
        return [spec(tile)], [src]
    lat, ctx = src
    nl = g.n_lat_tiles
    return ([spec(lambda i: jnp.minimum(i, nl - 1)), spec(lambda i: jnp.maximum(i - nl, 0))], [lat, ctx])


def _on_rows(i, n_lat_tiles, refs, body):
    if len(refs) == 1:
        body(refs[0])
        return

    @pl.when(i < n_lat_tiles)
    def _():
        body(refs[0])

    @pl.when(i >= n_lat_tiles)
    def _():
        body(refs[1])


def _prenorm_kernel(*refs, n_lat_tiles):
    *x_refs, sh_ref, sc_ref, o_ref = refs

    def body(x_ref):
        o_ref[...] = (_rms(x_ref[...]) * (1.0 + sc_ref[...]) + sh_ref[...]).astype(o_ref.dtype)

    _on_rows(pl.program_id(0), n_lat_tiles, x_refs, body)


def _prenorm(g, x_src, mods3, which_shift, which_scale, n_tiles):
    x_specs, x_args = _row_sources(g, x_src, g.d, None, 1)
    return pl.pallas_call(
        functools.partial(_prenorm_kernel, n_lat_tiles=g.n_lat_tiles),
        grid=(n_tiles,),
        in_specs=[*x_specs, _mod_spec(g, which_shift), _mod_spec(g, which_scale)],
        out_specs=pl.BlockSpec((g.tm, g.d), lambda i: (i, 0)),
        out_shape=jax.ShapeDtypeStruct((n_tiles * g.tm, g.d), BF16),
        compiler_params=_cp("parallel"),
        name="prenorm",
    )(*x_args, mods3, mods3)


def _epi_raw(acc, rs, o_ref):
    o_ref[rs, :] = acc.astype(o_ref.dtype)


def _epi_sigmoid(acc, rs, o_ref):
    o_ref[rs, :] = _sigmoid(acc).astype(o_ref.dtype)


def _epi_normrope(acc, rs, gain_ref, cos_ref, sin_ref, o_ref):
    cos = cos_ref[rs, :]
    sin = sin_ref[rs, :]
    for j in range(acc.shape[1] // LANES):
        sl = slice(j * LANES, (j + 1) * LANES)
        x = _rms(acc[:, sl]) * gain_ref[:, sl]
        o_ref[rs, sl] = (x * cos + pltpu.roll(x, LANES // 2, 1) * sin).astype(o_ref.dtype)


def _proj_kernel(*refs, shift, epilogue, patch):
    if shift:
        h_ref, wa_ref, wb_ref, ba_ref, bb_ref, *rest = refs
    else:
        h_ref, wa_ref, ba_ref, *rest = refs
    if patch is not None:
        wp_ref, bp_ref, *rest = rest
    *extra, o_ref, w_s, b_s = rest
    tn = w_s.shape[0]

    @pl.when(pl.program_id(1) == 0)
    def _():
        if shift:
            w_s[0:tn - shift, :] = wa_ref[shift:tn, :].astype(BF16)
            w_s[tn - shift:tn, :] = wb_ref[0:shift, :].astype(BF16)
            bb = jnp.concatenate([ba_ref[...], bb_ref[...]], axis=1)
            bb = jnp.broadcast_to(bb, (b_s.shape[0], 2 * tn))
            b_s[...] = pltpu.roll(bb, 2 * tn - shift, 1)[:, :tn]
        else:
            w_s[...] = wa_ref[...].astype(BF16)
            b_s[...] = jnp.broadcast_to(ba_ref[...], b_s.shape)
        if patch is not None:
            p_tile, p_off = patch

            @pl.when(pl.program_id(0) == p_tile)
            def _():
                w_s[p_off:p_off + LANES, :] = wp_ref[...].astype(BF16)
                b_s[:, p_off:p_off + LANES] = jnp.broadcast_to(bp_ref[...], (b_s.shape[0], LANES))

    rows = h_ref.shape[0] // PROJ_ROW_GROUPS
    for u in range(PROJ_ROW_GROUPS):
        rs = slice(u * rows, (u + 1) * rows)
        acc = _dot_nt(h_ref[rs, :], w_s[...]) + b_s[0:1, :]
        epilogue(acc, rs, *extra, o_ref)


def _proj(g, epilogue, h, w_in_t, b_in3, l, col0, width, out_dtype, *, tn=512, out_tn=None,
          extra=(), extra_specs=(), patch=None, name="proj"):
    ncols, kdim = w_in_t.shape[1], w_in_t.shape[2]
    blk0 = col0 // tn
    shift = col0 - blk0 * tn
    assert shift % 16 == 0
    nj = pl.cdiv(width, tn)
    last_blk = pl.cdiv(ncols, tn) - 1
    out_tn = tn if out_tn is None else out_tn
    w_specs = [pl.BlockSpec((None, tn, kdim), lambda j, i: (l, blk0 + j, 0))]
    b_specs = [pl.BlockSpec((None, 1, tn), lambda j, i: (l, 0, blk0 + j))]
    if shift:
        nxt = lambda j: jnp.minimum(blk0 + j + 1, last_blk)
        w_specs.append(pl.BlockSpec((None, tn, kdim), lambda j, i: (l, nxt(j), 0)))
        b_specs.append(pl.BlockSpec((None, 1, tn), lambda j, i: (l, 0, nxt(j))))
    n_w = len(w_specs)
    p_specs, p_args, k_patch = [], [], None
    if patch is not None:
        src_col, dst_col = patch
        assert src_col % LANES == 0 and dst_col % LANES == 0 and dst_col >= width
        p_specs = [pl.BlockSpec((None, LANES, kdim), lambda j, i: (l, src_col // LANES, 0)),
                   pl.BlockSpec((None, 1, LANES), lambda j, i: (l, 0, src_col // LANES))]
        p_args = [w_in_t, b_in3]
        k_patch = (dst_col // tn, dst_col % tn)
    return pl.pallas_call(
        functools.partial(_proj_kernel, shift=shift, epilogue=epilogue, patch=k_patch),
        grid=(nj, g.n_tiles),
        in_specs=[pl.BlockSpec((g.tm, kdim), lambda j, i: (i, 0)), *w_specs, *b_specs, *p_specs, *extra_specs],
        out_specs=pl.BlockSpec((g.tm, out_tn), lambda j, i: (i, j)),
        out_shape=jax.ShapeDtypeStruct((g.r, nj * out_tn), out_dtype),
        scratch_shapes=[pltpu.VMEM((tn, kdim), BF16), pltpu.VMEM((MOD_ROWS, tn), F32)],
        compiler_params=_cp("arbitrary", "arbitrary"),
        name=name,
    )(h, *([w_in_t] * n_w), *([b_in3] * n_w), *p_args, *extra)


def _mconv_kernel(x_ref, prev_ref, next_ref, w_ref, b_ref, s_ref, sh_ref, o_ref, *, blocks_per_seq, blocks_per_ctx,
                  n_lat_blocks):
    i = pl.program_id(0)
    rows = x_ref.shape[0]
    halo = prev_ref.shape[0]
    edge = MOD_ROWS
    in_lat = i < n_lat_blocks
    pos = jnp.where(in_lat, i % blocks_per_seq, (i - n_lat_blocks) % blocks_per_ctx)
    per_seq = jnp.where(in_lat, blocks_per_seq, blocks_per_ctx)
    x = x_ref[...]
    w = [w_ref[j:j + 1, :] for j in range(M_CONV)]
    acc = x.astype(F32) * w[M_CONV // 2] + b_ref[...]
    taps = [j for j in range(M_CONV) if j != M_CONV // 2]
    for n_, j in enumerate(taps):
        acc = acc + _dot(sh_ref[n_], x) * w[j]
    p = jnp.where(pos != 0, prev_ref[...].astype(F32), 0.0)
    nx = jnp.where(pos != per_seq - 1, next_ref[...].astype(F32), 0.0)
    r8 = lax.broadcasted_iota(jnp.int32, (edge, 1), 0)
    top = (jnp.where(r8 < 2, pltpu.roll(p, 2, 0)[0:edge], 0.0) * w[0]
           + jnp.where(r8 < 1, pltpu.roll(p, 1, 0)[0:edge], 0.0) * w[1])
    bot = (jnp.where(r8 >= edge - 1, pltpu.roll(nx, edge - 1, 0)[0:edge], 0.0) * w[3]
           + jnp.where(r8 >= edge - 2, pltpu.roll(nx, edge - 2, 0)[0:edge], 0.0) * w[4])
    zeros = jnp.zeros((halo - edge, x.shape[1]), F32)

    def finish(a):
        return (a * _sigmoid(a) * s_ref[...]).astype(o_ref.dtype)

    o_ref[0:halo, :] = finish(acc[0:halo] + jnp.concatenate([top, zeros], axis=0))
    o_ref[halo:rows - halo, :] = finish(acc[halo:rows - halo])
    o_ref[rows - halo:rows, :] = finish(acc[rows - halo:rows] + jnp.concatenate([zeros, bot], axis=0))


def _mconv(g, pm, conv_w, conv_b, scale_row):
    rows = g.mblk
    halo = 16
    tn = 2 * BRANCH_W
    per = rows // halo
    nblk = g.r // rows
    last = g.r // halo - 1
    taps = [j for j in range(M_CONV) if j != M_CONV // 2]
    shifts = np.zeros((len(taps), rows, rows), np.float32)
    for n_, j in enumerate(taps):
        for t_ in range(rows):
            if 0 <= t_ + j - M_CONV // 2 < rows:
                shifts[n_, t_, t_ + j - M_CONV // 2] = 1.0
    shifts = jnp.asarray(shifts, BF16)
    return pl.pallas_call(
        functools.partial(_mconv_kernel, blocks_per_seq=g.t // rows, blocks_per_ctx=g.lc // rows,
                          n_lat_blocks=g.rl // rows),
        grid=(nblk, 2 * BRANCH_W // tn),
        in_specs=[
            pl.BlockSpec((rows, tn), lambda i, j: (i, j)),
            pl.BlockSpec((halo, tn), lambda i, j: (jnp.maximum(i * per - 1, 0), j)),
            pl.BlockSpec((halo, tn), lambda i, j: (jnp.minimum((i + 1) * per, last), j)),
            pl.BlockSpec((M_CONV, tn), lambda i, j: (0, j)),
            pl.BlockSpec((1, tn), lambda i, j: (0, j)),
            pl.BlockSpec((1, tn), lambda i, j: (0, j)),
            pl.BlockSpec(shifts.shape, lambda i, j: (0, 0, 0)),
        ],
        out_specs=pl.BlockSpec((rows, tn), lambda i, j: (i, j)),
        out_shape=jax.ShapeDtypeStruct((g.r, 2 * BRANCH_W), BF16),
        compiler_params=_cp("parallel", "parallel"),
        name="mlstm_conv",
    )(pm, pm, pm, conv_w, conv_b, scale_row, shifts)


def _mscan_kernel(qkf_ref, vf_ref, gf_ref, qkb_ref, vb_ref, gb_ref, ol_ref, oc_ref, ng_ref, yl_ref, yc_ref,
                  xsf, bcf, xsb, bcb, dtf, dtb, hf, hb, ct, nv, ms, *, lc, t, mblk, chunk):
    i = pl.program_id(1)
    nblk = pl.num_programs(1)
    L = chunk
    cpb = mblk // L
    nbc = lc // mblk
    nbl = t // mblk
    n = lc + t

    @pl.when(i == 0)
    def _():
        ct[...] = jnp.zeros_like(ct)
        nv[...] = jnp.zeros_like(nv)
        ms[...] = jnp.zeros_like(ms)

    lane = lax.broadcasted_iota(jnp.int32, (1, LANES), 1)
    is_f = jnp.logical_or(jnp.logical_and(lane >= M_HEADS, lane < 2 * M_HEADS),
                          jnp.logical_and(lane >= 3 * M_HEADS, lane < 4 * M_HEADS))
    pos = lax.broadcasted_iota(jnp.int32, (mblk, 1), 0) % L

    def gate_prep(g_ref, x_out, bc_out, forward):
        gx = g_ref[...]
        lsg = jnp.minimum(gx, 0.0) - jnp.log(1.0 + jnp.exp(-jnp.abs(gx)))
        x = jnp.where(is_f, lsg, gx)
        x_out[...] = x
        run = x
        k = 1
        while k < L:
            if forward:
                run = run + jnp.where(pos >= k, pltpu.roll(run, k, 0), 0.0)
            else:
                run = run + jnp.where(pos < L - k, pltpu.roll(run, mblk - k, 0), 0.0)
            k *= 2
        bc_out[...] = run

    gate_prep(gf_ref, xsf, bcf, True)
    gate_prep(gb_ref, xsb, bcb, False)

    pos_f = i * mblk
    pos_b = jnp.where(i < nbc, (nbc - 1 - i) * mblk, lc + (nbl - 1 - (i - nbc)) * mblk)

    def row_table(x_ref, bc_ref, out_ref):
        dm = x_ref[...] - pltpu.roll(bc_ref[...], LANES - M_HEADS, 1)
        for c in range(cpb):
            out_ref[c] = dm[c * L:(c + 1) * L, :].T

    row_table(xsf, bcf, dtf)
    row_table(xsb, bcb, dtb)

    H = M_HEADS
    r_i = lax.broadcasted_iota(jnp.int32, (H * L, L), 0) % L
    c_i = lax.broadcasted_iota(jnp.int32, (H * L, L), 1)
    masks = (c_i <= r_i, c_i >= r_i)

    def stack(parts):
        return jnp.concatenate(parts, axis=0)

    def dir_step(c, d, mvals):
        qk_ref, v_ref, x_ref, bc_ref, dt_ref, h_ref, base = (
            (qkf_ref, vf_ref, xsf, bcf, dtf, hf, pos_f), (qkb_ref, vb_ref, xsb, bcb, dtb, hb, pos_b))[d]
        r0 = pl.multiple_of(c * L, L)
        bg = bc_ref[pl.ds(r0, L), :]
        dt = dt_ref[c]
        qs = [qk_ref[pl.ds(r0, L), j * M_DH:(j + 1) * M_DH] for j in range(H)]
        ks = [qk_ref[pl.ds(r0, L), BRANCH_W + j * M_DH:BRANCH_W + (j + 1) * M_DH] for j in range(H)]
        vs_ = [v_ref[pl.ds(r0, L), j * M_DH:(j + 1) * M_DH] for j in range(H)]
        li0 = 2 * d * H
        lf0 = (2 * d + 1) * H
        a_h = [bg[:, lf0 + j:lf0 + j + 1] for j in range(H)]
        bl_h = [a[L - 1:L, :] if d == 0 else a[0:1, :] for a in a_h]
        a4 = stack(a_h)
        d_rows = [dt[li0 + j:li0 + j + 1, :] for j in range(H)]
        d4 = stack([jnp.broadcast_to(r_, (L, L)) for r_ in d_rows])
        s4 = stack([_dot_nt(qs[j], ks[j]) for j in range(H)])
        logw = jnp.where(masks[d], a4 + d4, -jnp.inf)
        m_loc = jnp.max(logw, axis=1, keepdims=True)
        w4b = (jnp.exp(logw - m_loc) * s4).astype(BF16)
        den_loc = _dot(w4b, jnp.ones((L, LANES), BF16))[:, 0:1]
        num_loc = [_dot(w4b[j * L:(j + 1) * L, :], vs_[j]) for j in range(H)]
        s_rows = [bl_h[j] + d_rows[j] for j in range(H)]
        msrc_h = [jnp.max(s_, axis=1, keepdims=True) for s_ in s_rows]
        g_rows = [jnp.exp(s_rows[j] - msrc_h[j]) for j in range(H)]
        k_t = [ks[j].astype(F32).T for j in range(H)]
        d_c = [_dot((k_t[j] * g_rows[j]).astype(BF16), vs_[j]) for j in range(H)]
        d_n = [_dot(jnp.broadcast_to(g_rows[j], (MOD_ROWS, L)).astype(BF16), ks[j])[0:1, :] for j in range(H)]
        idx = [d * H + j for j in range(H)]
        nrow = [nv[i_] for i_ in idx]
        inter = [_dot(qs[j], ct[idx[j]].astype(BF16)) for j in range(H)]
        qn4 = stack([_dot_nt(qs[j], jnp.broadcast_to(nrow[j], (MOD_ROWS, M_DH)).astype(BF16))[:, 0:1]
                     for j in range(H)])
        m4 = stack([jnp.broadcast_to(mvals[idx[j]], (L, 1)) for j in range(H)])
        bm4 = a4 + m4
        m_t = jnp.maximum(bm4, m_loc)
        e_loc = jnp.exp(m_loc - m_t)
        e_int = jnp.exp(bm4 - m_t)
        den4 = e_loc * den_loc + e_int * qn4
        r4 = 1.0 / jnp.maximum(jnp.abs(den4), jnp.exp(-m_t))
        el = e_loc * r4
        ei = e_int * r4
        rows_out = pl.ds(pl.multiple_of(base + r0, L), L)
        out = list(mvals)
        for j in range(H):
            rs = slice(j * L, (j + 1) * L)
            h_ref[rows_out, j * M_DH:(j + 1) * M_DH] = el[rs] * num_loc[j] + ei[rs] * inter[j]
            m_old = mvals[idx[j]]
            m_new = jnp.maximum(bl_h[j] + m_old, msrc_h[j])
            keep = jnp.exp(bl_h[j] + m_old - m_new)
            scale = jnp.exp(msrc_h[j] - m_new)
            ct[idx[j]] = keep * ct[idx[j]] + scale * d_c[j]
            nv[idx[j]] = keep * nrow[j] + scale * d_n[j]
            out[idx[j]] = m_new
        return tuple(out)

    def body(s, mvals):
        mvals = dir_step(s, 0, mvals)
        return dir_step(cpb - 1 - s, 1, mvals)

    m0 = tuple(ms[q_:q_ + 1, 0:1] for q_ in range(2 * M_HEADS))
    m1 = lax.fori_loop(0, cpb, body, m0, unroll=min(2, cpb))
    for q_ in range(2 * M_HEADS):
        ms[q_:q_ + 1, :] = jnp.broadcast_to(m1[q_], (1, LANES))

    @pl.when(i == nblk - 1)
    def _():
        for blk in range(n // mblk):
            rs = slice(blk * mblk, (blk + 1) * mblk)
            for j in range(M_HEADS):
                hs = slice(j * M_DH, (j + 1) * M_DH)
                hn = _rms(hf[rs, hs] + hb[rs, hs]) * ng_ref[:, hs]
                if blk < nbc:
                    yc_ref[rs, hs] = (hn * _sigmoid(oc_ref[rs, hs].astype(F32))).astype(BF16)
                else:
                    ls = slice(blk * mblk - lc, (blk + 1) * mblk - lc)
                    yl_ref[ls, hs] = (hn * _sigmoid(ol_ref[ls, hs].astype(F32))).astype(BF16)


def _mscan(g, qkc, pm, pg, norm_g):
    mblk = g.mblk
    nbc, nbl = g.lc // mblk, g.t // mblk
    lat_blocks = g.rl // mblk
    n = g.lc + g.t

    def fwd(b, i):
        return jnp.where(i < nbc, lat_blocks + b * nbc + i, b * nbl + (i - nbc))

    def bwd(b, i):
        return jnp.where(i < nbc, lat_blocks + b * nbc + (nbc - 1 - i), b * nbl + (nbl - 1 - (i - nbc)))

    v_col = 2 * BRANCH_W // BRANCH_W
    o_col = 3 * BRANCH_W // BRANCH_W
    ctx0 = g.rl // g.lc

    def side(blk_of):
        return [pl.BlockSpec((mblk, 2 * BRANCH_W), lambda b, i: (blk_of(b, i), 0)),
                pl.BlockSpec((mblk, BRANCH_W), lambda b, i: (blk_of(b, i), v_col)),
                pl.BlockSpec((mblk, LANES), lambda b, i: (blk_of(b, i), PC_GATE_COL // LANES))]

    return pl.pallas_call(
        functools.partial(_mscan_kernel, lc=g.lc, t=g.t, mblk=mblk, chunk=g.chunk),
        grid=(g.b, nbc + nbl),
        in_specs=[
            *side(fwd), *side(bwd),
            pl.BlockSpec((g.t, BRANCH_W), lambda b, i: (b, o_col)),
            pl.BlockSpec((g.lc, BRANCH_W), lambda b, i: (ctx0 + b, o_col)),
            pl.BlockSpec((1, BRANCH_W), lambda b, i: (0, 0)),
        ],
        out_specs=[
            pl.BlockSpec((g.t, BRANCH_W), lambda b, i: (b, 0)),
            pl.BlockSpec((g.lc, BRANCH_W), lambda b, i: (b, 0)),
        ],
        out_shape=[
            jax.ShapeDtypeStruct((g.rl, BRANCH_W), BF16),
            jax.ShapeDtypeStruct((g.rc, BRANCH_W), BF16),
        ],
        scratch_shapes=[
            pltpu.VMEM((mblk, LANES), F32), pltpu.VMEM((mblk, LANES), F32),
            pltpu.VMEM((mblk, LANES), F32), pltpu.VMEM((mblk, LANES), F32),
            pltpu.VMEM((mblk // g.chunk, LANES, g.chunk), F32), pltpu.VMEM((mblk // g.chunk, LANES, g.chunk), F32),
            pltpu.VMEM((n, BRANCH_W), F32), pltpu.VMEM((n, BRANCH_W), F32),
            pltpu.VMEM((2 * M_HEADS, M_DH, M_DH), F32), pltpu.VMEM((2 * M_HEADS, 1, M_DH), F32),
            pltpu.VMEM((2 * M_HEADS, LANES), F32),
        ],
        compiler_params=_cp("parallel", "arbitrary"),
        name="mlstm_scan",
    )(qkc, pm, pg, qkc, pm, pg, pm, pm, norm_g)


def _mla_prep_kernel(cq_ref, ckv_ref, kr_ref, wuq_ref, wukv_ref, cqg_ref, ckvg_ref, qng_ref, qrg_ref,
                     kng_ref, krg_ref, cos_ref, sin_ref, aq_ref, ak_ref, av_ref, wuq_s, wukv_s):
    @pl.when(pl.program_id(0) == 0)
    def _():
        wuq_s[...] = wuq_ref[...].astype(BF16)
        wukv_s[...] = wukv_ref[...].astype(BF16)

    half = LANES // 2
    lane = lax.broadcasted_iota(jnp.int32, (1, LANES), 1)
    lo = lane < half
    first = (lane % half) < (half // 2)

    def rms_half(x):
        x2 = x * x
        s_lo = jnp.sum(jnp.where(lo, x2, 0.0), axis=-1, keepdims=True)
        s_hi = jnp.sum(jnp.where(lo, 0.0, x2), axis=-1, keepdims=True)
        ms = jnp.where(lo, s_lo, s_hi) * (1.0 / half)
        return x * lax.rsqrt(ms + EPS)

    def rows_group(rs):
        cos = cos_ref[rs, :]
        sin = sin_ref[rs, :]

        def rope_half(x):
            partner = jnp.where(first, pltpu.roll(x, LANES - half // 2, 1), pltpu.roll(x, half // 2, 1))
            return x * cos + partner * sin

        cq = (_rms(cq_ref[rs, :]) * cqg_ref[...]).astype(BF16)
        q = _dot(cq, wuq_s[...])
        ckv = (_rms(ckv_ref[rs, :]) * ckvg_ref[...]).astype(BF16)
        kv = _dot(ckv, wukv_s[...])

        krn = rope_half(rms_half(kr_ref[rs, :]) * krg_ref[...]).astype(BF16)

        rope0 = MLA_HEADS * MLA_NOPE
        for hp in range(MLA_HEADS // 2):
            slab = q[:, rope0 + hp * LANES: rope0 + (hp + 1) * LANES]
            r = rope_half(rms_half(slab) * qrg_ref[...])
            parts = (jnp.where(lo, r, 0.0), jnp.where(lo, pltpu.roll(r, half, 1), 0.0))
            for e in range(2):
                h = 2 * hp + e
                nope = _rms(q[:, h * MLA_NOPE:(h + 1) * MLA_NOPE]) * qng_ref[...]
                aq_ref[rs, h * MLA_HW: h * MLA_HW + MLA_NOPE] = nope.astype(BF16)
                aq_ref[rs, h * MLA_HW + MLA_NOPE:(h + 1) * MLA_HW] = parts[e].astype(BF16)
        kvw = MLA_NOPE + MLA_DV
        for h in range(MLA_HEADS):
            kn = _rms(kv[:, h * kvw: h * kvw + MLA_NOPE]) * kng_ref[...]
            ak_ref[rs, h * MLA_HW: h * MLA_HW + MLA_NOPE] = kn.astype(BF16)
            ak_ref[rs, h * MLA_HW + MLA_NOPE:(h + 1) * MLA_HW] = krn
            av_ref[rs, h * MLA_DV:(h + 1) * MLA_DV] = kv[:, h * kvw + MLA_NOPE:(h + 1) * kvw].astype(BF16)

    rows_group(slice(0, cq_ref.shape[0]))


def _mla_prep(g, pc, wuq, wukv, gains, cos_t, sin_t):
    tm = g.tm
    full = lambda shape: pl.BlockSpec(shape, lambda i: (0, 0))
    kr_col = (MLA_Q_RANK + MLA_KV_RANK) // LANES
    return pl.pallas_call(
        _mla_prep_kernel,
        grid=(g.n_tiles,),
        in_specs=[
            pl.BlockSpec((tm, MLA_Q_RANK), lambda i: (i, 0)),
            pl.BlockSpec((tm, MLA_KV_RANK), lambda i: (i, MLA_Q_RANK // MLA_KV_RANK)),
            pl.BlockSpec((tm, LANES), lambda i: (i, kr_col)),
            full(wuq.shape), full(wukv.shape),
            full((1, MLA_Q_RANK)), full((1, MLA_KV_RANK)),
            full((1, LANES)), full((1, LANES)), full((1, LANES)), full((1, LANES)),
            pl.BlockSpec((tm, LANES), lambda i: (g.pos_block(i), 0)),
            pl.BlockSpec((tm, LANES), lambda i: (g.pos_block(i), 0)),
        ],
        out_specs=[
            pl.BlockSpec((tm, MLA_HEADS * MLA_HW), lambda i: (i, 0)),
            pl.BlockSpec((tm, MLA_HEADS * MLA_HW), lambda i: (i, 0)),
            pl.BlockSpec((tm, BRANCH_W), lambda i: (i, 0)),
        ],
        out_shape=[
            jax.ShapeDtypeStruct((g.r, MLA_HEADS * MLA_HW), BF16),
            jax.ShapeDtypeStruct((g.r, MLA_HEADS * MLA_HW), BF16),
            jax.ShapeDtypeStruct((g.r, BRANCH_W), BF16),
        ],
        scratch_shapes=[pltpu.VMEM(wuq.shape, BF16), pltpu.VMEM(wukv.shape, BF16)],
        compiler_params=_cp("arbitrary"),
        name="mla_prep",
    )(pc, pc, pc, wuq, wukv, *gains, cos_t, sin_t)


def _attn_kernel(*refs, n_soft, dh, has_lat, diff, lam_init):
    refs = list(refs)
    vs = refs.pop()
    kts = refs.pop()
    o_ref = refs.pop()
    if diff:
        lam_ref, sg_ref = refs[0], refs[1]
        refs = refs[2:]
    q_ref, kc_ref, vc_ref = refs[:3]
    lc = kc_ref.shape[0]

    @pl.when(pl.program_id(2) == 0)
    def _():
        kts[:, 0:lc] = kc_ref[...].T
        vs[0:lc, :] = vc_ref[...]
        if has_lat:
            kts[:, lc:] = refs[3][...].T
            vs[lc:, :] = refs[4][...]

    q = q_ref[...]
    tq = q.shape[0]
    n_sub = 4 if tq % 64 == 0 else 1
    rsub = tq // n_sub
    outs = []
    for s in range(n_soft):
        sl = slice(s * dh, (s + 1) * dh)
        scs = [_dot(q[u * rsub:(u + 1) * rsub, sl], kts[sl, :]) for u in range(n_sub)]
        ps, ls = [], []
        for sc in scs:
            m = jnp.max(sc, axis=-1, keepdims=True)
            p = jnp.exp(sc - m)
            ls.append(jnp.sum(p, axis=-1, keepdims=True))
            ps.append(p.astype(BF16))
        os_ = [_dot(p, vs[...]) / l for p, l in zip(ps, ls)]
        outs.append(jnp.concatenate(os_, axis=0))
    if diff:
        lp = lam_ref[...]
        lam = (jnp.exp(jnp.sum(lp[0:1] * lp[1:2], axis=-1, keepdims=True))
               - jnp.exp(jnp.sum(lp[2:3] * lp[3:4], axis=-1, keepdims=True)) + lam_init)
        o = outs[0] - lam * outs[1]
        o = _rms(o) * sg_ref[...] * (1.0 - lam_init)
    else:
        o = outs[0]
    o_ref[...] = o.astype(o_ref.dtype)


def _attention(g, q_arr, k_arr, v_arr, *, q_col0, k_col0, heads, qk_w, dv, n_soft, ctx_queries,
               lam=None, subln_g=None, lam_init=0.0):
    diff = lam is not None
    ctx0 = g.rl // g.lc
    n_keys = g.lc if ctx_queries else g.lc + g.t
    if ctx_queries:
        tq = g.lc
        nq = 1
        q_row = lambda b, qi: ctx0 + b
        out_rows = g.rc
        o_row = lambda b, qi: b
    else:
        tq = min(ATT_TQ, g.t)
        nq = g.t // tq
        q_row = lambda b, qi: b * nq + qi
        out_rows = g.rl
        o_row = q_row
    in_specs = []
    args = []
    if diff:
        in_specs += [pl.BlockSpec(lam.shape, lambda b, h, qi: (0, 0)),
                     pl.BlockSpec((1, dv), lambda b, h, qi: (0, 0))]
        args += [lam, subln_g]
    in_specs += [
        pl.BlockSpec((tq, qk_w), lambda b, h, qi: (q_row(b, qi), q_col0 + h)),
        pl.BlockSpec((g.lc, qk_w), lambda b, h, qi: (ctx0 + b, k_col0 + h)),
        pl.BlockSpec((g.lc, dv), lambda b, h, qi: (ctx0 + b, h)),
    ]
    args += [q_arr, k_arr, v_arr]
    if not ctx_queries:
        in_specs += [
            pl.BlockSpec((g.t, qk_w), lambda b, h, qi: (b, k_col0 + h)),
            pl.BlockSpec((g.t, dv), lambda b, h, qi: (b, h)),
        ]
        args += [k_arr, v_arr]
    return pl.pallas_call(
        functools.partial(_attn_kernel, n_soft=n_soft, dh=qk_w // n_soft, has_lat=not ctx_queries,
                          diff=diff, lam_init=lam_init),
        grid=(g.b, heads, nq),
        in_specs=in_specs,
        out_specs=pl.BlockSpec((tq, dv), lambda b, h, qi: (o_row(b, qi), h)),
        out_shape=jax.ShapeDtypeStruct((out_rows, heads * dv), BF16),
        scratch_shapes=[pltpu.VMEM((qk_w, n_keys), BF16), pltpu.VMEM((n_keys, dv), BF16)],
        compiler_params=_cp("parallel", "parallel", "arbitrary"),
        name="attn_diff" if diff else "attn_mla",
    )(*args)


def _branch_kernel(*refs, n_src, n_lat_tiles):
    y_refs = refs[:N_BRANCH * n_src]
    wb_ref, g0_ref, g1_ref, g2_ref, o_ref, wb_s = refs[N_BRANCH * n_src:]
    gates = (g0_ref, g1_ref, g2_ref)

    @pl.when(pl.program_id(1) == 0)
    def _():
        wb_s[...] = wb_ref[...].astype(BF16)

    def body(which):
        def run(_):
            acc = None
            for n in range(N_BRANCH):
                term = gates[n][...].astype(F32) * _dot(y_refs[n * n_src + which][...], wb_s[n])
                acc = term if acc is None else acc + term
            o_ref[...] = acc.astype(o_ref.dtype)
        return run

    i = pl.program_id(1)
    if n_src == 1:
        body(0)(None)
    else:
        pl.when(i < n_lat_tiles)(lambda: body(0)(None))
        pl.when(i >= n_lat_tiles)(lambda: body(1)(None))


def _branch_merge(g, ys, w_branch, l, gt, n_tiles):
    tn = 512
    nj = g.d // tn
    y_specs, y_args = [], []
    for y in ys:
        sp, ar = _row_sources(g, y, BRANCH_W, lambda j: 0, 2)
        y_specs += sp
        y_args += ar
    n_src = len(y_args) // N_BRANCH
    return pl.pallas_call(
        functools.partial(_branch_kernel, n_src=n_src, n_lat_tiles=g.n_lat_tiles),
        grid=(nj, n_tiles),
        in_specs=[
            *y_specs,
            pl.BlockSpec((None, N_BRANCH, BRANCH_W, tn), lambda j, i: (l, 0, 0, j)),
            pl.BlockSpec((g.tm, tn), lambda j, i: (i, j)),
            pl.BlockSpec((g.tm, tn), lambda j, i: (i, nj + j)),
            pl.BlockSpec((g.tm, tn), lambda j, i: (i, 2 * nj + j)),
        ],
        out_specs=pl.BlockSpec((g.tm, tn), lambda j, i: (i, j)),
        out_shape=jax.ShapeDtypeStruct((n_tiles * g.tm, g.d), BF16),
        scratch_shapes=[pltpu.VMEM((N_BRANCH, BRANCH_W, tn), BF16)],
        compiler_params=_cp("arbitrary", "arbitrary"),
        name="branch_merge",
    )(*y_args, w_branch, gt, gt, gt)


def _outproj_kernel(*refs, n_lat_tiles):
    z_ref, w_ref, *x_refs, g_ref, o_ref, w_s = refs

    @pl.when(pl.program_id(1) == 0)
    def _():
        w_s[...] = w_ref[...].astype(BF16)

    def body(x_ref):
        o_ref[...] = x_ref[...] + g_ref[...] * _dot(z_ref[...], w_s[...])

    _on_rows(pl.program_id(1), n_lat_tiles, x_refs, body)


def _outproj(g, z, w_out, l, x_src, mods3, which_gate, n_tiles):
    tn = 512
    per = g.d // tn
    x_specs, x_args = _row_sources(g, x_src, tn, lambda j: j, 2)
    return pl.pallas_call(
        functools.partial(_outproj_kernel, n_lat_tiles=g.n_lat_tiles),
        grid=(per, n_tiles),
        in_specs=[
            pl.BlockSpec((g.tm, g.d), lambda j, i: (i, 0)),
            pl.BlockSpec((None, g.d, tn), lambda j, i: (l, 0, j)),
            *x_specs,
            pl.BlockSpec((None, 1, tn), lambda j, i: (g.mod_row(i), 0, which_gate * per + j)),
        ],
        out_specs=pl.BlockSpec((g.tm, tn), lambda j, i: (i, j)),
        out_shape=jax.ShapeDtypeStruct((n_tiles * g.tm, g.d), F32),
        scratch_shapes=[pltpu.VMEM((g.d, tn), BF16)],
        compiler_params=_cp("arbitrary", "arbitrary"),
        name="outproj",
    )(z, w_out, *x_args, mods3)


def _moe_prep_kernel(x_ref, sh_ref, sc_ref, rwt_ref, rb_ref, tri_ref, xr_ref, rank_ref, grp_ref, cnt_ref,
                     carry, *, mod_row):
    i = pl.program_id(0)

    @pl.when(i == 0)
    def _():
        carry[...] = jnp.zeros_like(carry)

    x = x_ref[...]
    h = _rms(x) * (1.0 + sc_ref[...]) + sh_ref[...]
    logits = _dot_nt(rwt_ref[...].astype(BF16), h.astype(BF16))
    scores = _sigmoid(logits)
    selm = scores + rb_ref[...]
    sc_rows = [scores[e:e + 1, :] for e in range(N_EXPERTS)]
    sel = [selm[e:e + 1, :] for e in range(N_EXPERTS)]
    gscore = []
    for gi in range(N_GROUPS):
        v = sel[gi * GROUP_SIZE:(gi + 1) * GROUP_SIZE]
        pair = [v[a] + v[b] for a in range(GROUP_SIZE) for b in range(a + 1, GROUP_SIZE)]
        gscore.append(functools.reduce(jnp.maximum, pair))
    gmax = functools.reduce(jnp.maximum, gscore)
    taken = jnp.zeros_like(gmax, dtype=jnp.bool_)
    in_group = []
    for gi in range(N_GROUPS):
        hit = jnp.logical_and(gscore[gi] == gmax, jnp.logical_not(taken))
        taken = jnp.logical_or(taken, hit)
        in_group.append(hit)
    neg = -jnp.inf
    masked = [jnp.where(in_group[e // GROUP_SIZE], sel[e], neg) for e in range(N_EXPERTS)]

    def pick(vals):
        mx = functools.reduce(jnp.maximum, vals)
        seen = jnp.zeros_like(mx, dtype=jnp.bool_)
        hot = []
        for v in vals:
            hit = jnp.logical_and(v == mx, jnp.logical_not(seen))
            seen = jnp.logical_or(seen, hit)
            hot.append(hit)
        return hot

    hot1 = pick(masked)
    hot2 = pick([jnp.where(hot1[e], neg, masked[e]) for e in range(N_EXPERTS)])
    zero = jnp.zeros_like(gmax)
    s1 = functools.reduce(lambda a, b: a + b, [jnp.where(hot1[e], sc_rows[e], zero) for e in range(N_EXPERTS)])
    s2 = functools.reduce(lambda a, b: a + b, [jnp.where(hot2[e], sc_rows[e], zero) for e in range(N_EXPERTS)])
    tot = s1 + s2
    w1 = s1 / tot
    w2 = s2 / tot
    comb = [jnp.where(hot1[e], w1, zero) + jnp.where(hot2[e], w2, zero) for e in range(N_EXPERTS)]

    tm = x.shape[0]
    gmat = jnp.concatenate([jnp.where(in_group[gi], 1.0, zero) for gi in range(N_GROUPS)]
                           + [jnp.zeros((MOD_ROWS - N_GROUPS, tm), F32)], axis=0)
    incl = _dot(gmat.astype(BF16), tri_ref[...])
    excl = incl - gmat + carry[:, 0:1]
    rank = functools.reduce(lambda a, b: a + b,
                            [jnp.where(in_group[gi], excl[gi:gi + 1, :], zero) for gi in range(N_GROUPS)])
    grp = functools.reduce(lambda a, b: a + b,
                           [jnp.where(in_group[gi], float(gi), zero) for gi in range(N_GROUPS)])
    rank_ref[...] = rank.astype(jnp.int32)
    grp_ref[...] = grp.astype(jnp.int32)
    carry[...] = carry[...] + jnp.sum(gmat, axis=1, keepdims=True)
    cnt_ref[...] = carry[...]

    side = jnp.concatenate(comb + [jnp.zeros((LANES - N_EXPERTS, tm), F32)], axis=0).T
    lane = lax.broadcasted_iota(jnp.int32, (1, LANES), 1)
    side = jnp.where(lane == N_EXPERTS + mod_row(i), 1.0, side)
    d = x.shape[1]
    xr_ref[:, 0:d] = x
    xr_ref[:, d:d + LANES] = side


def _moe_prep(g, x_mid, mods3, which_shift, which_scale, rwt, rb, tri, n_tiles):
    n_rows = n_tiles * g.tm
    return pl.pallas_call(
        functools.partial(_moe_prep_kernel, mod_row=g.mod_row),
        grid=(n_tiles,),
        in_specs=[
            pl.BlockSpec((g.tm, g.d), lambda i: (i, 0)),
            _mod_spec(g, which_shift),
            _mod_spec(g, which_scale),
            pl.BlockSpec((N_EXPERTS, g.d), lambda i: (0, 0)),
            pl.BlockSpec((N_EXPERTS, 1), lambda i: (0, 0)),
            pl.BlockSpec((g.tm, g.tm), lambda i: (0, 0)),
        ],
        out_specs=[
            pl.BlockSpec((g.tm, g.d + LANES), lambda i: (i, 0)),
            pl.BlockSpec((1, g.tm), lambda i: (0, i)),
            pl.BlockSpec((1, g.tm), lambda i: (0, i)),
            pl.BlockSpec((MOD_ROWS, LANES), lambda i: (0, 0)),
        ],
        out_shape=[
            jax.ShapeDtypeStruct((n_rows, g.d + LANES), F32),
            jax.ShapeDtypeStruct((1, n_rows), jnp.int32),
            jax.ShapeDtypeStruct((1, n_rows), jnp.int32),
            jax.ShapeDtypeStruct((MOD_ROWS, LANES), F32),
        ],
        scratch_shapes=[pltpu.VMEM((MOD_ROWS, LANES), F32)],
        compiler_params=_cp("arbitrary"),
        name="moe_prep",
    )(x_mid, mods3, mods3, rwt, rb, tri)


def _moe_group_kernel(tg_ref, nv_ref, nu_ref, pos_ref, xr_hbm, mods_ref, w1_hbm, w3_hbm, w2_hbm, out_hbm,
                      gbuf, obuf, w1s, w3s, w2s, stg_a, stg_b, src_ref, gsem, ssem, wsem, *, n_mod_rows, e_base):
    k = pl.program_id(0)
    n_used = nu_ref[0]
    slot = k % 2
    d = obuf.shape[2]
    tme = obuf.shape[1]

    def gather_start(kk, sl):
        base = kk * tme

        first = src_ref[base]

        def copy_row(r, tok):
            pltpu.make_async_copy(xr_hbm.at[pl.ds(tok, 1), :], gbuf.at[sl, pl.ds(r, 1), :], gsem.at[sl]).start()

        def issue(r, c):
            copy_row(r, src_ref[base + r])
            return c

        def issue_pad(r, c):
            copy_row(r, first)
            return c

        rows_of(kk, issue)
        lax.fori_loop(nv_ref[kk], tme, issue_pad, 0)

    def build_sorted_index():
        def put(t, c):
            src_ref[pos_ref[t]] = t
            return c

        lax.fori_loop(0, pos_ref.shape[0], put, 0, unroll=8)

    def gather_wait(sl):
        def w(r, c):
            pltpu.make_async_copy(xr_hbm.at[pl.ds(0, 1), :], gbuf.at[sl, pl.ds(0, 1), :], gsem.at[sl]).wait()
            return c

        lax.fori_loop(0, tme, w, 0, unroll=8)

    def scatter_start(kk, sl):
        base = kk * tme

        def issue(r, c):
            tok = src_ref[base + r]
            pltpu.make_async_copy(obuf.at[sl, pl.ds(r, 1), :], out_hbm.at[pl.ds(tok, 1), :], ssem.at[sl]).start()
            return c

        rows_of(kk, issue)

    def scatter_wait(kk, sl):
        def w(r, c):
            pltpu.make_async_copy(obuf.at[sl, pl.ds(0, 1), :], out_hbm.at[pl.ds(0, 1), :], ssem.at[sl]).wait()
            return c

        rows_of(kk, w)

    def rows_of(kk, fn):
        n_rows = nv_ref[kk]

        @pl.when(n_rows == tme)
        def _():
            lax.fori_loop(0, tme, fn, 0, unroll=8)

        @pl.when(n_rows != tme)
        def _():
            lax.fori_loop(0, n_rows, fn, 0)

    def load_weights(grp):
        rows_a = stg_a.shape[1]
        rows_b = stg_b.shape[1]
        chunks = []
        n_a = n_b = 0
        for e in range(GROUP_SIZE):
            ex = e_base + grp * GROUP_SIZE + e
            for src, dst in ((w1_hbm, w1s), (w3_hbm, w3s)):
                for c0 in range(0, src.shape[1], rows_a):
                    sl = n_a % 2
                    n_a += 1
                    cp = pltpu.make_async_copy(src.at[ex, pl.ds(c0, rows_a), :], stg_a.at[sl], wsem.at[sl])
                    chunks.append((cp, stg_a.at[sl], dst, e, c0, rows_a))
            for c0 in range(0, w2_hbm.shape[1], rows_b):
                sl = n_b % 2
                n_b += 1
                cp = pltpu.make_async_copy(w2_hbm.at[ex, pl.ds(c0, rows_b), :], stg_b.at[sl], wsem.at[2 + sl])
                chunks.append((cp, stg_b.at[sl], w2s, e, c0, rows_b))
        chunks[0][0].start()
        for idx, (cp, stg, dst, e, c0, rows) in enumerate(chunks):
            if idx + 1 < len(chunks):
                chunks[idx + 1][0].start()
            cp.wait()
            dst[e, c0:c0 + rows, :] = stg[...].astype(BF16)

    @pl.when(k < n_used)
    def _():
        grp = tg_ref[k]

        @pl.when(k == 0)
        def _():
            build_sorted_index()
            gather_start(0, 0)

        @pl.when(k + 1 < n_used)
        def _():
            gather_start(k + 1, 1 - slot)

        @pl.when(jnp.logical_or(k == 0, grp != tg_ref[jnp.maximum(k - 1, 0)]))
        def _():
            load_weights(grp)

        gather_wait(slot)

        @pl.when(k >= 2)
        def _():
            scatter_wait(k - 2, slot)

        lane = lax.broadcasted_iota(jnp.int32, (1, LANES), 1)
        rows = tme // MOE_ROW_GROUPS
        for u in range(MOE_ROW_GROUPS):
            rs = slice(u * rows, (u + 1) * rows)
            x = gbuf[slot, rs, 0:d]
            side = gbuf[slot, rs, d:d + LANES]

            def cond_vec(which):
                out = jnp.broadcast_to(mods_ref[0:1, which * d:(which + 1) * d], (rows, d))
                for r in range(1, n_mod_rows):
                    hot = side[:, N_EXPERTS + r:N_EXPERTS + r + 1] > 0.5
                    out = jnp.where(hot, mods_ref[r:r + 1, which * d:(which + 1) * d], out)
                return out

            h = (_rms(x) * (1.0 + cond_vec(1)) + cond_vec(0)).astype(BF16)
            y = jnp.zeros((rows, d), F32)
            for e in range(GROUP_SIZE):
                a = _dot(h, w1s[e])
                b = _dot(h, w3s[e])
                cw = jnp.sum(jnp.where(lane == grp * GROUP_SIZE + e, side, 0.0), axis=1, keepdims=True)
                hid = (a * _sigmoid(a) * b * cw).astype(BF16)
                y = y + _dot(hid, w2s[e])
            obuf[slot, rs, :] = x + cond_vec(2) * y

        scatter_start(k, slot)

        @pl.when(k == n_used - 1)
        def _():
            @pl.when(k >= 1)
            def _():
                scatter_wait(k - 1, 1 - slot)

            scatter_wait(k, slot)


def _moe_group(g, xr, mods_tail, w1, w3, w2, meta, n_rows, n_mod_rows, e_base):
    kt = meta[0].shape[0]
    tme = MOE_TM
    any_spec = pl.BlockSpec(memory_space=pl.ANY)
    grid_spec = pltpu.PrefetchScalarGridSpec(
        num_scalar_prefetch=len(meta),
        grid=(kt,),
        in_specs=[any_spec, pl.BlockSpec(mods_tail.shape, lambda k, *_: (0, 0)), any_spec, any_spec, any_spec],
        out_specs=any_spec,
        scratch_shapes=[
            pltpu.VMEM((2, tme, g.d + LANES), F32),
            pltpu.VMEM((2, tme, g.d), F32),
            pltpu.VMEM((GROUP_SIZE, g.d, D_EXPERT), BF16),
            pltpu.VMEM((GROUP_SIZE, g.d, D_EXPERT), BF16),
            pltpu.VMEM((GROUP_SIZE, D_EXPERT, g.d), BF16),
            pltpu.VMEM((2, g.d // 2, D_EXPERT), F32),
            pltpu.VMEM((2, D_EXPERT // 2, g.d), F32),
            pltpu.SMEM((kt * tme,), jnp.int32),
            pltpu.SemaphoreType.DMA((2,)),
            pltpu.SemaphoreType.DMA((2,)),
            pltpu.SemaphoreType.DMA((4,)),
        ],
    )
    return pl.pallas_call(
        functools.partial(_moe_group_kernel, n_mod_rows=n_mod_rows, e_base=e_base),
        grid_spec=grid_spec,
        out_shape=jax.ShapeDtypeStruct((n_rows, g.d), F32),
        compiler_params=_cp("arbitrary"),
        name="moe_group",
    )(*meta, xr, mods_tail, w1, w3, w2)


def _moe_meta(rank, grp, cnt, n_rows):
    tme = MOE_TM
    kt = n_rows // tme + N_GROUPS
    counts = cnt[:N_GROUPS, 0].astype(jnp.int32)
    ntile = (counts + tme - 1) // tme
    tile_end = jnp.cumsum(ntile)
    tile_off = tile_end - ntile
    gsel = [grp[0] == gi for gi in range(N_GROUPS)]
    row_off = functools.reduce(lambda a, b: a + b,
                               [jnp.where(gsel[gi], tile_off[gi] * tme, 0) for gi in range(N_GROUPS)])
    pos = row_off + rank[0]
    ks = jnp.arange(kt, dtype=jnp.int32)
    tile_group = jnp.minimum(functools.reduce(lambda a, b: a + b,
                                              [(ks >= tile_end[gi]).astype(jnp.int32) for gi in range(N_GROUPS)]),
                             N_GROUPS - 1)
    n_used = tile_end[-1:]
    cnt_k = functools.reduce(lambda a, b: a + b,
                             [jnp.where(tile_group == gi, counts[gi] - (ks - tile_off[gi]) * tme, 0)
                              for gi in range(N_GROUPS)])
    n_valid = jnp.where(ks < n_used[0], jnp.clip(cnt_k, 0, tme), 0)
    return tile_group, n_valid, n_used, pos


def _rope_tables(t, rot_dim, tile_rows, reps):
    f32 = np.float32
    rows = t // GRID_W
    r = np.repeat(np.arange(rows, dtype=f32), GRID_W)
    col = np.tile(np.arange(GRID_W, dtype=f32), rows)
    n_freq = rot_dim // 4
    inv = np.power(f32(ROPE_BASE), -np.arange(n_freq, dtype=f32) / f32(n_freq)).astype(f32)
    ang = np.concatenate([r[:, None] * inv, col[:, None] * inv], axis=-1).astype(f32)
    cos = np.tile(np.concatenate([np.cos(ang), np.cos(ang)], axis=-1), (1, reps))
    sin = np.tile(np.concatenate([-np.sin(ang), np.sin(ang)], axis=-1), (1, reps))
    cos = np.concatenate([cos, np.ones((tile_rows, LANES), f32)], axis=0).astype(f32)
    sin = np.concatenate([sin, np.zeros((tile_rows, LANES), f32)], axis=0).astype(f32)
    return jnp.asarray(cos), jnp.asarray(sin)


def _uq_perm():
    nope = [h * MLA_QK + i for h in range(MLA_HEADS) for i in range(MLA_NOPE)]
    rope = [h * MLA_QK + MLA_NOPE + i for h in range(MLA_HEADS) for i in range(MLA_ROPE)]
    return jnp.asarray(nope + rope)


def kernel(x, c, ctx, c_ctx, w_ada, b_ada, w_in, b_in, m_conv_w, m_conv_b, m_norm_g, da_q_norm_g, da_k_norm_g,
           da_lambda, da_subln_g, mla_cq_norm_g, mla_ckv_norm_g, mla_w_uq, mla_w_ukv, mla_q_norm_g, mla_k_norm_g,
           w_branch, w_out, moe_w1, moe_w3, moe_w2, router_w, router_bias):
    bsz, t, d = x.shape
    lc = ctx.shape[1]
    depth = w_ada.shape[0]
    g = _Geom(bsz, t, lc, d, min(1024, bsz * lc, t))
    assert bsz + 1 <= MOD_ROWS

    cvec = jnp.concatenate([c, c_ctx[None, :], jnp.zeros((MOD_ROWS - bsz - 1, d), F32)], axis=0)
    mods = _mod_vectors(cvec, w_ada, b_ada)

    cos_d, sin_d = _rope_tables(t, DA_DH, g.tm, 1)
    cos_a, sin_a = _rope_tables(t, MLA_ROPE, g.tm, 2)
    rwt = router_w.T
    rb = router_bias.reshape(N_EXPERTS, 1)
    uq_perm = _uq_perm()
    x_all = (x.reshape(g.rl, d), ctx.reshape(g.rc, d))
    pos_spec = pl.BlockSpec((g.tm, LANES), lambda j, i: (g.pos_block(i), 0))
    b_in3 = b_in.reshape(depth, 1, -1)
    w_in_t = jnp.swapaxes(w_in, 1, 2)
    qk_scale = jnp.concatenate([jnp.full((1, BRANCH_W), M_DH ** -0.5, F32), jnp.ones((1, BRANCH_W), F32)], axis=1)
    tri = (jnp.arange(g.tm)[:, None] <= jnp.arange(g.tm)[None, :]).astype(BF16)
    w1r = moe_w1.reshape(depth * N_EXPERTS, d, D_EXPERT)
    w3r = moe_w3.reshape(depth * N_EXPERTS, d, D_EXPERT)
    w2r = moe_w2.reshape(depth * N_EXPERTS, D_EXPERT, d)

    for l in range(depth):
        last = l == depth - 1
        n_tiles = g.n_lat_tiles if last else g.n_tiles
        n_rows = n_tiles * g.tm
        lam_init = 0.8 - 0.6 * math.exp(-0.3 * l)
        mods3 = mods[l].reshape(MOD_ROWS, 1, 6 * d)

        h1 = _prenorm(g, x_all, mods3, 0, 1, g.n_tiles)

        row = lambda v: v.reshape(1, -1)
        proj = functools.partial(_proj, g, h=h1, w_in_t=w_in_t, b_in3=b_in3, l=l)
        pm = proj(_epi_raw, col0=OFF_MQK, width=OFF_MG - OFF_MQK, out_dtype=BF16, tn=1024, name="proj_mlstm")
        da_gain = jnp.concatenate([jnp.tile(da_q_norm_g[l] * DA_DH ** -0.5, 2 * DA_HEADS),
                                   jnp.tile(da_k_norm_g[l], 2 * DA_HEADS)])
        dqk = proj(_epi_normrope, col0=OFF_DQ, width=2 * DA_QK_W, out_dtype=BF16,
                   extra=(row(da_gain), cos_d, sin_d),
                   extra_specs=(pl.BlockSpec((1, 512), lambda j, i: (0, j)), pos_spec, pos_spec),
                   name="proj_dqk")
        dv = proj(_epi_raw, col0=OFF_DV, width=BRANCH_W, out_dtype=BF16, name="proj_dv")
        pc = proj(_epi_raw, col0=OFF_CQ, width=OFF_G - OFF_CQ, out_dtype=F32, patch=(OFF_MG, PC_GATE_COL),
                  name="proj_small")
        gt = proj(_epi_sigmoid, col0=OFF_G, width=N_BRANCH * d, out_dtype=BF16, name="proj_gates")

        qkc = _mconv(g, pm, m_conv_w[l], row(m_conv_b[l]), qk_scale)
        ym_l, ym_c = _mscan(g, qkc, pm, pc, row(m_norm_g[l]))

        qg, kg = mla_q_norm_g[l], mla_k_norm_g[l]
        a_scale = MLA_QK ** -0.5
        gains = (row(mla_cq_norm_g[l]), row(mla_ckv_norm_g[l]),
                 row(qg[:MLA_NOPE] * a_scale), row(jnp.tile(qg[MLA_NOPE:], 2) * a_scale),
                 row(kg[:MLA_NOPE]), row(jnp.concatenate([kg[MLA_NOPE:], jnp.zeros((LANES - MLA_ROPE,), F32)])))
        aq, ak, av = _mla_prep(g, pc, mla_w_uq[l][:, uq_perm], mla_w_ukv[l], gains, cos_a, sin_a)

        lam_p = da_lambda[l]
        sub_g = row(da_subln_g[l])
        da_kw = dict(q_col0=0, k_col0=DA_HEADS, heads=DA_HEADS, qk_w=2 * DA_DH, dv=DA_DV, n_soft=2,
                     lam=lam_p, subln_g=sub_g, lam_init=lam_init)
        mla_kw = dict(q_col0=0, k_col0=0, heads=MLA_HEADS, qk_w=MLA_HW, dv=MLA_DV, n_soft=1)
        yd_l = _attention(g, dqk, dqk, dv, ctx_queries=False, **da_kw)
        ya_l = _attention(g, aq, ak, av, ctx_queries=False, **mla_kw)
        if last:
            ys = (ym_l, yd_l, ya_l)
        else:
            yd_c = _attention(g, dqk, dqk, dv, ctx_queries=True, **da_kw)
            ya_c = _attention(g, aq, ak, av, ctx_queries=True, **mla_kw)
            ys = ((ym_l, ym_c), (yd_l, yd_c), (ya_l, ya_c))

        z = _branch_merge(g, ys, w_branch, l, gt, n_tiles)
        x_mid = _outproj(g, z, w_out, l, x_all, mods3, 2, n_tiles)

        xr, rank, grp, cnt = _moe_prep(g, x_mid, mods3, 3, 4, rwt, rb, tri, n_tiles)
        meta = _moe_meta(rank, grp, cnt, n_rows)
        n_cond = bsz if last else bsz + 1
        x_all = _moe_group(g, xr, mods[l][:, 3 * d:], w1r, w3r, w2r, meta, n_rows, n_cond, l * N_EXPERTS)

    return x_all.reshape(bsz, t, d)
```

```python
import functools
import math

import jax
import jax.numpy as jnp
import numpy as np
from jax import lax
from jax.experimental import pallas as pl
from jax.experimental.pallas import tpu as pltpu

F32 = jnp.float32
BF16 = jnp.bfloat16

GRID_W = 64
ROPE_BASE = 10000.0
EPS = 1e-6
N_BRANCH = 3
BRANCH_W = 1024
M_HEADS = 4
M_DH = BRANCH_W // M_HEADS
M_CHUNK = 256
M_CONV = 5
DA_HEADS = 4
DA_DV = BRANCH_W // DA_HEADS
DA_DH = DA_DV // 2
DA_QK_W = DA_HEADS * 2 * DA_DH
MLA_HEADS = 8
MLA_Q_RANK = 512
MLA_KV_RANK = 256
MLA_NOPE = 128
MLA_ROPE = 64
MLA_DV = BRANCH_W // MLA_HEADS
MLA_QK = MLA_NOPE + MLA_ROPE
N_EXPERTS = 16
N_GROUPS = 4
GROUP_SIZE = N_EXPERTS // N_GROUPS
D_EXPERT = 512

OFF_MQK = 0
OFF_MV = OFF_MQK + 2 * BRANCH_W
OFF_MO = OFF_MV + BRANCH_W
OFF_MG = OFF_MO + BRANCH_W
OFF_DQ = OFF_MG + 4 * M_HEADS
OFF_DK = OFF_DQ + DA_QK_W
OFF_DV = OFF_DK + DA_QK_W
OFF_CQ = OFF_DV + BRANCH_W
OFF_CKV = OFF_CQ + MLA_Q_RANK
OFF_KR = OFF_CKV + MLA_KV_RANK
OFF_G = OFF_KR + MLA_ROPE

LANES = 128
MOD_ROWS = 8
VMEM_LIMIT = 56 * 1024 * 1024
MLA_HW = 256
MOE_TM = 256
ATT_TQ = 2048
PC_GATE_COL = 896
PROJ_ROW_GROUPS = 4


def _cp(*sem):
    return pltpu.CompilerParams(dimension_semantics=sem, vmem_limit_bytes=VMEM_LIMIT)


def _rms(x):
    return x * lax.rsqrt(jnp.mean(x * x, axis=-1, keepdims=True) + EPS)


def _sigmoid(x):
    return 0.5 * jnp.tanh(0.5 * x) + 0.5


def _dot(a, b):
    return jnp.dot(a, b, preferred_element_type=F32)


def _dot_nt(a, b):
    return lax.dot_general(a, b, (((1,), (1,)), ((), ())), preferred_element_type=F32)


def _mod_kernel(c_ref, w_ref, b_ref, o_ref):
    c = c_ref[...]
    s = (c * _sigmoid(c)).astype(BF16)
    o_ref[...] = _dot(s, w_ref[...].astype(BF16)) + b_ref[...]


def _mod_vectors(cvec, w_ada, b_ada):
    depth, d, n = w_ada.shape
    tn = 1024
    return pl.pallas_call(
        _mod_kernel,
        grid=(depth, n // tn),
        in_specs=[
            pl.BlockSpec((MOD_ROWS, d), lambda l, j: (0, 0)),
            pl.BlockSpec((None, d, tn), lambda l, j: (l, 0, j)),
            pl.BlockSpec((None, 1, tn), lambda l, j: (l, 0, j)),
        ],
        out_specs=pl.BlockSpec((None, MOD_ROWS, tn), lambda l, j: (l, 0, j)),
        out_shape=jax.ShapeDtypeStruct((depth, MOD_ROWS, n), F32),
        compiler_params=_cp("parallel", "parallel"),
        name="adaln_mod",
    )(cvec, w_ada, b_ada.reshape(depth, 1, n))


class _Geom:
    def __init__(self, b, t, lc, d, tm):
        assert t % tm == 0 and (b * lc) % tm == 0 and t % lc == 0 and t % GRID_W == 0
        self.b, self.t, self.lc, self.d, self.tm = b, t, lc, d, tm
        self.rl, self.rc = b * t, b * lc
        self.r = self.rl + self.rc
        self.mblk = min(256, lc)
        self.chunk = min(M_CHUNK, self.mblk)
        assert lc % self.mblk == 0 and t % self.mblk == 0 and self.mblk % self.chunk == 0
        self.n_lat_tiles = self.rl // tm
        self.n_tiles = self.r // tm
        self.tiles_per_seq = t // tm

    def mod_row(self, i):
        return jnp.minimum(i // self.tiles_per_seq, self.b)

    def pos_block(self, i):
        return jnp.where(i < self.n_lat_tiles, i % self.tiles_per_seq, self.tiles_per_seq)


def _mod_spec(g, which):
    return pl.BlockSpec((None, 1, g.d), lambda i: (g.mod_row(i), 0, which))


def _row_sources(g, src, width, col_of, n_grid):
    tile = lambda i: i
    if n_grid == 2:
        spec = lambda rows_of: pl.BlockSpec((g.tm, width), lambda j, i: (rows_of(i), col_of(j)))
    else:
        spec = lambda rows_of: pl.BlockSpec((g.tm, width), lambda i: (rows_of(i), 0))
    if not isinstance(src, tuple):
        return [spec(tile)], [src]
    lat, ctx = src
    nl = g.n_lat_tiles
    return ([spec(lambda i: jnp.minimum(i, nl - 1)), spec(lambda i: jnp.maximum(i - nl, 0))], [lat, ctx])


def _on_rows(i, n_lat_tiles, refs, body):
    if len(refs) == 1:
        body(refs[0])
        return

    @pl.when(i < n_lat_tiles)
    def _():
        body(refs[0])

    @pl.when(i >= n_lat_tiles)
    def _():
        body(refs[1])


def _prenorm_kernel(*refs, n_lat_tiles):
    *x_refs, sh_ref, sc_ref, o_ref = refs

    def body(x_ref):
        o_ref[...] = (_rms(x_ref[...]) * (1.0 + sc_ref[...]) + sh_ref[...]).astype(o_ref.dtype)

    _on_rows(pl.program_id(0), n_lat_tiles, x_refs, body)


def _prenorm(g, x_src, mods3, which_shift, which_scale, n_tiles):
    x_specs, x_args = _row_sources(g, x_src, g.d, None, 1)
    return pl.pallas_call(
        functools.partial(_prenorm_kernel, n_lat_tiles=g.n_lat_tiles),
        grid=(n_tiles,),
        in_specs=[*x_specs, _mod_spec(g, which_shift), _mod_spec(g, which_scale)],
        out_specs=pl.BlockSpec((g.tm, g.d), lambda i: (i, 0)),
        out_shape=jax.ShapeDtypeStruct((n_tiles * g.tm, g.d), BF16),
        compiler_params=_cp("parallel"),
        name="prenorm",
    )(*x_args, mods3, mods3)


def _epi_raw(acc, rs, o_ref):
    o_ref[rs, :] = acc.astype(o_ref.dtype)


def _epi_sigmoid(acc, rs, o_ref):
    o_ref[rs, :] = _sigmoid(acc).astype(o_ref.dtype)


def _epi_normrope(acc, rs, gain_ref, cos_ref, sin_ref, o_ref):
    cos = cos_ref[rs, :]
    sin = sin_ref[rs, :]
    for j in range(acc.shape[1] // LANES):
        sl = slice(j * LANES, (j + 1) * LANES)
        x = _rms(acc[:, sl]) * gain_ref[:, sl]
        o_ref[rs, sl] = (x * cos + pltpu.roll(x, LANES // 2, 1) * sin).astype(o_ref.dtype)


def _proj_kernel(*refs, shift, epilogue, patch):
    if shift:
        h_ref, wa_ref, wb_ref, ba_ref, bb_ref, *rest = refs
    else:
        h_ref, wa_ref, ba_ref, *rest = refs
    if patch is not None:
        wp_ref, bp_ref, *rest = rest
    *extra, o_ref, w_s, b_s = rest
    tn = w_s.shape[0]

    @pl.when(pl.program_id(1) == 0)
    def _():
        if shift:
            w_s[0:tn - shift, :] = wa_ref[shift:tn, :].astype(BF16)
            w_s[tn - shift:tn, :] = wb_ref[0:shift, :].astype(BF16)
            bb = jnp.concatenate([ba_ref[...], bb_ref[...]], axis=1)
            bb = jnp.broadcast_to(bb, (b_s.shape[0], 2 * tn))
            b_s[...] = pltpu.roll(bb, 2 * tn - shift, 1)[:, :tn]
        else:
            w_s[...] = wa_ref[...].astype(BF16)
            b_s[...] = jnp.broadcast_to(ba_ref[...], b_s.shape)
        if patch is not None:
            p_tile, p_off = patch

            @pl.when(pl.program_id(0) == p_tile)
            def _():
                w_s[p_off:p_off + LANES, :] = wp_ref[...].astype(BF16)
                b_s[:, p_off:p_off + LANES] = jnp.broadcast_to(bp_ref[...], (b_s.shape[0], LANES))

    rows = h_ref.shape[0] // PROJ_ROW_GROUPS
    for u in range(PROJ_ROW_GROUPS):
        rs = slice(u * rows, (u + 1) * rows)
        acc = _dot_nt(h_ref[rs, :], w_s[...]) + b_s[0:1, :]
        epilogue(acc, rs, *extra, o_ref)


def _proj(g, epilogue, h, w_in_t, b_in3, l, col0, width, out_dtype, *, tn=512, out_tn=None,
          extra=(), extra_specs=(), patch=None, name="proj"):
    ncols, kdim = w_in_t.shape[1], w_in_t.shape[2]
    blk0 = col0 // tn
    shift = col0 - blk0 * tn
    assert shift % 16 == 0
    nj = pl.cdiv(width, tn)
    last_blk = pl.cdiv(ncols, tn) - 1
    out_tn = tn if out_tn is None else out_tn
    w_specs = [pl.BlockSpec((None, tn, kdim), lambda j, i: (l, blk0 + j, 0))]
    b_specs = [pl.BlockSpec((None, 1, tn), lambda j, i: (l, 0, blk0 + j))]
    if shift:
        nxt = lambda j: jnp.minimum(blk0 + j + 1, last_blk)
        w_specs.append(pl.BlockSpec((None, tn, kdim), lambda j, i: (l, nxt(j), 0)))
        b_specs.append(pl.BlockSpec((None, 1, tn), lambda j, i: (l, 0, nxt(j))))
    n_w = len(w_specs)
    p_specs, p_args, k_patch = [], [], None
    if patch is not None:
        src_col, dst_col = patch
        assert src_col % LANES == 0 and dst_col % LANES == 0 and dst_col >= width
        p_specs = [pl.BlockSpec((None, LANES, kdim), lambda j, i: (l, src_col // LANES, 0)),
                   pl.BlockSpec((None, 1, LANES), lambda j, i: (l, 0, src_col // LANES))]
        p_args = [w_in_t, b_in3]
        k_patch = (dst_col // tn, dst_col % tn)
    return pl.pallas_call(
        functools.partial(_proj_kernel, shift=shift, epilogue=epilogue, patch=k_patch),
        grid=(nj, g.n_tiles),
        in_specs=[pl.BlockSpec((g.tm, kdim), lambda j, i: (i, 0)), *w_specs, *b_specs, *p_specs, *extra_specs],
        out_specs=pl.BlockSpec((g.tm, out_tn), lambda j, i: (i, j)),
        out_shape=jax.ShapeDtypeStruct((g.r, nj * out_tn), out_dtype),
        scratch_shapes=[pltpu.VMEM((tn, kdim), BF16), pltpu.VMEM((MOD_ROWS, tn), F32)],
        compiler_params=_cp("arbitrary", "arbitrary"),
        name=name,
    )(h, *([w_in_t] * n_w), *([b_in3] * n_w), *p_args, *extra)


def _mconv_kernel(x_ref, prev_ref, next_ref, w_ref, b_ref, s_ref, sh_ref, o_ref, *, blocks_per_seq, blocks_per_ctx,
                  n_lat_blocks):
    i = pl.program_id(0)
    rows = x_ref.shape[0]
    halo = prev_ref.shape[0]
    edge = MOD_ROWS
    in_lat = i < n_lat_blocks
    pos = jnp.where(in_lat, i % blocks_per_seq, (i - n_lat_blocks) % blocks_per_ctx)
    per_seq = jnp.where(in_lat, blocks_per_seq, blocks_per_ctx)
    x = x_ref[...]
    w = [w_ref[j:j + 1, :] for j in range(M_CONV)]
    acc = x.astype(F32) * w[M_CONV // 2] + b_ref[...]
    taps = [j for j in range(M_CONV) if j != M_CONV // 2]
    for n_, j in enumerate(taps):
        acc = acc + _dot(sh_ref[n_], x) * w[j]
    p = jnp.where(pos != 0, prev_ref[...].astype(F32), 0.0)
    nx = jnp.where(pos != per_seq - 1, next_ref[...].astype(F32), 0.0)
    r8 = lax.broadcasted_iota(jnp.int32, (edge, 1), 0)
    top = (jnp.where(r8 < 2, pltpu.roll(p, 2, 0)[0:edge], 0.0) * w[0]
           + jnp.where(r8 < 1, pltpu.roll(p, 1, 0)[0:edge], 0.0) * w[1])
    bot = (jnp.where(r8 >= edge - 1, pltpu.roll(nx, edge - 1, 0)[0:edge], 0.0) * w[3]
           + jnp.where(r8 >= edge - 2, pltpu.roll(nx, edge - 2, 0)[0:edge], 0.0) * w[4])
    zeros = jnp.zeros((halo - edge, x.shape[1]), F32)

    def finish(a):
        return (a * _sigmoid(a) * s_ref[...]).astype(o_ref.dtype)

    o_ref[0:halo, :] = finish(acc[0:halo] + jnp.concatenate([top, zeros], axis=0))
    o_ref[halo:rows - halo, :] = finish(acc[halo:rows - halo])
    o_ref[rows - halo:rows, :] = finish(acc[rows - halo:rows] + jnp.concatenate([zeros, bot], axis=0))


def _mconv(g, pm, conv_w, conv_b, scale_row):
    rows = g.mblk
    halo = 16
    tn = 2 * BRANCH_W
    per = rows // halo
    nblk = g.r // rows
    last = g.r // halo - 1
    taps = [j for j in range(M_CONV) if j != M_CONV // 2]
    shifts = np.zeros((len(taps), rows, rows), np.float32)
    for n_, j in enumerate(taps):
        for t_ in range(rows):
            if 0 <= t_ + j - M_CONV // 2 < rows:
                shifts[n_, t_, t_ + j - M_CONV // 2] = 1.0
    shifts = jnp.asarray(shifts, BF16)
    return pl.pallas_call(
        functools.partial(_mconv_kernel, blocks_per_seq=g.t // rows, blocks_per_ctx=g.lc // rows,
                          n_lat_blocks=g.rl // rows),
        grid=(nblk, 2 * BRANCH_W // tn),
        in_specs=[
            pl.BlockSpec((rows, tn), lambda i, j: (i, j)),
            pl.BlockSpec((halo, tn), lambda i, j: (jnp.maximum(i * per - 1, 0), j)),
            pl.BlockSpec((halo, tn), lambda i, j: (jnp.minimum((i + 1) * per, last), j)),
            pl.BlockSpec((M_CONV, tn), lambda i, j: (0, j)),
            pl.BlockSpec((1, tn), lambda i, j: (0, j)),
            pl.BlockSpec((1, tn), lambda i, j: (0, j)),
            pl.BlockSpec(shifts.shape, lambda i, j: (0, 0, 0)),
        ],
        out_specs=pl.BlockSpec((rows, tn), lambda i, j: (i, j)),
        out_shape=jax.ShapeDtypeStruct((g.r, 2 * BRANCH_W), BF16),
        compiler_params=_cp("parallel", "parallel"),
        name="mlstm_conv",
    )(pm, pm, pm, conv_w, conv_b, scale_row, shifts)


def _mscan_kernel(qkf_ref, vf_ref, gf_ref, qkb_ref, vb_ref, gb_ref, ol_ref, oc_ref, ng_ref, yl_ref, yc_ref,
                  xsf, bcf, xsb, bcb, dtf, dtb, hf, hb, ct, nv, ms, *, lc, t, mblk, chunk):
    i = pl.program_id(1)
    nblk = pl.num_programs(1)
    L = chunk
    cpb = mblk // L
    nbc = lc // mblk
    nbl = t // mblk
    n = lc + t

    @pl.when(i == 0)
    def _():
        ct[...] = jnp.zeros_like(ct)
        nv[...] = jnp.zeros_like(nv)
        ms[...] = jnp.zeros_like(ms)

    lane = lax.broadcasted_iota(jnp.int32, (1, LANES), 1)
    is_f = jnp.logical_or(jnp.logical_and(lane >= M_HEADS, lane < 2 * M_HEADS),
                          jnp.logical_and(lane >= 3 * M_HEADS, lane < 4 * M_HEADS))
    pos = lax.broadcasted_iota(jnp.int32, (mblk, 1), 0) % L

    def gate_prep(g_ref, x_out, bc_out, forward):
        gx = g_ref[...]
        lsg = jnp.minimum(gx, 0.0) - jnp.log(1.0 + jnp.exp(-jnp.abs(gx)))
        x = jnp.where(is_f, lsg, gx)
        x_out[...] = x
        run = x
        k = 1
        while k < L:
            if forward:
                run = run + jnp.where(pos >= k, pltpu.roll(run, k, 0), 0.0)
            else:
                run = run + jnp.where(pos < L - k, pltpu.roll(run, mblk - k, 0), 0.0)
            k *= 2
        bc_out[...] = run

    gate_prep(gf_ref, xsf, bcf, True)
    gate_prep(gb_ref, xsb, bcb, False)

    pos_f = i * mblk
    pos_b = jnp.where(i < nbc, (nbc - 1 - i) * mblk, lc + (nbl - 1 - (i - nbc)) * mblk)

    def row_table(x_ref, bc_ref, out_ref):
        dm = x_ref[...] - pltpu.roll(bc_ref[...], LANES - M_HEADS, 1)
        for c in range(cpb):
            out_ref[c] = dm[c * L:(c + 1) * L, :].T

    row_table(xsf, bcf, dtf)
    row_table(xsb, bcb, dtb)

    H = M_HEADS
    r_i = lax.broadcasted_iota(jnp.int32, (H * L, L), 0) % L
    c_i = lax.broadcasted_iota(jnp.int32, (H * L, L), 1)
    masks = (c_i <= r_i, c_i >= r_i)

    def stack(parts):
        return jnp.concatenate(parts, axis=0)

    def dir_step(c, d, mvals):
        qk_ref, v_ref, x_ref, bc_ref, dt_ref, h_ref, base = (
            (qkf_ref, vf_ref, xsf, bcf, dtf, hf, pos_f), (qkb_ref, vb_ref, xsb, bcb, dtb, hb, pos_b))[d]
        r0 = pl.multiple_of(c * L, L)
        bg = bc_ref[pl.ds(r0, L), :]
        dt = dt_ref[c]
        qs = [qk_ref[pl.ds(r0, L), j * M_DH:(j + 1) * M_DH] for j in range(H)]
        ks = [qk_ref[pl.ds(r0, L), BRANCH_W + j * M_DH:BRANCH_W + (j + 1) * M_DH] for j in range(H)]
        vs_ = [v_ref[pl.ds(r0, L), j * M_DH:(j + 1) * M_DH] for j in range(H)]
        li0 = 2 * d * H
        lf0 = (2 * d + 1) * H
        a_h = [bg[:, lf0 + j:lf0 + j + 1] for j in range(H)]
        bl_h = [a[L - 1:L, :] if d == 0 else a[0:1, :] for a in a_h]
        a4 = stack(a_h)
        d_rows = [dt[li0 + j:li0 + j + 1, :] for j in range(H)]
        d4 = stack([jnp.broadcast_to(r_, (L, L)) for r_ in d_rows])
        s4 = stack([_dot_nt(qs[j], ks[j]) for j in range(H)])
        logw = jnp.where(masks[d], a4 + d4, -jnp.inf)
        m_loc = jnp.max(logw, axis=1, keepdims=True)
        w4b = (jnp.exp(logw - m_loc) * s4).astype(BF16)
        den_loc = _dot(w4b, jnp.ones((L, LANES), BF16))[:, 0:1]
        num_loc = [_dot(w4b[j * L:(j + 1) * L, :], vs_[j]) for j in range(H)]
        s_rows = [bl_h[j] + d_rows[j] for j in range(H)]
        msrc_h = [jnp.max(s_, axis=1, keepdims=True) for s_ in s_rows]
        g_rows = [jnp.exp(s_rows[j] - msrc_h[j]) for j in range(H)]
        k_t = [ks[j].astype(F32).T for j in range(H)]
        d_c = [_dot((k_t[j] * g_rows[j]).astype(BF16), vs_[j]) for j in range(H)]
        d_n = [_dot(jnp.broadcast_to(g_rows[j], (MOD_ROWS, L)).astype(BF16), ks[j])[0:1, :] for j in range(H)]
        idx = [d * H + j for j in range(H)]
        nrow = [nv[i_] for i_ in idx]
        inter = [_dot(qs[j], ct[idx[j]].astype(BF16)) for j in range(H)]
        qn4 = stack([_dot_nt(qs[j], jnp.broadcast_to(nrow[j], (MOD_ROWS, M_DH)).astype(BF16))[:, 0:1]
                     for j in range(H)])
        m4 = stack([jnp.broadcast_to(mvals[idx[j]], (L, 1)) for j in range(H)])
        bm4 = a4 + m4
        m_t = jnp.maximum(bm4, m_loc)
        e_loc = jnp.exp(m_loc - m_t)
        e_int = jnp.exp(bm4 - m_t)
        den4 = e_loc * den_loc + e_int * qn4
        r4 = 1.0 / jnp.maximum(jnp.abs(den4), jnp.exp(-m_t))
        el = e_loc * r4
        ei = e_int * r4
        rows_out = pl.ds(pl.multiple_of(base + r0, L), L)
        out = list(mvals)
        for j in range(H):
            rs = slice(j * L, (j + 1) * L)
            h_ref[rows_out, j * M_DH:(j + 1) * M_DH] = el[rs] * num_loc[j] + ei[rs] * inter[j]
            m_old = mvals[idx[j]]
            m_new = jnp.maximum(bl_h[j] + m_old, msrc_h[j])
            keep = jnp.exp(bl_h[j] + m_old - m_new)
            scale = jnp.exp(msrc_h[j] - m_new)
            ct[idx[j]] = keep * ct[idx[j]] + scale * d_c[j]
            nv[idx[j]] = keep * nrow[j] + scale * d_n[j]
            out[idx[j]] = m_new
        return tuple(out)

    def body(s, mvals):
        mvals = dir_step(s, 0, mvals)
        return dir_step(cpb - 1 - s, 1, mvals)

    m0 = tuple(ms[q_:q_ + 1, 0:1] for q_ in range(2 * M_HEADS))
    m1 = lax.fori_loop(0, cpb, body, m0, unroll=min(2, cpb))
    for q_ in range(2 * M_HEADS):
        ms[q_:q_ + 1, :] = jnp.broadcast_to(m1[q_], (1, LANES))

    @pl.when(i == nblk - 1)
    def _():
        for blk in range(n // mblk):
            rs = slice(blk * mblk, (blk + 1) * mblk)
            for j in range(M_HEADS):
                hs = slice(j * M_DH, (j + 1) * M_DH)
                hn = _rms(hf[rs, hs] + hb[rs, hs]) * ng_ref[:, hs]
                if blk < nbc:
                    yc_ref[rs, hs] = (hn * _sigmoid(oc_ref[rs, hs].astype(F32))).astype(BF16)
                else:
                    ls = slice(blk * mblk - lc, (blk + 1) * mblk - lc)
                    yl_ref[ls, hs] = (hn * _sigmoid(ol_ref[ls, hs].astype(F32))).astype(BF16)


def _mscan(g, qkc, pm, pg, norm_g):
    mblk = g.mblk
    nbc, nbl = g.lc // mblk, g.t // mblk
    lat_blocks = g.rl // mblk
    n = g.lc + g.t

    def fwd(b, i):
        return jnp.where(i < nbc, lat_blocks + b * nbc + i, b * nbl + (i - nbc))

    def bwd(b, i):
        return jnp.where(i < nbc, lat_blocks + b * nbc + (nbc - 1 - i), b * nbl + (nbl - 1 - (i - nbc)))

    v_col = 2 * BRANCH_W // BRANCH_W
    o_col = 3 * BRANCH_W // BRANCH_W
    ctx0 = g.rl // g.lc

    def side(blk_of):
        return [pl.BlockSpec((mblk, 2 * BRANCH_W), lambda b, i: (blk_of(b, i), 0)),
                pl.BlockSpec((mblk, BRANCH_W), lambda b, i: (blk_of(b, i), v_col)),
                pl.BlockSpec((mblk, LANES), lambda b, i: (blk_of(b, i), PC_GATE_COL // LANES))]

    return pl.pallas_call(
        functools.partial(_mscan_kernel, lc=g.lc, t=g.t, mblk=mblk, chunk=g.chunk),
        grid=(g.b, nbc + nbl),
        in_specs=[
            *side(fwd), *side(bwd),
            pl.BlockSpec((g.t, BRANCH_W), lambda b, i: (b, o_col)),
            pl.BlockSpec((g.lc, BRANCH_W), lambda b, i: (ctx0 + b, o_col)),
            pl.BlockSpec((1, BRANCH_W), lambda b, i: (0, 0)),
        ],
        out_specs=[
            pl.BlockSpec((g.t, BRANCH_W), lambda b, i: (b, 0)),
            pl.BlockSpec((g.lc, BRANCH_W), lambda b, i: (b, 0)),
        ],
        out_shape=[
            jax.ShapeDtypeStruct((g.rl, BRANCH_W), BF16),
            jax.ShapeDtypeStruct((g.rc, BRANCH_W), BF16),
        ],
        scratch_shapes=[
            pltpu.VMEM((mblk, LANES), F32), pltpu.VMEM((mblk, LANES), F32),
            pltpu.VMEM((mblk, LANES), F32), pltpu.VMEM((mblk, LANES), F32),
            pltpu.VMEM((mblk // g.chunk, LANES, g.chunk), F32), pltpu.VMEM((mblk // g.chunk, LANES, g.chunk), F32),
            pltpu.VMEM((n, BRANCH_W), F32), pltpu.VMEM((n, BRANCH_W), F32),
            pltpu.VMEM((2 * M_HEADS, M_DH, M_DH), F32), pltpu.VMEM((2 * M_HEADS, 1, M_DH), F32),
            pltpu.VMEM((2 * M_HEADS, LANES), F32),
        ],
        compiler_params=_cp("parallel", "arbitrary"),
        name="mlstm_scan",
    )(qkc, pm, pg, qkc, pm, pg, pm, pm, norm_g)


def _mla_prep_kernel(cq_ref, ckv_ref, kr_ref, wuq_ref, wukv_ref, cqg_ref, ckvg_ref, qng_ref, qrg_ref,
                     kng_ref, krg_ref, cos_ref, sin_ref, aq_ref, ak_ref, av_ref, wuq_s, wukv_s):
    @pl.when(pl.program_id(0) == 0)
    def _():
        wuq_s[...] = wuq_ref[...].astype(BF16)
        wukv_s[...] = wukv_ref[...].astype(BF16)

    half = LANES // 2
    lane = lax.broadcasted_iota(jnp.int32, (1, LANES), 1)
    lo = lane < half
    first = (lane % half) < (half // 2)

    def rms_half(x):
        x2 = x * x
        s_lo = jnp.sum(jnp.where(lo, x2, 0.0), axis=-1, keepdims=True)
        s_hi = jnp.sum(jnp.where(lo, 0.0, x2), axis=-1, keepdims=True)
        ms = jnp.where(lo, s_lo, s_hi) * (1.0 / half)
        return x * lax.rsqrt(ms + EPS)

    def rows_group(rs):
        cos = cos_ref[rs, :]
        sin = sin_ref[rs, :]

        def rope_half(x):
            partner = jnp.where(first, pltpu.roll(x, LANES - half // 2, 1), pltpu.roll(x, half // 2, 1))
            return x * cos + partner * sin

        cq = (_rms(cq_ref[rs, :]) * cqg_ref[...]).astype(BF16)
        q = _dot(cq, wuq_s[...])
        ckv = (_rms(ckv_ref[rs, :]) * ckvg_ref[...]).astype(BF16)
        kv = _dot(ckv, wukv_s[...])

        krn = rope_half(rms_half(kr_ref[rs, :]) * krg_ref[...]).astype(BF16)

        rope0 = MLA_HEADS * MLA_NOPE
        for hp in range(MLA_HEADS // 2):
            slab = q[:, rope0 + hp * LANES: rope0 + (hp + 1) * LANES]
            r = rope_half(rms_half(slab) * qrg_ref[...])
            parts = (jnp.where(lo, r, 0.0), jnp.where(lo, pltpu.roll(r, half, 1), 0.0))
            for e in range(2):
                h = 2 * hp + e
                nope = _rms(q[:, h * MLA_NOPE:(h + 1) * MLA_NOPE]) * qng_ref[...]
                aq_ref[rs, h * MLA_HW: h * MLA_HW + MLA_NOPE] = nope.astype(BF16)
                aq_ref[rs, h * MLA_HW + MLA_NOPE:(h + 1) * MLA_HW] = parts[e].astype(BF16)
        kvw = MLA_NOPE + MLA_DV
        for h in range(MLA_HEADS):
            kn = _rms(kv[:, h * kvw: h * kvw + MLA_NOPE]) * kng_ref[...]
            ak_ref[rs, h * MLA_HW: h * MLA_HW + MLA_NOPE] = kn.astype(BF16)
            ak_ref[rs, h * MLA_HW + MLA_NOPE:(h + 1) * MLA_HW] = krn
            av_ref[rs, h * MLA_DV:(h + 1) * MLA_DV] = kv[:, h * kvw + MLA_NOPE:(h + 1) * kvw].astype(BF16)

    rows_group(slice(0, cq_ref.shape[0]))


def _mla_prep(g, pc, wuq, wukv, gains, cos_t, sin_t):
    tm = g.tm
    full = lambda shape: pl.BlockSpec(shape, lambda i: (0, 0))
    kr_col = (MLA_Q_RANK + MLA_KV_RANK) // LANES
    return pl.pallas_call(
        _mla_prep_kernel,
        grid=(g.n_tiles,),
        in_specs=[
            pl.BlockSpec((tm, MLA_Q_RANK), lambda i: (i, 0)),
            pl.BlockSpec((tm, MLA_KV_RANK), lambda i: (i, MLA_Q_RANK // MLA_KV_RANK)),
            pl.BlockSpec((tm, LANES), lambda i: (i, kr_col)),
            full(wuq.shape), full(wukv.shape),
            full((1, MLA_Q_RANK)), full((1, MLA_KV_RANK)),
            full((1, LANES)), full((1, LANES)), full((1, LANES)), full((1, LANES)),
            pl.BlockSpec((tm, LANES), lambda i: (g.pos_block(i), 0)),
            pl.BlockSpec((tm, LANES), lambda i: (g.pos_block(i), 0)),
        ],
        out_specs=[
            pl.BlockSpec((tm, MLA_HEADS * MLA_HW), lambda i: (i, 0)),
            pl.BlockSpec((tm, MLA_HEADS * MLA_HW), lambda i: (i, 0)),
            pl.BlockSpec((tm, BRANCH_W), lambda i: (i, 0)),
        ],
        out_shape=[
            jax.ShapeDtypeStruct((g.r, MLA_HEADS * MLA_HW), BF16),
            jax.ShapeDtypeStruct((g.r, MLA_HEADS * MLA_HW), BF16),
            jax.ShapeDtypeStruct((g.r, BRANCH_W), BF16),
        ],
        scratch_shapes=[pltpu.VMEM(wuq.shape, BF16), pltpu.VMEM(wukv.shape, BF16)],
        compiler_params=_cp("arbitrary"),
        name="mla_prep",
    )(pc, pc, pc, wuq, wukv, *gains, cos_t, sin_t)


def _attn_kernel(*refs, n_soft, dh, has_lat, diff, lam_init):
    refs = list(refs)
    vs = refs.pop()
    kts = refs.pop()
    o_ref = refs.pop()
    if diff:
        lam_ref, sg_ref = refs[0], refs[1]
        refs = refs[2:]
    q_ref, kc_ref, vc_ref = refs[:3]
    lc = kc_ref.shape[0]

    @pl.when(pl.program_id(2) == 0)
    def _():
        kts[:, 0:lc] = kc_ref[...].T
        vs[0:lc, :] = vc_ref[...]
        if has_lat:
            kts[:, lc:] = refs[3][...].T
            vs[lc:, :] = refs[4][...]

    q = q_ref[...]
    tq = q.shape[0]
    n_sub = 4 if tq % 64 == 0 else 1
    rsub = tq // n_sub
    outs = []
    for s in range(n_soft):
        sl = slice(s * dh, (s + 1) * dh)
        scs = [_dot(q[u * rsub:(u + 1) * rsub, sl], kts[sl, :]) for u in range(n_sub)]
        ps, ls = [], []
        for sc in scs:
            m = jnp.max(sc, axis=-1, keepdims=True)
            p = jnp.exp(sc - m)
            ls.append(jnp.sum(p, axis=-1, keepdims=True))
            ps.append(p.astype(BF16))
        os_ = [_dot(p, vs[...]) / l for p, l in zip(ps, ls)]
        outs.append(jnp.concatenate(os_, axis=0))
    if diff:
        lp = lam_ref[...]
        lam = (jnp.exp(jnp.sum(lp[0:1] * lp[1:2], axis=-1, keepdims=True))
               - jnp.exp(jnp.sum(lp[2:3] * lp[3:4], axis=-1, keepdims=True)) + lam_init)
        o = outs[0] - lam * outs[1]
        o = _rms(o) * sg_ref[...] * (1.0 - lam_init)
    else:
        o = outs[0]
    o_ref[...] = o.astype(o_ref.dtype)


def _attention(g, q_arr, k_arr, v_arr, *, q_col0, k_col0, heads, qk_w, dv, n_soft, ctx_queries,
               lam=None, subln_g=None, lam_init=0.0):
    diff = lam is not None
    ctx0 = g.rl // g.lc
    n_keys = g.lc if ctx_queries else g.lc + g.t
    if ctx_queries:
        tq = g.lc
        nq = 1
        q_row = lambda b, qi: ctx0 + b
        out_rows = g.rc
        o_row = lambda b, qi: b
    else:
        tq = min(ATT_TQ, g.t)
        nq = g.t // tq
        q_row = lambda b, qi: b * nq + qi
        out_rows = g.rl
        o_row = q_row
    in_specs = []
    args = []
    if diff:
        in_specs += [pl.BlockSpec(lam.shape, lambda b, h, qi: (0, 0)),
                     pl.BlockSpec((1, dv), lambda b, h, qi: (0, 0))]
        args += [lam, subln_g]
    in_specs += [
        pl.BlockSpec((tq, qk_w), lambda b, h, qi: (q_row(b, qi), q_col0 + h)),
        pl.BlockSpec((g.lc, qk_w), lambda b, h, qi: (ctx0 + b, k_col0 + h)),
        pl.BlockSpec((g.lc, dv), lambda b, h, qi: (ctx0 + b, h)),
    ]
    args += [q_arr, k_arr, v_arr]
    if not ctx_queries:
        in_specs += [
            pl.BlockSpec((g.t, qk_w), lambda b, h, qi: (b, k_col0 + h)),
            pl.BlockSpec((g.t, dv), lambda b, h, qi: (b, h)),
        ]
        args += [k_arr, v_arr]
    return pl.pallas_call(
        functools.partial(_attn_kernel, n_soft=n_soft, dh=qk_w // n_soft, has_lat=not ctx_queries,
                          diff=diff, lam_init=lam_init),
        grid=(g.b, heads, nq),
        in_specs=in_specs,
        out_specs=pl.BlockSpec((tq, dv), lambda b, h, qi: (o_row(b, qi), h)),
        out_shape=jax.ShapeDtypeStruct((out_rows, heads * dv), BF16),
        scratch_shapes=[pltpu.VMEM((qk_w, n_keys), BF16), pltpu.VMEM((n_keys, dv), BF16)],
        compiler_params=_cp("parallel", "parallel", "arbitrary"),
        name="attn_diff" if diff else "attn_mla",
    )(*args)


def _branch_kernel(*refs, n_src, n_lat_tiles):
    y_refs = refs[:N_BRANCH * n_src]
    wb_ref, g0_ref, g1_ref, g2_ref, o_ref, wb_s = refs[N_BRANCH * n_src:]
    gates = (g0_ref, g1_ref, g2_ref)

    @pl.when(pl.program_id(1) == 0)
    def _():
        wb_s[...] = wb_ref[...].astype(BF16)

    def body(which):
        def run(_):
            acc = None
            for n in range(N_BRANCH):
                term = gates[n][...].astype(F32) * _dot(y_refs[n * n_src + which][...], wb_s[n])
                acc = term if acc is None else acc + term
            o_ref[...] = acc.astype(o_ref.dtype)
        return run

    i = pl.program_id(1)
    if n_src == 1:
        body(0)(None)
    else:
        pl.when(i < n_lat_tiles)(lambda: body(0)(None))
        pl.when(i >= n_lat_tiles)(lambda: body(1)(None))


def _branch_merge(g, ys, w_branch, l, gt, n_tiles):
    tn = 512
    nj = g.d // tn
    y_specs, y_args = [], []
    for y in ys:
        sp, ar = _row_sources(g, y, BRANCH_W, lambda j: 0, 2)
        y_specs += sp
        y_args += ar
    n_src = len(y_args) // N_BRANCH
    return pl.pallas_call(
        functools.partial(_branch_kernel, n_src=n_src, n_lat_tiles=g.n_lat_tiles),
        grid=(nj, n_tiles),
        in_specs=[
            *y_specs,
            pl.BlockSpec((None, N_BRANCH, BRANCH_W, tn), lambda j, i: (l, 0, 0, j)),
            pl.BlockSpec((g.tm, tn), lambda j, i: (i, j)),
            pl.BlockSpec((g.tm, tn), lambda j, i: (i, nj + j)),
            pl.BlockSpec((g.tm, tn), lambda j, i: (i, 2 * nj + j)),
        ],
        out_specs=pl.BlockSpec((g.tm, tn), lambda j, i: (i, j)),
        out_shape=jax.ShapeDtypeStruct((n_tiles * g.tm, g.d), BF16),
        scratch_shapes=[pltpu.VMEM((N_BRANCH, BRANCH_W, tn), BF16)],
        compiler_params=_cp("arbitrary", "arbitrary"),
        name="branch_merge",
    )(*y_args, w_branch, gt, gt, gt)


def _outproj_kernel(*refs, n_lat_tiles):
    z_ref, w_ref, *x_refs, g_ref, o_ref, w_s = refs

    @pl.when(pl.program_id(1) == 0)
    def _():
        w_s[...] = w_ref[...].astype(BF16)

    def body(x_ref):
        o_ref[...] = x_ref[...] + g_ref[...] * _dot(z_ref[...], w_s[...])

    _on_rows(pl.program_id(1), n_lat_tiles, x_refs, body)


def _outproj(g, z, w_out, l, x_src, mods3, which_gate, n_tiles):
    tn = 512
    per = g.d // tn
    x_specs, x_args = _row_sources(g, x_src, tn, lambda j: j, 2)
    return pl.pallas_call(
        functools.partial(_outproj_kernel, n_lat_tiles=g.n_lat_tiles),
        grid=(per, n_tiles),
        in_specs=[
            pl.BlockSpec((g.tm, g.d), lambda j, i: (i, 0)),
            pl.BlockSpec((None, g.d, tn), lambda j, i: (l, 0, j)),
            *x_specs,
            pl.BlockSpec((None, 1, tn), lambda j, i: (g.mod_row(i), 0, which_gate * per + j)),
        ],
        out_specs=pl.BlockSpec((g.tm, tn), lambda j, i: (i, j)),
        out_shape=jax.ShapeDtypeStruct((n_tiles * g.tm, g.d), F32),
        scratch_shapes=[pltpu.VMEM((g.d, tn), BF16)],
        compiler_params=_cp("arbitrary", "arbitrary"),
        name="outproj",
    )(z, w_out, *x_args, mods3)


def _moe_prep_kernel(x_ref, sh_ref, sc_ref, rwt_ref, rb_ref, tri_ref, xr_ref, rank_ref, grp_ref, cnt_ref,
                     carry, *, mod_row):
    i = pl.program_id(0)

    @pl.when(i == 0)
    def _():
        carry[...] = jnp.zeros_like(carry)

    x = x_ref[...]
    h = _rms(x) * (1.0 + sc_ref[...]) + sh_ref[...]
    logits = _dot_nt(rwt_ref[...].astype(BF16), h.astype(BF16))
    scores = _sigmoid(logits)
    selm = scores + rb_ref[...]
    sc_rows = [scores[e:e + 1, :] for e in range(N_EXPERTS)]
    sel = [selm[e:e + 1, :] for e in range(N_EXPERTS)]
    gscore = []
    for gi in range(N_GROUPS):
        v = sel[gi * GROUP_SIZE:(gi + 1) * GROUP_SIZE]
        pair = [v[a] + v[b] for a in range(GROUP_SIZE) for b in range(a + 1, GROUP_SIZE)]
        gscore.append(functools.reduce(jnp.maximum, pair))
    gmax = functools.reduce(jnp.maximum, gscore)
    taken = jnp.zeros_like(gmax, dtype=jnp.bool_)
    in_group = []
    for gi in range(N_GROUPS):
        hit = jnp.logical_and(gscore[gi] == gmax, jnp.logical_not(taken))
        taken = jnp.logical_or(taken, hit)
        in_group.append(hit)
    neg = -jnp.inf
    masked = [jnp.where(in_group[e // GROUP_SIZE], sel[e], neg) for e in range(N_EXPERTS)]

    def pick(vals):
        mx = functools.reduce(jnp.maximum, vals)
        seen = jnp.zeros_like(mx, dtype=jnp.bool_)
        hot = []
        for v in vals:
            hit = jnp.logical_and(v == mx, jnp.logical_not(seen))
            seen = jnp.logical_or(seen, hit)
            hot.append(hit)
        return hot

    hot1 = pick(masked)
    hot2 = pick([jnp.where(hot1[e], neg, masked[e]) for e in range(N_EXPERTS)])
    zero = jnp.zeros_like(gmax)
    s1 = functools.reduce(lambda a, b: a + b, [jnp.where(hot1[e], sc_rows[e], zero) for e in range(N_EXPERTS)])
    s2 = functools.reduce(lambda a, b: a + b, [jnp.where(hot2[e], sc_rows[e], zero) for e in range(N_EXPERTS)])
    tot = s1 + s2
    w1 = s1 / tot
    w2 = s2 / tot
    comb = [jnp.where(hot1[e], w1, zero) + jnp.where(hot2[e], w2, zero) for e in range(N_EXPERTS)]

    tm = x.shape[0]
    gmat = jnp.concatenate([jnp.where(in_group[gi], 1.0, zero) for gi in range(N_GROUPS)]
                           + [jnp.zeros((MOD_ROWS - N_GROUPS, tm), F32)], axis=0)
    incl = _dot(gmat.astype(BF16), tri_ref[...])
    excl = incl - gmat + carry[:, 0:1]
    rank = functools.reduce(lambda a, b: a + b,
                            [jnp.where(in_group[gi], excl[gi:gi + 1, :], zero) for gi in range(N_GROUPS)])
    grp = functools.reduce(lambda a, b: a + b,
                           [jnp.where(in_group[gi], float(gi), zero) for gi in range(N_GROUPS)])
    rank_ref[...] = rank.astype(jnp.int32)
    grp_ref[...] = grp.astype(jnp.int32)
    carry[...] = carry[...] + jnp.sum(gmat, axis=1, keepdims=True)
    cnt_ref[...] = carry[...]

    side = jnp.concatenate(comb + [jnp.zeros((LANES - N_EXPERTS, tm), F32)], axis=0).T
    lane = lax.broadcasted_iota(jnp.int32, (1, LANES), 1)
    side = jnp.where(lane == N_EXPERTS + mod_row(i), 1.0, side)
    d = x.shape[1]
    xr_ref[:, 0:d] = x
    xr_ref[:, d:d + LANES] = side


def _moe_prep(g, x_mid, mods3, which_shift, which_scale, rwt, rb, tri, n_tiles):
    n_rows = n_tiles * g.tm
    return pl.pallas_call(
        functools.partial(_moe_prep_kernel, mod_row=g.mod_row),
        grid=(n_tiles,),
        in_specs=[
            pl.BlockSpec((g.tm, g.d), lambda i: (i, 0)),
            _mod_spec(g, which_shift),
            _mod_spec(g, which_scale),
            pl.BlockSpec((N_EXPERTS, g.d), lambda i: (0, 0)),
            pl.BlockSpec((N_EXPERTS, 1), lambda i: (0, 0)),
            pl.BlockSpec((g.tm, g.tm), lambda i: (0, 0)),
        ],
        out_specs=[
            pl.BlockSpec((g.tm, g.d + LANES), lambda i: (i, 0)),
            pl.BlockSpec((1, g.tm), lambda i: (0, i)),
            pl.BlockSpec((1, g.tm), lambda i: (0, i)),
            pl.BlockSpec((MOD_ROWS, LANES), lambda i: (0, 0)),
        ],
        out_shape=[
            jax.ShapeDtypeStruct((n_rows, g.d + LANES), F32),
            jax.ShapeDtypeStruct((1, n_rows), jnp.int32),
            jax.ShapeDtypeStruct((1, n_rows), jnp.int32),
            jax.ShapeDtypeStruct((MOD_ROWS, LANES), F32),
        ],
        scratch_shapes=[pltpu.VMEM((MOD_ROWS, LANES), F32)],
        compiler_params=_cp("arbitrary"),
        name="moe_prep",
    )(x_mid, mods3, mods3, rwt, rb, tri)


def _moe_group_kernel(tg_ref, nv_ref, nu_ref, pos_ref, xr_hbm, mods_ref, w1_hbm, w3_hbm, w2_hbm, out_hbm,
                      gbuf, obuf, w1s, w3s, w2s, stg_a, stg_b, src_ref, gsem, ssem, wsem, *, n_mod_rows, e_base):
    k = pl.program_id(0)
    n_used = nu_ref[0]
    slot = k % 2
    d = obuf.shape[2]
    tme = obuf.shape[1]

    def gather_start(kk, sl):
        base = kk * tme

        first = src_ref[base]

        def copy_row(r, tok):
            pltpu.make_async_copy(xr_hbm.at[pl.ds(tok, 1), :], gbuf.at[sl, pl.ds(r, 1), :], gsem.at[sl]).start()

        def issue(r, c):
            copy_row(r, src_ref[base + r])
            return c

        def issue_pad(r, c):
            copy_row(r, first)
            return c

        rows_of(kk, issue)
        lax.fori_loop(nv_ref[kk], tme, issue_pad, 0)

    def build_sorted_index():
        def put(t, c):
            src_ref[pos_ref[t]] = t
            return c

        lax.fori_loop(0, pos_ref.shape[0], put, 0, unroll=8)

    def gather_wait(sl):
        def w(r, c):
            pltpu.make_async_copy(xr_hbm.at[pl.ds(0, 1), :], gbuf.at[sl, pl.ds(0, 1), :], gsem.at[sl]).wait()
            return c

        lax.fori_loop(0, tme, w, 0, unroll=8)

    def scatter_start(kk, sl):
        base = kk * tme

        def issue(r, c):
            tok = src_ref[base + r]
            pltpu.make_async_copy(obuf.at[sl, pl.ds(r, 1), :], out_hbm.at[pl.ds(tok, 1), :], ssem.at[sl]).start()
            return c

        rows_of(kk, issue)

    def scatter_wait(kk, sl):
        def w(r, c):
            pltpu.make_async_copy(obuf.at[sl, pl.ds(0, 1), :], out_hbm.at[pl.ds(0, 1), :], ssem.at[sl]).wait()
            return c

        rows_of(kk, w)

    def rows_of(kk, fn):
        n_rows = nv_ref[kk]

        @pl.when(n_rows == tme)
        def _():
            lax.fori_loop(0, tme, fn, 0, unroll=8)

        @pl.when(n_rows != tme)
        def _():
            lax.fori_loop(0, n_rows, fn, 0)

    def load_weights(grp):
        rows_a = stg_a.shape[1]
        rows_b = stg_b.shape[1]
        chunks = []
        n_a = n_b = 0
        for e in range(GROUP_SIZE):
            ex = e_base + grp * GROUP_SIZE + e
            for src, dst in ((w1_hbm, w1s), (w3_hbm, w3s)):
                for c0 in range(0, src.shape[1], rows_a):
                    sl = n_a % 2
                    n_a += 1
                    cp = pltpu.make_async_copy(src.at[ex, pl.ds(c0, rows_a), :], stg_a.at[sl], wsem.at[sl])
                    chunks.append((cp, stg_a.at[sl], dst, e, c0, rows_a))
            for c0 in range(0, w2_hbm.shape[1], rows_b):
                sl = n_b % 2
                n_b += 1
                cp = pltpu.make_async_copy(w2_hbm.at[ex, pl.ds(c0, rows_b), :], stg_b.at[sl], wsem.at[2 + sl])
                chunks.append((cp, stg_b.at[sl], w2s, e, c0, rows_b))
        chunks[0][0].start()
        for idx, (cp, stg, dst, e, c0, rows) in enumerate(chunks):
            if idx + 1 < len(chunks):
                chunks[idx + 1][0].start()
            cp.wait()
            dst[e, c0:c0 + rows, :] = stg[...].astype(BF16)

    @pl.when(k < n_used)
    def _():
        grp = tg_ref[k]

        @pl.when(k == 0)
        def _():
            build_sorted_index()
            gather_start(0, 0)

        @pl.when(k + 1 < n_used)
        def _():
            gather_start(k + 1, 1 - slot)

        @pl.when(jnp.logical_or(k == 0, grp != tg_ref[jnp.maximum(k - 1, 0)]))
        def _():
            load_weights(grp)

        gather_wait(slot)

        @pl.when(k >= 2)
        def _():
            scatter_wait(k - 2, slot)

        lane = lax.broadcasted_iota(jnp.int32, (1, LANES), 1)
        x = gbuf[slot, :, 0:d]
        side = gbuf[slot, :, d:d + LANES]

        def cond_vec(which):
            out = jnp.broadcast_to(mods_ref[0:1, which * d:(which + 1) * d], (tme, d))
            for r in range(1, n_mod_rows):
                hot = side[:, N_EXPERTS + r:N_EXPERTS + r + 1] > 0.5
                out = jnp.where(hot, mods_ref[r:r + 1, which * d:(which + 1) * d], out)
            return out

        h = (_rms(x) * (1.0 + cond_vec(1)) + cond_vec(0)).astype(BF16)
        y = jnp.zeros((tme, d), F32)
        for e in range(GROUP_SIZE):
            a = _dot(h, w1s[e])
            b = _dot(h, w3s[e])
            cw = jnp.sum(jnp.where(lane == grp * GROUP_SIZE + e, side, 0.0), axis=1, keepdims=True)
            hid = (a * _sigmoid(a) * b * cw).astype(BF16)
            y = y + _dot(hid, w2s[e])
        obuf[slot] = x + cond_vec(2) * y

        scatter_start(k, slot)

        @pl.when(k == n_used - 1)
        def _():
            @pl.when(k >= 1)
            def _():
                scatter_wait(k - 1, 1 - slot)

            scatter_wait(k, slot)


def _moe_group(g, xr, mods_tail, w1, w3, w2, meta, n_rows, n_mod_rows, e_base):
    kt = meta[0].shape[0]
    tme = MOE_TM
    any_spec = pl.BlockSpec(memory_space=pl.ANY)
    grid_spec = pltpu.PrefetchScalarGridSpec(
        num_scalar_prefetch=len(meta),
        grid=(kt,),
        in_specs=[any_spec, pl.BlockSpec(mods_tail.shape, lambda k, *_: (0, 0)), any_spec, any_spec, any_spec],
        out_specs=any_spec,
        scratch_shapes=[
            pltpu.VMEM((2, tme, g.d + LANES), F32),
            pltpu.VMEM((2, tme, g.d), F32),
            pltpu.VMEM((GROUP_SIZE, g.d, D_EXPERT), BF16),
            pltpu.VMEM((GROUP_SIZE, g.d, D_EXPERT), BF16),
            pltpu.VMEM((GROUP_SIZE, D_EXPERT, g.d), BF16),
            pltpu.VMEM((2, g.d // 2, D_EXPERT), F32),
            pltpu.VMEM((2, D_EXPERT // 2, g.d), F32),
            pltpu.SMEM((kt * tme,), jnp.int32),
            pltpu.SemaphoreType.DMA((2,)),
            pltpu.SemaphoreType.DMA((2,)),
            pltpu.SemaphoreType.DMA((4,)),
        ],
    )
    return pl.pallas_call(
        functools.partial(_moe_group_kernel, n_mod_rows=n_mod_rows, e_base=e_base),
        grid_spec=grid_spec,
        out_shape=jax.ShapeDtypeStruct((n_rows, g.d), F32),
        compiler_params=_cp("arbitrary"),
        name="moe_group",
    )(*meta, xr, mods_tail, w1, w3, w2)


def _moe_meta(rank, grp, cnt, n_rows):
    tme = MOE_TM
    kt = n_rows // tme + N_GROUPS
    counts = cnt[:N_GROUPS, 0].astype(jnp.int32)
    ntile = (counts + tme - 1) // tme
    tile_end = jnp.cumsum(ntile)
    tile_off = tile_end - ntile
    gsel = [grp[0] == gi for gi in range(N_GROUPS)]
    row_off = functools.reduce(lambda a, b: a + b,
                               [jnp.where(gsel[gi], tile_off[gi] * tme, 0) for gi in range(N_GROUPS)])
    pos = row_off + rank[0]
    ks = jnp.arange(kt, dtype=jnp.int32)
    tile_group = jnp.minimum(functools.reduce(lambda a, b: a + b,
                                              [(ks >= tile_end[gi]).astype(jnp.int32) for gi in range(N_GROUPS)]),
                             N_GROUPS - 1)
    n_used = tile_end[-1:]
    cnt_k = functools.reduce(lambda a, b: a + b,
                             [jnp.where(tile_group == gi, counts[gi] - (ks - tile_off[gi]) * tme, 0)
                              for gi in range(N_GROUPS)])
    n_valid = jnp.where(ks < n_used[0], jnp.clip(cnt_k, 0, tme), 0)
    return tile_group, n_valid, n_used, pos


def _rope_tables(t, rot_dim, tile_rows, reps):
    f32 = np.float32
    rows = t // GRID_W
    r = np.repeat(np.arange(rows, dtype=f32), GRID_W)
    col = np.tile(np.arange(GRID_W, dtype=f32), rows)
    n_freq = rot_dim // 4
    inv = np.power(f32(ROPE_BASE), -np.arange(n_freq, dtype=f32) / f32(n_freq)).astype(f32)
    ang = np.concatenate([r[:, None] * inv, col[:, None] * inv], axis=-1).astype(f32)
    cos = np.tile(np.concatenate([np.cos(ang), np.cos(ang)], axis=-1), (1, reps))
    sin = np.tile(np.concatenate([-np.sin(ang), np.sin(ang)], axis=-1), (1, reps))
    cos = np.concatenate([cos, np.ones((tile_rows, LANES), f32)], axis=0).astype(f32)
    sin = np.concatenate([sin, np.zeros((tile_rows, LANES), f32)], axis=0).astype(f32)
    return jnp.asarray(cos), jnp.asarray(sin)


def _uq_perm():
    nope = [h * MLA_QK + i for h in range(MLA_HEADS) for i in range(MLA_NOPE)]
    rope = [h * MLA_QK + MLA_NOPE + i for h in range(MLA_HEADS) for i in range(MLA_ROPE)]
    return jnp.asarray(nope + rope)


def kernel(x, c, ctx, c_ctx, w_ada, b_ada, w_in, b_in, m_conv_w, m_conv_b, m_norm_g, da_q_norm_g, da_k_norm_g,
           da_lambda, da_subln_g, mla_cq_norm_g, mla_ckv_norm_g, mla_w_uq, mla_w_ukv, mla_q_norm_g, mla_k_norm_g,
           w_branch, w_out, moe_w1, moe_w3, moe_w2, router_w, router_bias):
    bsz, t, d = x.shape
    lc = ctx.shape[1]
    depth = w_ada.shape[0]
    g = _Geom(bsz, t, lc, d, min(1024, bsz * lc, t))
    assert bsz + 1 <= MOD_ROWS

    cvec = jnp.concatenate([c, c_ctx[None, :], jnp.zeros((MOD_ROWS - bsz - 1, d), F32)], axis=0)
    mods = _mod_vectors(cvec, w_ada, b_ada)

    cos_d, sin_d = _rope_tables(t, DA_DH, g.tm, 1)
    cos_a, sin_a = _rope_tables(t, MLA_ROPE, g.tm, 2)
    rwt = router_w.T
    rb = router_bias.reshape(N_EXPERTS, 1)
    uq_perm = _uq_perm()
    x_all = (x.reshape(g.rl, d), ctx.reshape(g.rc, d))
    pos_spec = pl.BlockSpec((g.tm, LANES), lambda j, i: (g.pos_block(i), 0))
    b_in3 = b_in.reshape(depth, 1, -1)
    w_in_t = jnp.swapaxes(w_in, 1, 2)
    qk_scale = jnp.concatenate([jnp.full((1, BRANCH_W), M_DH ** -0.5, F32), jnp.ones((1, BRANCH_W), F32)], axis=1)
    tri = (jnp.arange(g.tm)[:, None] <= jnp.arange(g.tm)[None, :]).astype(BF16)
    w1r = moe_w1.reshape(depth * N_EXPERTS, d, D_EXPERT)
    w3r = moe_w3.reshape(depth * N_EXPERTS, d, D_EXPERT)
    w2r = moe_w2.reshape(depth * N_EXPERTS, D_EXPERT, d)

    for l in range(depth):
        last = l == depth - 1
        n_tiles = g.n_lat_tiles if last else g.n_tiles
        n_rows = n_tiles * g.tm
        lam_init = 0.8 - 0.6 * math.exp(-0.3 * l)
        mods3 = mods[l].reshape(MOD_ROWS, 1, 6 * d)

        h1 = _prenorm(g, x_all, mods3, 0, 1, g.n_tiles)

        row = lambda v: v.reshape(1, -1)
        proj = functools.partial(_proj, g, h=h1, w_in_t=w_in_t, b_in3=b_in3, l=l)
        pm = proj(_epi_raw, col0=OFF_MQK, width=OFF_MG - OFF_MQK, out_dtype=BF16, tn=1024, name="proj_mlstm")
        da_gain = jnp.concatenate([jnp.tile(da_q_norm_g[l] * DA_DH ** -0.5, 2 * DA_HEADS),
                                   jnp.tile(da_k_norm_g[l], 2 * DA_HEADS)])
        dqk = proj(_epi_normrope, col0=OFF_DQ, width=2 * DA_QK_W, out_dtype=BF16,
                   extra=(row(da_gain), cos_d, sin_d),
                   extra_specs=(pl.BlockSpec((1, 512), lambda j, i: (0, j)), pos_spec, pos_spec),
                   name="proj_dqk")
        dv = proj(_epi_raw, col0=OFF_DV, width=BRANCH_W, out_dtype=BF16, name="proj_dv")
        pc = proj(_epi_raw, col0=OFF_CQ, width=OFF_G - OFF_CQ, out_dtype=F32, patch=(OFF_MG, PC_GATE_COL),
                  name="proj_small")
        gt = proj(_epi_sigmoid, col0=OFF_G, width=N_BRANCH * d, out_dtype=BF16, name="proj_gates")

        qkc = _mconv(g, pm, m_conv_w[l], row(m_conv_b[l]), qk_scale)
        ym_l, ym_c = _mscan(g, qkc, pm, pc, row(m_norm_g[l]))

        qg, kg = mla_q_norm_g[l], mla_k_norm_g[l]
        a_scale = MLA_QK ** -0.5
        gains = (row(mla_cq_norm_g[l]), row(mla_ckv_norm_g[l]),
                 row(qg[:MLA_NOPE] * a_scale), row(jnp.tile(qg[MLA_NOPE:], 2) * a_scale),
                 row(kg[:MLA_NOPE]), row(jnp.concatenate([kg[MLA_NOPE:], jnp.zeros((LANES - MLA_ROPE,), F32)])))
        aq, ak, av = _mla_prep(g, pc, mla_w_uq[l][:, uq_perm], mla_w_ukv[l], gains, cos_a, sin_a)

        lam_p = da_lambda[l]
        sub_g = row(da_subln_g[l])
        da_kw = dict(q_col0=0, k_col0=DA_HEADS, heads=DA_HEADS, qk_w=2 * DA_DH, dv=DA_DV, n_soft=2,
                     lam=lam_p, subln_g=sub_g, lam_init=lam_init)
        mla_kw = dict(q_col0=0, k_col0=0, heads=MLA_HEADS, qk_w=MLA_HW, dv=MLA_DV, n_soft=1)
        yd_l = _attention(g, dqk, dqk, dv, ctx_queries=False, **da_kw)
        ya_l = _attention(g, aq, ak, av, ctx_queries=False, **mla_kw)
        if last:
            ys = (ym_l, yd_l, ya_l)
        else:
            yd_c = _attention(g, dqk, dqk, dv, ctx_queries=True, **da_kw)
            ya_c = _attention(g, aq, ak, av, ctx_queries=True, **mla_kw)
            ys = ((ym_l, ym_c), (yd_l, yd_c), (ya_l, ya_c))

        z = _branch_merge(g, ys, w_branch, l, gt, n_tiles)
        x_mid = _outproj(g, z, w_out, l, x_all, mods3, 2, n_tiles)

        xr, rank, grp, cnt = _moe_prep(g, x_mid, mods3, 3, 4, rwt, rb, tri, n_tiles)
        meta = _moe_meta(rank, grp, cnt, n_rows)
        n_cond = bsz if last else bsz + 1
        x_all = _moe_group(g, xr, mods[l][:, 3 * d:], w1r, w3r, w2r, meta, n_rows, n_cond, l * N_EXPERTS)

    return x_all.reshape(bsz, t, d)
```

```python
import functools
import math

import jax
import jax.numpy as jnp
import numpy as np
from jax import lax
from jax.experimental import pallas as pl
from jax.experimental.pallas import tpu as pltpu

F32 = jnp.float32
BF16 = jnp.bfloat16

GRID_W = 64
ROPE_BASE = 10000.0
EPS = 1e-6
N_BRANCH = 3
BRANCH_W = 1024
M_HEADS = 4
M_DH = BRANCH_W // M_HEADS
M_CHUNK = 256
M_CONV = 5
DA_HEADS = 4
DA_DV = BRANCH_W // DA_HEADS
DA_DH = DA_DV // 2
DA_QK_W = DA_HEADS * 2 * DA_DH
MLA_HEADS = 8
MLA_Q_RANK = 512
MLA_KV_RANK = 256
MLA_NOPE = 128
MLA_ROPE = 64
MLA_DV = BRANCH_W // MLA_HEADS
MLA_QK = MLA_NOPE + MLA_ROPE
N_EXPERTS = 16
N_GROUPS = 4
GROUP_SIZE = N_EXPERTS // N_GROUPS
D_EXPERT = 512

OFF_MQK = 0
OFF_MV = OFF_MQK + 2 * BRANCH_W
OFF_MO = OFF_MV + BRANCH_W
OFF_MG = OFF_MO + BRANCH_W
OFF_DQ = OFF_MG + 4 * M_HEADS
OFF_DK = OFF_DQ + DA_QK_W
OFF_DV = OFF_DK + DA_QK_W
OFF_CQ = OFF_DV + BRANCH_W
OFF_CKV = OFF_CQ + MLA_Q_RANK
OFF_KR = OFF_CKV + MLA_KV_RANK
OFF_G = OFF_KR + MLA_ROPE

LANES = 128
MOD_ROWS = 8
VMEM_LIMIT = 56 * 1024 * 1024
MLA_HW = 256
MOE_TM = 256
ATT_TQ = 2048
PC_GATE_COL = 896
PROJ_ROW_GROUPS = 4


def _cp(*sem):
    return pltpu.CompilerParams(dimension_semantics=sem, vmem_limit_bytes=VMEM_LIMIT)


def _rms(x):
    return x * lax.rsqrt(jnp.mean(x * x, axis=-1, keepdims=True) + EPS)


def _sigmoid(x):
    return 0.5 * jnp.tanh(0.5 * x) + 0.5


def _dot(a, b):
    return jnp.dot(a, b, preferred_element_type=F32)


def _dot_nt(a, b):
    return lax.dot_general(a, b, (((1,), (1,)), ((), ())), preferred_element_type=F32)


def _mod_kernel(c_ref, w_ref, b_ref, o_ref):
    c = c_ref[...]
    s = (c * _sigmoid(c)).astype(BF16)
    o_ref[...] = _dot(s, w_ref[...].astype(BF16)) + b_ref[...]


def _mod_vectors(cvec, w_ada, b_ada):
    depth, d, n = w_ada.shape
    tn = 1024
    return pl.pallas_call(
        _mod_kernel,
        grid=(depth, n // tn),
        in_specs=[
            pl.BlockSpec((MOD_ROWS, d), lambda l, j: (0, 0)),
            pl.BlockSpec((None, d, tn), lambda l, j: (l, 0, j)),
            pl.BlockSpec((None, 1, tn), lambda l, j: (l, 0, j)),
        ],
        out_specs=pl.BlockSpec((None, MOD_ROWS, tn), lambda l, j: (l, 0, j)),
        out_shape=jax.ShapeDtypeStruct((depth, MOD_ROWS, n), F32),
        compiler_params=_cp("parallel", "parallel"),
        name="adaln_mod",
    )(cvec, w_ada, b_ada.reshape(depth, 1, n))


class _Geom:
    def __init__(self, b, t, lc, d, tm):
        assert t % tm == 0 and (b * lc) % tm == 0 and t % lc == 0 and t % GRID_W == 0
        self.b, self.t, self.lc, self.d, self.tm = b, t, lc, d, tm
        self.rl, self.rc = b * t, b * lc
        self.r = self.rl + self.rc
        self.mblk = min(256, lc)
        self.chunk = min(M_CHUNK, self.mblk)
        assert lc % self.mblk == 0 and t % self.mblk == 0 and self.mblk % self.chunk == 0
        self.n_lat_tiles = self.rl // tm
        self.n_tiles = self.r // tm
        self.tiles_per_seq = t // tm

    def mod_row(self, i):
        return jnp.minimum(i // self.tiles_per_seq, self.b)

    def pos_block(self, i):
        return jnp.where(i < self.n_lat_tiles, i % self.tiles_per_seq, self.tiles_per_seq)


def _mod_spec(g, which):
    return pl.BlockSpec((None, 1, g.d), lambda i: (g.mod_row(i), 0, which))


def _row_sources(g, src, width, col_of, n_grid):
    tile = lambda i: i
    if n_grid == 2:
        spec = lambda rows_of: pl.BlockSpec((g.tm, width), lambda j, i: (rows_of(i), col_of(j)))
    else:
        spec = lambda rows_of: pl.BlockSpec((g.tm, width), lambda i: (rows_of(i), 0))
    if not isinstance(src, tuple):
        return [spec(tile)], [src]
    lat, ctx = src
    nl = g.n_lat_tiles
    return ([spec(lambda i: jnp.minimum(i, nl - 1)), spec(lambda i: jnp.maximum(i - nl, 0))], [lat, ctx])


def _on_rows(i, n_lat_tiles, refs, body):
    if len(refs) == 1:
        body(refs[0])
        return

    @pl.when(i < n_lat_tiles)
    def _():
        body(refs[0])

    @pl.when(i >= n_lat_tiles)
    def _():
        body(refs[1])


def _prenorm_kernel(*refs, n_lat_tiles):
    *x_refs, sh_ref, sc_ref, o_ref = refs

    def body(x_ref):
        o_ref[...] = (_rms(x_ref[...]) * (1.0 + sc_ref[...]) + sh_ref[...]).astype(o_ref.dtype)

    _on_rows(pl.program_id(0), n_lat_tiles, x_refs, body)


def _prenorm(g, x_src, mods3, which_shift, which_scale, n_tiles):
    x_specs, x_args = _row_sources(g, x_src, g.d, None, 1)
    return pl.pallas_call(
        functools.partial(_prenorm_kernel, n_lat_tiles=g.n_lat_tiles),
        grid=(n_tiles,),
        in_specs=[*x_specs, _mod_spec(g, which_shift), _mod_spec(g, which_scale)],
        out_specs=pl.BlockSpec((g.tm, g.d), lambda i: (i, 0)),
        out_shape=jax.ShapeDtypeStruct((n_tiles * g.tm, g.d), BF16),
        compiler_params=_cp("parallel"),
        name="prenorm",
    )(*x_args, mods3, mods3)


def _epi_raw(acc, rs, o_ref):
    o_ref[rs, :] = acc.astype(o_ref.dtype)


def _epi_sigmoid(acc, rs, o_ref):
    o_ref[rs, :] = _sigmoid(acc).astype(o_ref.dtype)


def _epi_normrope(acc, rs, gain_ref, cos_ref, sin_ref, o_ref):
    cos = cos_ref[rs, :]
    sin = sin_ref[rs, :]
    for j in range(acc.shape[1] // LANES):
        sl = slice(j * LANES, (j + 1) * LANES)
        x = _rms(acc[:, sl]) * gain_ref[:, sl]
        o_ref[rs, sl] = (x * cos + pltpu.roll(x, LANES // 2, 1) * sin).astype(o_ref.dtype)


def _proj_kernel(*refs, shift, epilogue, patch):
    if shift:
        h_ref, wa_ref, wb_ref, ba_ref, bb_ref, *rest = refs
    else:
        h_ref, wa_ref, ba_ref, *rest = refs
    if patch is not None:
        wp_ref, bp_ref, *rest = rest
    *extra, o_ref, w_s, b_s = rest
    tn = w_s.shape[0]

    @pl.when(pl.program_id(1) == 0)
    def _():
        if shift:
            w_s[0:tn - shift, :] = wa_ref[shift:tn, :].astype(BF16)
            w_s[tn - shift:tn, :] = wb_ref[0:shift, :].astype(BF16)
            bb = jnp.concatenate([ba_ref[...], bb_ref[...]], axis=1)
            bb = jnp.broadcast_to(bb, (b_s.shape[0], 2 * tn))
            b_s[...] = pltpu.roll(bb, 2 * tn - shift, 1)[:, :tn]
        else:
            w_s[...] = wa_ref[...].astype(BF16)
            b_s[...] = jnp.broadcast_to(ba_ref[...], b_s.shape)
        if patch is not None:
            p_tile, p_off = patch

            @pl.when(pl.program_id(0) == p_tile)
            def _():
                w_s[p_off:p_off + LANES, :] = wp_ref[...].astype(BF16)
                b_s[:, p_off:p_off + LANES] = jnp.broadcast_to(bp_ref[...], (b_s.shape[0], LANES))

    rows = h_ref.shape[0] // PROJ_ROW_GROUPS
    for u in range(PROJ_ROW_GROUPS):
        rs = slice(u * rows, (u + 1) * rows)
        acc = _dot_nt(h_ref[rs, :], w_s[...]) + b_s[0:1, :]
        epilogue(acc, rs, *extra, o_ref)


def _proj(g, epilogue, h, w_in_t, b_in3, l, col0, width, out_dtype, *, tn=512, out_tn=None,
          extra=(), extra_specs=(), patch=None, name="proj"):
    ncols, kdim = w_in_t.shape[1], w_in_t.shape[2]
    blk0 = col0 // tn
    shift = col0 - blk0 * tn
    assert shift % 16 == 0
    nj = pl.cdiv(width, tn)
    last_blk = pl.cdiv(ncols, tn) - 1
    out_tn = tn if out_tn is None else out_tn
    w_specs = [pl.BlockSpec((None, tn, kdim), lambda j, i: (l, blk0 + j, 0))]
    b_specs = [pl.BlockSpec((None, 1, tn), lambda j, i: (l, 0, blk0 + j))]
    if shift:
        nxt = lambda j: jnp.minimum(blk0 + j + 1, last_blk)
        w_specs.append(pl.BlockSpec((None, tn, kdim), lambda j, i: (l, nxt(j), 0)))
        b_specs.append(pl.BlockSpec((None, 1, tn), lambda j, i: (l, 0, nxt(j))))
    n_w = len(w_specs)
    p_specs, p_args, k_patch = [], [], None
    if patch is not None:
        src_col, dst_col = patch
        assert src_col % LANES == 0 and dst_col % LANES == 0 and dst_col >= width
        p_specs = [pl.BlockSpec((None, LANES, kdim), lambda j, i: (l, src_col // LANES, 0)),
                   pl.BlockSpec((None, 1, LANES), lambda j, i: (l, 0, src_col // LANES))]
        p_args = [w_in_t, b_in3]
        k_patch = (dst_col // tn, dst_col % tn)
    return pl.pallas_call(
        functools.partial(_proj_kernel, shift=shift, epilogue=epilogue, patch=k_patch),
        grid=(nj, g.n_tiles),
        in_specs=[pl.BlockSpec((g.tm, kdim), lambda j, i: (i, 0)), *w_specs, *b_specs, *p_specs, *extra_specs],
        out_specs=pl.BlockSpec((g.tm, out_tn), lambda j, i: (i, j)),
        out_shape=jax.ShapeDtypeStruct((g.r, nj * out_tn), out_dtype),
        scratch_shapes=[pltpu.VMEM((tn, kdim), BF16), pltpu.VMEM((MOD_ROWS, tn), F32)],
        compiler_params=_cp("arbitrary", "arbitrary"),
        name=name,
    )(h, *([w_in_t] * n_w), *([b_in3] * n_w), *p_args, *extra)


def _mconv_kernel(x_ref, prev_ref, next_ref, w_ref, b_ref, s_ref, sh_ref, o_ref, *, blocks_per_seq, blocks_per_ctx,
                  n_lat_blocks):
    i = pl.program_id(0)
    rows = x_ref.shape[0]
    halo = prev_ref.shape[0]
    edge = MOD_ROWS
    in_lat = i < n_lat_blocks
    pos = jnp.where(in_lat, i % blocks_per_seq, (i - n_lat_blocks) % blocks_per_ctx)
    per_seq = jnp.where(in_lat, blocks_per_seq, blocks_per_ctx)
    x = x_ref[...]
    w = [w_ref[j:j + 1, :] for j in range(M_CONV)]
    acc = x.astype(F32) * w[M_CONV // 2] + b_ref[...]
    taps = [j for j in range(M_CONV) if j != M_CONV // 2]
    for n_, j in enumerate(taps):
        acc = acc + _dot(sh_ref[n_], x) * w[j]
    p = jnp.where(pos != 0, prev_ref[...].astype(F32), 0.0)
    nx = jnp.where(pos != per_seq - 1, next_ref[...].astype(F32), 0.0)
    r8 = lax.broadcasted_iota(jnp.int32, (edge, 1), 0)
    top = (jnp.where(r8 < 2, pltpu.roll(p, 2, 0)[0:edge], 0.0) * w[0]
           + jnp.where(r8 < 1, pltpu.roll(p, 1, 0)[0:edge], 0.0) * w[1])
    bot = (jnp.where(r8 >= edge - 1, pltpu.roll(nx, edge - 1, 0)[0:edge], 0.0) * w[3]
           + jnp.where(r8 >= edge - 2, pltpu.roll(nx, edge - 2, 0)[0:edge], 0.0) * w[4])
    zeros = jnp.zeros((halo - edge, x.shape[1]), F32)

    def finish(a):
        return (a * _sigmoid(a) * s_ref[...]).astype(o_ref.dtype)

    o_ref[0:halo, :] = finish(acc[0:halo] + jnp.concatenate([top, zeros], axis=0))
    o_ref[halo:rows - halo, :] = finish(acc[halo:rows - halo])
    o_ref[rows - halo:rows, :] = finish(acc[rows - halo:rows] + jnp.concatenate([zeros, bot], axis=0))


def _mconv(g, pm, conv_w, conv_b, scale_row):
    rows = g.mblk
    halo = 16
    tn = 2 * BRANCH_W
    per = rows // halo
    nblk = g.r // rows
    last = g.r // halo - 1
    taps = [j for j in range(M_CONV) if j != M_CONV // 2]
    shifts = np.zeros((len(taps), rows, rows), np.float32)
    for n_, j in enumerate(taps):
        for t_ in range(rows):
            if 0 <= t_ + j - M_CONV // 2 < rows:
                shifts[n_, t_, t_ + j - M_CONV // 2] = 1.0
    shifts = jnp.asarray(shifts, BF16)
    return pl.pallas_call(
        functools.partial(_mconv_kernel, blocks_per_seq=g.t // rows, blocks_per_ctx=g.lc // rows,
                          n_lat_blocks=g.rl // rows),
        grid=(nblk, 2 * BRANCH_W // tn),
        in_specs=[
            pl.BlockSpec((rows, tn), lambda i, j: (i, j)),
            pl.BlockSpec((halo, tn), lambda i, j: (jnp.maximum(i * per - 1, 0), j)),
            pl.BlockSpec((halo, tn), lambda i, j: (jnp.minimum((i + 1) * per, last), j)),
            pl.BlockSpec((M_CONV, tn), lambda i, j: (0, j)),
            pl.BlockSpec((1, tn), lambda i, j: (0, j)),
            pl.BlockSpec((1, tn), lambda i, j: (0, j)),
            pl.BlockSpec(shifts.shape, lambda i, j: (0, 0, 0)),
        ],
        out_specs=pl.BlockSpec((rows, tn), lambda i, j: (i, j)),
        out_shape=jax.ShapeDtypeStruct((g.r, 2 * BRANCH_W), BF16),
        compiler_params=_cp("parallel", "parallel"),
        name="mlstm_conv",
    )(pm, pm, pm, conv_w, conv_b, scale_row, shifts)


def _mscan_kernel(qkf_ref, vf_ref, gf_ref, qkb_ref, vb_ref, gb_ref, ol_ref, oc_ref, ng_ref, yl_ref, yc_ref,
                  xsf, bcf, xsb, bcb, dtf, dtb, hf, hb, ct, nv, ms, *, lc, t, mblk, chunk):
    i = pl.program_id(1)
    nblk = pl.num_programs(1)
    L = chunk
    cpb = mblk // L
    nbc = lc // mblk
    nbl = t // mblk
    n = lc + t

    @pl.when(i == 0)
    def _():
        ct[...] = jnp.zeros_like(ct)
        nv[...] = jnp.zeros_like(nv)
        ms[...] = jnp.zeros_like(ms)

    lane = lax.broadcasted_iota(jnp.int32, (1, LANES), 1)
    is_f = jnp.logical_or(jnp.logical_and(lane >= M_HEADS, lane < 2 * M_HEADS),
                          jnp.logical_and(lane >= 3 * M_HEADS, lane < 4 * M_HEADS))
    pos = lax.broadcasted_iota(jnp.int32, (mblk, 1), 0) % L

    def gate_prep(g_ref, x_out, bc_out, forward):
        gx = g_ref[...]
        lsg = jnp.minimum(gx, 0.0) - jnp.log(1.0 + jnp.exp(-jnp.abs(gx)))
        x = jnp.where(is_f, lsg, gx)
        x_out[...] = x
        run = x
        k = 1
        while k < L:
            if forward:
                run = run + jnp.where(pos >= k, pltpu.roll(run, k, 0), 0.0)
            else:
                run = run + jnp.where(pos < L - k, pltpu.roll(run, mblk - k, 0), 0.0)
            k *= 2
        bc_out[...] = run

    gate_prep(gf_ref, xsf, bcf, True)
    gate_prep(gb_ref, xsb, bcb, False)

    pos_f = i * mblk
    pos_b = jnp.where(i < nbc, (nbc - 1 - i) * mblk, lc + (nbl - 1 - (i - nbc)) * mblk)

    def row_table(x_ref, bc_ref, out_ref):
        dm = x_ref[...] - pltpu.roll(bc_ref[...], LANES - M_HEADS, 1)
        for c in range(cpb):
            out_ref[c] = dm[c * L:(c + 1) * L, :].T

    row_table(xsf, bcf, dtf)
    row_table(xsb, bcb, dtb)

    H = M_HEADS
    r_i = lax.broadcasted_iota(jnp.int32, (H * L, L), 0) % L
    c_i = lax.broadcasted_iota(jnp.int32, (H * L, L), 1)
    masks = (c_i <= r_i, c_i >= r_i)

    def stack(parts):
        return jnp.concatenate(parts, axis=0)

    def dir_step(c, d, mvals):
        qk_ref, v_ref, x_ref, bc_ref, dt_ref, h_ref, base = (
            (qkf_ref, vf_ref, xsf, bcf, dtf, hf, pos_f), (qkb_ref, vb_ref, xsb, bcb, dtb, hb, pos_b))[d]
        r0 = pl.multiple_of(c * L, L)
        bg = bc_ref[pl.ds(r0, L), :]
        dt = dt_ref[c]
        qs = [qk_ref[pl.ds(r0, L), j * M_DH:(j + 1) * M_DH] for j in range(H)]
        ks = [qk_ref[pl.ds(r0, L), BRANCH_W + j * M_DH:BRANCH_W + (j + 1) * M_DH] for j in range(H)]
        vs_ = [v_ref[pl.ds(r0, L), j * M_DH:(j + 1) * M_DH] for j in range(H)]
        li0 = 2 * d * H
        lf0 = (2 * d + 1) * H
        a_h = [bg[:, lf0 + j:lf0 + j + 1] for j in range(H)]
        bl_h = [a[L - 1:L, :] if d == 0 else a[0:1, :] for a in a_h]
        a4 = stack(a_h)
        d_rows = [dt[li0 + j:li0 + j + 1, :] for j in range(H)]
        d4 = stack([jnp.broadcast_to(r_, (L, L)) for r_ in d_rows])
        s4 = stack([_dot_nt(qs[j], ks[j]) for j in range(H)])
        logw = jnp.where(masks[d], a4 + d4, -jnp.inf)
        m_loc = jnp.max(logw, axis=1, keepdims=True)
        w4b = (jnp.exp(logw - m_loc) * s4).astype(BF16)
        den_loc = _dot(w4b, jnp.ones((L, LANES), BF16))[:, 0:1]
        num_loc = [_dot(w4b[j * L:(j + 1) * L, :], vs_[j]) for j in range(H)]
        s_rows = [bl_h[j] + d_rows[j] for j in range(H)]
        msrc_h = [jnp.max(s_, axis=1, keepdims=True) for s_ in s_rows]
        g_rows = [jnp.exp(s_rows[j] - msrc_h[j]) for j in range(H)]
        k_t = [ks[j].astype(F32).T for j in range(H)]
        d_c = [_dot((k_t[j] * g_rows[j]).astype(BF16), vs_[j]) for j in range(H)]
        d_n = [_dot(jnp.broadcast_to(g_rows[j], (MOD_ROWS, L)).astype(BF16), ks[j])[0:1, :] for j in range(H)]
        idx = [d * H + j for j in range(H)]
        nrow = [nv[i_] for i_ in idx]
        inter = [_dot(qs[j], ct[idx[j]].astype(BF16)) for j in range(H)]
        qn4 = stack([_dot_nt(qs[j], jnp.broadcast_to(nrow[j], (MOD_ROWS, M_DH)).astype(BF16))[:, 0:1]
                     for j in range(H)])
        m4 = stack([jnp.broadcast_to(mvals[idx[j]], (L, 1)) for j in range(H)])
        bm4 = a4 + m4
        m_t = jnp.maximum(bm4, m_loc)
        e_loc = jnp.exp(m_loc - m_t)
        e_int = jnp.exp(bm4 - m_t)
        den4 = e_loc * den_loc + e_int * qn4
        r4 = 1.0 / jnp.maximum(jnp.abs(den4), jnp.exp(-m_t))
        el = e_loc * r4
        ei = e_int * r4
        rows_out = pl.ds(pl.multiple_of(base + r0, L), L)
        out = list(mvals)
        for j in range(H):
            rs = slice(j * L, (j + 1) * L)
            h_ref[rows_out, j * M_DH:(j + 1) * M_DH] = el[rs] * num_loc[j] + ei[rs] * inter[j]
            m_old = mvals[idx[j]]
            m_new = jnp.maximum(bl_h[j] + m_old, msrc_h[j])
            keep = jnp.exp(bl_h[j] + m_old - m_new)
            scale = jnp.exp(msrc_h[j] - m_new)
            ct[idx[j]] = keep * ct[idx[j]] + scale * d_c[j]
            nv[idx[j]] = keep * nrow[j] + scale * d_n[j]
            out[idx[j]] = m_new
        return tuple(out)

    def body(s, mvals):
        mvals = dir_step(s, 0, mvals)
        return dir_step(cpb - 1 - s, 1, mvals)

    m0 = tuple(ms[q_:q_ + 1, 0:1] for q_ in range(2 * M_HEADS))
    m1 = lax.fori_loop(0, cpb, body, m0, unroll=min(2, cpb))
    for q_ in range(2 * M_HEADS):
        ms[q_:q_ + 1, :] = jnp.broadcast_to(m1[q_], (1, LANES))

    @pl.when(i == nblk - 1)
    def _():
        for blk in range(n // mblk):
            rs = slice(blk * mblk, (blk + 1) * mblk)
            for j in range(M_HEADS):
                hs = slice(j * M_DH, (j + 1) * M_DH)
                hn = _rms(hf[rs, hs] + hb[rs, hs]) * ng_ref[:, hs]
                if blk < nbc:
                    yc_ref[rs, hs] = (hn * _sigmoid(oc_ref[rs, hs].astype(F32))).astype(BF16)
                else:
                    ls = slice(blk * mblk - lc, (blk + 1) * mblk - lc)
                    yl_ref[ls, hs] = (hn * _sigmoid(ol_ref[ls, hs].astype(F32))).astype(BF16)


def _mscan(g, qkc, pm, pg, norm_g):
    mblk = g.mblk
    nbc, nbl = g.lc // mblk, g.t // mblk
    lat_blocks = g.rl // mblk
    n = g.lc + g.t

    def fwd(b, i):
        return jnp.where(i < nbc, lat_blocks + b * nbc + i, b * nbl + (i - nbc))

    def bwd(b, i):
        return jnp.where(i < nbc, lat_blocks + b * nbc + (nbc - 1 - i), b * nbl + (nbl - 1 - (i - nbc)))

    v_col = 2 * BRANCH_W // BRANCH_W
    o_col = 3 * BRANCH_W // BRANCH_W
    ctx0 = g.rl // g.lc

    def side(blk_of):
        return [pl.BlockSpec((mblk, 2 * BRANCH_W), lambda b, i: (blk_of(b, i), 0)),
                pl.BlockSpec((mblk, BRANCH_W), lambda b, i: (blk_of(b, i), v_col)),
                pl.BlockSpec((mblk, LANES), lambda b, i: (blk_of(b, i), PC_GATE_COL // LANES))]

    return pl.pallas_call(
        functools.partial(_mscan_kernel, lc=g.lc, t=g.t, mblk=mblk, chunk=g.chunk),
        grid=(g.b, nbc + nbl),
        in_specs=[
            *side(fwd), *side(bwd),
            pl.BlockSpec((g.t, BRANCH_W), lambda b, i: (b, o_col)),
            pl.BlockSpec((g.lc, BRANCH_W), lambda b, i: (ctx0 + b, o_col)),
            pl.BlockSpec((1, BRANCH_W), lambda b, i: (0, 0)),
        ],
        out_specs=[
            pl.BlockSpec((g.t, BRANCH_W), lambda b, i: (b, 0)),
            pl.BlockSpec((g.lc, BRANCH_W), lambda b, i: (b, 0)),
        ],
        out_shape=[
            jax.ShapeDtypeStruct((g.rl, BRANCH_W), BF16),
            jax.ShapeDtypeStruct((g.rc, BRANCH_W), BF16),
        ],
        scratch_shapes=[
            pltpu.VMEM((mblk, LANES), F32), pltpu.VMEM((mblk, LANES), F32),
            pltpu.VMEM((mblk, LANES), F32), pltpu.VMEM((mblk, LANES), F32),
            pltpu.VMEM((mblk // g.chunk, LANES, g.chunk), F32), pltpu.VMEM((mblk // g.chunk, LANES, g.chunk), F32),
            pltpu.VMEM((n, BRANCH_W), F32), pltpu.VMEM((n, BRANCH_W), F32),
            pltpu.VMEM((2 * M_HEADS, M_DH, M_DH), F32), pltpu.VMEM((2 * M_HEADS, 1, M_DH), F32),
            pltpu.VMEM((2 * M_HEADS, LANES), F32),
        ],
        compiler_params=_cp("parallel", "arbitrary"),
        name="mlstm_scan",
    )(qkc, pm, pg, qkc, pm, pg, pm, pm, norm_g)


def _mla_prep_kernel(cq_ref, ckv_ref, kr_ref, wuq_ref, wukv_ref, cqg_ref, ckvg_ref, qng_ref, qrg_ref,
                     kng_ref, krg_ref, cos_ref, sin_ref, aq_ref, ak_ref, av_ref, wuq_s, wukv_s):
    @pl.when(pl.program_id(0) == 0)
    def _():
        wuq_s[...] = wuq_ref[...].astype(BF16)
        wukv_s[...] = wukv_ref[...].astype(BF16)

    half = LANES // 2
    lane = lax.broadcasted_iota(jnp.int32, (1, LANES), 1)
    lo = lane < half
    first = (lane % half) < (half // 2)

    def rms_half(x):
        x2 = x * x
        s_lo = jnp.sum(jnp.where(lo, x2, 0.0), axis=-1, keepdims=True)
        s_hi = jnp.sum(jnp.where(lo, 0.0, x2), axis=-1, keepdims=True)
        ms = jnp.where(lo, s_lo, s_hi) * (1.0 / half)
        return x * lax.rsqrt(ms + EPS)

    def rows_group(rs):
        cos = cos_ref[rs, :]
        sin = sin_ref[rs, :]

        def rope_half(x):
            partner = jnp.where(first, pltpu.roll(x, LANES - half // 2, 1), pltpu.roll(x, half // 2, 1))
            return x * cos + partner * sin

        cq = (_rms(cq_ref[rs, :]) * cqg_ref[...]).astype(BF16)
        q = _dot(cq, wuq_s[...])
        ckv = (_rms(ckv_ref[rs, :]) * ckvg_ref[...]).astype(BF16)
        kv = _dot(ckv, wukv_s[...])

        krn = rope_half(rms_half(kr_ref[rs, :]) * krg_ref[...]).astype(BF16)

        rope0 = MLA_HEADS * MLA_NOPE
        for hp in range(MLA_HEADS // 2):
            slab = q[:, rope0 + hp * LANES: rope0 + (hp + 1) * LANES]
            r = rope_half(rms_half(slab) * qrg_ref[...])
            parts = (jnp.where(lo, r, 0.0), jnp.where(lo, pltpu.roll(r, half, 1), 0.0))
            for e in range(2):
                h = 2 * hp + e
                nope = _rms(q[:, h * MLA_NOPE:(h + 1) * MLA_NOPE]) * qng_ref[...]
                aq_ref[rs, h * MLA_HW: h * MLA_HW + MLA_NOPE] = nope.astype(BF16)
                aq_ref[rs, h * MLA_HW + MLA_NOPE:(h + 1) * MLA_HW] = parts[e].astype(BF16)
        kvw = MLA_NOPE + MLA_DV
        for h in range(MLA_HEADS):
            kn = _rms(kv[:, h * kvw: h * kvw + MLA_NOPE]) * kng_ref[...]
            ak_ref[rs, h * MLA_HW: h * MLA_HW + MLA_NOPE] = kn.astype(BF16)
            ak_ref[rs, h * MLA_HW + MLA_NOPE:(h + 1) * MLA_HW] = krn
            av_ref[rs, h * MLA_DV:(h + 1) * MLA_DV] = kv[:, h * kvw + MLA_NOPE:(h + 1) * kvw].astype(BF16)

    rows_group(slice(0, cq_ref.shape[0]))


def _mla_prep(g, pc, wuq, wukv, gains, cos_t, sin_t):
    tm = g.tm
    full = lambda shape: pl.BlockSpec(shape, lambda i: (0, 0))
    kr_col = (MLA_Q_RANK + MLA_KV_RANK) // LANES
    return pl.pallas_call(
        _mla_prep_kernel,
        grid=(g.n_tiles,),
        in_specs=[
            pl.BlockSpec((tm, MLA_Q_RANK), lambda i: (i, 0)),
            pl.BlockSpec((tm, MLA_KV_RANK), lambda i: (i, MLA_Q_RANK // MLA_KV_RANK)),
            pl.BlockSpec((tm, LANES), lambda i: (i, kr_col)),
            full(wuq.shape), full(wukv.shape),
            full((1, MLA_Q_RANK)), full((1, MLA_KV_RANK)),
            full((1, LANES)), full((1, LANES)), full((1, LANES)), full((1, LANES)),
            pl.BlockSpec((tm, LANES), lambda i: (g.pos_block(i), 0)),
            pl.BlockSpec((tm, LANES), lambda i: (g.pos_block(i), 0)),
        ],
        out_specs=[
            pl.BlockSpec((tm, MLA_HEADS * MLA_HW), lambda i: (i, 0)),
            pl.BlockSpec((tm, MLA_HEADS * MLA_HW), lambda i: (i, 0)),
            pl.BlockSpec((tm, BRANCH_W), lambda i: (i, 0)),
        ],
        out_shape=[
            jax.ShapeDtypeStruct((g.r, MLA_HEADS * MLA_HW), BF16),
            jax.ShapeDtypeStruct((g.r, MLA_HEADS * MLA_HW), BF16),
            jax.ShapeDtypeStruct((g.r, BRANCH_W), BF16),
        ],
        scratch_shapes=[pltpu.VMEM(wuq.shape, BF16), pltpu.VMEM(wukv.shape, BF16)],
        compiler_params=_cp("arbitrary"),
        name="mla_prep",
    )(pc, pc, pc, wuq, wukv, *gains, cos_t, sin_t)


def _attn_kernel(*refs, n_soft, dh, has_lat, diff, lam_init):
    refs = list(refs)
    vs = refs.pop()
    kts = refs.pop()
    o_ref = refs.pop()
    if diff:
        lam_ref, sg_ref = refs[0], refs[1]
        refs = refs[2:]
    q_ref, kc_ref, vc_ref = refs[:3]
    lc = kc_ref.shape[0]

    @pl.when(pl.program_id(2) == 0)
    def _():
        kts[:, 0:lc] = kc_ref[...].T
        vs[0:lc, :] = vc_ref[...]
        if has_lat:
            kts[:, lc:] = refs[3][...].T
            vs[lc:, :] = refs[4][...]

    q = q_ref[...]
    tq = q.shape[0]
    n_sub = max(1, min(4, tq // LANES))
    rsub = tq // n_sub
    outs = []
    for s in range(n_soft):
        sl = slice(s * dh, (s + 1) * dh)
        scs = [_dot(q[u * rsub:(u + 1) * rsub, sl], kts[sl, :]) for u in range(n_sub)]
        ps, ls = [], []
        for sc in scs:
            m = jnp.max(sc, axis=-1, keepdims=True)
            p = jnp.exp(sc - m)
            ls.append(jnp.sum(p, axis=-1, keepdims=True))
            ps.append(p.astype(BF16))
        os_ = [_dot(p, vs[...]) / l for p, l in zip(ps, ls)]
        outs.append(jnp.concatenate(os_, axis=0))
    if diff:
        lp = lam_ref[...]
        lam = (jnp.exp(jnp.sum(lp[0:1] * lp[1:2], axis=-1, keepdims=True))
               - jnp.exp(jnp.sum(lp[2:3] * lp[3:4], axis=-1, keepdims=True)) + lam_init)
        o = outs[0] - lam * outs[1]
        o = _rms(o) * sg_ref[...] * (1.0 - lam_init)
    else:
        o = outs[0]
    o_ref[...] = o.astype(o_ref.dtype)


def _attention(g, q_arr, k_arr, v_arr, *, q_col0, k_col0, heads, qk_w, dv, n_soft, ctx_queries,
               lam=None, subln_g=None, lam_init=0.0):
    diff = lam is not None
    ctx0 = g.rl // g.lc
    n_keys = g.lc if ctx_queries else g.lc + g.t
    if ctx_queries:
        tq = g.lc
        nq = 1
        q_row = lambda b, qi: ctx0 + b
        out_rows = g.rc
        o_row = lambda b, qi: b
    else:
        tq = min(ATT_TQ, g.t)
        nq = g.t // tq
        q_row = lambda b, qi: b * nq + qi
        out_rows = g.rl
        o_row = q_row
    in_specs = []
    args = []
    if diff:
        in_specs += [pl.BlockSpec(lam.shape, lambda b, h, qi: (0, 0)),
                     pl.BlockSpec((1, dv), lambda b, h, qi: (0, 0))]
        args += [lam, subln_g]
    in_specs += [
        pl.BlockSpec((tq, qk_w), lambda b, h, qi: (q_row(b, qi), q_col0 + h)),
        pl.BlockSpec((g.lc, qk_w), lambda b, h, qi: (ctx0 + b, k_col0 + h)),
        pl.BlockSpec((g.lc, dv), lambda b, h, qi: (ctx0 + b, h)),
    ]
    args += [q_arr, k_arr, v_arr]
    if not ctx_queries:
        in_specs += [
            pl.BlockSpec((g.t, qk_w), lambda b, h, qi: (b, k_col0 + h)),
            pl.BlockSpec((g.t, dv), lambda b, h, qi: (b, h)),
        ]
        args += [k_arr, v_arr]
    return pl.pallas_call(
        functools.partial(_attn_kernel, n_soft=n_soft, dh=qk_w // n_soft, has_lat=not ctx_queries,
                          diff=diff, lam_init=lam_init),
        grid=(g.b, heads, nq),
        in_specs=in_specs,
        out_specs=pl.BlockSpec((tq, dv), lambda b, h, qi: (o_row(b, qi), h)),
        out_shape=jax.ShapeDtypeStruct((out_rows, heads * dv), BF16),
        scratch_shapes=[pltpu.VMEM((qk_w, n_keys), BF16), pltpu.VMEM((n_keys, dv), BF16)],
        compiler_params=_cp("parallel", "parallel", "arbitrary"),
        name="attn_diff" if diff else "attn_mla",
    )(*args)


def _branch_kernel(*refs, n_src, n_lat_tiles):
    y_refs = refs[:N_BRANCH * n_src]
    wb_ref, g0_ref, g1_ref, g2_ref, o_ref, wb_s = refs[N_BRANCH * n_src:]
    gates = (g0_ref, g1_ref, g2_ref)

    @pl.when(pl.program_id(1) == 0)
    def _():
        wb_s[...] = wb_ref[...].astype(BF16)

    def body(which):
        def run(_):
            acc = None
            for n in range(N_BRANCH):
                term = gates[n][...].astype(F32) * _dot(y_refs[n * n_src + which][...], wb_s[n])
                acc = term if acc is None else acc + term
            o_ref[...] = acc.astype(o_ref.dtype)
        return run

    i = pl.program_id(1)
    if n_src == 1:
        body(0)(None)
    else:
        pl.when(i < n_lat_tiles)(lambda: body(0)(None))
        pl.when(i >= n_lat_tiles)(lambda: body(1)(None))


def _branch_merge(g, ys, w_branch, l, gt, n_tiles):
    tn = 512
    nj = g.d // tn
    y_specs, y_args = [], []
    for y in ys:
        sp, ar = _row_sources(g, y, BRANCH_W, lambda j: 0, 2)
        y_specs += sp
        y_args += ar
    n_src = len(y_args) // N_BRANCH
    return pl.pallas_call(
        functools.partial(_branch_kernel, n_src=n_src, n_lat_tiles=g.n_lat_tiles),
        grid=(nj, n_tiles),
        in_specs=[
            *y_specs,
            pl.BlockSpec((None, N_BRANCH, BRANCH_W, tn), lambda j, i: (l, 0, 0, j)),
            pl.BlockSpec((g.tm, tn), lambda j, i: (i, j)),
            pl.BlockSpec((g.tm, tn), lambda j, i: (i, nj + j)),
            pl.BlockSpec((g.tm, tn), lambda j, i: (i, 2 * nj + j)),
        ],
        out_specs=pl.BlockSpec((g.tm, tn), lambda j, i: (i, j)),
        out_shape=jax.ShapeDtypeStruct((n_tiles * g.tm, g.d), BF16),
        scratch_shapes=[pltpu.VMEM((N_BRANCH, BRANCH_W, tn), BF16)],
        compiler_params=_cp("arbitrary", "arbitrary"),
        name="branch_merge",
    )(*y_args, w_branch, gt, gt, gt)


def _outproj_kernel(*refs, n_lat_tiles):
    z_ref, w_ref, *x_refs, g_ref, o_ref, w_s = refs

    @pl.when(pl.program_id(1) == 0)
    def _():
        w_s[...] = w_ref[...].astype(BF16)

    def body(x_ref):
        o_ref[...] = x_ref[...] + g_ref[...] * _dot(z_ref[...], w_s[...])

    _on_rows(pl.program_id(1), n_lat_tiles, x_refs, body)


def _outproj(g, z, w_out, l, x_src, mods3, which_gate, n_tiles):
    tn = 1024
    per = g.d // tn
    x_specs, x_args = _row_sources(g, x_src, tn, lambda j: j, 2)
    return pl.pallas_call(
        functools.partial(_outproj_kernel, n_lat_tiles=g.n_lat_tiles),
        grid=(per, n_tiles),
        in_specs=[
            pl.BlockSpec((g.tm, g.d), lambda j, i: (i, 0)),
            pl.BlockSpec((None, g.d, tn), lambda j, i: (l, 0, j)),
            *x_specs,
            pl.BlockSpec((None, 1, tn), lambda j, i: (g.mod_row(i), 0, which_gate * per + j)),
        ],
        out_specs=pl.BlockSpec((g.tm, tn), lambda j, i: (i, j)),
        out_shape=jax.ShapeDtypeStruct((n_tiles * g.tm, g.d), F32),
        scratch_shapes=[pltpu.VMEM((g.d, tn), BF16)],
        compiler_params=_cp("arbitrary", "arbitrary"),
        name="outproj",
    )(z, w_out, *x_args, mods3)


def _moe_prep_kernel(x_ref, sh_ref, sc_ref, rwt_ref, rb_ref, tri_ref, xr_ref, rank_ref, grp_ref, cnt_ref,
                     carry, *, mod_row):
    i = pl.program_id(0)

    @pl.when(i == 0)
    def _():
        carry[...] = jnp.zeros_like(carry)

    x = x_ref[...]
    h = _rms(x) * (1.0 + sc_ref[...]) + sh_ref[...]
    logits = _dot_nt(rwt_ref[...].astype(BF16), h.astype(BF16))
    scores = _sigmoid(logits)
    selm = scores + rb_ref[...]
    sc_rows = [scores[e:e + 1, :] for e in range(N_EXPERTS)]
    sel = [selm[e:e + 1, :] for e in range(N_EXPERTS)]
    gscore = []
    for gi in range(N_GROUPS):
        v = sel[gi * GROUP_SIZE:(gi + 1) * GROUP_SIZE]
        pair = [v[a] + v[b] for a in range(GROUP_SIZE) for b in range(a + 1, GROUP_SIZE)]
        gscore.append(functools.reduce(jnp.maximum, pair))
    gmax = functools.reduce(jnp.maximum, gscore)
    taken = jnp.zeros_like(gmax, dtype=jnp.bool_)
    in_group = []
    for gi in range(N_GROUPS):
        hit = jnp.logical_and(gscore[gi] == gmax, jnp.logical_not(taken))
        taken = jnp.logical_or(taken, hit)
        in_group.append(hit)
    neg = -jnp.inf
    masked = [jnp.where(in_group[e // GROUP_SIZE], sel[e], neg) for e in range(N_EXPERTS)]

    def pick(vals):
        mx = functools.reduce(jnp.maximum, vals)
        seen = jnp.zeros_like(mx, dtype=jnp.bool_)
        hot = []
        for v in vals:
            hit = jnp.logical_and(v == mx, jnp.logical_not(seen))
            seen = jnp.logical_or(seen, hit)
            hot.append(hit)
        return hot

    hot1 = pick(masked)
    hot2 = pick([jnp.where(hot1[e], neg, masked[e]) for e in range(N_EXPERTS)])
    zero = jnp.zeros_like(gmax)
    s1 = functools.reduce(lambda a, b: a + b, [jnp.where(hot1[e], sc_rows[e], zero) for e in range(N_EXPERTS)])
    s2 = functools.reduce(lambda a, b: a + b, [jnp.where(hot2[e], sc_rows[e], zero) for e in range(N_EXPERTS)])
    tot = s1 + s2
    w1 = s1 / tot
    w2 = s2 / tot
    comb = [jnp.where(hot1[e], w1, zero) + jnp.where(hot2[e], w2, zero) for e in range(N_EXPERTS)]

    tm = x.shape[0]
    gmat = jnp.concatenate([jnp.where(in_group[gi], 1.0, zero) for gi in range(N_GROUPS)]
                           + [jnp.zeros((MOD_ROWS - N_GROUPS, tm), F32)], axis=0)
    incl = _dot(gmat.astype(BF16), tri_ref[...])
    excl = incl - gmat + carry[:, 0:1]
    rank = functools.reduce(lambda a, b: a + b,
                            [jnp.where(in_group[gi], excl[gi:gi + 1, :], zero) for gi in range(N_GROUPS)])
    grp = functools.reduce(lambda a, b: a + b,
                           [jnp.where(in_group[gi], float(gi), zero) for gi in range(N_GROUPS)])
    rank_ref[...] = rank.astype(jnp.int32)
    grp_ref[...] = grp.astype(jnp.int32)
    carry[...] = carry[...] + jnp.sum(gmat, axis=1, keepdims=True)
    cnt_ref[...] = carry[...]

    side = jnp.concatenate(comb + [jnp.zeros((LANES - N_EXPERTS, tm), F32)], axis=0).T
    lane = lax.broadcasted_iota(jnp.int32, (1, LANES), 1)
    side = jnp.where(lane == N_EXPERTS + mod_row(i), 1.0, side)
    d = x.shape[1]
    xr_ref[:, 0:d] = x
    xr_ref[:, d:d + LANES] = side


def _moe_prep(g, x_mid, mods3, which_shift, which_scale, rwt, rb, tri, n_tiles):
    n_rows = n_tiles * g.tm
    return pl.pallas_call(
        functools.partial(_moe_prep_kernel, mod_row=g.mod_row),
        grid=(n_tiles,),
        in_specs=[
            pl.BlockSpec((g.tm, g.d), lambda i: (i, 0)),
            _mod_spec(g, which_shift),
            _mod_spec(g, which_scale),
            pl.BlockSpec((N_EXPERTS, g.d), lambda i: (0, 0)),
            pl.BlockSpec((N_EXPERTS, 1), lambda i: (0, 0)),
            pl.BlockSpec((g.tm, g.tm), lambda i: (0, 0)),
        ],
        out_specs=[
            pl.BlockSpec((g.tm, g.d + LANES), lambda i: (i, 0)),
            pl.BlockSpec((1, g.tm), lambda i: (0, i)),
            pl.BlockSpec((1, g.tm), lambda i: (0, i)),
            pl.BlockSpec((MOD_ROWS, LANES), lambda i: (0, 0)),
        ],
        out_shape=[
            jax.ShapeDtypeStruct((n_rows, g.d + LANES), F32),
            jax.ShapeDtypeStruct((1, n_rows), jnp.int32),
            jax.ShapeDtypeStruct((1, n_rows), jnp.int32),
            jax.ShapeDtypeStruct((MOD_ROWS, LANES), F32),
        ],
        scratch_shapes=[pltpu.VMEM((MOD_ROWS, LANES), F32)],
        compiler_params=_cp("arbitrary"),
        name="moe_prep",
    )(x_mid, mods3, mods3, rwt, rb, tri)


def _moe_group_kernel(tg_ref, nv_ref, nu_ref, pos_ref, xr_hbm, mods_ref, w1_hbm, w3_hbm, w2_hbm, out_hbm,
                      gbuf, obuf, w1s, w3s, w2s, stg_a, stg_b, src_ref, gsem, ssem, wsem, *, n_mod_rows, e_base):
    k = pl.program_id(0)
    n_used = nu_ref[0]
    slot = k % 2
    d = obuf.shape[2]
    tme = obuf.shape[1]

    def gather_start(kk, sl):
        base = kk * tme

        first = src_ref[base]

        def copy_row(r, tok):
            pltpu.make_async_copy(xr_hbm.at[pl.ds(tok, 1), :], gbuf.at[sl, pl.ds(r, 1), :], gsem.at[sl]).start()

        def issue(r, c):
            copy_row(r, src_ref[base + r])
            return c

        def issue_pad(r, c):
            copy_row(r, first)
            return c

        rows_of(kk, issue)
        lax.fori_loop(nv_ref[kk], tme, issue_pad, 0)

    def build_sorted_index():
        def put(t, c):
            src_ref[pos_ref[t]] = t
            return c

        lax.fori_loop(0, pos_ref.shape[0], put, 0, unroll=16)

    def gather_wait(sl):
        def w(r, c):
            pltpu.make_async_copy(xr_hbm.at[pl.ds(0, 1), :], gbuf.at[sl, pl.ds(0, 1), :], gsem.at[sl]).wait()
            return c

        lax.fori_loop(0, tme, w, 0, unroll=8)

    def scatter_start(kk, sl):
        base = kk * tme

        def issue(r, c):
            tok = src_ref[base + r]
            pltpu.make_async_copy(obuf.at[sl, pl.ds(r, 1), :], out_hbm.at[pl.ds(tok, 1), :], ssem.at[sl]).start()
            return c

        rows_of(kk, issue)

    def scatter_wait(kk, sl):
        def w(r, c):
            pltpu.make_async_copy(obuf.at[sl, pl.ds(0, 1), :], out_hbm.at[pl.ds(0, 1), :], ssem.at[sl]).wait()
            return c

        rows_of(kk, w)

    def rows_of(kk, fn):
        n_rows = nv_ref[kk]

        @pl.when(n_rows == tme)
        def _():
            lax.fori_loop(0, tme, fn, 0, unroll=8)

        @pl.when(n_rows != tme)
        def _():
            lax.fori_loop(0, n_rows, fn, 0)

    def load_weights(grp):
        rows_a = stg_a.shape[1]
        rows_b = stg_b.shape[1]
        chunks = []
        n_a = n_b = 0
        for e in range(GROUP_SIZE):
            ex = e_base + grp * GROUP_SIZE + e
            for src, dst in ((w1_hbm, w1s), (w3_hbm, w3s)):
                for c0 in range(0, src.shape[1], rows_a):
                    sl = n_a % 2
                    n_a += 1
                    cp = pltpu.make_async_copy(src.at[ex, pl.ds(c0, rows_a), :], stg_a.at[sl], wsem.at[sl])
                    chunks.append((cp, stg_a.at[sl], dst, e, c0, rows_a))
            for c0 in range(0, w2_hbm.shape[1], rows_b):
                sl = n_b % 2
                n_b += 1
                cp = pltpu.make_async_copy(w2_hbm.at[ex, pl.ds(c0, rows_b), :], stg_b.at[sl], wsem.at[2 + sl])
                chunks.append((cp, stg_b.at[sl], w2s, e, c0, rows_b))
        chunks[0][0].start()
        for idx, (cp, stg, dst, e, c0, rows) in enumerate(chunks):
            if idx + 1 < len(chunks):
                chunks[idx + 1][0].start()
            cp.wait()
            dst[e, c0:c0 + rows, :] = stg[...].astype(BF16)

    @pl.when(k < n_used)
    def _():
        grp = tg_ref[k]

        @pl.when(k == 0)
        def _():
            build_sorted_index()
            gather_start(0, 0)

        @pl.when(k + 1 < n_used)
        def _():
            gather_start(k + 1, 1 - slot)

        @pl.when(jnp.logical_or(k == 0, grp != tg_ref[jnp.maximum(k - 1, 0)]))
        def _():
            load_weights(grp)

        gather_wait(slot)

        @pl.when(k >= 2)
        def _():
            scatter_wait(k - 2, slot)

        lane = lax.broadcasted_iota(jnp.int32, (1, LANES), 1)
        x = gbuf[slot, :, 0:d]
        side = gbuf[slot, :, d:d + LANES]

        def cond_vec(which):
            out = jnp.broadcast_to(mods_ref[0:1, which * d:(which + 1) * d], (tme, d))
            for r in range(1, n_mod_rows):
                hot = side[:, N_EXPERTS + r:N_EXPERTS + r + 1] > 0.5
                out = jnp.where(hot, mods_ref[r:r + 1, which * d:(which + 1) * d], out)
            return out

        h = (_rms(x) * (1.0 + cond_vec(1)) + cond_vec(0)).astype(BF16)
        y = jnp.zeros((tme, d), F32)
        for e in range(GROUP_SIZE):
            a = _dot(h, w1s[e])
            b = _dot(h, w3s[e])
            cw = jnp.sum(jnp.where(lane == grp * GROUP_SIZE + e, side, 0.0), axis=1, keepdims=True)
            hid = (a * _sigmoid(a) * b * cw).astype(BF16)
            y = y + _dot(hid, w2s[e])
        obuf[slot] = x + cond_vec(2) * y

        scatter_start(k, slot)

        @pl.when(k == n_used - 1)
        def _():
            @pl.when(k >= 1)
            def _():
                scatter_wait(k - 1, 1 - slot)

            scatter_wait(k, slot)


def _moe_group(g, xr, mods_tail, w1, w3, w2, meta, n_rows, n_mod_rows, e_base):
    kt = meta[0].shape[0]
    tme = MOE_TM
    any_spec = pl.BlockSpec(memory_space=pl.ANY)
    grid_spec = pltpu.PrefetchScalarGridSpec(
        num_scalar_prefetch=len(meta),
        grid=(kt,),
        in_specs=[any_spec, pl.BlockSpec(mods_tail.shape, lambda k, *_: (0, 0)), any_spec, any_spec, any_spec],
        out_specs=any_spec,
        scratch_shapes=[
            pltpu.VMEM((2, tme, g.d + LANES), F32),
            pltpu.VMEM((2, tme, g.d), F32),
            pltpu.VMEM((GROUP_SIZE, g.d, D_EXPERT), BF16),
            pltpu.VMEM((GROUP_SIZE, g.d, D_EXPERT), BF16),
            pltpu.VMEM((GROUP_SIZE, D_EXPERT, g.d), BF16),
            pltpu.VMEM((2, g.d // 2, D_EXPERT), F32),
            pltpu.VMEM((2, D_EXPERT // 2, g.d), F32),
            pltpu.SMEM((kt * tme,), jnp.int32),
            pltpu.SemaphoreType.DMA((2,)),
            pltpu.SemaphoreType.DMA((2,)),
            pltpu.SemaphoreType.DMA((4,)),
        ],
    )
    return pl.pallas_call(
        functools.partial(_moe_group_kernel, n_mod_rows=n_mod_rows, e_base=e_base),
        grid_spec=grid_spec,
        out_shape=jax.ShapeDtypeStruct((n_rows, g.d), F32),
        compiler_params=_cp("arbitrary"),
        name="moe_group",
    )(*meta, xr, mods_tail, w1, w3, w2)


def _moe_meta(rank, grp, cnt, n_rows):
    tme = MOE_TM
    kt = n_rows // tme + N_GROUPS
    counts = cnt[:N_GROUPS, 0].astype(jnp.int32)
    ntile = (counts + tme - 1) // tme
    tile_end = jnp.cumsum(ntile)
    tile_off = tile_end - ntile
    gsel = [grp[0] == gi for gi in range(N_GROUPS)]
    row_off = functools.reduce(lambda a, b: a + b,
                               [jnp.where(gsel[gi], tile_off[gi] * tme, 0) for gi in range(N_GROUPS)])
    pos = row_off + rank[0]
    ks = jnp.arange(kt, dtype=jnp.int32)
    tile_group = jnp.minimum(functools.reduce(lambda a, b: a + b,
                                              [(ks >= tile_end[gi]).astype(jnp.int32) for gi in range(N_GROUPS)]),
                             N_GROUPS - 1)
    n_used = tile_end[-1:]
    cnt_k = functools.reduce(lambda a, b: a + b,
                             [jnp.where(tile_group == gi, counts[gi] - (ks - tile_off[gi]) * tme, 0)
                              for gi in range(N_GROUPS)])
    n_valid = jnp.where(ks < n_used[0], jnp.clip(cnt_k, 0, tme), 0)
    return tile_group, n_valid, n_used, pos


def _rope_tables(t, rot_dim, tile_rows, reps):
    f32 = np.float32
    rows = t // GRID_W
    r = np.repeat(np.arange(rows, dtype=f32), GRID_W)
    col = np.tile(np.arange(GRID_W, dtype=f32), rows)
    n_freq = rot_dim // 4
    inv = np.power(f32(ROPE_BASE), -np.arange(n_freq, dtype=f32) / f32(n_freq)).astype(f32)
    ang = np.concatenate([r[:, None] * inv, col[:, None] * inv], axis=-1).astype(f32)
    cos = np.tile(np.concatenate([np.cos(ang), np.cos(ang)], axis=-1), (1, reps))
    sin = np.tile(np.concatenate([-np.sin(ang), np.sin(ang)], axis=-1), (1, reps))
    cos = np.concatenate([cos, np.ones((tile_rows, LANES), f32)], axis=0).astype(f32)
    sin = np.concatenate([sin, np.zeros((tile_rows, LANES), f32)], axis=0).astype(f32)
    return jnp.asarray(cos), jnp.asarray(sin)


def _uq_perm():
    nope = [h * MLA_QK + i for h in range(MLA_HEADS) for i in range(MLA_NOPE)]
    rope = [h * MLA_QK + MLA_NOPE + i for h in range(MLA_HEADS) for i in range(MLA_ROPE)]
    return jnp.asarray(nope + rope)


def kernel(x, c, ctx, c_ctx, w_ada, b_ada, w_in, b_in, m_conv_w, m_conv_b, m_norm_g, da_q_norm_g, da_k_norm_g,
           da_lambda, da_subln_g, mla_cq_norm_g, mla_ckv_norm_g, mla_w_uq, mla_w_ukv, mla_q_norm_g, mla_k_norm_g,
           w_branch, w_out, moe_w1, moe_w3, moe_w2, router_w, router_bias):
    bsz, t, d = x.shape
    lc = ctx.shape[1]
    depth = w_ada.shape[0]
    g = _Geom(bsz, t, lc, d, min(1024, bsz * lc, t))
    assert bsz + 1 <= MOD_ROWS

    cvec = jnp.concatenate([c, c_ctx[None, :], jnp.zeros((MOD_ROWS - bsz - 1, d), F32)], axis=0)
    mods = _mod_vectors(cvec, w_ada, b_ada)

    cos_d, sin_d = _rope_tables(t, DA_DH, g.tm, 1)
    cos_a, sin_a = _rope_tables(t, MLA_ROPE, g.tm, 2)
    rwt = router_w.T
    rb = router_bias.reshape(N_EXPERTS, 1)
    uq_perm = _uq_perm()
    x_all = (x.reshape(g.rl, d), ctx.reshape(g.rc, d))
    pos_spec = pl.BlockSpec((g.tm, LANES), lambda j, i: (g.pos_block(i), 0))
    b_in3 = b_in.reshape(depth, 1, -1)
    w_in_t = jnp.swapaxes(w_in, 1, 2)
    qk_scale = jnp.concatenate([jnp.full((1, BRANCH_W), M_DH ** -0.5, F32), jnp.ones((1, BRANCH_W), F32)], axis=1)
    tri = (jnp.arange(g.tm)[:, None] <= jnp.arange(g.tm)[None, :]).astype(BF16)
    w1r = moe_w1.reshape(depth * N_EXPERTS, d, D_EXPERT)
    w3r = moe_w3.reshape(depth * N_EXPERTS, d, D_EXPERT)
    w2r = moe_w2.reshape(depth * N_EXPERTS, D_EXPERT, d)

    for l in range(depth):
        last = l == depth - 1
        n_tiles = g.n_lat_tiles if last else g.n_tiles
        n_rows = n_tiles * g.tm
        lam_init = 0.8 - 0.6 * math.exp(-0.3 * l)
        mods3 = mods[l].reshape(MOD_ROWS, 1, 6 * d)

        h1 = _prenorm(g, x_all, mods3, 0, 1, g.n_tiles)

        row = lambda v: v.reshape(1, -1)
        proj = functools.partial(_proj, g, h=h1, w_in_t=w_in_t, b_in3=b_in3, l=l)
        pm = proj(_epi_raw, col0=OFF_MQK, width=OFF_MG - OFF_MQK, out_dtype=BF16, tn=1024, name="proj_mlstm")
        da_gain = jnp.concatenate([jnp.tile(da_q_norm_g[l] * DA_DH ** -0.5, 2 * DA_HEADS),
                                   jnp.tile(da_k_norm_g[l], 2 * DA_HEADS)])
        dqk = proj(_epi_normrope, col0=OFF_DQ, width=2 * DA_QK_W, out_dtype=BF16,
                   extra=(row(da_gain), cos_d, sin_d),
                   extra_specs=(pl.BlockSpec((1, 512), lambda j, i: (0, j)), pos_spec, pos_spec),
                   name="proj_dqk")
        dv = proj(_epi_raw, col0=OFF_DV, width=BRANCH_W, out_dtype=BF16, name="proj_dv")
        pc = proj(_epi_raw, col0=OFF_CQ, width=OFF_G - OFF_CQ, out_dtype=F32, patch=(OFF_MG, PC_GATE_COL),
                  name="proj_small")
        gt = proj(_epi_sigmoid, col0=OFF_G, width=N_BRANCH * d, out_dtype=BF16, name="proj_gates")

        qkc = _mconv(g, pm, m_conv_w[l], row(m_conv_b[l]), qk_scale)
        ym_l, ym_c = _mscan(g, qkc, pm, pc, row(m_norm_g[l]))

        qg, kg = mla_q_norm_g[l], mla_k_norm_g[l]
        a_scale = MLA_QK ** -0.5
        gains = (row(mla_cq_norm_g[l]), row(mla_ckv_norm_g[l]),
                 row(qg[:MLA_NOPE] * a_scale), row(jnp.tile(qg[MLA_NOPE:], 2) * a_scale),
                 row(kg[:MLA_NOPE]), row(jnp.concatenate([kg[MLA_NOPE:], jnp.zeros((LANES - MLA_ROPE,), F32)])))
        aq, ak, av = _mla_prep(g, pc, mla_w_uq[l][:, uq_perm], mla_w_ukv[l], gains, cos_a, sin_a)

        lam_p = da_lambda[l]
        sub_g = row(da_subln_g[l])
        da_kw = dict(q_col0=0, k_col0=DA_HEADS, heads=DA_HEADS, qk_w=2 * DA_DH, dv=DA_DV, n_soft=2,
                     lam=lam_p, subln_g=sub_g, lam_init=lam_init)
        mla_kw = dict(q_col0=0, k_col0=0, heads=MLA_HEADS, qk_w=MLA_HW, dv=MLA_DV, n_soft=1)
        yd_l = _attention(g, dqk, dqk, dv, ctx_queries=False, **da_kw)
        ya_l = _attention(g, aq, ak, av, ctx_queries=False, **mla_kw)
        if last:
            ys = (ym_l, yd_l, ya_l)
        else:
            yd_c = _attention(g, dqk, dqk, dv, ctx_queries=True, **da_kw)
            ya_c = _attention(g, aq, ak, av, ctx_queries=True, **mla_kw)
            ys = ((ym_l, ym_c), (yd_l, yd_c), (ya_l, ya_c))

        z = _branch_merge(g, ys, w_branch, l, gt, n_tiles)
        x_mid = _outproj(g, z, w_out, l, x_all, mods3, 2, n_tiles)

        xr, rank, grp, cnt = _moe_prep(g, x_mid, mods3, 3, 4, rwt, rb, tri, n_tiles)
        meta = _moe_meta(rank, grp, cnt, n_rows)
        n_cond = bsz if last else bsz + 1
        x_all = _moe_group(g, xr, mods[l][:, 3 * d:], w1r, w3r, w2r, meta, n_rows, n_cond, l * N_EXPERTS)

    return x_all.reshape(bsz, t, d)
```
